```python
import math
import jax
import jax.numpy as jnp
from jax import lax
import numpy as np

D_MODEL = 2048
BATCH = 2
SEQ = 8192
DEPTH = 1
DEC_BATCH = 32
DEC_SEQ = 4
PAST_LEN = 16384
PAGE_SIZE = 128

HEAD_DIM = 128
RET_HEADS = 8
RET_WIDTH = RET_HEADS * HEAD_DIM
RET_CHUNK = 128
ROPE_BASE = 10000.0
GN_EPS = 1e-5
DIL_PAIRS = ((128, 1), (512, 4), (2048, 16))
DIL_HEADS_PER_GROUP = 4
DIL_HEADS = DIL_HEADS_PER_GROUP * len(DIL_PAIRS)
DIL_WIDTH = DIL_HEADS * HEAD_DIM
DIL_OUT_WIDTH = DIL_HEADS_PER_GROUP * HEAD_DIM
DIL_BLOCK = 128
ATTN_SCALE = HEAD_DIM ** -0.5
REL_BUCKETS = 32
REL_MAX_DIST = 2048
N_GROUPS = 4
EXPERTS_PER_GROUP = 8
TOP_K_IN_GROUP = 2
D_EXPERT = 512
NORM_EPS = 1e-6
IN_WIDTH = 4 * RET_WIDTH + 3 * DIL_WIDTH + 2 * D_MODEL

kernel_name = "retention_dilated_attn_hier_moe_step"


def rms_norm(x, g):
    xf = x.astype(jnp.float32)
    y = xf * lax.rsqrt(jnp.mean(xf * xf, axis=-1, keepdims=True) + NORM_EPS)
    return (y * g.astype(jnp.float32)).astype(x.dtype)


def split_proj(p):
    sizes = (RET_WIDTH,) * 4 + (DIL_WIDTH,) * 3 + (D_MODEL,) * 2
    return jnp.split(p, np.cumsum(sizes)[:-1].tolist(), axis=-1)


def rotary(x, pos):
    half = HEAD_DIM // 2
    inv = ROPE_BASE ** (-jnp.arange(half, dtype=jnp.float32) / half)
    ang = pos.astype(jnp.float32)[:, None] * inv[None, :]
    cos = jnp.cos(ang)[None, :, None, :]
    sin = jnp.sin(ang)[None, :, None, :]
    x1, x2 = x[..., :half], x[..., half:]
    return jnp.concatenate([x1 * cos - x2 * sin, x2 * cos + x1 * sin], axis=-1)


def retention(q, k, v, s0):
    B, L, H, D = q.shape
    c = RET_CHUNK if L % RET_CHUNK == 0 else L
    n = L // c
    lg = jnp.log1p(-jnp.exp2(-5.0 - jnp.arange(H, dtype=jnp.float32)))
    idx = jnp.arange(c, dtype=jnp.float32)
    rel = idx[:, None] - idx[None, :]
    dmask = jnp.where(rel[None] >= 0, jnp.exp(lg[:, None, None] * jnp.maximum(rel, 0.0)[None]), 0.0)
    q_dec = jnp.exp(lg[:, None] * (idx + 1.0)[None, :])
    k_dec = jnp.exp(lg[:, None] * (c - 1.0 - idx)[None, :])
    c_dec = jnp.exp(lg * c)
    qc = q.reshape(B, n, c, H, D)
    kc = k.reshape(B, n, c, H, D)
    vc = v.reshape(B, n, c, H, D)
    scores = jnp.einsum('bnihd,bnjhd->bnhij', qc, kc) * dmask
    intra = jnp.einsum('bnhij,bnjhe->bnihe', scores, vc)
    kv = jnp.einsum('bnjhd,hj,bnjhe->nbhde', kc, k_dec, vc)

    def step(s, kv_n):
        return s * c_dec[None, :, None, None] + kv_n, s

    s_fin, s_prev = lax.scan(step, s0.astype(jnp.float32), kv)
    cross = jnp.einsum('bnihd,hi,nbhde->bnihe', qc, q_dec, s_prev)
    return (intra + cross).reshape(B, L, H, D), s_fin


def head_group_norm(o):
    mu = jnp.mean(o, axis=-1, keepdims=True)
    var = jnp.mean(jnp.square(o - mu), axis=-1, keepdims=True)
    return (o - mu) * lax.rsqrt(var + GN_EPS)


def t5_bucket(dist):
    max_exact = REL_BUCKETS // 2
    d = jnp.maximum(dist, 0)
    df = jnp.maximum(d, 1).astype(jnp.float32)
    large = max_exact + (jnp.log(df / max_exact) / math.log(REL_MAX_DIST / max_exact)
                         * (REL_BUCKETS - max_exact)).astype(jnp.int32)
    large = jnp.minimum(large, REL_BUCKETS - 1)
    return jnp.where(d < max_exact, d, large)


def dilated_group_prompt(q, k, v, bias_tab, r):
    B, S, H, D = q.shape
    blk = DIL_BLOCK
    lc = S // r
    nb = -(-lc // blk)
    lp = nb * blk
    n = B * r

    def to_residue(t):
        t = t.reshape(B, lc, r, H, D).transpose(0, 2, 1, 3, 4).reshape(n, lc, H, D)
        return jnp.pad(t, ((0, 0), (0, lp - lc), (0, 0), (0, 0)))

    def band_windows(t):
        tp = jnp.pad(t, ((0, 0), (blk, 0), (0, 0), (0, 0))).reshape(n, nb + 1, blk, H, D)
        return jnp.concatenate([tp[:, :-1], tp[:, 1:]], axis=2)

    qb = to_residue(q).reshape(n, nb, blk, H, D)
    kw = band_windows(to_residue(k))
    vw = band_windows(to_residue(v))
    qi = jnp.arange(blk)[:, None]
    kj = jnp.arange(2 * blk)[None, :]
    dc = blk + qi - kj
    band = (dc >= 0) & (dc <= blk)
    has_prev = (jnp.arange(nb)[:, None, None] > 0) | (kj[None] >= blk)
    mask = band[None] & has_prev
    bias = bias_tab[t5_bucket(dc * r)].astype(jnp.float32).transpose(2, 0, 1)
    s = jnp.einsum('nbqhd,nbkhd->nbhqk', qb, kw).astype(jnp.float32) * ATTN_SCALE + bias
    s = jnp.where(mask[None, :, None], s, -jnp.inf)
    m = jnp.max(s, axis=-1, keepdims=True)
    p = jnp.exp(s - m)
    l = jnp.sum(p, axis=-1, keepdims=True)
    o = jnp.einsum('nbhqk,nbkhd->nbqhd', p / l, vw.astype(jnp.float32))
    lse = (m + jnp.log(l))[..., 0].transpose(0, 1, 3, 2).reshape(n, lp, H)[:, :lc]
    o = o.reshape(n, lp, H, D)[:, :lc]
    o = o.reshape(B, r, lc, H, D).transpose(0, 2, 1, 3, 4).reshape(B, S, H, D)
    lse = lse.reshape(B, r, lc, H).transpose(0, 2, 1, 3).reshape(B, S, H)
    return o, lse


def dilated_group_sample(q, k_new, v_new, kv_buf, bias_tab, r):
    T = q.shape[1]
    lb = kv_buf.shape[1]
    kc = jnp.concatenate([kv_buf[:, :, 0].astype(k_new.dtype), k_new], axis=1)
    vc = jnp.concatenate([kv_buf[:, :, 1].astype(v_new.dtype), v_new], axis=1)
    j = jnp.arange(DIL_BLOCK + 1)
    idx = lb + jnp.arange(T)[:, None] - j[None, :] * r
    valid = idx >= 0
    idx = jnp.maximum(idx, 0)
    kg = kc[:, idx]
    vg = vc[:, idx]
    bias = bias_tab[t5_bucket(j * r)].astype(jnp.float32).T
    s = jnp.einsum('bthd,btjhd->bthj', q, kg).astype(jnp.float32) * ATTN_SCALE + bias
    s = jnp.where(valid[None, :, None, :], s, -jnp.inf)
    m = jnp.max(s, axis=-1, keepdims=True)
    p = jnp.exp(s - m)
    l = jnp.sum(p, axis=-1, keepdims=True)
    o = jnp.einsum('bthj,btjhd->bthd', p / l, vg.astype(jnp.float32))
    lse = (m + jnp.log(l))[..., 0]
    new_buf = jnp.stack([kc[:, kc.shape[1] - lb:], vc[:, vc.shape[1] - lb:]], axis=2)
    return o, lse, new_buf


def hier_moe(u, w_rg, b_rg, w_re, b_re, w_gate, w_up, w_down):
    uf = u.astype(jnp.float32)
    lg = uf @ w_rg.astype(jnp.float32) + b_rg.astype(jnp.float32)
    pg = jax.nn.softmax(lg, axis=-1)
    gsel = jax.nn.one_hot(jnp.argmax(lg, axis=-1), N_GROUPS, dtype=jnp.float32)
    w_coarse = jnp.max(pg, axis=-1, keepdims=True)
    le = jnp.einsum('nd,gde->nge', uf, w_re.astype(jnp.float32)) + b_re.astype(jnp.float32)
    le_sel = jnp.sum(le * gsel[:, :, None], axis=1)
    tv, ti = lax.top_k(le_sel, TOP_K_IN_GROUP)
    tw = jax.nn.softmax(tv, axis=-1) * w_coarse
    ew = jnp.einsum('nk,nke->ne', tw, jax.nn.one_hot(ti, EXPERTS_PER_GROUP, dtype=jnp.float32))
    gate = (gsel[:, :, None] * ew[:, None, :]).astype(u.dtype)
    y = jnp.zeros_like(u)
    for g in range(N_GROUPS):
        a = jnp.einsum('nd,edf->nef', u, w_gate[g])
        b = jnp.einsum('nd,edf->nef', u, w_up[g])
        hmid = jax.nn.silu(a) * b * gate[:, g, :, None]
        y = y + jnp.einsum('nef,efd->nd', hmid, w_down[g])
    return y


def layer(x, pos, ret_s0, kv_bufs, norm_attn, w_in, q_norm, k_norm, rel_bias, w_ret_out, w_dil_out,
          w_o, norm_ffn, w_rg, b_rg, w_re, b_re, w_gate, w_up, w_down):
    B, L, _ = x.shape
    u = rms_norm(x, norm_attn)
    q_r, k_r, v_r, g_r, q_a, k_a, v_a, gate_r, gate_a = split_proj(u @ w_in)

    rh = lambda t: t.reshape(B, L, RET_HEADS, HEAD_DIM).astype(jnp.float32)
    qr = rotary(rh(q_r), pos)
    kr = rotary(rh(k_r), pos) * (HEAD_DIM ** -0.5)
    o_r, s_new = retention(qr, kr, rh(v_r), ret_s0)
    y_r = (jax.nn.silu(g_r.astype(jnp.float32)) * head_group_norm(o_r).reshape(B, L, RET_WIDTH)).astype(x.dtype)
    y_r = y_r @ w_ret_out

    ah = lambda t: t.reshape(B, L, DIL_HEADS, HEAD_DIM)
    qa = rms_norm(ah(q_a), q_norm)
    ka = rms_norm(ah(k_a), k_norm)
    va = ah(v_a)
    outs, lses, bufs = [], [], []
    for gi, (w, r) in enumerate(DIL_PAIRS):
        hs = slice(gi * DIL_HEADS_PER_GROUP, (gi + 1) * DIL_HEADS_PER_GROUP)
        bias_tab = rel_bias[:, hs]
        if kv_bufs is None:
            o, lse = dilated_group_prompt(qa[:, :, hs], ka[:, :, hs], va[:, :, hs], bias_tab, r)
            lw = min(w, L)
            buf = jnp.stack([ka[:, L - lw:, hs], va[:, L - lw:, hs]], axis=2)
        else:
            o, lse, buf = dilated_group_sample(qa[:, :, hs], ka[:, :, hs], va[:, :, hs], kv_bufs[gi], bias_tab, r)
        outs.append(o)
        lses.append(lse)
        bufs.append(buf)
    alpha = jax.nn.softmax(jnp.stack(lses, axis=0), axis=0)
    o_a = jnp.sum(alpha[..., None] * jnp.stack(outs, axis=0), axis=0)
    y_a = o_a.reshape(B, L, DIL_OUT_WIDTH).astype(x.dtype) @ w_dil_out

    merged = jax.nn.sigmoid(gate_r) * y_r + jax.nn.sigmoid(gate_a) * y_a
    h = x + merged @ w_o

    u2 = rms_norm(h, norm_ffn).reshape(B * L, D_MODEL)
    y = h + hier_moe(u2, w_rg, b_rg, w_re, b_re, w_gate, w_up, w_down).reshape(B, L, D_MODEL)
    return y, s_new, bufs


def setup_inputs(seed: int = 0) -> dict:
    key = jax.random.key(seed)
    ks = jax.random.split(key, 24)
    nrm = lambda k, shape, s: jax.random.normal(k, shape, jnp.float32) * s
    lb = [min(w, PAST_LEN) for (w, _) in DIL_PAIRS]
    return {
        "x_prompt": nrm(ks[0], (BATCH, SEQ, D_MODEL), 1.0),
        "x_sample": nrm(ks[1], (DEC_BATCH, DEC_SEQ, D_MODEL), 1.0),
        "cache_kv_g0": nrm(ks[2], (DEC_BATCH, lb[0], 2, DIL_HEADS_PER_GROUP, HEAD_DIM), 1.0),
        "cache_kv_g1": nrm(ks[3], (DEC_BATCH, lb[1], 2, DIL_HEADS_PER_GROUP, HEAD_DIM), 1.0),
        "cache_kv_g2": nrm(ks[4], (DEC_BATCH, lb[2], 2, DIL_HEADS_PER_GROUP, HEAD_DIM), 1.0),
        "state_ret": nrm(ks[5], (DEC_BATCH, RET_HEADS, HEAD_DIM, HEAD_DIM), 1.0),
        "norm_attn": 1.0 + nrm(ks[6], (D_MODEL,), 0.02),
        "w_in": nrm(ks[7], (D_MODEL, IN_WIDTH), D_MODEL ** -0.5),
        "q_norm": 1.0 + nrm(ks[8], (HEAD_DIM,), 0.02),
        "k_norm": 1.0 + nrm(ks[9], (HEAD_DIM,), 0.02),
        "rel_bias": nrm(ks[10], (REL_BUCKETS, DIL_HEADS), 0.5),
        "w_ret_out": nrm(ks[11], (RET_WIDTH, D_MODEL), RET_WIDTH ** -0.5),
        "w_dil_out": nrm(ks[12], (DIL_OUT_WIDTH, D_MODEL), DIL_OUT_WIDTH ** -0.5),
        "w_o": nrm(ks[13], (D_MODEL, D_MODEL), D_MODEL ** -0.5),
        "norm_ffn": 1.0 + nrm(ks[14], (D_MODEL,), 0.02),
        "w_router_group": nrm(ks[15], (D_MODEL, N_GROUPS), D_MODEL ** -0.5),
        "b_router_group": nrm(ks[16], (N_GROUPS,), 0.01),
        "w_router_expert": nrm(ks[17], (N_GROUPS, D_MODEL, EXPERTS_PER_GROUP), D_MODEL ** -0.5),
        "b_router_expert": nrm(ks[18], (N_GROUPS, EXPERTS_PER_GROUP), 0.01),
        "w_gate": nrm(ks[19], (N_GROUPS, EXPERTS_PER_GROUP, D_MODEL, D_EXPERT), D_MODEL ** -0.5),
        "w_up": nrm(ks[20], (N_GROUPS, EXPERTS_PER_GROUP, D_MODEL, D_EXPERT), D_MODEL ** -0.5),
        "w_down": nrm(ks[21], (N_GROUPS, EXPERTS_PER_GROUP, D_EXPERT, D_MODEL), D_EXPERT ** -0.5),
    }


def reference(x_prompt, x_sample, cache_kv_g0, cache_kv_g1, cache_kv_g2, state_ret, norm_attn, w_in,
              q_norm, k_norm, rel_bias, w_ret_out, w_dil_out, w_o, norm_ffn, w_router_group,
              b_router_group, w_router_expert, b_router_expert, w_gate, w_up, w_down):
    weights = (norm_attn, w_in, q_norm, k_norm, rel_bias, w_ret_out, w_dil_out, w_o, norm_ffn,
               w_router_group, b_router_group, w_router_expert, b_router_expert, w_gate, w_up, w_down)
    h_p = x_prompt
    h_s = x_sample
    for _ in range(DEPTH):
        s0_p = jnp.zeros((h_p.shape[0], RET_HEADS, HEAD_DIM, HEAD_DIM), jnp.float32)
        h_p, ret_p, bufs_p = layer(h_p, jnp.arange(h_p.shape[1]), s0_p, None, *weights)
        h_s, ret_s, bufs_s = layer(h_s, PAST_LEN + jnp.arange(h_s.shape[1]), state_ret,
                                   (cache_kv_g0, cache_kv_g1, cache_kv_g2), *weights)
    return (h_p, h_s, bufs_p[0], bufs_p[1], bufs_p[2], ret_p, bufs_s[0], bufs_s[1], bufs_s[2], ret_s)
```

```python
import functools
import math

import numpy as np
import jax
import jax.numpy as jnp
from jax import lax
from jax.experimental import pallas as pl
from jax.experimental.pallas import tpu as pltpu

HEAD_DIM = 128
RET_HEADS = 8
RET_WIDTH = RET_HEADS * HEAD_DIM
ROPE_BASE = 10000.0
GN_EPS = 1e-5
DIL_PAIRS = ((128, 1), (512, 4), (2048, 16))
DIL_HPG = 4
DIL_HEADS = DIL_HPG * len(DIL_PAIRS)
DIL_WIDTH = DIL_HEADS * HEAD_DIM
DIL_OUT_WIDTH = DIL_HPG * HEAD_DIM
DIL_BLOCK = 128
ATTN_SCALE = HEAD_DIM ** -0.5
REL_BUCKETS = 32
REL_MAX_DIST = 2048
N_GROUPS = 4
EXPERTS_PER_GROUP = 8
N_EXPERTS = N_GROUPS * EXPERTS_PER_GROUP
TOP_K = 2
NORM_EPS = 1e-6
PAST_LEN = 16384

MOE_TILE = 256
MIB = 1 << 20
BF16 = jnp.bfloat16
F32 = jnp.float32


def _params(semantics, vmem_mib):
    return pltpu.CompilerParams(dimension_semantics=semantics, vmem_limit_bytes=vmem_mib * MIB)


def _rmsnorm_kernel(x_ref, g_ref, o_ref):
    x = x_ref[...]
    y = x * lax.rsqrt(jnp.mean(x * x, axis=-1, keepdims=True) + NORM_EPS)
    o_ref[...] = (y * g_ref[...]).astype(o_ref.dtype)


def _rmsnorm(x, g, tm):
    n, d = x.shape
    return pl.pallas_call(
        _rmsnorm_kernel,
        out_shape=jax.ShapeDtypeStruct((n, d), BF16),
        grid=(n // tm,),
        in_specs=[pl.BlockSpec((tm, d), lambda i: (i, 0)), pl.BlockSpec((1, d), lambda i: (0, 0))],
        out_specs=pl.BlockSpec((tm, d), lambda i: (i, 0)),
        compiler_params=_params(("parallel",), 40),
        name="rmsnorm",
    )(x, g.reshape(1, d))


def _proj_kernel(u_ref, w_ref, *refs, epilogue):
    acc = jnp.dot(u_ref[...], w_ref[...], preferred_element_type=F32)
    epilogue(acc, *refs)


def _proj(u, w, epilogue, extra, extra_specs, out_dtype, tm, tn, name):
    n, k = u.shape
    nw = w.shape[1]
    in_specs = [pl.BlockSpec((tm, k), lambda j, i: (i, 0)),
                pl.BlockSpec((k, tn), lambda j, i: (0, j))] + list(extra_specs)
    return pl.pallas_call(
        functools.partial(_proj_kernel, epilogue=epilogue),
        out_shape=jax.ShapeDtypeStruct((n, nw), out_dtype),
        grid=(nw // tn, n // tm),
        in_specs=in_specs,
        out_specs=pl.BlockSpec((tm, tn), lambda j, i: (i, j)),
        compiler_params=_params(("parallel", "parallel"), 48),
        name=name,
    )(u, w, *extra)


def _epi_rotary(acc, tab_ref, o_ref):
    c = tab_ref[0]
    s = tab_ref[1]
    for h in range(acc.shape[1] // HEAD_DIM):
        xh = acc[:, h * HEAD_DIM:(h + 1) * HEAD_DIM]
        o_ref[:, h * HEAD_DIM:(h + 1) * HEAD_DIM] = (
            xh * c + pltpu.roll(xh, HEAD_DIM // 2, 1) * s).astype(o_ref.dtype)


def _epi_value_gate(acc, o_ref):
    j = pl.program_id(0)

    @pl.when(j == 0)
    def _():
        o_ref[...] = acc.astype(o_ref.dtype)

    @pl.when(j == 1)
    def _():
        o_ref[...] = (acc * jax.nn.sigmoid(acc)).astype(o_ref.dtype)


def _epi_head_norm(acc, g_ref, o_ref, *, scale):
    g = g_ref[...]
    for h in range(acc.shape[1] // HEAD_DIM):
        xh = acc[:, h * HEAD_DIM:(h + 1) * HEAD_DIM]
        y = xh * lax.rsqrt(jnp.mean(xh * xh, axis=-1, keepdims=True) + NORM_EPS) * g
        if scale != 1.0:
            y = y * scale
        o_ref[:, h * HEAD_DIM:(h + 1) * HEAD_DIM] = y.astype(o_ref.dtype)


def _epi_plain(acc, o_ref):
    o_ref[...] = acc.astype(o_ref.dtype)


def _epi_sigmoid(acc, o_ref):
    o_ref[...] = jax.nn.sigmoid(acc).astype(o_ref.dtype)


def _rotary_tables(pos, rows):
    half = HEAD_DIM // 2
    inv = ROPE_BASE ** (-jnp.arange(half, dtype=F32) / half)
    ang = pos.astype(F32)[:, None] * inv[None, :]
    cos = jnp.cos(ang)
    sin = jnp.sin(ang)
    c = jnp.concatenate([cos, cos], axis=-1)
    s = jnp.concatenate([-sin, sin], axis=-1)
    tab = jnp.stack([jnp.stack([c, s]), jnp.stack([c, s]) * (HEAD_DIM ** -0.5)])
    reps = rows // pos.shape[0]
    return jnp.tile(tab, (1, 1, reps, 1))


def _input_projections(u, w_in_b, pos_tab, q_norm, k_norm, tm):
    n = u.shape[0]
    tab_rows = pos_tab.shape[2]
    nblk = tab_rows // tm
    o = 0
    qk_r = _proj(u, w_in_b[:, o:o + 2 * RET_WIDTH], _epi_rotary, [pos_tab],
                 [pl.BlockSpec((None, 2, tm, HEAD_DIM), lambda j, i: (j, 0, i % nblk, 0))],
                 BF16, tm, RET_WIDTH, "proj_qk_ret")
    o += 2 * RET_WIDTH
    vg_r = _proj(u, w_in_b[:, o:o + 2 * RET_WIDTH], _epi_value_gate, [], [], BF16, tm, RET_WIDTH,
                 "proj_vg_ret")
    o += 2 * RET_WIDTH
    gspec = [pl.BlockSpec((1, HEAD_DIM), lambda j, i: (0, 0))]
    qa = _proj(u, w_in_b[:, o:o + DIL_WIDTH], functools.partial(_epi_head_norm, scale=ATTN_SCALE),
               [q_norm.reshape(1, HEAD_DIM)], gspec, BF16, tm, DIL_WIDTH, "proj_q_dil")
    o += DIL_WIDTH
    ka = _proj(u, w_in_b[:, o:o + DIL_WIDTH], functools.partial(_epi_head_norm, scale=1.0),
               [k_norm.reshape(1, HEAD_DIM)], gspec, F32, tm, DIL_WIDTH, "proj_k_dil")
    o += DIL_WIDTH
    va = _proj(u, w_in_b[:, o:o + DIL_WIDTH], _epi_plain, [], [], F32, tm, DIL_WIDTH, "proj_v_dil")
    o += DIL_WIDTH
    d2 = w_in_b.shape[1] - o
    gates = _proj(u, w_in_b[:, o:], _epi_sigmoid, [], [], BF16, tm, d2 // 2, "proj_gates")
    return qk_r, vg_r, qa, ka, va, gates


def _retention_kernel(q_ref, k_ref, v_ref, g_ref, s0_ref, dm_ref, qd_ref, kd_ref, cd_ref,
                      y_ref, sf_ref, state, *, chunk, nchunks):
    t = pl.program_id(2)

    @pl.when(t == 0)
    def _():
        state[...] = s0_ref[...]

    dmask = dm_ref[...]
    qdec = qd_ref[...]
    kdec = kd_ref[...]
    cdec = cd_ref[...]
    for ci in range(nchunks):
        rows = pl.ds(ci * chunk, chunk)
        q = q_ref[rows, :]
        k = k_ref[rows, :]
        v = v_ref[rows, :]
        s_prev = state[...]
        sc = lax.dot_general(q, k, (((1,), (1,)), ((), ())), preferred_element_type=F32) * dmask
        intra = jnp.dot(sc.astype(BF16), v, preferred_element_type=F32)
        qd = (q.astype(F32) * qdec).astype(BF16)
        cross = jnp.dot(qd, s_prev.astype(BF16), preferred_element_type=F32)
        kd = (k.astype(F32) * kdec).astype(BF16)
        kv = lax.dot_general(kd, v, (((0,), (0,)), ((), ())), preferred_element_type=F32)
        state[...] = s_prev * cdec + kv
        o = intra + cross
        mu = jnp.mean(o, axis=-1, keepdims=True)
        oc = o - mu
        var = jnp.mean(oc * oc, axis=-1, keepdims=True)
        y = g_ref[rows, :].astype(F32) * (oc * lax.rsqrt(var + GN_EPS))
        y_ref[rows, :] = y.astype(y_ref.dtype)

    @pl.when(t == pl.num_programs(2) - 1)
    def _():
        sf_ref[...] = state[...]


def _retention_decay(chunk, valid):
    lg = jnp.log1p(-jnp.exp2(-5.0 - jnp.arange(RET_HEADS, dtype=F32)))
    idx = jnp.arange(chunk, dtype=F32)
    rel = idx[:, None] - idx[None, :]
    dmask = jnp.where(rel[None] >= 0, jnp.exp(lg[:, None, None] * jnp.maximum(rel, 0.0)[None]), 0.0)
    qdec = jnp.exp(lg[:, None] * (idx + 1.0)[None, :])[..., None]
    kdec = jnp.exp(lg[:, None] * (valid - 1.0 - idx)[None, :])[..., None]
    kdec = jnp.where((idx < valid)[None, :, None], kdec, 0.0)
    cdec = jnp.exp(lg * valid)[:, None, None]
    return dmask, qdec, kdec, cdec


def _retention_prompt(qk_r, vg_r, s0, chunk, rows_per_step):
    b, l, _ = qk_r.shape
    h = RET_HEADS
    dmask, qdec, kdec, cdec = _retention_decay(chunk, chunk)
    nsteps = l // rows_per_step
    blk = lambda off: pl.BlockSpec((None, rows_per_step, HEAD_DIM), lambda bi, hi, ti: (bi, ti, hi + off))
    per_head = lambda shape: pl.BlockSpec((None,) + shape, lambda bi, hi, ti: (hi,) + (0,) * len(shape))
    state_spec = pl.BlockSpec((None, None, HEAD_DIM, HEAD_DIM), lambda bi, hi, ti: (bi, hi, 0, 0))
    return pl.pallas_call(
        functools.partial(_retention_kernel, chunk=chunk, nchunks=rows_per_step // chunk),
        out_shape=(jax.ShapeDtypeStruct((b, l, RET_WIDTH), BF16),
                   jax.ShapeDtypeStruct((b, h, HEAD_DIM, HEAD_DIM), F32)),
        grid=(b, h, nsteps),
        in_specs=[blk(0), blk(h), blk(0), blk(h), state_spec,
                  per_head((chunk, chunk)), per_head((chunk, 1)), per_head((chunk, 1)), per_head((1, 1))],
        out_specs=(blk(0), state_spec),
        scratch_shapes=[pltpu.VMEM((HEAD_DIM, HEAD_DIM), F32)],
        compiler_params=_params(("parallel", "parallel", "arbitrary"), 32),
        name="retention_prompt",
    )(qk_r, qk_r, vg_r, vg_r, s0, dmask, qdec, kdec, cdec)


def _retention_sample_kernel(q_ref, k_ref, v_ref, g_ref, s0_ref, dm_ref, qd_ref, kd_ref, cd_ref,
                             y_ref, sf_ref, *, ntok):
    for h in range(RET_HEADS):
        q = q_ref[h]
        k = k_ref[h]
        v = v_ref[h]
        s_prev = s0_ref[h]
        dmask = dm_ref[h]
        o = jnp.dot(q * qd_ref[h], s_prev, preferred_element_type=F32)
        for j in range(ntok):
            sj = jnp.sum(q * k[j:j + 1, :], axis=-1, keepdims=True) * dmask[:, j:j + 1]
            o = o + sj * v[j:j + 1, :]
        kd = k * kd_ref[h]
        kv = lax.dot_general(kd, v, (((0,), (0,)), ((), ())), preferred_element_type=F32)
        sf_ref[h] = s_prev * cd_ref[h] + kv
        mu = jnp.mean(o, axis=-1, keepdims=True)
        oc = o - mu
        var = jnp.mean(oc * oc, axis=-1, keepdims=True)
        y_ref[h] = g_ref[h] * (oc * lax.rsqrt(var + GN_EPS))


def _retention_sample(q, k, v, g, s0, ntok):
    bd, h, tp, _ = q.shape
    dmask, qdec, kdec, cdec = _retention_decay(tp, ntok)
    tok = pl.BlockSpec((None, h, tp, HEAD_DIM), lambda bi: (bi, 0, 0, 0))
    st = pl.BlockSpec((None, h, HEAD_DIM, HEAD_DIM), lambda bi: (bi, 0, 0, 0))
    const = lambda a: pl.BlockSpec(a.shape, lambda bi: (0,) * a.ndim)
    return pl.pallas_call(
        functools.partial(_retention_sample_kernel, ntok=ntok),
        out_shape=(jax.ShapeDtypeStruct((bd, h, tp, HEAD_DIM), F32),
                   jax.ShapeDtypeStruct((bd, h, HEAD_DIM, HEAD_DIM), F32)),
        grid=(bd,),
        in_specs=[tok, tok, tok, tok, st, const(dmask), const(qdec), const(kdec), const(cdec)],
        out_specs=(tok, st),
        compiler_params=_params(("parallel",), 32),
        name="retention_sample",
    )(q, k, v, g, s0, dmask, qdec, kdec, cdec)


def _t5_bucket(dist):
    max_exact = REL_BUCKETS // 2
    d = jnp.maximum(dist, 0)
    df = jnp.maximum(d, 1).astype(F32)
    large = max_exact + (jnp.log(df / max_exact) / math.log(REL_MAX_DIST / max_exact)
                         * (REL_BUCKETS - max_exact)).astype(jnp.int32)
    large = jnp.minimum(large, REL_BUCKETS - 1)
    return jnp.where(d < max_exact, d, large)


def _dilated_kernel(q_ref, kp_ref, kc_ref, vp_ref, vc_ref, bias_ref, o_ref, lse_ref, kfull, vfull, *, nsub):
    i = pl.program_id(1)
    blk = DIL_BLOCK
    kfull[0:blk, :] = kp_ref[...]
    kfull[blk:, :] = kc_ref[...]
    vfull[0:blk, :] = vp_ref[...]
    vfull[blk:, :] = vc_ref[...]
    col = lax.broadcasted_iota(jnp.int32, (blk, 2 * blk), 1)
    for s in range(nsub):
        rows = pl.ds(s * blk, blk)
        win = pl.ds(s * blk, 2 * blk)
        for h in range(DIL_HPG):
            cols = pl.ds(h * HEAD_DIM, HEAD_DIM)
            q = q_ref[rows, cols]
            kw = kfull[win, cols]
            vw = vfull[win, cols]
            sc = lax.dot_general(q, kw, (((1,), (1,)), ((), ())), preferred_element_type=F32) + bias_ref[h]
            if s == 0:
                sc = jnp.where((col >= blk) | (i > 0), sc, -jnp.inf)
            m = jnp.max(sc, axis=-1, keepdims=True)
            p = jnp.exp(sc - m)
            l = jnp.sum(p, axis=-1, keepdims=True)
            o = jnp.dot(p.astype(BF16), vw, preferred_element_type=F32) / l
            o_ref[rows, cols] = o.astype(o_ref.dtype)
            lse_ref[rows, h:h + 1] = m + jnp.log(l)


def _dilated_bias(bias_tab, r):
    blk = DIL_BLOCK
    qi = jnp.arange(blk)[:, None]
    kj = jnp.arange(2 * blk)[None, :]
    dc = blk + qi - kj
    band = (dc >= 0) & (dc <= blk)
    bias = bias_tab[_t5_bucket(dc * r)].astype(F32).transpose(2, 0, 1)
    return jnp.where(band[None], bias, -jnp.inf)


def _dilated_prompt(q, k, v, bias, nsub):
    n, lc, w = q.shape
    blk = DIL_BLOCK
    tq = nsub * blk
    cur = pl.BlockSpec((None, tq, w), lambda ni, i: (ni, i, 0))
    prev = pl.BlockSpec((None, blk, w), lambda ni, i: (ni, jnp.maximum(i * nsub - 1, 0), 0))
    return pl.pallas_call(
        functools.partial(_dilated_kernel, nsub=nsub),
        out_shape=(jax.ShapeDtypeStruct((n, lc, w), F32),
                   jax.ShapeDtypeStruct((n, lc, DIL_HPG), F32)),
        grid=(n, lc // tq),
        in_specs=[cur, prev, cur, prev, cur, pl.BlockSpec(bias.shape, lambda ni, i: (0, 0, 0))],
        out_specs=(cur, pl.BlockSpec((None, tq, DIL_HPG), lambda ni, i: (ni, i, 0))),
        scratch_shapes=[pltpu.VMEM((tq + blk, w), BF16), pltpu.VMEM((tq + blk, w), BF16)],
        compiler_params=_params(("parallel", "parallel"), 32),
        name="dilated_prompt",
    )(q, k, k, v, v, bias)


def _dilated_sample_kernel(q_ref, kn_ref, vn_ref, c0_ref, c1_ref, c2_ref, bc_ref, bn_ref,
                           o_ref, lse_ref, *, ntok):
    caches = (c0_ref, c1_ref, c2_ref)
    for gi in range(len(DIL_PAIRS)):
        cache = caches[gi]
        nclass = cache.shape[1]
        for t in range(ntok):
            cls = t if nclass > 1 else 0
            qt = q_ref[t, gi]
            kc = cache[:, cls, 0]
            vc = cache[:, cls, 1]
            sc = jnp.sum(kc * qt[None], axis=-1, keepdims=True) + bc_ref[gi, t]
            sn = jnp.sum(kn_ref[:, gi] * qt[None], axis=-1, keepdims=True) + bn_ref[gi, t]
            m = jnp.maximum(jnp.max(sc, axis=0), jnp.max(sn, axis=0))
            pc = jnp.exp(sc - m[None])
            pn = jnp.exp(sn - m[None])
            l = jnp.sum(pc, axis=0) + jnp.sum(pn, axis=0)
            o = jnp.sum(pc * vc, axis=0) + jnp.sum(pn * vn_ref[:, gi], axis=0)
            o_ref[t, gi] = o / l
            lse_ref[t, gi] = m + jnp.log(l)


def _dilated_sample(q, kn, vn, caches, bias_cache, bias_new, ntok):
    bd = q.shape[0]
    tok = pl.BlockSpec((None, ntok, len(DIL_PAIRS), DIL_HPG, HEAD_DIM), lambda bi: (bi, 0, 0, 0, 0))
    cspecs = []
    for c in caches:
        ncls = min(c.shape[2], ntok)
        cspecs.append(pl.BlockSpec((None, DIL_BLOCK, ncls, 2, DIL_HPG, HEAD_DIM),
                                   lambda bi: (bi, 0, 0, 0, 0, 0)))
    const = lambda a: pl.BlockSpec(a.shape, lambda bi: (0,) * a.ndim)
    return pl.pallas_call(
        functools.partial(_dilated_sample_kernel, ntok=ntok),
        out_shape=(jax.ShapeDtypeStruct(q.shape, F32),
                   jax.ShapeDtypeStruct((bd, ntok, len(DIL_PAIRS), DIL_HPG, 1), F32)),
        grid=(bd,),
        in_specs=[tok, tok, tok] + cspecs + [const(bias_cache), const(bias_new)],
        out_specs=(tok, pl.BlockSpec((None, ntok, len(DIL_PAIRS), DIL_HPG, 1), lambda bi: (bi, 0, 0, 0, 0))),
        compiler_params=_params(("parallel",), 32),
        name="dilated_sample",
    )(q, kn, vn, *caches, bias_cache, bias_new)


def _dilated_sample_bias(rel_bias, ntok):
    bc, bn = [], []
    m = jnp.arange(DIL_BLOCK)
    tn = jnp.arange(ntok)
    for gi, (_, r) in enumerate(DIL_PAIRS):
        tab = rel_bias[:, gi * DIL_HPG:(gi + 1) * DIL_HPG].astype(F32)
        rows_c, rows_n = [], []
        for t in range(ntok):
            if r == 1:
                jc = DIL_BLOCK + t - m
                okc = m >= t
                jn = t - tn
                okn = tn <= t
            else:
                jc = DIL_BLOCK - m
                okc = jnp.ones_like(m, bool)
                jn = jnp.zeros_like(tn)
                okn = tn == t
            rows_c.append(jnp.where(okc[:, None], tab[_t5_bucket(jc * r)], -jnp.inf))
            rows_n.append(jnp.where(okn[:, None], tab[_t5_bucket(jn * r)], -jnp.inf))
        bc.append(jnp.stack(rows_c))
        bn.append(jnp.stack(rows_n))
    return jnp.stack(bc)[..., None], jnp.stack(bn)[..., None]


def _cache_shift_kernel(*refs, ntok, nsplit):
    ngrp = len(DIL_PAIRS)
    caches = refs[:ngrp]
    news = refs[ngrp:2 * ngrp]
    outs = refs[2 * ngrp:3 * ngrp]
    sem = refs[3 * ngrp]
    copies = []
    for gi in range(ngrp):
        bd, lb = caches[gi].shape[:2]
        per = bd // nsplit
        for sp in range(nsplit):
            bs = pl.ds(sp * per, per)
            copies.append(pltpu.make_async_copy(caches[gi].at[bs, pl.ds(ntok, lb - ntok)],
                                                outs[gi].at[bs, pl.ds(0, lb - ntok)],
                                                sem.at[len(copies)]))
        copies.append(pltpu.make_async_copy(news[gi], outs[gi].at[:, pl.ds(lb - ntok, ntok)],
                                            sem.at[len(copies)]))
    for c in copies:
        c.start()
    for c in copies:
        c.wait()


def _cache_shift(caches, news, ntok):
    nsplit = math.gcd(4, caches[0].shape[0])
    ncopies = len(caches) * (nsplit + 1)
    any_spec = pl.BlockSpec(memory_space=pl.ANY)
    return pl.pallas_call(
        functools.partial(_cache_shift_kernel, ntok=ntok, nsplit=nsplit),
        out_shape=tuple(jax.ShapeDtypeStruct(c.shape, c.dtype) for c in caches),
        in_specs=[any_spec] * (2 * len(caches)),
        out_specs=tuple([any_spec] * len(caches)),
        scratch_shapes=[pltpu.SemaphoreType.DMA((ncopies,))],
        name="cache_shift",
    )(*caches, *news)


def _merge_kernel(o0_ref, o1_ref, o2_ref, l0_ref, l1_ref, l2_ref, yr_ref, gt_ref, wr_ref, wd_ref,
                  out_ref, oa_ref):
    ls = (l0_ref[...], l1_ref[...], l2_ref[...])
    os_ = (o0_ref, o1_ref, o2_ref)
    m = jnp.maximum(jnp.maximum(ls[0], ls[1]), ls[2])
    es = [jnp.exp(l - m) for l in ls]
    den = es[0] + es[1] + es[2]
    for h in range(DIL_HPG):
        cols = pl.ds(h * HEAD_DIM, HEAD_DIM)
        acc = None
        for g in range(3):
            term = (es[g][:, h:h + 1] / den[:, h:h + 1]) * os_[g][:, cols]
            acc = term if acc is None else acc + term
        oa_ref[:, cols] = acc.astype(oa_ref.dtype)
    ya = jnp.dot(oa_ref[...], wd_ref[...], preferred_element_type=F32)
    yr = jnp.dot(yr_ref[...], wr_ref[...], preferred_element_type=F32)
    d = ya.shape[1]
    out_ref[...] = (gt_ref[:, :d].astype(F32) * yr + gt_ref[:, d:].astype(F32) * ya).astype(out_ref.dtype)


def _merge(o_g, lse_g, yr_in, gates, w_ret_b, w_dil_b, tm):
    n = yr_in.shape[0]
    d = w_ret_b.shape[1]
    row = lambda w: pl.BlockSpec((tm, w), lambda i: (i, 0))
    const = lambda a: pl.BlockSpec(a.shape, lambda i: (0, 0))
    return pl.pallas_call(
        _merge_kernel,
        out_shape=jax.ShapeDtypeStruct((n, d), BF16),
        grid=(n // tm,),
        in_specs=[row(DIL_OUT_WIDTH)] * 3 + [row(DIL_HPG)] * 3 + [row(RET_WIDTH), row(2 * d),
                                                                   const(w_ret_b), const(w_dil_b)],
        out_specs=row(d),
        scratch_shapes=[pltpu.VMEM((tm, DIL_OUT_WIDTH), BF16)],
        compiler_params=_params(("parallel",), 48),
        name="merge_branches",
    )(*o_g, *lse_g, yr_in, gates, w_ret_b, w_dil_b)


ROUTER_ROWS = 8 + N_EXPERTS


def _outproj_router_kernel(mg_ref, x_ref, wo_ref, g2_ref, wr_ref, br_ref, h_ref, u2_ref, ei_ref, wt_ref):
    h = x_ref[...] + jnp.dot(mg_ref[...], wo_ref[...], preferred_element_type=F32)
    h_ref[...] = h
    u2 = h * lax.rsqrt(jnp.mean(h * h, axis=-1, keepdims=True) + NORM_EPS) * g2_ref[...]
    u2_ref[...] = u2
    lt = lax.dot_general(wr_ref[...], u2.astype(BF16), (((1,), (1,)), ((), ())),
                         preferred_element_type=F32) + br_ref[...]
    lg = lt[0:N_GROUPS]
    gmax = jnp.max(lg, axis=0, keepdims=True)
    w_coarse = 1.0 / jnp.sum(jnp.exp(lg - gmax), axis=0, keepdims=True)
    gid = lax.broadcasted_iota(jnp.int32, lg.shape, 0)
    gsel = jnp.min(jnp.where(lg == gmax, gid, N_GROUPS), axis=0, keepdims=True)
    le = jnp.zeros((EXPERTS_PER_GROUP, lt.shape[1]), F32)
    for g in range(N_GROUPS):
        le = jnp.where(gsel == g, lt[8 + g * EXPERTS_PER_GROUP:8 + (g + 1) * EXPERTS_PER_GROUP], le)
    eid = lax.broadcasted_iota(jnp.int32, le.shape, 0)
    v1 = jnp.max(le, axis=0, keepdims=True)
    i1 = jnp.min(jnp.where(le == v1, eid, EXPERTS_PER_GROUP), axis=0, keepdims=True)
    le2 = jnp.where(eid == i1, -jnp.inf, le)
    v2 = jnp.max(le2, axis=0, keepdims=True)
    i2 = jnp.min(jnp.where(le2 == v2, eid, EXPERTS_PER_GROUP), axis=0, keepdims=True)
    e21 = jnp.exp(v2 - v1)
    w1 = w_coarse / (1.0 + e21)
    w2 = w_coarse * e21 / (1.0 + e21)
    row = lax.broadcasted_iota(jnp.int32, ei_ref.shape, 0)
    base = gsel * EXPERTS_PER_GROUP
    ei_ref[...] = jnp.where(row == 0, base + i1, jnp.where(row == 1, base + i2, 0))
    wt_ref[...] = jnp.where(row == 0, w1, jnp.where(row == 1, w2, 0.0))


def _outproj_router(merged, x, w_o_b, norm_ffn, wr_t, br_t, tm):
    n, d = x.shape
    row = lambda: pl.BlockSpec((tm, d), lambda i: (i, 0))
    lane = lambda: pl.BlockSpec((8, tm), lambda i: (0, i))
    const = lambda a: pl.BlockSpec(a.shape, lambda i: (0, 0))
    return pl.pallas_call(
        _outproj_router_kernel,
        out_shape=(jax.ShapeDtypeStruct((n, d), F32), jax.ShapeDtypeStruct((n, d), F32),
                   jax.ShapeDtypeStruct((8, n), jnp.int32), jax.ShapeDtypeStruct((8, n), F32)),
        grid=(n // tm,),
        in_specs=[row(), row(), const(w_o_b), pl.BlockSpec((1, d), lambda i: (0, 0)),
                  const(wr_t), const(br_t)],
        out_specs=(row(), row(), lane(), lane()),
        compiler_params=_params(("parallel",), 56),
        name="outproj_router",
    )(merged, x, w_o_b, norm_ffn.reshape(1, d), wr_t, br_t)


def _router_weights(w_rg, b_rg, w_re, b_re):
    d = w_rg.shape[0]
    wr = jnp.zeros((ROUTER_ROWS, d), F32)
    wr = wr.at[0:N_GROUPS].set(w_rg.T)
    wr = wr.at[8:].set(w_re.transpose(0, 2, 1).reshape(N_EXPERTS, d))
    br = jnp.zeros((ROUTER_ROWS, 1), F32)
    br = br.at[0:N_GROUPS, 0].set(b_rg)
    br = br.at[8:, 0].set(b_re.reshape(N_EXPERTS))
    return wr.astype(BF16), br


def _dispatch_kernel(pos_ref, u_ref, xs_in_ref, xs_ref, sem, *, tm, ntok_total):
    del xs_in_ref
    i = pl.program_id(0)

    def row_copy(r, kk):
        slot = pos_ref[kk * ntok_total + i * tm + r]
        return pltpu.make_async_copy(u_ref.at[pl.ds(r, 1)], xs_ref.at[pl.ds(slot, 1)], sem)

    def issue(r, c):
        row_copy(r, 0).start()
        row_copy(r, 1).start()
        return c

    lax.fori_loop(0, tm, issue, 0)

    def drain(r, c):
        row_copy(r, 0).wait()
        row_copy(r, 1).wait()
        return c

    lax.fori_loop(0, tm, drain, 0)


def _dispatch(pos_flat, u2, xs, tm):
    n, d = u2.shape
    grid_spec = pltpu.PrefetchScalarGridSpec(
        num_scalar_prefetch=1,
        grid=(n // tm,),
        in_specs=[pl.BlockSpec((tm, d), lambda i, pos: (i, 0)), pl.BlockSpec(memory_space=pl.ANY)],
        out_specs=pl.BlockSpec(memory_space=pl.ANY),
        scratch_shapes=[pltpu.SemaphoreType.DMA(())],
    )
    return pl.pallas_call(
        functools.partial(_dispatch_kernel, tm=tm, ntok_total=n),
        out_shape=jax.ShapeDtypeStruct(xs.shape, xs.dtype),
        grid_spec=grid_spec,
        input_output_aliases={2: 0},
        compiler_params=_params(("arbitrary",), 32),
        name="moe_dispatch",
    )(pos_flat, u2, xs)


def _gmm_kernel(te_ref, nt_ref, x_ref, wg_ref, wu_ref, wd_ref, o_ref, wgb, wub, wdb):
    i = pl.program_id(0)
    fresh = jnp.logical_or(i == 0, te_ref[i] != te_ref[jnp.maximum(i - 1, 0)])

    @pl.when(jnp.logical_and(fresh, i < nt_ref[0]))
    def _():
        wgb[...] = wg_ref[...].astype(BF16)
        wub[...] = wu_ref[...].astype(BF16)
        wdb[...] = wd_ref[...].astype(BF16)

    @pl.when(i < nt_ref[0])
    def _():
        x = x_ref[...].astype(BF16)
        a = jnp.dot(x, wgb[...], preferred_element_type=F32)
        b = jnp.dot(x, wub[...], preferred_element_type=F32)
        hm = (a * jax.nn.sigmoid(a) * b).astype(BF16)
        o_ref[...] = jnp.dot(hm, wdb[...], preferred_element_type=F32)

    @pl.when(i >= nt_ref[0])
    def _():
        o_ref[...] = jnp.zeros_like(o_ref)


def _gmm(tile_expert, ntiles_used, xs, w_gate, w_up, w_down):
    npad, d = xs.shape
    f = w_gate.shape[-1]
    ntiles = npad // MOE_TILE
    epg = w_gate.shape[1]

    def xmap(i, te, nt):
        return (jnp.minimum(i, nt[0] - 1), 0)

    def wmap(i, te, nt):
        e = te[i]
        return (e // epg, e % epg, 0, 0)

    grid_spec = pltpu.PrefetchScalarGridSpec(
        num_scalar_prefetch=2,
        grid=(ntiles,),
        in_specs=[pl.BlockSpec((MOE_TILE, d), xmap),
                  pl.BlockSpec((None, None, d, f), wmap),
                  pl.BlockSpec((None, None, d, f), wmap),
                  pl.BlockSpec((None, None, f, d), wmap)],
        out_specs=pl.BlockSpec((MOE_TILE, d), lambda i, te, nt: (i, 0)),
        scratch_shapes=[pltpu.VMEM((d, f), BF16), pltpu.VMEM((d, f), BF16), pltpu.VMEM((f, d), BF16)],
    )
    return pl.pallas_call(
        _gmm_kernel,
        out_shape=jax.ShapeDtypeStruct((npad, d), F32),
        grid_spec=grid_spec,
        compiler_params=_params(("arbitrary",), 56),
        name="moe_grouped_matmul",
    )(tile_expert, ntiles_used, xs, w_gate, w_up, w_down)


def _combine_kernel(pos_ref, h_ref, wt_ref, os_ref, y_ref, g0, g1, sem, *, tm, ntok_total, tok_offset):
    i = pl.program_id(0)

    def row_copy(r, kk, buf):
        slot = pos_ref[kk * ntok_total + tok_offset + i * tm + r]
        return pltpu.make_async_copy(os_ref.at[pl.ds(slot, 1)], buf.at[pl.ds(r, 1)], sem)

    def issue(r, c):
        row_copy(r, 0, g0).start()
        row_copy(r, 1, g1).start()
        return c

    lax.fori_loop(0, tm, issue, 0)

    def drain(r, c):
        row_copy(r, 0, g0).wait()
        row_copy(r, 1, g1).wait()
        return c

    lax.fori_loop(0, tm, drain, 0)
    y_ref[...] = h_ref[...] + wt_ref[:, 0:1] * g0[...] + wt_ref[:, 1:2] * g1[...]


def _combine(pos_flat, h, wt_cols, out_sorted, tm, ntok_total, tok_offset):
    n, d = h.shape
    grid_spec = pltpu.PrefetchScalarGridSpec(
        num_scalar_prefetch=1,
        grid=(n // tm,),
        in_specs=[pl.BlockSpec((tm, d), lambda i, pos: (i, 0)),
                  pl.BlockSpec((tm, 8), lambda i, pos: (i, 0)),
                  pl.BlockSpec(memory_space=pl.ANY)],
        out_specs=pl.BlockSpec((tm, d), lambda i, pos: (i, 0)),
        scratch_shapes=[pltpu.VMEM((tm, d), F32), pltpu.VMEM((tm, d), F32), pltpu.SemaphoreType.DMA(())],
    )
    return pl.pallas_call(
        functools.partial(_combine_kernel, tm=tm, ntok_total=ntok_total, tok_offset=tok_offset),
        out_shape=jax.ShapeDtypeStruct((n, d), F32),
        grid_spec=grid_spec,
        compiler_params=_params(("arbitrary",), 32),
        name="moe_combine",
    )(pos_flat, h, wt_cols, out_sorted)


def _moe_plan(experts):
    n = experts.shape[1]
    flat = experts.reshape(-1)
    onehot = (flat[:, None] == jnp.arange(N_EXPERTS)[None, :]).astype(jnp.int32)
    csum = jnp.cumsum(onehot, axis=0)
    rank = jnp.sum((csum - onehot) * onehot, axis=1)
    counts = csum[-1]
    tiles = (counts + MOE_TILE - 1) // MOE_TILE
    tile_end = jnp.cumsum(tiles)
    offs = (tile_end - tiles) * MOE_TILE
    pos = offs[flat] + rank
    ntiles = (TOP_K * n + N_EXPERTS * (MOE_TILE - 1)) // MOE_TILE
    tile_expert = jnp.searchsorted(tile_end, jnp.arange(ntiles), side="right").astype(jnp.int32)
    tile_expert = jnp.minimum(tile_expert, N_EXPERTS - 1)
    return pos.astype(jnp.int32), tile_expert, tile_end[-1:].astype(jnp.int32), ntiles


def _to_classes(t, r):
    b, s, w = t.shape
    return t.reshape(b, s // r, r, w).transpose(0, 2, 1, 3).reshape(b * r, s // r, w)


def _from_classes(t, b, r):
    n, lc, w = t.shape
    return t.reshape(b, r, lc, w).transpose(0, 2, 1, 3).reshape(b, lc * r, w)


def _mixers_prompt(x, wts):
    b, l, d = x.shape
    n = b * l
    xf = x.reshape(n, d)
    u = _rmsnorm(xf, wts["norm_attn"], 512)
    tab = _rotary_tables(jnp.arange(l), l)
    qk_r, vg_r, qa, ka, va, gates = _input_projections(u, wts["w_in"], tab, wts["q_norm"], wts["k_norm"], 512)

    s0 = jnp.zeros((b, RET_HEADS, HEAD_DIM, HEAD_DIM), F32)
    yr_in, s_fin = _retention_prompt(qk_r.reshape(b, l, -1), vg_r.reshape(b, l, -1), s0, 256, min(l, 1024))

    qa3, ka3, va3 = (t.reshape(b, l, DIL_WIDTH) for t in (qa, ka, va))
    o_g, lse_g, bufs = [], [], []
    for gi, (w, r) in enumerate(DIL_PAIRS):
        cols = slice(gi * DIL_OUT_WIDTH, (gi + 1) * DIL_OUT_WIDTH)
        qg = _to_classes(qa3[..., cols], r)
        kg = _to_classes(ka3[..., cols].astype(BF16), r)
        vg = _to_classes(va3[..., cols].astype(BF16), r)
        bias = _dilated_bias(wts["rel_bias"][:, gi * DIL_HPG:(gi + 1) * DIL_HPG], r)
        o, lse = _dilated_prompt(qg, kg, vg, bias, min(4, l // r // DIL_BLOCK))
        o_g.append(_from_classes(o, b, r).reshape(n, DIL_OUT_WIDTH))
        lse_g.append(_from_classes(lse, b, r).reshape(n, DIL_HPG))
        lw = min(w, l)
        kv = jnp.stack([ka3[:, l - lw:, cols], va3[:, l - lw:, cols]], axis=2)
        bufs.append(kv.reshape(b, lw, 2, DIL_HPG, HEAD_DIM))

    merged = _merge(o_g, lse_g, yr_in.reshape(n, RET_WIDTH), gates, wts["w_ret_out"], wts["w_dil_out"], 512)
    h, u2, ei, wt = _outproj_router(merged, xf, wts["w_o"], wts["norm_ffn"], wts["wr_t"], wts["br_t"], 256)
    return h, u2, ei, wt, s_fin, bufs


def _mixers_sample(x, caches, state, past_len, wts):
    bd, t, d = x.shape
    n = bd * t
    xf = x.reshape(n, d)
    u = _rmsnorm(xf, wts["norm_attn"], n)
    tab = _rotary_tables(past_len + jnp.arange(t), n)
    qk_r, vg_r, qa, ka, va, gates = _input_projections(u, wts["w_in"], tab, wts["q_norm"], wts["k_norm"], n)

    def heads(a):
        a = a.astype(F32).reshape(bd, t, RET_HEADS, HEAD_DIM).transpose(0, 2, 1, 3)
        return jnp.pad(a, ((0, 0), (0, 0), (0, 8 - t), (0, 0)))

    y_r, s_new = _retention_sample(heads(qk_r[:, :RET_WIDTH]), heads(qk_r[:, RET_WIDTH:]),
                                   heads(vg_r[:, :RET_WIDTH]), heads(vg_r[:, RET_WIDTH:]), state, t)
    yr_in = y_r[:, :, :t].transpose(0, 2, 1, 3).reshape(n, RET_WIDTH).astype(BF16)

    grp = lambda a: a.astype(F32).reshape(bd, t, len(DIL_PAIRS), DIL_HPG, HEAD_DIM)
    cviews = [c.reshape(bd, DIL_BLOCK, r, 2, DIL_HPG, HEAD_DIM) for c, (_, r) in zip(caches, DIL_PAIRS)]
    bias_c, bias_n = _dilated_sample_bias(wts["rel_bias"], t)
    o, lse = _dilated_sample(grp(qa), grp(ka), grp(va), cviews, bias_c, bias_n, t)
    o_g = [o[:, :, gi].reshape(n, DIL_OUT_WIDTH) for gi in range(len(DIL_PAIRS))]
    lse_g = [lse[:, :, gi].reshape(n, DIL_HPG) for gi in range(len(DIL_PAIRS))]

    kg, vg = grp(ka), grp(va)
    news = [jnp.stack([kg[:, :, gi], vg[:, :, gi]], axis=2) for gi in range(len(DIL_PAIRS))]
    bufs = _cache_shift(caches, news, t)

    merged = _merge(o_g, lse_g, yr_in, gates, wts["w_ret_out"], wts["w_dil_out"], n)
    h, u2, ei, wt = _outproj_router(merged, xf, wts["w_o"], wts["norm_ffn"], wts["wr_t"], wts["br_t"], n)
    return h, u2, ei, wt, s_new, list(bufs)


def kernel(x_prompt, x_sample, cache_kv_g0, cache_kv_g1, cache_kv_g2, state_ret, norm_attn, w_in, q_norm,
           k_norm, rel_bias, w_ret_out, w_dil_out, w_o, norm_ffn, w_router_group, b_router_group,
           w_router_expert, b_router_expert, w_gate, w_up, w_down):
    caches = (cache_kv_g0, cache_kv_g1, cache_kv_g2)
    ntok = x_sample.shape[1]
    for c, (w, r) in zip(caches, DIL_PAIRS):
        assert c.shape[1] == w == DIL_BLOCK * r and (r == 1 or ntok <= r) and ntok <= 8
    past_len = PAST_LEN
    wr_t, br_t = _router_weights(w_router_group, b_router_group, w_router_expert, b_router_expert)
    wts = dict(norm_attn=norm_attn, w_in=w_in.astype(BF16), q_norm=q_norm, k_norm=k_norm, rel_bias=rel_bias,
               w_ret_out=w_ret_out.astype(BF16), w_dil_out=w_dil_out.astype(BF16), w_o=w_o.astype(BF16),
               norm_ffn=norm_ffn, wr_t=wr_t, br_t=br_t)

    hp, u2p, eip, wtp, s_p, bufs_p = _mixers_prompt(x_prompt, wts)
    hs, u2s, eis, wts_s, s_s, bufs_s = _mixers_sample(x_sample, caches, state_ret, past_len, wts)

    npr, nsa = hp.shape[0], hs.shape[0]
    ntot = npr + nsa
    experts = jnp.concatenate([eip[:TOP_K], eis[:TOP_K]], axis=1)
    pos, tile_expert, ntiles_used, ntiles = _moe_plan(experts)
    d = hp.shape[1]
    xs = jnp.zeros((ntiles * MOE_TILE, d), F32)
    pos_p = jnp.concatenate([pos[:npr], pos[ntot:ntot + npr]])
    pos_s = jnp.concatenate([pos[npr:ntot], pos[ntot + npr:]])
    xs = _dispatch(pos_p, u2p, xs, 256)
    xs = _dispatch(pos_s, u2s, xs, nsa)
    out_sorted = _gmm(tile_expert, ntiles_used, xs, w_gate, w_up, w_down)
    yp = _combine(pos_p, hp, wtp.T, out_sorted, 256, npr, 0)
    ys = _combine(pos_s, hs, wts_s.T, out_sorted, nsa, nsa, 0)

    return (yp.reshape(x_prompt.shape), ys.reshape(x_sample.shape), bufs_p[0], bufs_p[1], bufs_p[2], s_p,
            bufs_s[0], bufs_s[1], bufs_s[2], s_s)
```

```python
import functools
import math

import numpy as np
import jax
import jax.numpy as jnp
from jax import lax
from jax.experimental import pallas as pl
from jax.experimental.pallas import tpu as pltpu

HEAD_DIM = 128
RET_HEADS = 8
RET_WIDTH = RET_HEADS * HEAD_DIM
ROPE_BASE = 10000.0
GN_EPS = 1e-5
DIL_PAIRS = ((128, 1), (512, 4), (2048, 16))
DIL_HPG = 4
DIL_HEADS = DIL_HPG * len(DIL_PAIRS)
DIL_WIDTH = DIL_HEADS * HEAD_DIM
DIL_OUT_WIDTH = DIL_HPG * HEAD_DIM
DIL_BLOCK = 128
ATTN_SCALE = HEAD_DIM ** -0.5
REL_BUCKETS = 32
REL_MAX_DIST = 2048
N_GROUPS = 4
EXPERTS_PER_GROUP = 8
N_EXPERTS = N_GROUPS * EXPERTS_PER_GROUP
TOP_K = 2
NORM_EPS = 1e-6
PAST_LEN = 16384

MOE_TILE = 256
MIB = 1 << 20
BF16 = jnp.bfloat16
F32 = jnp.float32


def _params(semantics, vmem_mib):
    return pltpu.CompilerParams(dimension_semantics=semantics, vmem_limit_bytes=vmem_mib * MIB)


def _rmsnorm_kernel(x_ref, g_ref, o_ref):
    x = x_ref[...]
    y = x * lax.rsqrt(jnp.mean(x * x, axis=-1, keepdims=True) + NORM_EPS)
    o_ref[...] = (y * g_ref[...]).astype(o_ref.dtype)


def _rmsnorm(x, g, tm):
    n, d = x.shape
    return pl.pallas_call(
        _rmsnorm_kernel,
        out_shape=jax.ShapeDtypeStruct((n, d), BF16),
        grid=(n // tm,),
        in_specs=[pl.BlockSpec((tm, d), lambda i: (i, 0)), pl.BlockSpec((1, d), lambda i: (0, 0))],
        out_specs=pl.BlockSpec((tm, d), lambda i: (i, 0)),
        compiler_params=_params(("parallel",), 40),
        name="rmsnorm",
    )(x, g.reshape(1, d))


def _proj_kernel(u_ref, w_ref, *refs, epilogue):
    acc = jnp.dot(u_ref[...], w_ref[...], preferred_element_type=F32)
    epilogue(acc, *refs)


def _proj(u, w, epilogue, extra, extra_specs, out_dtype, tm, tn, name):
    n, k = u.shape
    nw = w.shape[1]
    in_specs = [pl.BlockSpec((tm, k), lambda j, i: (i, 0)),
                pl.BlockSpec((k, tn), lambda j, i: (0, j))] + list(extra_specs)
    return pl.pallas_call(
        functools.partial(_proj_kernel, epilogue=epilogue),
        out_shape=jax.ShapeDtypeStruct((n, nw), out_dtype),
        grid=(nw // tn, n // tm),
        in_specs=in_specs,
        out_specs=pl.BlockSpec((tm, tn), lambda j, i: (i, j)),
        compiler_params=_params(("parallel", "parallel"), 48),
        name=name,
    )(u, w, *extra)


def _epi_rotary(acc, tab_ref, o_ref):
    c = tab_ref[0]
    s = tab_ref[1]
    for h in range(acc.shape[1] // HEAD_DIM):
        xh = acc[:, h * HEAD_DIM:(h + 1) * HEAD_DIM]
        o_ref[:, h * HEAD_DIM:(h + 1) * HEAD_DIM] = (
            xh * c + pltpu.roll(xh, HEAD_DIM // 2, 1) * s).astype(o_ref.dtype)


def _epi_value_gate(acc, o_ref):
    j = pl.program_id(0)

    @pl.when(j == 0)
    def _():
        o_ref[...] = acc.astype(o_ref.dtype)

    @pl.when(j == 1)
    def _():
        o_ref[...] = (acc * jax.nn.sigmoid(acc)).astype(o_ref.dtype)


def _epi_head_norm(acc, g_ref, o_ref, *, scale):
    g = g_ref[...]
    for h in range(acc.shape[1] // HEAD_DIM):
        xh = acc[:, h * HEAD_DIM:(h + 1) * HEAD_DIM]
        y = xh * lax.rsqrt(jnp.mean(xh * xh, axis=-1, keepdims=True) + NORM_EPS) * g
        if scale != 1.0:
            y = y * scale
        o_ref[:, h * HEAD_DIM:(h + 1) * HEAD_DIM] = y.astype(o_ref.dtype)


def _epi_plain(acc, o_ref):
    o_ref[...] = acc.astype(o_ref.dtype)


def _epi_sigmoid(acc, o_ref):
    o_ref[...] = jax.nn.sigmoid(acc).astype(o_ref.dtype)


def _rotary_tables(pos, rows):
    half = HEAD_DIM // 2
    inv = ROPE_BASE ** (-jnp.arange(half, dtype=F32) / half)
    ang = pos.astype(F32)[:, None] * inv[None, :]
    cos = jnp.cos(ang)
    sin = jnp.sin(ang)
    c = jnp.concatenate([cos, cos], axis=-1)
    s = jnp.concatenate([-sin, sin], axis=-1)
    tab = jnp.stack([jnp.stack([c, s]), jnp.stack([c, s]) * (HEAD_DIM ** -0.5)])
    reps = rows // pos.shape[0]
    return jnp.tile(tab, (1, 1, reps, 1))


def _input_projections(u, w_in_b, pos_tab, q_norm, k_norm, tm):
    n = u.shape[0]
    tab_rows = pos_tab.shape[2]
    nblk = tab_rows // tm
    o = 0
    qk_r = _proj(u, w_in_b[:, o:o + 2 * RET_WIDTH], _epi_rotary, [pos_tab],
                 [pl.BlockSpec((None, 2, tm, HEAD_DIM), lambda j, i: (j, 0, i % nblk, 0))],
                 BF16, tm, RET_WIDTH, "proj_qk_ret")
    o += 2 * RET_WIDTH
    vg_r = _proj(u, w_in_b[:, o:o + 2 * RET_WIDTH], _epi_value_gate, [], [], BF16, tm, RET_WIDTH,
                 "proj_vg_ret")
    o += 2 * RET_WIDTH
    gspec = [pl.BlockSpec((1, HEAD_DIM), lambda j, i: (0, 0))]
    qa = _proj(u, w_in_b[:, o:o + DIL_WIDTH], functools.partial(_epi_head_norm, scale=ATTN_SCALE),
               [q_norm.reshape(1, HEAD_DIM)], gspec, BF16, tm, DIL_WIDTH, "proj_q_dil")
    o += DIL_WIDTH
    ka = _proj(u, w_in_b[:, o:o + DIL_WIDTH], functools.partial(_epi_head_norm, scale=1.0),
               [k_norm.reshape(1, HEAD_DIM)], gspec, F32, tm, DIL_WIDTH, "proj_k_dil")
    o += DIL_WIDTH
    va = _proj(u, w_in_b[:, o:o + DIL_WIDTH], _epi_plain, [], [], F32, tm, DIL_WIDTH, "proj_v_dil")
    o += DIL_WIDTH
    d2 = w_in_b.shape[1] - o
    gates = _proj(u, w_in_b[:, o:], _epi_sigmoid, [], [], BF16, tm, d2 // 2, "proj_gates")
    return qk_r, vg_r, qa, ka, va, gates


def _retention_kernel(q_ref, k_ref, v_ref, g_ref, s0_ref, dm_ref, qd_ref, kd_ref, cd_ref,
                      y_ref, sf_ref, state, *, chunk, nchunks):
    t = pl.program_id(2)

    @pl.when(t == 0)
    def _():
        state[...] = s0_ref[...]

    dmask = dm_ref[...]
    qdec = qd_ref[...]
    kdec = kd_ref[...]
    cdec = cd_ref[...]
    for ci in range(nchunks):
        rows = pl.ds(ci * chunk, chunk)
        q = q_ref[rows, :]
        k = k_ref[rows, :]
        v = v_ref[rows, :]
        s_prev = state[...]
        sc = lax.dot_general(q, k, (((1,), (1,)), ((), ())), preferred_element_type=F32) * dmask
        intra = jnp.dot(sc.astype(BF16), v, preferred_element_type=F32)
        qd = (q.astype(F32) * qdec).astype(BF16)
        cross = jnp.dot(qd, s_prev.astype(BF16), preferred_element_type=F32)
        kd = (k.astype(F32) * kdec).astype(BF16)
        kv = lax.dot_general(kd, v, (((0,), (0,)), ((), ())), preferred_element_type=F32)
        state[...] = s_prev * cdec + kv
        o = intra + cross
        mu = jnp.mean(o, axis=-1, keepdims=True)
        oc = o - mu
        var = jnp.mean(oc * oc, axis=-1, keepdims=True)
        y = g_ref[rows, :].astype(F32) * (oc * lax.rsqrt(var + GN_EPS))
        y_ref[rows, :] = y.astype(y_ref.dtype)

    @pl.when(t == pl.num_programs(2) - 1)
    def _():
        sf_ref[...] = state[...]


def _retention_decay(chunk, valid):
    lg = jnp.log1p(-jnp.exp2(-5.0 - jnp.arange(RET_HEADS, dtype=F32)))
    idx = jnp.arange(chunk, dtype=F32)
    rel = idx[:, None] - idx[None, :]
    dmask = jnp.where(rel[None] >= 0, jnp.exp(lg[:, None, None] * jnp.maximum(rel, 0.0)[None]), 0.0)
    qdec = jnp.exp(lg[:, None] * (idx + 1.0)[None, :])[..., None]
    kdec = jnp.exp(lg[:, None] * (valid - 1.0 - idx)[None, :])[..., None]
    kdec = jnp.where((idx < valid)[None, :, None], kdec, 0.0)
    cdec = jnp.exp(lg * valid)[:, None, None]
    return dmask, qdec, kdec, cdec


def _retention_prompt(qk_r, vg_r, s0, chunk, rows_per_step):
    b, l, _ = qk_r.shape
    h = RET_HEADS
    dmask, qdec, kdec, cdec = _retention_decay(chunk, chunk)
    nsteps = l // rows_per_step
    blk = lambda off: pl.BlockSpec((None, rows_per_step, HEAD_DIM), lambda bi, hi, ti: (bi, ti, hi + off))
    per_head = lambda shape: pl.BlockSpec((None,) + shape, lambda bi, hi, ti: (hi,) + (0,) * len(shape))
    state_spec = pl.BlockSpec((None, None, HEAD_DIM, HEAD_DIM), lambda bi, hi, ti: (bi, hi, 0, 0))
    return pl.pallas_call(
        functools.partial(_retention_kernel, chunk=chunk, nchunks=rows_per_step // chunk),
        out_shape=(jax.ShapeDtypeStruct((b, l, RET_WIDTH), BF16),
                   jax.ShapeDtypeStruct((b, h, HEAD_DIM, HEAD_DIM), F32)),
        grid=(b, h, nsteps),
        in_specs=[blk(0), blk(h), blk(0), blk(h), state_spec,
                  per_head((chunk, chunk)), per_head((chunk, 1)), per_head((chunk, 1)), per_head((1, 1))],
        out_specs=(blk(0), state_spec),
        scratch_shapes=[pltpu.VMEM((HEAD_DIM, HEAD_DIM), F32)],
        compiler_params=_params(("parallel", "parallel", "arbitrary"), 32),
        name="retention_prompt",
    )(qk_r, qk_r, vg_r, vg_r, s0, dmask, qdec, kdec, cdec)


def _retention_sample_kernel(q_ref, k_ref, v_ref, g_ref, s0_ref, dm_ref, qd_ref, kd_ref, cd_ref,
                             y_ref, sf_ref, *, ntok):
    for h in range(RET_HEADS):
        q = q_ref[h]
        k = k_ref[h]
        v = v_ref[h]
        s_prev = s0_ref[h]
        dmask = dm_ref[h]
        o = jnp.dot(q * qd_ref[h], s_prev, preferred_element_type=F32)
        for j in range(ntok):
            sj = jnp.sum(q * k[j:j + 1, :], axis=-1, keepdims=True) * dmask[:, j:j + 1]
            o = o + sj * v[j:j + 1, :]
        kd = k * kd_ref[h]
        kv = lax.dot_general(kd, v, (((0,), (0,)), ((), ())), preferred_element_type=F32)
        sf_ref[h] = s_prev * cd_ref[h] + kv
        mu = jnp.mean(o, axis=-1, keepdims=True)
        oc = o - mu
        var = jnp.mean(oc * oc, axis=-1, keepdims=True)
        y_ref[h] = g_ref[h] * (oc * lax.rsqrt(var + GN_EPS))


def _retention_sample(q, k, v, g, s0, ntok):
    bd, h, tp, _ = q.shape
    dmask, qdec, kdec, cdec = _retention_decay(tp, ntok)
    tok = pl.BlockSpec((None, h, tp, HEAD_DIM), lambda bi: (bi, 0, 0, 0))
    st = pl.BlockSpec((None, h, HEAD_DIM, HEAD_DIM), lambda bi: (bi, 0, 0, 0))
    const = lambda a: pl.BlockSpec(a.shape, lambda bi: (0,) * a.ndim)
    return pl.pallas_call(
        functools.partial(_retention_sample_kernel, ntok=ntok),
        out_shape=(jax.ShapeDtypeStruct((bd, h, tp, HEAD_DIM), F32),
                   jax.ShapeDtypeStruct((bd, h, HEAD_DIM, HEAD_DIM), F32)),
        grid=(bd,),
        in_specs=[tok, tok, tok, tok, st, const(dmask), const(qdec), const(kdec), const(cdec)],
        out_specs=(tok, st),
        compiler_params=_params(("parallel",), 32),
        name="retention_sample",
    )(q, k, v, g, s0, dmask, qdec, kdec, cdec)


def _t5_bucket(dist):
    max_exact = REL_BUCKETS // 2
    d = jnp.maximum(dist, 0)
    df = jnp.maximum(d, 1).astype(F32)
    large = max_exact + (jnp.log(df / max_exact) / math.log(REL_MAX_DIST / max_exact)
                         * (REL_BUCKETS - max_exact)).astype(jnp.int32)
    large = jnp.minimum(large, REL_BUCKETS - 1)
    return jnp.where(d < max_exact, d, large)


def _dilated_kernel(q_ref, kp_ref, kc_ref, vp_ref, vc_ref, bias_ref, o_ref, lse_ref, kfull, vfull, *, nsub):
    i = pl.program_id(1)
    blk = DIL_BLOCK
    kfull[0:blk, :] = kp_ref[...]
    kfull[blk:, :] = kc_ref[...]
    vfull[0:blk, :] = vp_ref[...]
    vfull[blk:, :] = vc_ref[...]
    col = lax.broadcasted_iota(jnp.int32, (blk, 2 * blk), 1)
    for s in range(nsub):
        rows = pl.ds(s * blk, blk)
        win = pl.ds(s * blk, 2 * blk)
        for h in range(DIL_HPG):
            cols = pl.ds(h * HEAD_DIM, HEAD_DIM)
            q = q_ref[rows, cols]
            kw = kfull[win, cols]
            vw = vfull[win, cols]
            sc = lax.dot_general(q, kw, (((1,), (1,)), ((), ())), preferred_element_type=F32) + bias_ref[h]
            if s == 0:
                sc = jnp.where((col >= blk) | (i > 0), sc, -jnp.inf)
            m = jnp.max(sc, axis=-1, keepdims=True)
            p = jnp.exp(sc - m)
            l = jnp.sum(p, axis=-1, keepdims=True)
            o = jnp.dot(p.astype(BF16), vw, preferred_element_type=F32) / l
            o_ref[rows, cols] = o.astype(o_ref.dtype)
            lse_ref[rows, h:h + 1] = m + jnp.log(l)


def _dilated_bias(bias_tab, r):
    blk = DIL_BLOCK
    qi = jnp.arange(blk)[:, None]
    kj = jnp.arange(2 * blk)[None, :]
    dc = blk + qi - kj
    band = (dc >= 0) & (dc <= blk)
    bias = bias_tab[_t5_bucket(dc * r)].astype(F32).transpose(2, 0, 1)
    return jnp.where(band[None], bias, -jnp.inf)


def _dilated_prompt(q, k, v, bias, nsub):
    n, lc, w = q.shape
    blk = DIL_BLOCK
    tq = nsub * blk
    cur = pl.BlockSpec((None, tq, w), lambda ni, i: (ni, i, 0))
    prev = pl.BlockSpec((None, blk, w), lambda ni, i: (ni, jnp.maximum(i * nsub - 1, 0), 0))
    return pl.pallas_call(
        functools.partial(_dilated_kernel, nsub=nsub),
        out_shape=(jax.ShapeDtypeStruct((n, lc, w), F32),
                   jax.ShapeDtypeStruct((n, lc, DIL_HPG), F32)),
        grid=(n, lc // tq),
        in_specs=[cur, prev, cur, prev, cur, pl.BlockSpec(bias.shape, lambda ni, i: (0, 0, 0))],
        out_specs=(cur, pl.BlockSpec((None, tq, DIL_HPG), lambda ni, i: (ni, i, 0))),
        scratch_shapes=[pltpu.VMEM((tq + blk, w), BF16), pltpu.VMEM((tq + blk, w), BF16)],
        compiler_params=_params(("parallel", "parallel"), 32),
        name="dilated_prompt",
    )(q, k, k, v, v, bias)


def _dilated_sample_kernel(q_ref, kn_ref, vn_ref, c0_ref, c1_ref, c2_ref, bc_ref, bn_ref,
                           o_ref, lse_ref, *, ntok):
    caches = (c0_ref, c1_ref, c2_ref)
    for gi in range(len(DIL_PAIRS)):
        cache = caches[gi]
        nclass = cache.shape[1]
        for t in range(ntok):
            cls = t if nclass > 1 else 0
            qt = q_ref[t, gi]
            kc = cache[:, cls, 0]
            vc = cache[:, cls, 1]
            sc = jnp.sum(kc * qt[None], axis=-1, keepdims=True) + bc_ref[gi, t]
            sn = jnp.sum(kn_ref[:, gi] * qt[None], axis=-1, keepdims=True) + bn_ref[gi, t]
            m = jnp.maximum(jnp.max(sc, axis=0), jnp.max(sn, axis=0))
            pc = jnp.exp(sc - m[None])
            pn = jnp.exp(sn - m[None])
            l = jnp.sum(pc, axis=0) + jnp.sum(pn, axis=0)
            o = jnp.sum(pc * vc, axis=0) + jnp.sum(pn * vn_ref[:, gi], axis=0)
            o_ref[t, gi] = o / l
            lse_ref[t, gi] = m + jnp.log(l)


def _dilated_sample(q, kn, vn, caches, bias_cache, bias_new, ntok):
    bd = q.shape[0]
    tok = pl.BlockSpec((None, ntok, len(DIL_PAIRS), DIL_HPG, HEAD_DIM), lambda bi: (bi, 0, 0, 0, 0))
    cspecs = []
    for c in caches:
        ncls = min(c.shape[2], ntok)
        cspecs.append(pl.BlockSpec((None, DIL_BLOCK, ncls, 2, DIL_HPG, HEAD_DIM),
                                   lambda bi: (bi, 0, 0, 0, 0, 0)))
    const = lambda a: pl.BlockSpec(a.shape, lambda bi: (0,) * a.ndim)
    return pl.pallas_call(
        functools.partial(_dilated_sample_kernel, ntok=ntok),
        out_shape=(jax.ShapeDtypeStruct(q.shape, F32),
                   jax.ShapeDtypeStruct((bd, ntok, len(DIL_PAIRS), DIL_HPG, 1), F32)),
        grid=(bd,),
        in_specs=[tok, tok, tok] + cspecs + [const(bias_cache), const(bias_new)],
        out_specs=(tok, pl.BlockSpec((None, ntok, len(DIL_PAIRS), DIL_HPG, 1), lambda bi: (bi, 0, 0, 0, 0))),
        compiler_params=_params(("parallel",), 32),
        name="dilated_sample",
    )(q, kn, vn, *caches, bias_cache, bias_new)


def _dilated_sample_bias(rel_bias, ntok):
    bc, bn = [], []
    m = jnp.arange(DIL_BLOCK)
    tn = jnp.arange(ntok)
    for gi, (_, r) in enumerate(DIL_PAIRS):
        tab = rel_bias[:, gi * DIL_HPG:(gi + 1) * DIL_HPG].astype(F32)
        rows_c, rows_n = [], []
        for t in range(ntok):
            if r == 1:
                jc = DIL_BLOCK + t - m
                okc = m >= t
                jn = t - tn
                okn = tn <= t
            else:
                jc = DIL_BLOCK - m
                okc = jnp.ones_like(m, bool)
                jn = jnp.zeros_like(tn)
                okn = tn == t
            rows_c.append(jnp.where(okc[:, None], tab[_t5_bucket(jc * r)], -jnp.inf))
            rows_n.append(jnp.where(okn[:, None], tab[_t5_bucket(jn * r)], -jnp.inf))
        bc.append(jnp.stack(rows_c))
        bn.append(jnp.stack(rows_n))
    return jnp.stack(bc)[..., None], jnp.stack(bn)[..., None]


def _cache_shift_kernel(c_ref, n_ref, o_ref, *, ntok):
    i = pl.program_id(1)
    last = pl.num_programs(1) - 1
    rows = o_ref.shape[0]

    @pl.when(i < last)
    def _():
        o_ref[...] = c_ref[0]

    @pl.when(i == last)
    def _():
        o_ref[0:rows - ntok] = c_ref[0, ntok:rows]
        o_ref[rows - ntok:rows] = n_ref[...]


def _cache_shift(cache, new, ntok):
    bd, lb = cache.shape[:2]
    tail = cache.shape[2:]
    rows = min(lb, 512)
    zeros = (0,) * len(tail)
    return pl.pallas_call(
        functools.partial(_cache_shift_kernel, ntok=ntok),
        out_shape=jax.ShapeDtypeStruct(cache.shape, cache.dtype),
        grid=(bd, lb // rows),
        in_specs=[pl.BlockSpec(tuple(pl.Element(s) for s in (1, rows) + tail),
                               lambda b, i: (b, jnp.minimum(i * rows + ntok, lb - rows)) + zeros),
                  pl.BlockSpec((None, ntok) + tail, lambda b, i: (b, 0) + zeros)],
        out_specs=pl.BlockSpec((None, rows) + tail, lambda b, i: (b, i) + zeros),
        compiler_params=_params(("parallel", "arbitrary"), 40),
        name="cache_shift",
    )(cache, new)


def _merge_kernel(o0_ref, o1_ref, o2_ref, l0_ref, l1_ref, l2_ref, yr_ref, gt_ref, wr_ref, wd_ref,
                  out_ref, oa_ref):
    ls = (l0_ref[...], l1_ref[...], l2_ref[...])
    os_ = (o0_ref, o1_ref, o2_ref)
    m = jnp.maximum(jnp.maximum(ls[0], ls[1]), ls[2])
    es = [jnp.exp(l - m) for l in ls]
    den = es[0] + es[1] + es[2]
    for h in range(DIL_HPG):
        cols = pl.ds(h * HEAD_DIM, HEAD_DIM)
        acc = None
        for g in range(3):
            term = (es[g][:, h:h + 1] / den[:, h:h + 1]) * os_[g][:, cols]
            acc = term if acc is None else acc + term
        oa_ref[:, cols] = acc.astype(oa_ref.dtype)
    ya = jnp.dot(oa_ref[...], wd_ref[...], preferred_element_type=F32)
    yr = jnp.dot(yr_ref[...], wr_ref[...], preferred_element_type=F32)
    d = ya.shape[1]
    out_ref[...] = (gt_ref[:, :d].astype(F32) * yr + gt_ref[:, d:].astype(F32) * ya).astype(out_ref.dtype)


def _merge(o_g, lse_g, yr_in, gates, w_ret_b, w_dil_b, tm):
    n = yr_in.shape[0]
    d = w_ret_b.shape[1]
    row = lambda w: pl.BlockSpec((tm, w), lambda i: (i, 0))
    const = lambda a: pl.BlockSpec(a.shape, lambda i: (0, 0))
    return pl.pallas_call(
        _merge_kernel,
        out_shape=jax.ShapeDtypeStruct((n, d), BF16),
        grid=(n // tm,),
        in_specs=[row(DIL_OUT_WIDTH)] * 3 + [row(DIL_HPG)] * 3 + [row(RET_WIDTH), row(2 * d),
                                                                   const(w_ret_b), const(w_dil_b)],
        out_specs=row(d),
        scratch_shapes=[pltpu.VMEM((tm, DIL_OUT_WIDTH), BF16)],
        compiler_params=_params(("parallel",), 48),
        name="merge_branches",
    )(*o_g, *lse_g, yr_in, gates, w_ret_b, w_dil_b)


ROUTER_ROWS = 8 + N_EXPERTS


def _outproj_router_kernel(mg_ref, x_ref, wo_ref, g2_ref, wr_ref, br_ref, h_ref, u2_ref, ei_ref, wt_ref):
    h = x_ref[...] + jnp.dot(mg_ref[...], wo_ref[...], preferred_element_type=F32)
    h_ref[...] = h
    u2 = h * lax.rsqrt(jnp.mean(h * h, axis=-1, keepdims=True) + NORM_EPS) * g2_ref[...]
    u2_ref[...] = u2
    lt = lax.dot_general(wr_ref[...], u2.astype(BF16), (((1,), (1,)), ((), ())),
                         preferred_element_type=F32) + br_ref[...]
    lg = lt[0:N_GROUPS]
    gmax = jnp.max(lg, axis=0, keepdims=True)
    w_coarse = 1.0 / jnp.sum(jnp.exp(lg - gmax), axis=0, keepdims=True)
    gid = lax.broadcasted_iota(jnp.int32, lg.shape, 0)
    gsel = jnp.min(jnp.where(lg == gmax, gid, N_GROUPS), axis=0, keepdims=True)
    le = jnp.zeros((EXPERTS_PER_GROUP, lt.shape[1]), F32)
    for g in range(N_GROUPS):
        le = jnp.where(gsel == g, lt[8 + g * EXPERTS_PER_GROUP:8 + (g + 1) * EXPERTS_PER_GROUP], le)
    eid = lax.broadcasted_iota(jnp.int32, le.shape, 0)
    v1 = jnp.max(le, axis=0, keepdims=True)
    i1 = jnp.min(jnp.where(le == v1, eid, EXPERTS_PER_GROUP), axis=0, keepdims=True)
    le2 = jnp.where(eid == i1, -jnp.inf, le)
    v2 = jnp.max(le2, axis=0, keepdims=True)
    i2 = jnp.min(jnp.where(le2 == v2, eid, EXPERTS_PER_GROUP), axis=0, keepdims=True)
    e21 = jnp.exp(v2 - v1)
    w1 = w_coarse / (1.0 + e21)
    w2 = w_coarse * e21 / (1.0 + e21)
    row = lax.broadcasted_iota(jnp.int32, ei_ref.shape, 0)
    base = gsel * EXPERTS_PER_GROUP
    ei_ref[...] = jnp.where(row == 0, base + i1, jnp.where(row == 1, base + i2, 0))
    wt_ref[...] = jnp.where(row == 0, w1, jnp.where(row == 1, w2, 0.0))


def _outproj_router(merged, x, w_o_b, norm_ffn, wr_t, br_t, tm):
    n, d = x.shape
    row = lambda: pl.BlockSpec((tm, d), lambda i: (i, 0))
    lane = lambda: pl.BlockSpec((8, tm), lambda i: (0, i))
    const = lambda a: pl.BlockSpec(a.shape, lambda i: (0, 0))
    return pl.pallas_call(
        _outproj_router_kernel,
        out_shape=(jax.ShapeDtypeStruct((n, d), F32), jax.ShapeDtypeStruct((n, d), F32),
                   jax.ShapeDtypeStruct((8, n), jnp.int32), jax.ShapeDtypeStruct((8, n), F32)),
        grid=(n // tm,),
        in_specs=[row(), row(), const(w_o_b), pl.BlockSpec((1, d), lambda i: (0, 0)),
                  const(wr_t), const(br_t)],
        out_specs=(row(), row(), lane(), lane()),
        compiler_params=_params(("parallel",), 56),
        name="outproj_router",
    )(merged, x, w_o_b, norm_ffn.reshape(1, d), wr_t, br_t)


def _router_weights(w_rg, b_rg, w_re, b_re):
    d = w_rg.shape[0]
    wr = jnp.zeros((ROUTER_ROWS, d), F32)
    wr = wr.at[0:N_GROUPS].set(w_rg.T)
    wr = wr.at[8:].set(w_re.transpose(0, 2, 1).reshape(N_EXPERTS, d))
    br = jnp.zeros((ROUTER_ROWS, 1), F32)
    br = br.at[0:N_GROUPS, 0].set(b_rg)
    br = br.at[8:, 0].set(b_re.reshape(N_EXPERTS))
    return wr.astype(BF16), br


def _dispatch_kernel(pos_ref, u_ref, xs_in_ref, xs_ref, sem, *, tm, ntok_total):
    del xs_in_ref
    i = pl.program_id(0)

    def row_copy(r, kk):
        slot = pos_ref[kk * ntok_total + i * tm + r]
        return pltpu.make_async_copy(u_ref.at[pl.ds(r, 1)], xs_ref.at[pl.ds(slot, 1)], sem)

    def issue(r, c):
        row_copy(r, 0).start()
        row_copy(r, 1).start()
        return c

    lax.fori_loop(0, tm, issue, 0)

    def drain(r, c):
        row_copy(r, 0).wait()
        row_copy(r, 1).wait()
        return c

    lax.fori_loop(0, tm, drain, 0)


def _dispatch(pos_flat, u2, xs, tm):
    n, d = u2.shape
    grid_spec = pltpu.PrefetchScalarGridSpec(
        num_scalar_prefetch=1,
        grid=(n // tm,),
        in_specs=[pl.BlockSpec((tm, d), lambda i, pos: (i, 0)), pl.BlockSpec(memory_space=pl.ANY)],
        out_specs=pl.BlockSpec(memory_space=pl.ANY),
        scratch_shapes=[pltpu.SemaphoreType.DMA(())],
    )
    return pl.pallas_call(
        functools.partial(_dispatch_kernel, tm=tm, ntok_total=n),
        out_shape=jax.ShapeDtypeStruct(xs.shape, xs.dtype),
        grid_spec=grid_spec,
        input_output_aliases={2: 0},
        compiler_params=_params(("arbitrary",), 32),
        name="moe_dispatch",
    )(pos_flat, u2, xs)


def _gmm_kernel(te_ref, nt_ref, x_ref, wg_ref, wu_ref, wd_ref, o_ref, wgb, wub, wdb):
    i = pl.program_id(0)
    fresh = jnp.logical_or(i == 0, te_ref[i] != te_ref[jnp.maximum(i - 1, 0)])

    @pl.when(jnp.logical_and(fresh, i < nt_ref[0]))
    def _():
        wgb[...] = wg_ref[...].astype(BF16)
        wub[...] = wu_ref[...].astype(BF16)
        wdb[...] = wd_ref[...].astype(BF16)

    @pl.when(i < nt_ref[0])
    def _():
        x = x_ref[...].astype(BF16)
        a = jnp.dot(x, wgb[...], preferred_element_type=F32)
        b = jnp.dot(x, wub[...], preferred_element_type=F32)
        hm = (a * jax.nn.sigmoid(a) * b).astype(BF16)
        o_ref[...] = jnp.dot(hm, wdb[...], preferred_element_type=F32)

    @pl.when(i >= nt_ref[0])
    def _():
        o_ref[...] = jnp.zeros_like(o_ref)


def _gmm(tile_expert, ntiles_used, xs, w_gate, w_up, w_down):
    npad, d = xs.shape
    f = w_gate.shape[-1]
    ntiles = npad // MOE_TILE
    epg = w_gate.shape[1]

    def xmap(i, te, nt):
        return (jnp.minimum(i, nt[0] - 1), 0)

    def wmap(i, te, nt):
        e = te[i]
        return (e // epg, e % epg, 0, 0)

    grid_spec = pltpu.PrefetchScalarGridSpec(
        num_scalar_prefetch=2,
        grid=(ntiles,),
        in_specs=[pl.BlockSpec((MOE_TILE, d), xmap),
                  pl.BlockSpec((None, None, d, f), wmap),
                  pl.BlockSpec((None, None, d, f), wmap),
                  pl.BlockSpec((None, None, f, d), wmap)],
        out_specs=pl.BlockSpec((MOE_TILE, d), lambda i, te, nt: (i, 0)),
        scratch_shapes=[pltpu.VMEM((d, f), BF16), pltpu.VMEM((d, f), BF16), pltpu.VMEM((f, d), BF16)],
    )
    return pl.pallas_call(
        _gmm_kernel,
        out_shape=jax.ShapeDtypeStruct((npad, d), F32),
        grid_spec=grid_spec,
        compiler_params=_params(("arbitrary",), 56),
        name="moe_grouped_matmul",
    )(tile_expert, ntiles_used, xs, w_gate, w_up, w_down)


def _combine_kernel(pos_ref, h_ref, wt_ref, os_ref, y_ref, g0, g1, sem, *, tm, ntok_total, tok_offset):
    i = pl.program_id(0)

    def row_copy(r, kk, buf):
        slot = pos_ref[kk * ntok_total + tok_offset + i * tm + r]
        return pltpu.make_async_copy(os_ref.at[pl.ds(slot, 1)], buf.at[pl.ds(r, 1)], sem)

    def issue(r, c):
        row_copy(r, 0, g0).start()
        row_copy(r, 1, g1).start()
        return c

    lax.fori_loop(0, tm, issue, 0)

    def drain(r, c):
        row_copy(r, 0, g0).wait()
        row_copy(r, 1, g1).wait()
        return c

    lax.fori_loop(0, tm, drain, 0)
    y_ref[...] = h_ref[...] + wt_ref[:, 0:1] * g0[...] + wt_ref[:, 1:2] * g1[...]


def _combine(pos_flat, h, wt_cols, out_sorted, tm, ntok_total, tok_offset):
    n, d = h.shape
    grid_spec = pltpu.PrefetchScalarGridSpec(
        num_scalar_prefetch=1,
        grid=(n // tm,),
        in_specs=[pl.BlockSpec((tm, d), lambda i, pos: (i, 0)),
                  pl.BlockSpec((tm, 8), lambda i, pos: (i, 0)),
                  pl.BlockSpec(memory_space=pl.ANY)],
        out_specs=pl.BlockSpec((tm, d), lambda i, pos: (i, 0)),
        scratch_shapes=[pltpu.VMEM((tm, d), F32), pltpu.VMEM((tm, d), F32), pltpu.SemaphoreType.DMA(())],
    )
    return pl.pallas_call(
        functools.partial(_combine_kernel, tm=tm, ntok_total=ntok_total, tok_offset=tok_offset),
        out_shape=jax.ShapeDtypeStruct((n, d), F32),
        grid_spec=grid_spec,
        compiler_params=_params(("arbitrary",), 32),
        name="moe_combine",
    )(pos_flat, h, wt_cols, out_sorted)


def _moe_plan(experts):
    n = experts.shape[1]
    flat = experts.reshape(-1)
    onehot = (flat[:, None] == jnp.arange(N_EXPERTS)[None, :]).astype(jnp.int32)
    csum = jnp.cumsum(onehot, axis=0)
    rank = jnp.sum((csum - onehot) * onehot, axis=1)
    counts = csum[-1]
    tiles = (counts + MOE_TILE - 1) // MOE_TILE
    tile_end = jnp.cumsum(tiles)
    offs = (tile_end - tiles) * MOE_TILE
    pos = offs[flat] + rank
    ntiles = (TOP_K * n + N_EXPERTS * (MOE_TILE - 1)) // MOE_TILE
    tile_expert = jnp.searchsorted(tile_end, jnp.arange(ntiles), side="right").astype(jnp.int32)
    tile_expert = jnp.minimum(tile_expert, N_EXPERTS - 1)
    return pos.astype(jnp.int32), tile_expert, tile_end[-1:].astype(jnp.int32), ntiles


def _to_classes(t, r):
    b, s, w = t.shape
    return t.reshape(b, s // r, r, w).transpose(0, 2, 1, 3).reshape(b * r, s // r, w)


def _from_classes(t, b, r):
    n, lc, w = t.shape
    return t.reshape(b, r, lc, w).transpose(0, 2, 1, 3).reshape(b, lc * r, w)


def _mixers_prompt(x, wts):
    b, l, d = x.shape
    n = b * l
    xf = x.reshape(n, d)
    u = _rmsnorm(xf, wts["norm_attn"], 512)
    tab = _rotary_tables(jnp.arange(l), l)
    qk_r, vg_r, qa, ka, va, gates = _input_projections(u, wts["w_in"], tab, wts["q_norm"], wts["k_norm"], 512)

    s0 = jnp.zeros((b, RET_HEADS, HEAD_DIM, HEAD_DIM), F32)
    yr_in, s_fin = _retention_prompt(qk_r.reshape(b, l, -1), vg_r.reshape(b, l, -1), s0, 256, min(l, 1024))

    qa3, ka3, va3 = (t.reshape(b, l, DIL_WIDTH) for t in (qa, ka, va))
    o_g, lse_g, bufs = [], [], []
    for gi, (w, r) in enumerate(DIL_PAIRS):
        cols = slice(gi * DIL_OUT_WIDTH, (gi + 1) * DIL_OUT_WIDTH)
        qg = _to_classes(qa3[..., cols], r)
        kg = _to_classes(ka3[..., cols].astype(BF16), r)
        vg = _to_classes(va3[..., cols].astype(BF16), r)
        bias = _dilated_bias(wts["rel_bias"][:, gi * DIL_HPG:(gi + 1) * DIL_HPG], r)
        o, lse = _dilated_prompt(qg, kg, vg, bias, min(4, l // r // DIL_BLOCK))
        o_g.append(_from_classes(o, b, r).reshape(n, DIL_OUT_WIDTH))
        lse_g.append(_from_classes(lse, b, r).reshape(n, DIL_HPG))
        lw = min(w, l)
        kv = jnp.stack([ka3[:, l - lw:, cols], va3[:, l - lw:, cols]], axis=2)
        bufs.append(kv.reshape(b, lw, 2, DIL_HPG, HEAD_DIM))

    merged = _merge(o_g, lse_g, yr_in.reshape(n, RET_WIDTH), gates, wts["w_ret_out"], wts["w_dil_out"], 512)
    h, u2, ei, wt = _outproj_router(merged, xf, wts["w_o"], wts["norm_ffn"], wts["wr_t"], wts["br_t"], 256)
    return h, u2, ei, wt, s_fin, bufs


def _mixers_sample(x, caches, state, past_len, wts):
    bd, t, d = x.shape
    n = bd * t
    xf = x.reshape(n, d)
    u = _rmsnorm(xf, wts["norm_attn"], n)
    tab = _rotary_tables(past_len + jnp.arange(t), n)
    qk_r, vg_r, qa, ka, va, gates = _input_projections(u, wts["w_in"], tab, wts["q_norm"], wts["k_norm"], n)

    def heads(a):
        a = a.astype(F32).reshape(bd, t, RET_HEADS, HEAD_DIM).transpose(0, 2, 1, 3)
        return jnp.pad(a, ((0, 0), (0, 0), (0, 8 - t), (0, 0)))

    y_r, s_new = _retention_sample(heads(qk_r[:, :RET_WIDTH]), heads(qk_r[:, RET_WIDTH:]),
                                   heads(vg_r[:, :RET_WIDTH]), heads(vg_r[:, RET_WIDTH:]), state, t)
    yr_in = y_r[:, :, :t].transpose(0, 2, 1, 3).reshape(n, RET_WIDTH).astype(BF16)

    grp = lambda a: a.astype(F32).reshape(bd, t, len(DIL_PAIRS), DIL_HPG, HEAD_DIM)
    cviews = [c.reshape(bd, DIL_BLOCK, r, 2, DIL_HPG, HEAD_DIM) for c, (_, r) in zip(caches, DIL_PAIRS)]
    bias_c, bias_n = _dilated_sample_bias(wts["rel_bias"], t)
    o, lse = _dilated_sample(grp(qa), grp(ka), grp(va), cviews, bias_c, bias_n, t)
    o_g = [o[:, :, gi].reshape(n, DIL_OUT_WIDTH) for gi in range(len(DIL_PAIRS))]
    lse_g = [lse[:, :, gi].reshape(n, DIL_HPG) for gi in range(len(DIL_PAIRS))]

    kg, vg = grp(ka), grp(va)
    news = [jnp.stack([kg[:, :, gi], vg[:, :, gi]], axis=2) for gi in range(len(DIL_PAIRS))]
    bufs = [_cache_shift(c, nw, t) for c, nw in zip(caches, news)]

    merged = _merge(o_g, lse_g, yr_in, gates, wts["w_ret_out"], wts["w_dil_out"], n)
    h, u2, ei, wt = _outproj_router(merged, xf, wts["w_o"], wts["norm_ffn"], wts["wr_t"], wts["br_t"], n)
    return h, u2, ei, wt, s_new, list(bufs)


def kernel(x_prompt, x_sample, cache_kv_g0, cache_kv_g1, cache_kv_g2, state_ret, norm_attn, w_in, q_norm,
           k_norm, rel_bias, w_ret_out, w_dil_out, w_o, norm_ffn, w_router_group, b_router_group,
           w_router_expert, b_router_expert, w_gate, w_up, w_down):
    caches = (cache_kv_g0, cache_kv_g1, cache_kv_g2)
    ntok = x_sample.shape[1]
    for c, (w, r) in zip(caches, DIL_PAIRS):
        assert c.shape[1] == w == DIL_BLOCK * r and (r == 1 or ntok <= r) and ntok <= 8
    past_len = PAST_LEN
    wr_t, br_t = _router_weights(w_router_group, b_router_group, w_router_expert, b_router_expert)
    wts = dict(norm_attn=norm_attn, w_in=w_in.astype(BF16), q_norm=q_norm, k_norm=k_norm, rel_bias=rel_bias,
               w_ret_out=w_ret_out.astype(BF16), w_dil_out=w_dil_out.astype(BF16), w_o=w_o.astype(BF16),
               norm_ffn=norm_ffn, wr_t=wr_t, br_t=br_t)

    hp, u2p, eip, wtp, s_p, bufs_p = _mixers_prompt(x_prompt, wts)
    hs, u2s, eis, wts_s, s_s, bufs_s = _mixers_sample(x_sample, caches, state_ret, past_len, wts)

    npr, nsa = hp.shape[0], hs.shape[0]
    ntot = npr + nsa
    experts = jnp.concatenate([eip[:TOP_K], eis[:TOP_K]], axis=1)
    pos, tile_expert, ntiles_used, ntiles = _moe_plan(experts)
    d = hp.shape[1]
    xs = jnp.zeros((ntiles * MOE_TILE, d), F32)
    pos_p = jnp.concatenate([pos[:npr], pos[ntot:ntot + npr]])
    pos_s = jnp.concatenate([pos[npr:ntot], pos[ntot + npr:]])
    xs = _dispatch(pos_p, u2p, xs, 256)
    xs = _dispatch(pos_s, u2s, xs, nsa)
    out_sorted = _gmm(tile_expert, ntiles_used, xs, w_gate, w_up, w_down)
    yp = _combine(pos_p, hp, wtp.T, out_sorted, 256, npr, 0)
    ys = _combine(pos_s, hs, wts_s.T, out_sorted, nsa, nsa, 0)

    return (yp.reshape(x_prompt.shape), ys.reshape(x_sample.shape), bufs_p[0], bufs_p[1], bufs_p[2], s_p,
            bufs_s[0], bufs_s[1], bufs_s[2], s_s)
```

```python
import functools
import math

import jax
import jax.numpy as jnp
from jax import lax
from jax.experimental import pallas as pl
from jax.experimental.pallas import tpu as pltpu

HEAD_DIM = 128
RET_HEADS = 8
RET_WIDTH = RET_HEADS * HEAD_DIM
ROPE_BASE = 10000.0
GN_EPS = 1e-5
DIL_PAIRS = ((128, 1), (512, 4), (2048, 16))
DIL_GROUPS = len(DIL_PAIRS)
DIL_HPG = 4
DIL_HEADS = DIL_HPG * DIL_GROUPS
DIL_WIDTH = DIL_HEADS * HEAD_DIM
DIL_OUT_WIDTH = DIL_HPG * HEAD_DIM
DIL_BLOCK = 128
LSE_LANES = HEAD_DIM // DIL_HPG
ATTN_SCALE = HEAD_DIM ** -0.5
REL_BUCKETS = 32
REL_MAX_DIST = 2048
N_GROUPS = 4
EXPERTS_PER_GROUP = 8
N_EXPERTS = N_GROUPS * EXPERTS_PER_GROUP
TOP_K = 2
NORM_EPS = 1e-6
PAST_LEN = 16384

COL_QR = 0
COL_KR = COL_QR + RET_WIDTH
COL_VR = COL_KR + RET_WIDTH
COL_GR = COL_VR + RET_WIDTH
COL_QA = COL_GR + RET_WIDTH
COL_KA = COL_QA + DIL_WIDTH
COL_VA = COL_KA + DIL_WIDTH
COL_GATES = COL_VA + DIL_WIDTH

MOE_TILE = 256
ROW_TILE = 512
ROUTER_TILE = 256
MIB = 1 << 20
BF16 = jnp.bfloat16
F32 = jnp.float32


def _params(semantics, vmem_mib):
    return pltpu.CompilerParams(dimension_semantics=semantics, vmem_limit_bytes=vmem_mib * MIB)


def _head_cols(h):
    return slice(h * HEAD_DIM, (h + 1) * HEAD_DIM)


def _rmsnorm_kernel(x_ref, g_ref, o_ref):
    x = x_ref[...]
    y = x * lax.rsqrt(jnp.mean(x * x, axis=-1, keepdims=True) + NORM_EPS)
    o_ref[...] = (y * g_ref[...]).astype(o_ref.dtype)


def _rmsnorm(x, g, tm):
    n, d = x.shape
    return pl.pallas_call(
        _rmsnorm_kernel,
        out_shape=jax.ShapeDtypeStruct((n, d), BF16),
        grid=(n // tm,),
        in_specs=[pl.BlockSpec((tm, d), lambda i: (i, 0)), pl.BlockSpec((1, d), lambda i: (0, 0))],
        out_specs=pl.BlockSpec((tm, d), lambda i: (i, 0)),
        compiler_params=_params(("parallel",), 40),
        name="rmsnorm",
    )(x, g.reshape(1, d))


def _proj_kernel(u_ref, *refs, epilogue, n_w):
    w_refs = refs[:n_w]
    wb_refs = refs[len(refs) - n_w:]
    rest = refs[n_w:len(refs) - n_w]

    @pl.when(pl.program_id(1) == 0)
    def _():
        for w_ref, wb_ref in zip(w_refs, wb_refs):
            wb_ref[...] = w_ref[...].astype(BF16)

    u = u_ref[...]
    accs = [jnp.dot(u, wb_ref[...], preferred_element_type=F32) for wb_ref in wb_refs]
    epilogue(accs, *rest)


def _proj(u, w_in, col_offsets, ncol_blocks, tn, tm, epilogue, extra, extra_specs, out_shape, out_specs,
          scratch, name, vmem_mib=52):
    n, k = u.shape
    w_specs = [pl.BlockSpec((pl.Element(k), pl.Element(tn)),
                            functools.partial(lambda j, i, o: (0, pl.multiple_of(o + j * tn, HEAD_DIM)), o=o))
               for o in col_offsets]
    return pl.pallas_call(
        functools.partial(_proj_kernel, epilogue=epilogue, n_w=len(col_offsets)),
        out_shape=out_shape,
        grid=(ncol_blocks, n // tm),
        in_specs=[pl.BlockSpec((tm, k), lambda j, i: (i, 0))] + w_specs + list(extra_specs),
        out_specs=out_specs,
        scratch_shapes=list(scratch) + [pltpu.VMEM((k, tn), BF16) for _ in col_offsets],
        compiler_params=_params(("arbitrary", "arbitrary"), vmem_mib),
        name=name,
    )(u, *([w_in] * len(col_offsets)), *extra)


def _epi_rotary(accs, tab_ref, o_ref):
    acc = accs[0]
    c = tab_ref[0]
    s = tab_ref[1]
    for h in range(acc.shape[1] // HEAD_DIM):
        xh = acc[:, _head_cols(h)]
        o_ref[:, _head_cols(h)] = (xh * c + pltpu.roll(xh, HEAD_DIM // 2, 1) * s).astype(o_ref.dtype)


def _epi_value_gate(accs, o_ref):
    acc = accs[0]
    j = pl.program_id(0)

    @pl.when(j == 0)
    def _():
        o_ref[...] = acc.astype(o_ref.dtype)

    @pl.when(j == 1)
    def _():
        o_ref[...] = (acc * jax.nn.sigmoid(acc)).astype(o_ref.dtype)


def _epi_sigmoid(accs, o_ref):
    o_ref[...] = jax.nn.sigmoid(accs[0]).astype(o_ref.dtype)


def _head_rms(xh, g):
    return xh * lax.rsqrt(jnp.mean(xh * xh, axis=-1, keepdims=True) + NORM_EPS) * g


def _epi_dilated_prompt(accs, qn_ref, kn_ref, q_ref, k_ref, v_ref, kt_ref, vt_ref, scr, *, r, blocks_per_seq,
                        tail_blocks):
    tm = scr.shape[1]
    ib = pl.program_id(1) % blocks_per_seq
    tail_rows = kt_ref.shape[0]
    gains = (qn_ref[...] * ATTN_SCALE, kn_ref[...], None)
    for acc, gain, out_ref, tail_ref in zip(accs, gains, (q_ref, k_ref, v_ref), (None, kt_ref, vt_ref)):
        for h in range(DIL_HPG):
            xh = acc[:, _head_cols(h)]
            scr[h] = xh if gain is None else _head_rms(xh, gain)
        if tail_ref is not None:
            @pl.when(ib >= blocks_per_seq - tail_blocks)
            def _():
                for h in range(DIL_HPG):
                    tail_ref[:, _head_cols(h)] = scr[h, tm - tail_rows:tm, :]
        for h in range(DIL_HPG):
            if r == 1:
                out_ref[:, _head_cols(h)] = scr[h].astype(out_ref.dtype)
            else:
                for c in range(r):
                    out_ref[c, :, _head_cols(h)] = scr[h, pl.ds(c, tm // r, stride=r), :].astype(out_ref.dtype)


def _epi_dilated_sample(accs, qn_ref, kn_ref, o_ref):
    acc = accs[0]
    j = pl.program_id(0)
    gain = jnp.where(j < DIL_GROUPS, qn_ref[...] * ATTN_SCALE, kn_ref[...])

    @pl.when(j < 2 * DIL_GROUPS)
    def _():
        for h in range(DIL_HPG):
            o_ref[:, _head_cols(h)] = _head_rms(acc[:, _head_cols(h)], gain)

    @pl.when(j >= 2 * DIL_GROUPS)
    def _():
        o_ref[...] = acc


def _rotary_tables(pos, rows):
    half = HEAD_DIM // 2
    inv = ROPE_BASE ** (-jnp.arange(half, dtype=F32) / half)
    ang = pos.astype(F32)[:, None] * inv[None, :]
    cos = jnp.cos(ang)
    sin = jnp.sin(ang)
    c = jnp.concatenate([cos, cos], axis=-1)
    s = jnp.concatenate([-sin, sin], axis=-1)
    tab = jnp.stack([jnp.stack([c, s]), jnp.stack([c, s]) * (HEAD_DIM ** -0.5)])
    reps = rows // pos.shape[0]
    return jnp.tile(tab, (1, 1, reps, 1))


def _proj_retention_and_gates(u, w_in, pos_tab, tm):
    n = u.shape[0]
    nblk = pos_tab.shape[2] // tm
    tile_out = lambda: pl.BlockSpec((tm, RET_WIDTH), lambda j, i: (i, j))
    qk_r = _proj(u, w_in, [COL_QR], 2, RET_WIDTH, tm, _epi_rotary, [pos_tab],
                 [pl.BlockSpec((None, 2, tm, HEAD_DIM), lambda j, i: (j, 0, i % nblk, 0))],
                 jax.ShapeDtypeStruct((n, 2 * RET_WIDTH), BF16), tile_out(), [], "proj_qk_ret")
    vg_r = _proj(u, w_in, [COL_VR], 2, RET_WIDTH, tm, _epi_value_gate, [], [],
                 jax.ShapeDtypeStruct((n, 2 * RET_WIDTH), BF16), tile_out(), [], "proj_vg_ret")
    ngate = w_in.shape[1] - COL_GATES
    gates = _proj(u, w_in, [COL_GATES], ngate // RET_WIDTH, RET_WIDTH, tm, _epi_sigmoid, [], [],
                  jax.ShapeDtypeStruct((n, ngate), BF16), tile_out(), [], "proj_gates")
    return qk_r, vg_r, gates


def _proj_dilated_prompt(u, w_in, q_norm, k_norm, gi, b, l, tm):
    w, r = DIL_PAIRS[gi]
    lw = min(w, l)
    bps = l // tm
    tail_rows = min(lw, tm)
    tail_blocks = lw // tail_rows
    cm = (b, r, l // r, DIL_OUT_WIDTH)
    if r == 1:
        cm_shape = jax.ShapeDtypeStruct((b * l, DIL_OUT_WIDTH), BF16)
        cm_spec = lambda: pl.BlockSpec((tm, DIL_OUT_WIDTH), lambda j, i: (i, 0))
    else:
        cm_shape = jax.ShapeDtypeStruct(cm, BF16)
        cm_spec = lambda: pl.BlockSpec((None, r, tm // r, DIL_OUT_WIDTH), lambda j, i: (i // bps, 0, i % bps, 0))
    tail_shape = jax.ShapeDtypeStruct((b, lw, DIL_OUT_WIDTH), F32)
    tail_spec = lambda: pl.BlockSpec(
        (None, tail_rows, DIL_OUT_WIDTH),
        lambda j, i: (i // bps, jnp.maximum(i % bps - (bps - tail_blocks), 0), 0))
    gspec = pl.BlockSpec((1, HEAD_DIM), lambda j, i: (0, 0))
    off = gi * DIL_OUT_WIDTH
    q, k, v, kt, vt = _proj(
        u, w_in, [COL_QA + off, COL_KA + off, COL_VA + off], 1, DIL_OUT_WIDTH, tm,
        functools.partial(_epi_dilated_prompt, r=r, blocks_per_seq=bps, tail_blocks=tail_blocks),
        [q_norm.reshape(1, HEAD_DIM), k_norm.reshape(1, HEAD_DIM)], [gspec, gspec],
        (cm_shape, cm_shape, cm_shape, tail_shape, tail_shape),
        (cm_spec(), cm_spec(), cm_spec(), tail_spec(), tail_spec()),
        [pltpu.VMEM((DIL_HPG, tm, HEAD_DIM), F32)], "proj_dilated_prompt")
    cls = lambda t: t.reshape(b * r, l // r, DIL_OUT_WIDTH)
    return cls(q), cls(k), cls(v), kt, vt


def _proj_dilated_sample(u, w_in, q_norm, k_norm):
    n = u.shape[0]
    gspec = pl.BlockSpec((1, HEAD_DIM), lambda j, i: (0, 0))
    return _proj(u, w_in, [COL_QA], 3 * DIL_GROUPS, DIL_OUT_WIDTH, n, _epi_dilated_sample,
                 [q_norm.reshape(1, HEAD_DIM), k_norm.reshape(1, HEAD_DIM)], [gspec, gspec],
                 jax.ShapeDtypeStruct((n, 3 * DIL_WIDTH), F32),
                 pl.BlockSpec((n, DIL_OUT_WIDTH), lambda j, i: (i, j)), [], "proj_dilated_sample")


def _retention_kernel(q_ref, k_ref, v_ref, g_ref, s0_ref, dm_ref, qd_ref, kd_ref, cd_ref,
                      y_ref, sf_ref, state, *, chunk, nchunks):
    t = pl.program_id(2)

    @pl.when(t == 0)
    def _():
        state[...] = s0_ref[...]

    dmask = dm_ref[...]
    qdec = qd_ref[...]
    kdec = kd_ref[...]
    cdec = cd_ref[...]
    for ci in range(nchunks):
        rows = pl.ds(ci * chunk, chunk)
        q = q_ref[rows, :]
        k = k_ref[rows, :]
        v = v_ref[rows, :]
        s_prev = state[...]
        sc = lax.dot_general(q, k, (((1,), (1,)), ((), ())), preferred_element_type=F32) * dmask
        intra = jnp.dot(sc.astype(BF16), v, preferred_element_type=F32)
        qd = (q.astype(F32) * qdec).astype(BF16)
        cross = jnp.dot(qd, s_prev.astype(BF16), preferred_element_type=F32)
        kd = (k.astype(F32) * kdec).astype(BF16)
        kv = lax.dot_general(kd, v, (((0,), (0,)), ((), ())), preferred_element_type=F32)
        state[...] = s_prev * cdec + kv
        o = intra + cross
        mu = jnp.mean(o, axis=-1, keepdims=True)
        oc = o - mu
        var = jnp.mean(oc * oc, axis=-1, keepdims=True)
        y = g_ref[rows, :].astype(F32) * (oc * lax.rsqrt(var + GN_EPS))
        y_ref[rows, :] = y.astype(y_ref.dtype)

    @pl.when(t == pl.num_programs(2) - 1)
    def _():
        sf_ref[...] = state[...]


def _retention_decay(chunk, valid):
    lg = jnp.log1p(-jnp.exp2(-5.0 - jnp.arange(RET_HEADS, dtype=F32)))
    idx = jnp.arange(chunk, dtype=F32)
    rel = idx[:, None] - idx[None, :]
    dmask = jnp.where(rel[None] >= 0, jnp.exp(lg[:, None, None] * jnp.maximum(rel, 0.0)[None]), 0.0)
    qdec = jnp.exp(lg[:, None] * (idx + 1.0)[None, :])[..., None]
    kdec = jnp.exp(lg[:, None] * (valid - 1.0 - idx)[None, :])[..., None]
    kdec = jnp.where((idx < valid)[None, :, None], kdec, 0.0)
    cdec = jnp.exp(lg * valid)[:, None, None]
    return dmask, qdec, kdec, cdec


def _retention_prompt(qk_r, vg_r, s0, chunk, rows_per_step):
    b, l, _ = qk_r.shape
    h = RET_HEADS
    dmask, qdec, kdec, cdec = _retention_decay(chunk, chunk)
    nsteps = l // rows_per_step
    blk = lambda off: pl.BlockSpec((None, rows_per_step, HEAD_DIM), lambda bi, hi, ti: (bi, ti, hi + off))
    per_head = lambda shape: pl.BlockSpec((None,) + shape, lambda bi, hi, ti: (hi,) + (0,) * len(shape))
    state_spec = pl.BlockSpec((None, None, HEAD_DIM, HEAD_DIM), lambda bi, hi, ti: (bi, hi, 0, 0))
    return pl.pallas_call(
        functools.partial(_retention_kernel, chunk=chunk, nchunks=rows_per_step // chunk),
        out_shape=(jax.ShapeDtypeStruct((b, l, RET_WIDTH), BF16),
                   jax.ShapeDtypeStruct((b, h, HEAD_DIM, HEAD_DIM), F32)),
        grid=(b, h, nsteps),
        in_specs=[blk(0), blk(h), blk(0), blk(h), state_spec,
                  per_head((chunk, chunk)), per_head((chunk, 1)), per_head((chunk, 1)), per_head((1, 1))],
        out_specs=(blk(0), state_spec),
        scratch_shapes=[pltpu.VMEM((HEAD_DIM, HEAD_DIM), F32)],
        compiler_params=_params(("parallel", "parallel", "arbitrary"), 32),
        name="retention_prompt",
    )(qk_r, qk_r, vg_r, vg_r, s0, dmask, qdec, kdec, cdec)


def _retention_sample_kernel(q_ref, k_ref, v_ref, g_ref, s0_ref, dm_ref, qd_ref, kd_ref, cd_ref,
                             y_ref, sf_ref, *, ntok):
    for h in range(RET_HEADS):
        q = q_ref[h]
        k = k_ref[h]
        v = v_ref[h]
        s_prev = s0_ref[h]
        dmask = dm_ref[h]
        o = jnp.dot(q * qd_ref[h], s_prev, preferred_element_type=F32)
        for j in range(ntok):
            sj = jnp.sum(q * k[j:j + 1, :], axis=-1, keepdims=True) * dmask[:, j:j + 1]
            o = o + sj * v[j:j + 1, :]
        kd = k * kd_ref[h]
        kv = lax.dot_general(kd, v, (((0,), (0,)), ((), ())), preferred_element_type=F32)
        sf_ref[h] = s_prev * cd_ref[h] + kv
        mu = jnp.mean(o, axis=-1, keepdims=True)
        oc = o - mu
        var = jnp.mean(oc * oc, axis=-1, keepdims=True)
        y_ref[h] = g_ref[h] * (oc * lax.rsqrt(var + GN_EPS))


def _retention_sample(q, k, v, g, s0, ntok):
    bd, h, tp, _ = q.shape
    dmask, qdec, kdec, cdec = _retention_decay(tp, ntok)
    tok = pl.BlockSpec((None, h, tp, HEAD_DIM), lambda bi: (bi, 0, 0, 0))
    st = pl.BlockSpec((None, h, HEAD_DIM, HEAD_DIM), lambda bi: (bi, 0, 0, 0))
    const = lambda a: pl.BlockSpec(a.shape, lambda bi: (0,) * a.ndim)
    return pl.pallas_call(
        functools.partial(_retention_sample_kernel, ntok=ntok),
        out_shape=(jax.ShapeDtypeStruct((bd, h, tp, HEAD_DIM), F32),
                   jax.ShapeDtypeStruct((bd, h, HEAD_DIM, HEAD_DIM), F32)),
        grid=(bd,),
        in_specs=[tok, tok, tok, tok, st, const(dmask), const(qdec), const(kdec), const(cdec)],
        out_specs=(tok, st),
        compiler_params=_params(("parallel",), 32),
        name="retention_sample",
    )(q, k, v, g, s0, dmask, qdec, kdec, cdec)


def _t5_bucket(dist):
    max_exact = REL_BUCKETS // 2
    d = jnp.maximum(dist, 0)
    df = jnp.maximum(d, 1).astype(F32)
    large = max_exact + (jnp.log(df / max_exact) / math.log(REL_MAX_DIST / max_exact)
                         * (REL_BUCKETS - max_exact)).astype(jnp.int32)
    large = jnp.minimum(large, REL_BUCKETS - 1)
    return jnp.where(d < max_exact, d, large)


def _bias_lookup(tab, dist):
    onehot = _t5_bucket(dist)[..., None] == jnp.arange(REL_BUCKETS)
    return jnp.sum(jnp.where(onehot[..., None], tab.astype(F32), 0.0), axis=-2)


def _dilated_kernel(q_ref, kp_ref, kc_ref, vp_ref, vc_ref, bias_ref, o_ref, lse_ref, kfull, vfull, *, nsub):
    i = pl.program_id(1)
    blk = DIL_BLOCK
    kfull[0:blk, :] = kp_ref[...]
    kfull[blk:, :] = kc_ref[...]
    vfull[0:blk, :] = vp_ref[...]
    vfull[blk:, :] = vc_ref[...]
    col = lax.broadcasted_iota(jnp.int32, (blk, 2 * blk), 1)
    lane_head = lax.broadcasted_iota(jnp.int32, (blk, HEAD_DIM), 1) // LSE_LANES
    for s in range(nsub):
        rows = pl.ds(s * blk, blk)
        win = pl.ds(s * blk, 2 * blk)
        lse_tile = jnp.zeros((blk, HEAD_DIM), F32)
        for h in range(DIL_HPG):
            cols = pl.ds(h * HEAD_DIM, HEAD_DIM)
            q = q_ref[rows, cols]
            kw = kfull[win, cols]
            vw = vfull[win, cols]
            sc = lax.dot_general(q, kw, (((1,), (1,)), ((), ())), preferred_element_type=F32) + bias_ref[h]
            if s == 0:
                sc = jnp.where((col >= blk) | (i > 0), sc, -jnp.inf)
            m = jnp.max(sc, axis=-1, keepdims=True)
            p = jnp.exp(sc - m)
            l = jnp.sum(p, axis=-1, keepdims=True)
            o = jnp.dot(p.astype(BF16), vw, preferred_element_type=F32) / l
            o_ref[rows, cols] = o.astype(o_ref.dtype)
            lse_tile = jnp.where(lane_head == h, m + jnp.log(l), lse_tile)
        lse_ref[rows, :] = lse_tile


def _dilated_bias(bias_tab, r):
    blk = DIL_BLOCK
    qi = jnp.arange(blk)[:, None]
    kj = jnp.arange(2 * blk)[None, :]
    dc = blk + qi - kj
    band = (dc >= 0) & (dc <= blk)
    bias = _bias_lookup(bias_tab, dc * r).transpose(2, 0, 1)
    return jnp.where(band[None], bias, -jnp.inf)


def _dilated_prompt(q, k, v, bias, nsub):
    n, lc, w = q.shape
    blk = DIL_BLOCK
    tq = nsub * blk
    cur = pl.BlockSpec((None, tq, w), lambda ni, i: (ni, i, 0))
    prev = pl.BlockSpec((None, blk, w), lambda ni, i: (ni, jnp.maximum(i * nsub - 1, 0), 0))
    return pl.pallas_call(
        functools.partial(_dilated_kernel, nsub=nsub),
        out_shape=(jax.ShapeDtypeStruct((n, lc, w), F32),
                   jax.ShapeDtypeStruct((n, lc, HEAD_DIM), F32)),
        grid=(n, lc // tq),
        in_specs=[cur, prev, cur, prev, cur, pl.BlockSpec(bias.shape, lambda ni, i: (0, 0, 0))],
        out_specs=(cur, pl.BlockSpec((None, tq, HEAD_DIM), lambda ni, i: (ni, i, 0))),
        scratch_shapes=[pltpu.VMEM((tq + blk, w), BF16), pltpu.VMEM((tq + blk, w), BF16)],
        compiler_params=_params(("parallel", "parallel"), 32),
        name="dilated_prompt",
    )(q, k, k, v, v, bias)


def _dilated_sample_kernel(q_ref, kn_ref, vn_ref, c0_ref, c1_ref, c2_ref, bc_ref, bn_ref,
                           o_ref, lse_ref, *, ntok):
    caches = (c0_ref, c1_ref, c2_ref)
    for gi in range(DIL_GROUPS):
        cache = caches[gi]
        nclass = cache.shape[1]
        for t in range(ntok):
            cls = t if nclass > 1 else 0
            qt = q_ref[t, gi]
            kc = cache[:, cls, 0]
            vc = cache[:, cls, 1]
            sc = jnp.sum(kc * qt[None], axis=-1, keepdims=True) + bc_ref[gi, t]
            sn = jnp.sum(kn_ref[:, gi] * qt[None], axis=-1, keepdims=True) + bn_ref[gi, t]
            m = jnp.maximum(jnp.max(sc, axis=0), jnp.max(sn, axis=0))
            pc = jnp.exp(sc - m[None])
            pn = jnp.exp(sn - m[None])
            l = jnp.sum(pc, axis=0) + jnp.sum(pn, axis=0)
            o = jnp.sum(pc * vc, axis=0) + jnp.sum(pn * vn_ref[:, gi], axis=0)
            o_ref[t, gi] = o / l
            lse_ref[t, gi] = m + jnp.log(l)


def _dilated_sample(q, kn, vn, caches, bias_cache, bias_new, ntok):
    bd = q.shape[0]
    tok = pl.BlockSpec((None, ntok, DIL_GROUPS, DIL_HPG, HEAD_DIM), lambda bi: (bi, 0, 0, 0, 0))
    cspecs = []
    for c in caches:
        ncls = min(c.shape[2], ntok)
        cspecs.append(pl.BlockSpec((None, DIL_BLOCK, ncls, 2, DIL_HPG, HEAD_DIM),
                                   lambda bi: (bi, 0, 0, 0, 0, 0)))
    const = lambda a: pl.BlockSpec(a.shape, lambda bi: (0,) * a.ndim)
    return pl.pallas_call(
        functools.partial(_dilated_sample_kernel, ntok=ntok),
        out_shape=(jax.ShapeDtypeStruct(q.shape, F32),
                   jax.ShapeDtypeStruct((bd, ntok, DIL_GROUPS, DIL_HPG, 1), F32)),
        grid=(bd,),
        in_specs=[tok, tok, tok] + cspecs + [const(bias_cache), const(bias_new)],
        out_specs=(tok, pl.BlockSpec((None, ntok, DIL_GROUPS, DIL_HPG, 1), lambda bi: (bi, 0, 0, 0, 0))),
        compiler_params=_params(("parallel",), 32),
        name="dilated_sample",
    )(q, kn, vn, *caches, bias_cache, bias_new)


def _dilated_sample_bias(rel_bias, ntok):
    bc, bn = [], []
    m = jnp.arange(DIL_BLOCK)
    tn = jnp.arange(ntok)
    for gi, (_, r) in enumerate(DIL_PAIRS):
        tab = rel_bias[:, gi * DIL_HPG:(gi + 1) * DIL_HPG]
        rows_c, rows_n = [], []
        for t in range(ntok):
            if r == 1:
                jc = DIL_BLOCK + t - m
                okc = m >= t
                jn = t - tn
                okn = tn <= t
            else:
                jc = DIL_BLOCK - m
                okc = jnp.ones_like(m, bool)
                jn = jnp.zeros_like(tn)
                okn = tn == t
            rows_c.append(jnp.where(okc[:, None], _bias_lookup(tab, jc * r), -jnp.inf))
            rows_n.append(jnp.where(okn[:, None], _bias_lookup(tab, jn * r), -jnp.inf))
        bc.append(jnp.stack(rows_c))
        bn.append(jnp.stack(rows_n))
    return jnp.stack(bc)[..., None], jnp.stack(bn)[..., None]


def _cache_shift_kernel(c_ref, n_ref, o_ref, *, ntok):
    i = pl.program_id(1)
    last = pl.num_programs(1) - 1
    rows = o_ref.shape[0]

    @pl.when(i < last)
    def _():
        o_ref[...] = c_ref[0]

    @pl.when(i == last)
    def _():
        o_ref[0:rows - ntok] = c_ref[0, ntok:rows]
        o_ref[rows - ntok:rows] = n_ref[...]


def _cache_shift(cache, new, ntok):
    bd, lb = cache.shape[:2]
    tail = cache.shape[2:]
    rows = min(lb, 512)
    zeros = (0,) * len(tail)
    return pl.pallas_call(
        functools.partial(_cache_shift_kernel, ntok=ntok),
        out_shape=jax.ShapeDtypeStruct(cache.shape, cache.dtype),
        grid=(bd, lb // rows),
        in_specs=[pl.BlockSpec(tuple(pl.Element(s) for s in (1, rows) + tail),
                               lambda b, i: (b, jnp.minimum(i * rows + ntok, lb - rows)) + zeros),
                  pl.BlockSpec((None, ntok) + tail, lambda b, i: (b, 0) + zeros)],
        out_specs=pl.BlockSpec((None, rows) + tail, lambda b, i: (b, i) + zeros),
        compiler_params=_params(("parallel", "arbitrary"), 40),
        name="cache_shift",
    )(cache, new)


def _merge_kernel(o0_ref, o1_ref, o2_ref, l0_ref, l1_ref, l2_ref, yr_ref, gt_ref, wr_ref, wd_ref,
                  out_ref, oa_ref, *nat, dilations):
    tm = out_ref.shape[0]
    o_nat, l_nat = [], []
    for g, (o_ref, l_ref, r) in enumerate(zip((o0_ref, o1_ref, o2_ref), (l0_ref, l1_ref, l2_ref), dilations)):
        if r == 1:
            o_nat.append([o_ref[:, _head_cols(h)] for h in range(DIL_HPG)])
            l_nat.append(l_ref[...])
        else:
            on, ln = nat[2 * g], nat[2 * g + 1]
            for c in range(r):
                ln[pl.ds(c, tm // r, stride=r), :] = l_ref[c]
                for h in range(DIL_HPG):
                    on[h, pl.ds(c, tm // r, stride=r), :] = o_ref[c, :, _head_cols(h)]
            o_nat.append([on[h] for h in range(DIL_HPG)])
            l_nat.append(ln[...])
    m = jnp.maximum(jnp.maximum(l_nat[0], l_nat[1]), l_nat[2])
    es = [jnp.exp(l - m) for l in l_nat]
    den = es[0] + es[1] + es[2]
    for h in range(DIL_HPG):
        lane = h * LSE_LANES
        acc = None
        for g in range(DIL_GROUPS):
            term = (es[g][:, lane:lane + 1] / den[:, lane:lane + 1]) * o_nat[g][h]
            acc = term if acc is None else acc + term
        oa_ref[:, _head_cols(h)] = acc.astype(oa_ref.dtype)
    ya = jnp.dot(oa_ref[...], wd_ref[...], preferred_element_type=F32)
    yr = jnp.dot(yr_ref[...], wr_ref[...], preferred_element_type=F32)
    d = ya.shape[1]
    out_ref[...] = (gt_ref[:, :d].astype(F32) * yr + gt_ref[:, d:].astype(F32) * ya).astype(out_ref.dtype)


def _merge(o_g, lse_g, dilations, b, l, yr_in, gates, w_ret_b, w_dil_b, tm):
    n = yr_in.shape[0]
    d = w_ret_b.shape[1]
    bps = l // tm
    row = lambda w: pl.BlockSpec((tm, w), lambda i: (i, 0))
    const = lambda a: pl.BlockSpec(a.shape, lambda i: (0, 0))

    def group_specs(width):
        specs = []
        for r in dilations:
            if r == 1:
                specs.append(row(width))
            else:
                specs.append(pl.BlockSpec((None, r, tm // r, width), lambda i: (i // bps, 0, i % bps, 0)))
        return specs

    view = lambda t, r, width: t.reshape(n, width) if r == 1 else t.reshape(b, r, l // r, width)
    o_in = [view(t, r, DIL_OUT_WIDTH) for t, r in zip(o_g, dilations)]
    l_in = [view(t, r, HEAD_DIM) for t, r in zip(lse_g, dilations)]
    scratch = [pltpu.VMEM((tm, DIL_OUT_WIDTH), BF16)]
    for _ in dilations:
        scratch += [pltpu.VMEM((DIL_HPG, tm, HEAD_DIM), F32), pltpu.VMEM((tm, HEAD_DIM), F32)]
    return pl.pallas_call(
        functools.partial(_merge_kernel, dilations=tuple(dilations)),
        out_shape=jax.ShapeDtypeStruct((n, d), BF16),
        grid=(n // tm,),
        in_specs=group_specs(DIL_OUT_WIDTH) + group_specs(HEAD_DIM) + [row(RET_WIDTH), row(2 * d),
                                                                      const(w_ret_b), const(w_dil_b)],
        out_specs=row(d),
        scratch_shapes=scratch,
        compiler_params=_params(("parallel",), 48),
        name="merge_branches",
    )(*o_in, *l_in, yr_in, gates, w_ret_b, w_dil_b)


ROUTER_ROWS = 8 + N_EXPERTS


def _outproj_router_kernel(mg_ref, x_ref, wo_ref, g2_ref, wr_ref, br_ref, tri_ref, cin_ref,
                           h_ref, u2_ref, ei_ref, wt_ref, cnt_ref):
    @pl.when(pl.program_id(0) == 0)
    def _():
        cnt_ref[...] = cin_ref[...]

    h = x_ref[...] + jnp.dot(mg_ref[...], wo_ref[...], preferred_element_type=F32)
    h_ref[...] = h
    u2 = h * lax.rsqrt(jnp.mean(h * h, axis=-1, keepdims=True) + NORM_EPS) * g2_ref[...]
    u2_ref[...] = u2
    lt = lax.dot_general(wr_ref[...], u2.astype(BF16), (((1,), (1,)), ((), ())),
                         preferred_element_type=F32) + br_ref[...]
    lg = lt[0:N_GROUPS]
    gmax = jnp.max(lg, axis=0, keepdims=True)
    w_coarse = 1.0 / jnp.sum(jnp.exp(lg - gmax), axis=0, keepdims=True)
    gid = lax.broadcasted_iota(jnp.int32, lg.shape, 0)
    gsel = jnp.min(jnp.where(lg == gmax, gid, N_GROUPS), axis=0, keepdims=True)
    le = jnp.zeros((EXPERTS_PER_GROUP, lt.shape[1]), F32)
    for g in range(N_GROUPS):
        le = jnp.where(gsel == g, lt[8 + g * EXPERTS_PER_GROUP:8 + (g + 1) * EXPERTS_PER_GROUP], le)
    eid = lax.broadcasted_iota(jnp.int32, le.shape, 0)
    v1 = jnp.max(le, axis=0, keepdims=True)
    i1 = jnp.min(jnp.where(le == v1, eid, EXPERTS_PER_GROUP), axis=0, keepdims=True)
    le2 = jnp.where(eid == i1, -jnp.inf, le)
    v2 = jnp.max(le2, axis=0, keepdims=True)
    i2 = jnp.min(jnp.where(le2 == v2, eid, EXPERTS_PER_GROUP), axis=0, keepdims=True)
    e21 = jnp.exp(v2 - v1)
    w1 = w_coarse / (1.0 + e21)
    w2 = w_coarse * e21 / (1.0 + e21)
    e1 = gsel * EXPERTS_PER_GROUP + i1
    e2 = gsel * EXPERTS_PER_GROUP + i2

    xid = lax.broadcasted_iota(jnp.int32, (N_EXPERTS, lt.shape[1]), 0)
    oh1 = jnp.where(xid == e1, 1.0, 0.0)
    oh2 = jnp.where(xid == e2, 1.0, 0.0)
    n1 = jnp.sum(oh1, axis=1, keepdims=True)
    n2 = jnp.sum(oh2, axis=1, keepdims=True)
    before = cnt_ref[...]
    p1 = jnp.dot(oh1.astype(BF16), tri_ref[...], preferred_element_type=F32) + before
    p2 = jnp.dot(oh2.astype(BF16), tri_ref[...], preferred_element_type=F32) + before + n1
    rank1 = jnp.sum(oh1 * p1, axis=0, keepdims=True)
    rank2 = jnp.sum(oh2 * p2, axis=0, keepdims=True)
    cnt_ref[...] = before + n1 + n2

    row = lax.broadcasted_iota(jnp.int32, ei_ref.shape, 0)
    ei_ref[...] = jnp.where(row == 0, e1, jnp.where(row == 1, e2, jnp.where(
        row == 2, rank1.astype(jnp.int32), jnp.where(row == 3, rank2.astype(jnp.int32), 0))))
    wt_ref[...] = jnp.where(row == 0, w1, jnp.where(row == 1, w2, 0.0))


def _outproj_router(merged, x, w_o_b, norm_ffn, wr_t, br_t, counts_in, tm):
    n, d = x.shape
    tri = (jnp.arange(tm)[:, None] < jnp.arange(tm)[None, :]).astype(BF16)
    row = lambda: pl.BlockSpec((tm, d), lambda i: (i, 0))
    lane = lambda: pl.BlockSpec((8, tm), lambda i: (0, i))
    const = lambda a: pl.BlockSpec(a.shape, lambda i: (0, 0))
    return pl.pallas_call(
        _outproj_router_kernel,
        out_shape=(jax.ShapeDtypeStruct((n, d), F32), jax.ShapeDtypeStruct((n, d), F32),
                   jax.ShapeDtypeStruct((8, n), jnp.int32), jax.ShapeDtypeStruct((8, n), F32),
                   jax.ShapeDtypeStruct((N_EXPERTS, 1), F32)),
        grid=(n // tm,),
        in_specs=[row(), row(), const(w_o_b), pl.BlockSpec((1, d), lambda i: (0, 0)),
                  const(wr_t), const(br_t), const(tri), const(counts_in)],
        out_specs=(row(), row(), lane(), lane(), const(counts_in)),
        compiler_params=_params(("arbitrary",), 56),
        name="outproj_router",
    )(merged, x, w_o_b, norm_ffn.reshape(1, d), wr_t, br_t, tri, counts_in)


def _router_weights(w_rg, b_rg, w_re, b_re):
    d = w_rg.shape[0]
    wr = jnp.zeros((ROUTER_ROWS, d), F32)
    wr = wr.at[0:N_GROUPS].set(w_rg.T)
    wr = wr.at[8:].set(w_re.transpose(0, 2, 1).reshape(N_EXPERTS, d))
    br = jnp.zeros((ROUTER_ROWS, 1), F32)
    br = br.at[0:N_GROUPS, 0].set(b_rg)
    br = br.at[8:, 0].set(b_re.reshape(N_EXPERTS))
    return wr.astype(BF16), br


DMA_UNROLL = 8


def _dispatch_kernel(pos_ref, u_ref, xs_in_ref, xs_ref, sem, *, tm, ntok_total):
    del xs_in_ref
    i = pl.program_id(0)

    def row_copy(r, kk):
        slot = pos_ref[kk * ntok_total + i * tm + r]
        return pltpu.make_async_copy(u_ref.at[pl.ds(r, 1)], xs_ref.at[pl.ds(slot, 1)], sem)

    def issue(r, c):
        row_copy(r, 0).start()
        row_copy(r, 1).start()
        return c

    lax.fori_loop(0, tm, issue, 0, unroll=DMA_UNROLL)

    def drain(r, c):
        row_copy(r, 0).wait()
        row_copy(r, 1).wait()
        return c

    lax.fori_loop(0, tm, drain, 0, unroll=DMA_UNROLL)


def _dispatch(pos_flat, u2, xs, tm):
    n, d = u2.shape
    grid_spec = pltpu.PrefetchScalarGridSpec(
        num_scalar_prefetch=1,
        grid=(n // tm,),
        in_specs=[pl.BlockSpec((tm, d), lambda i, pos: (i, 0)), pl.BlockSpec(memory_space=pl.ANY)],
        out_specs=pl.BlockSpec(memory_space=pl.ANY),
        scratch_shapes=[pltpu.SemaphoreType.DMA(())],
    )
    return pl.pallas_call(
        functools.partial(_dispatch_kernel, tm=tm, ntok_total=n),
        out_shape=jax.ShapeDtypeStruct(xs.shape, xs.dtype),
        grid_spec=grid_spec,
        input_output_aliases={2: 0},
        compiler_params=_params(("arbitrary",), 32),
        name="moe_dispatch",
    )(pos_flat, u2, xs)


def _gmm_kernel(te_ref, nt_ref, x_ref, wg_ref, wu_ref, wd_ref, o_ref, wgb, wub, wdb):
    i = pl.program_id(0)
    fresh = jnp.logical_or(i == 0, te_ref[i] != te_ref[jnp.maximum(i - 1, 0)])

    @pl.when(jnp.logical_and(fresh, i < nt_ref[0]))
    def _():
        wgb[...] = wg_ref[...].astype(BF16)
        wub[...] = wu_ref[...].astype(BF16)
        wdb[...] = wd_ref[...].astype(BF16)

    @pl.when(i < nt_ref[0])
    def _():
        x = x_ref[...].astype(BF16)
        a = jnp.dot(x, wgb[...], preferred_element_type=F32)
        b = jnp.dot(x, wub[...], preferred_element_type=F32)
        hm = (a * jax.nn.sigmoid(a) * b).astype(BF16)
        o_ref[...] = jnp.dot(hm, wdb[...], preferred_element_type=F32)

    @pl.when(i >= nt_ref[0])
    def _():
        o_ref[...] = jnp.zeros_like(o_ref)


def _gmm(tile_expert, ntiles_used, xs, w_gate, w_up, w_down):
    npad, d = xs.shape
    f = w_gate.shape[-1]
    ntiles = npad // MOE_TILE
    epg = w_gate.shape[1]

    def xmap(i, te, nt):
        return (jnp.minimum(i, nt[0] - 1), 0)

    def wmap(i, te, nt):
        e = te[i]
        return (e // epg, e % epg, 0, 0)

    grid_spec = pltpu.PrefetchScalarGridSpec(
        num_scalar_prefetch=2,
        grid=(ntiles,),
        in_specs=[pl.BlockSpec((MOE_TILE, d), xmap),
                  pl.BlockSpec((None, None, d, f), wmap),
                  pl.BlockSpec((None, None, d, f), wmap),
                  pl.BlockSpec((None, None, f, d), wmap)],
        out_specs=pl.BlockSpec((MOE_TILE, d), lambda i, te, nt: (i, 0)),
        scratch_shapes=[pltpu.VMEM((d, f), BF16), pltpu.VMEM((d, f), BF16), pltpu.VMEM((f, d), BF16)],
    )
    return pl.pallas_call(
        _gmm_kernel,
        out_shape=jax.ShapeDtypeStruct((npad, d), F32),
        grid_spec=grid_spec,
        compiler_params=_params(("arbitrary",), 56),
        name="moe_grouped_matmul",
    )(tile_expert, ntiles_used, xs, w_gate, w_up, w_down)


def _combine_kernel(pos_ref, h_ref, wt_ref, os_ref, y_ref, g0, g1, sem, *, tm, ntok_total):
    i = pl.program_id(0)

    def row_copy(r, kk, buf):
        slot = pos_ref[kk * ntok_total + i * tm + r]
        return pltpu.make_async_copy(os_ref.at[pl.ds(slot, 1)], buf.at[pl.ds(r, 1)], sem)

    def issue(r, c):
        row_copy(r, 0, g0).start()
        row_copy(r, 1, g1).start()
        return c

    lax.fori_loop(0, tm, issue, 0, unroll=DMA_UNROLL)

    def drain(r, c):
        row_copy(r, 0, g0).wait()
        row_copy(r, 1, g1).wait()
        return c

    lax.fori_loop(0, tm, drain, 0, unroll=DMA_UNROLL)
    y_ref[...] = h_ref[...] + wt_ref[:, 0:1] * g0[...] + wt_ref[:, 1:2] * g1[...]


def _combine(pos_flat, h, wt_cols, out_sorted, tm):
    n, d = h.shape
    grid_spec = pltpu.PrefetchScalarGridSpec(
        num_scalar_prefetch=1,
        grid=(n // tm,),
        in_specs=[pl.BlockSpec((tm, d), lambda i, pos: (i, 0)),
                  pl.BlockSpec((tm, 8), lambda i, pos: (i, 0)),
                  pl.BlockSpec(memory_space=pl.ANY)],
        out_specs=pl.BlockSpec((tm, d), lambda i, pos: (i, 0)),
        scratch_shapes=[pltpu.VMEM((tm, d), F32), pltpu.VMEM((tm, d), F32), pltpu.SemaphoreType.DMA(())],
    )
    return pl.pallas_call(
        functools.partial(_combine_kernel, tm=tm, ntok_total=n),
        out_shape=jax.ShapeDtypeStruct((n, d), F32),
        grid_spec=grid_spec,
        compiler_params=_params(("arbitrary",), 32),
        name="moe_combine",
    )(pos_flat, h, wt_cols, out_sorted)


def _moe_plan(counts, route_sets):
    npairs = sum(e.shape[1] for e, _ in route_sets) * TOP_K
    tiles = (counts + MOE_TILE - 1) // MOE_TILE
    tile_end = jnp.cumsum(tiles)
    offs = (tile_end - tiles) * MOE_TILE
    ids = jnp.arange(N_EXPERTS)
    slots = []
    for experts, ranks in route_sets:
        base = jnp.sum(jnp.where(experts[..., None] == ids, offs, 0), axis=-1)
        slots.append((base + ranks).reshape(-1).astype(jnp.int32))
    ntiles = (npairs + N_EXPERTS * (MOE_TILE - 1)) // MOE_TILE
    tile_expert = jnp.sum(tile_end[None, :] <= jnp.arange(ntiles)[:, None], axis=1)
    tile_expert = jnp.minimum(tile_expert, N_EXPERTS - 1).astype(jnp.int32)
    return slots, tile_expert, tile_end[-1:].astype(jnp.int32), ntiles


def _mixers_prompt(x, wts, counts_in):
    b, l, d = x.shape
    n = b * l
    tm = ROW_TILE
    xf = x.reshape(n, d)
    u = _rmsnorm(xf, wts["norm_attn"], tm)
    tab = _rotary_tables(jnp.arange(l), l)
    qk_r, vg_r, gates = _proj_retention_and_gates(u, wts["w_in"], tab, tm)

    s0 = jnp.zeros((b, RET_HEADS, HEAD_DIM, HEAD_DIM), F32)
    yr_in, s_fin = _retention_prompt(qk_r.reshape(b, l, -1), vg_r.reshape(b, l, -1), s0, 256, min(l, 1024))

    o_g, lse_g, bufs = [], [], []
    for gi, (w, r) in enumerate(DIL_PAIRS):
        qg, kg, vg, kt, vt = _proj_dilated_prompt(u, wts["w_in"], wts["q_norm"], wts["k_norm"], gi, b, l, tm)
        bias = _dilated_bias(wts["rel_bias"][:, gi * DIL_HPG:(gi + 1) * DIL_HPG], r)
        o, lse = _dilated_prompt(qg, kg, vg, bias, min(4, l // r // DIL_BLOCK))
        o_g.append(o)
        lse_g.append(lse)
        bufs.append(jnp.stack([kt, vt], axis=2).reshape(b, kt.shape[1], 2, DIL_HPG, HEAD_DIM))

    merged = _merge(o_g, lse_g, [r for _, r in DIL_PAIRS], b, l, yr_in.reshape(n, RET_WIDTH), gates,
                    wts["w_ret_out"], wts["w_dil_out"], tm)
    h, u2, ei, wt, counts = _outproj_router(merged, xf, wts["w_o"], wts["norm_ffn"], wts["wr_t"], wts["br_t"],
                                            counts_in, ROUTER_TILE)
    return h, u2, ei, wt, counts, s_fin, bufs


def _mixers_sample(x, caches, state, wts, counts_in):
    bd, t, d = x.shape
    n = bd * t
    xf = x.reshape(n, d)
    u = _rmsnorm(xf, wts["norm_attn"], n)
    tab = _rotary_tables(PAST_LEN + jnp.arange(t), n)
    qk_r, vg_r, gates = _proj_retention_and_gates(u, wts["w_in"], tab, n)
    qkv_a = _proj_dilated_sample(u, wts["w_in"], wts["q_norm"], wts["k_norm"])

    def heads(a):
        a = a.astype(F32).reshape(bd, t, RET_HEADS, HEAD_DIM).transpose(0, 2, 1, 3)
        return jnp.pad(a, ((0, 0), (0, 0), (0, 8 - t), (0, 0)))

    y_r, s_new = _retention_sample(heads(qk_r[:, :RET_WIDTH]), heads(qk_r[:, RET_WIDTH:]),
                                   heads(vg_r[:, :RET_WIDTH]), heads(vg_r[:, RET_WIDTH:]), state, t)
    yr_in = y_r[:, :, :t].transpose(0, 2, 1, 3).reshape(n, RET_WIDTH).astype(BF16)

    grp = lambda a: a.reshape(bd, t, DIL_GROUPS, DIL_HPG, HEAD_DIM)
    qa, ka, va = (grp(qkv_a[:, s * DIL_WIDTH:(s + 1) * DIL_WIDTH]) for s in range(3))
    cviews = [c.reshape(bd, DIL_BLOCK, r, 2, DIL_HPG, HEAD_DIM) for c, (_, r) in zip(caches, DIL_PAIRS)]
    bias_c, bias_n = _dilated_sample_bias(wts["rel_bias"], t)
    o, lse = _dilated_sample(qa, ka, va, cviews, bias_c, bias_n, t)
    o_g = [o[:, :, gi].reshape(n, 1, DIL_OUT_WIDTH) for gi in range(DIL_GROUPS)]
    lse_g = [jnp.repeat(lse[:, :, gi].reshape(n, DIL_HPG), LSE_LANES, axis=1).reshape(n, 1, HEAD_DIM)
             for gi in range(DIL_GROUPS)]

    news = [jnp.stack([ka[:, :, gi], va[:, :, gi]], axis=2) for gi in range(DIL_GROUPS)]
    bufs = [_cache_shift(c, nw, t) for c, nw in zip(caches, news)]

    merged = _merge(o_g, lse_g, [1] * DIL_GROUPS, 1, n, yr_in, gates, wts["w_ret_out"], wts["w_dil_out"], n)
    h, u2, ei, wt, counts = _outproj_router(merged, xf, wts["w_o"], wts["norm_ffn"], wts["wr_t"], wts["br_t"],
                                            counts_in, n)
    return h, u2, ei, wt, counts, s_new, bufs


def kernel(x_prompt, x_sample, cache_kv_g0, cache_kv_g1, cache_kv_g2, state_ret, norm_attn, w_in, q_norm,
           k_norm, rel_bias, w_ret_out, w_dil_out, w_o, norm_ffn, w_router_group, b_router_group,
           w_router_expert, b_router_expert, w_gate, w_up, w_down):
    caches = (cache_kv_g0, cache_kv_g1, cache_kv_g2)
    ntok = x_sample.shape[1]
    for c, (w, r) in zip(caches, DIL_PAIRS):
        assert c.shape[1] == w == DIL_BLOCK * r and (r == 1 or ntok <= r) and ntok <= 8
    wr_t, br_t = _router_weights(w_router_group, b_router_group, w_router_expert, b_router_expert)
    wts = dict(norm_attn=norm_attn, w_in=w_in, q_norm=q_norm, k_norm=k_norm, rel_bias=rel_bias,
               w_ret_out=w_ret_out.astype(BF16), w_dil_out=w_dil_out.astype(BF16), w_o=w_o.astype(BF16),
               norm_ffn=norm_ffn, wr_t=wr_t, br_t=br_t)

    zero_counts = jnp.zeros((N_EXPERTS, 1), F32)
    hp, u2p, eip, wtp, counts_p, s_p, bufs_p = _mixers_prompt(x_prompt, wts, zero_counts)
    hs, u2s, eis, wts_s, counts, s_s, bufs_s = _mixers_sample(x_sample, caches, state_ret, wts, counts_p)

    slots, tile_expert, ntiles_used, ntiles = _moe_plan(
        counts[:, 0].astype(jnp.int32), [(eip[0:2], eip[2:4]), (eis[0:2], eis[2:4])])
    d = hp.shape[1]
    xs = jnp.zeros((ntiles * MOE_TILE, d), F32)
    xs = _dispatch(slots[0], u2p, xs, 256)
    xs = _dispatch(slots[1], u2s, xs, hs.shape[0])
    out_sorted = _gmm(tile_expert, ntiles_used, xs, w_gate, w_up, w_down)
    yp = _combine(slots[0], hp, wtp.T, out_sorted, 256)
    ys = _combine(slots[1], hs, wts_s.T, out_sorted, hs.shape[0])

    return (yp.reshape(x_prompt.shape), ys.reshape(x_sample.shape), bufs_p[0], bufs_p[1], bufs_p[2], s_p,
            bufs_s[0], bufs_s[1], bufs_s[2], s_s)
```

```python
import functools
import math

import jax
import jax.numpy as jnp
from jax import lax
from jax.experimental import pallas as pl
from jax.experimental.pallas import tpu as pltpu

HEAD_DIM = 128
RET_HEADS = 8
RET_WIDTH = RET_HEADS * HEAD_DIM
ROPE_BASE = 10000.0
GN_EPS = 1e-5
DIL_PAIRS = ((128, 1), (512, 4), (2048, 16))
DIL_GROUPS = len(DIL_PAIRS)
DIL_HPG = 4
DIL_HEADS = DIL_HPG * DIL_GROUPS
DIL_WIDTH = DIL_HEADS * HEAD_DIM
DIL_OUT_WIDTH = DIL_HPG * HEAD_DIM
DIL_BLOCK = 128
LSE_LANES = HEAD_DIM // DIL_HPG
ATTN_SCALE = HEAD_DIM ** -0.5
REL_BUCKETS = 32
REL_MAX_DIST = 2048
N_GROUPS = 4
EXPERTS_PER_GROUP = 8
N_EXPERTS = N_GROUPS * EXPERTS_PER_GROUP
TOP_K = 2
NORM_EPS = 1e-6
PAST_LEN = 16384

COL_QR = 0
COL_KR = COL_QR + RET_WIDTH
COL_VR = COL_KR + RET_WIDTH
COL_GR = COL_VR + RET_WIDTH
COL_QA = COL_GR + RET_WIDTH
COL_KA = COL_QA + DIL_WIDTH
COL_VA = COL_KA + DIL_WIDTH
COL_GATES = COL_VA + DIL_WIDTH

MOE_TILE = 256
ROW_TILE = 512
PROJ_CHUNK = 256
ROUTER_TILE = 256
MIB = 1 << 20
BF16 = jnp.bfloat16
F32 = jnp.float32


def _params(semantics, vmem_mib):
    return pltpu.CompilerParams(dimension_semantics=semantics, vmem_limit_bytes=vmem_mib * MIB)


def _head_cols(h):
    return slice(h * HEAD_DIM, (h + 1) * HEAD_DIM)


def _rmsnorm_kernel(x_ref, g_ref, o_ref):
    x = x_ref[...]
    y = x * lax.rsqrt(jnp.mean(x * x, axis=-1, keepdims=True) + NORM_EPS)
    o_ref[...] = (y * g_ref[...]).astype(o_ref.dtype)


def _rmsnorm(x, g, tm):
    n, d = x.shape
    return pl.pallas_call(
        _rmsnorm_kernel,
        out_shape=jax.ShapeDtypeStruct((n, d), BF16),
        grid=(n // tm,),
        in_specs=[pl.BlockSpec((tm, d), lambda i: (i, 0)), pl.BlockSpec((1, d), lambda i: (0, 0))],
        out_specs=pl.BlockSpec((tm, d), lambda i: (i, 0)),
        compiler_params=_params(("parallel",), 40),
        name="rmsnorm",
    )(x, g.reshape(1, d))


def _proj_kernel(u_ref, *refs, epilogue, n_w):
    w_refs = refs[:n_w]
    wb_refs = refs[len(refs) - n_w:]
    rest = refs[n_w:len(refs) - n_w]

    @pl.when(pl.program_id(1) == 0)
    def _():
        for w_ref, wb_ref in zip(w_refs, wb_refs):
            wb_ref[...] = w_ref[...].astype(BF16)

    tn = wb_refs[0].shape[1]
    for wi, wb_ref in enumerate(wb_refs):
        for c0 in range(0, tn, PROJ_CHUNK):
            acc = jnp.dot(u_ref[...], wb_ref[:, c0:c0 + PROJ_CHUNK], preferred_element_type=F32)
            epilogue(wi, c0, acc, *rest)


def _proj(u, w_in, col_offsets, ncol_blocks, tn, tm, epilogue, extra, extra_specs, out_shape, out_specs,
          scratch, name, vmem_mib=52):
    n, k = u.shape
    w_specs = [pl.BlockSpec((pl.Element(k), pl.Element(tn)),
                            functools.partial(lambda j, i, o: (0, pl.multiple_of(o + j * tn, HEAD_DIM)), o=o))
               for o in col_offsets]
    return pl.pallas_call(
        functools.partial(_proj_kernel, epilogue=epilogue, n_w=len(col_offsets)),
        out_shape=out_shape,
        grid=(ncol_blocks, n // tm),
        in_specs=[pl.BlockSpec((tm, k), lambda j, i: (i, 0))] + w_specs + list(extra_specs),
        out_specs=out_specs,
        scratch_shapes=list(scratch) + [pltpu.VMEM((k, tn), BF16) for _ in col_offsets],
        compiler_params=_params(("arbitrary", "arbitrary"), vmem_mib),
        name=name,
    )(u, *([w_in] * len(col_offsets)), *extra)


def _chunk_heads(c0, acc):
    return [(slice(c0 + h * HEAD_DIM, c0 + (h + 1) * HEAD_DIM), acc[:, _head_cols(h)])
            for h in range(acc.shape[1] // HEAD_DIM)]


def _epi_rotary(wi, c0, acc, tab_ref, o_ref):
    c = tab_ref[0]
    s = tab_ref[1]
    for cols, xh in _chunk_heads(c0, acc):
        o_ref[:, cols] = (xh * c + pltpu.roll(xh, HEAD_DIM // 2, 1) * s).astype(o_ref.dtype)


def _epi_value_gate(wi, c0, acc, o_ref):
    is_gate = pl.program_id(0) == 1
    o_ref[:, c0:c0 + acc.shape[1]] = jnp.where(is_gate, acc * jax.nn.sigmoid(acc), acc).astype(o_ref.dtype)


def _epi_sigmoid(wi, c0, acc, o_ref):
    o_ref[:, c0:c0 + acc.shape[1]] = jax.nn.sigmoid(acc).astype(o_ref.dtype)


def _head_rms(xh, g):
    return xh * lax.rsqrt(jnp.mean(xh * xh, axis=-1, keepdims=True) + NORM_EPS) * g


def _epi_dilated_prompt(wi, c0, acc, qn_ref, kn_ref, q_ref, k_ref, v_ref, kt_ref, vt_ref, scr, *, r):
    tm = scr.shape[2]
    tail_rows = kt_ref.shape[0]
    gain = (qn_ref[...] * ATTN_SCALE, kn_ref[...], None)[wi]
    out_ref = (q_ref, k_ref, v_ref)[wi]
    tail_ref = (None, kt_ref, vt_ref)[wi]
    for cols, xh in _chunk_heads(c0, acc):
        h = cols.start // HEAD_DIM
        y = xh if gain is None else _head_rms(xh, gain)
        slab = scr.at[wi, h]
        slab[...] = y
        if tail_ref is not None:
            tail_ref[:, cols] = y[tm - tail_rows:tm, :]
        if r == 1:
            out_ref[:, cols] = y.astype(out_ref.dtype)
        else:
            for c in range(r):
                out_ref[c, :, cols] = slab[pl.ds(c, tm // r, stride=r), :].astype(out_ref.dtype)


def _epi_dilated_sample(wi, c0, acc, qn_ref, kn_ref, o_ref):
    j = pl.program_id(0)
    gain = jnp.where(j < DIL_GROUPS, qn_ref[...] * ATTN_SCALE, kn_ref[...])
    for cols, xh in _chunk_heads(c0, acc):
        o_ref[:, cols] = jnp.where(j < 2 * DIL_GROUPS, _head_rms(xh, gain), xh)


def _rotary_tables(pos, rows):
    half = HEAD_DIM // 2
    inv = ROPE_BASE ** (-jnp.arange(half, dtype=F32) / half)
    ang = pos.astype(F32)[:, None] * inv[None, :]
    cos = jnp.cos(ang)
    sin = jnp.sin(ang)
    c = jnp.concatenate([cos, cos], axis=-1)
    s = jnp.concatenate([-sin, sin], axis=-1)
    tab = jnp.stack([jnp.stack([c, s]), jnp.stack([c, s]) * (HEAD_DIM ** -0.5)])
    reps = rows // pos.shape[0]
    return jnp.tile(tab, (1, 1, reps, 1))


def _proj_retention_and_gates(u, w_in, pos_tab, tm):
    n = u.shape[0]
    nblk = pos_tab.shape[2] // tm
    tile_out = lambda: pl.BlockSpec((tm, RET_WIDTH), lambda j, i: (i, j))
    qk_r = _proj(u, w_in, [COL_QR], 2, RET_WIDTH, tm, _epi_rotary, [pos_tab],
                 [pl.BlockSpec((None, 2, tm, HEAD_DIM), lambda j, i: (j, 0, i % nblk, 0))],
                 jax.ShapeDtypeStruct((n, 2 * RET_WIDTH), BF16), tile_out(), [], "proj_qk_ret")
    vg_r = _proj(u, w_in, [COL_VR], 2, RET_WIDTH, tm, _epi_value_gate, [], [],
                 jax.ShapeDtypeStruct((n, 2 * RET_WIDTH), BF16), tile_out(), [], "proj_vg_ret")
    ngate = w_in.shape[1] - COL_GATES
    gates = _proj(u, w_in, [COL_GATES], ngate // RET_WIDTH, RET_WIDTH, tm, _epi_sigmoid, [], [],
                  jax.ShapeDtypeStruct((n, ngate), BF16), tile_out(), [], "proj_gates")
    return qk_r, vg_r, gates


def _proj_dilated_prompt(u, w_in, q_norm, k_norm, gi, b, l, tm):
    w, r = DIL_PAIRS[gi]
    lw = min(w, l)
    bps = l // tm
    tail_rows = min(lw, tm)
    tail_blocks = lw // tail_rows
    cm = (b, r, l // r, DIL_OUT_WIDTH)
    if r == 1:
        cm_shape = jax.ShapeDtypeStruct((b * l, DIL_OUT_WIDTH), BF16)
        cm_spec = lambda: pl.BlockSpec((tm, DIL_OUT_WIDTH), lambda j, i: (i, 0))
    else:
        cm_shape = jax.ShapeDtypeStruct(cm, BF16)
        cm_spec = lambda: pl.BlockSpec((None, r, tm // r, DIL_OUT_WIDTH), lambda j, i: (i // bps, 0, i % bps, 0))
    tail_shape = jax.ShapeDtypeStruct((b, lw, DIL_OUT_WIDTH), F32)
    tail_spec = lambda: pl.BlockSpec(
        (None, tail_rows, DIL_OUT_WIDTH),
        lambda j, i: (i // bps, jnp.maximum(i % bps - (bps - tail_blocks), 0), 0))
    gspec = pl.BlockSpec((1, HEAD_DIM), lambda j, i: (0, 0))
    off = gi * DIL_OUT_WIDTH
    q, k, v, kt, vt = _proj(
        u, w_in, [COL_QA + off, COL_KA + off, COL_VA + off], 1, DIL_OUT_WIDTH, tm,
        functools.partial(_epi_dilated_prompt, r=r),
        [q_norm.reshape(1, HEAD_DIM), k_norm.reshape(1, HEAD_DIM)], [gspec, gspec],
        (cm_shape, cm_shape, cm_shape, tail_shape, tail_shape),
        (cm_spec(), cm_spec(), cm_spec(), tail_spec(), tail_spec()),
        [pltpu.VMEM((3, DIL_HPG, tm, HEAD_DIM), F32)], "proj_dilated_prompt")
    cls = lambda t: t.reshape(b * r, l // r, DIL_OUT_WIDTH)
    return cls(q), cls(k), cls(v), kt, vt


def _proj_dilated_sample(u, w_in, q_norm, k_norm):
    n = u.shape[0]
    gspec = pl.BlockSpec((1, HEAD_DIM), lambda j, i: (0, 0))
    return _proj(u, w_in, [COL_QA], 3 * DIL_GROUPS, DIL_OUT_WIDTH, n, _epi_dilated_sample,
                 [q_norm.reshape(1, HEAD_DIM), k_norm.reshape(1, HEAD_DIM)], [gspec, gspec],
                 jax.ShapeDtypeStruct((n, 3 * DIL_WIDTH), F32),
                 pl.BlockSpec((n, DIL_OUT_WIDTH), lambda j, i: (i, j)), [], "proj_dilated_sample")


def _retention_kernel(q_ref, k_ref, v_ref, g_ref, s0_ref, dm_ref, qd_ref, kd_ref, cd_ref,
                      y_ref, sf_ref, state, *, chunk, nchunks):
    t = pl.program_id(2)

    @pl.when(t == 0)
    def _():
        state[...] = s0_ref[...]

    dmask = dm_ref[...]
    qdec = qd_ref[...]
    kdec = kd_ref[...]
    cdec = cd_ref[...]
    for ci in range(nchunks):
        rows = pl.ds(ci * chunk, chunk)
        q = q_ref[rows, :]
        k = k_ref[rows, :]
        v = v_ref[rows, :]
        s_prev = state[...]
        sc = lax.dot_general(q, k, (((1,), (1,)), ((), ())), preferred_element_type=F32) * dmask
        intra = jnp.dot(sc.astype(BF16), v, preferred_element_type=F32)
        qd = (q.astype(F32) * qdec).astype(BF16)
        cross = jnp.dot(qd, s_prev.astype(BF16), preferred_element_type=F32)
        kd = (k.astype(F32) * kdec).astype(BF16)
        kv = lax.dot_general(kd, v, (((0,), (0,)), ((), ())), preferred_element_type=F32)
        state[...] = s_prev * cdec + kv
        o = intra + cross
        mu = jnp.mean(o, axis=-1, keepdims=True)
        oc = o - mu
        var = jnp.mean(oc * oc, axis=-1, keepdims=True)
        y = g_ref[rows, :].astype(F32) * (oc * lax.rsqrt(var + GN_EPS))
        y_ref[rows, :] = y.astype(y_ref.dtype)

    @pl.when(t == pl.num_programs(2) - 1)
    def _():
        sf_ref[...] = state[...]


def _retention_decay(chunk, valid):
    lg = jnp.log1p(-jnp.exp2(-5.0 - jnp.arange(RET_HEADS, dtype=F32)))
    idx = jnp.arange(chunk, dtype=F32)
    rel = idx[:, None] - idx[None, :]
    dmask = jnp.where(rel[None] >= 0, jnp.exp(lg[:, None, None] * jnp.maximum(rel, 0.0)[None]), 0.0)
    qdec = jnp.exp(lg[:, None] * (idx + 1.0)[None, :])[..., None]
    kdec = jnp.exp(lg[:, None] * (valid - 1.0 - idx)[None, :])[..., None]
    kdec = jnp.where((idx < valid)[None, :, None], kdec, 0.0)
    cdec = jnp.exp(lg * valid)[:, None, None]
    return dmask, qdec, kdec, cdec


def _retention_prompt(qk_r, vg_r, s0, chunk, rows_per_step):
    b, l, _ = qk_r.shape
    h = RET_HEADS
    dmask, qdec, kdec, cdec = _retention_decay(chunk, chunk)
    nsteps = l // rows_per_step
    blk = lambda off: pl.BlockSpec((None, rows_per_step, HEAD_DIM), lambda bi, hi, ti: (bi, ti, hi + off))
    per_head = lambda shape: pl.BlockSpec((None,) + shape, lambda bi, hi, ti: (hi,) + (0,) * len(shape))
    state_spec = pl.BlockSpec((None, None, HEAD_DIM, HEAD_DIM), lambda bi, hi, ti: (bi, hi, 0, 0))
    return pl.pallas_call(
        functools.partial(_retention_kernel, chunk=chunk, nchunks=rows_per_step // chunk),
        out_shape=(jax.ShapeDtypeStruct((b, l, RET_WIDTH), BF16),
                   jax.ShapeDtypeStruct((b, h, HEAD_DIM, HEAD_DIM), F32)),
        grid=(b, h, nsteps),
        in_specs=[blk(0), blk(h), blk(0), blk(h), state_spec,
                  per_head((chunk, chunk)), per_head((chunk, 1)), per_head((chunk, 1)), per_head((1, 1))],
        out_specs=(blk(0), state_spec),
        scratch_shapes=[pltpu.VMEM((HEAD_DIM, HEAD_DIM), F32)],
        compiler_params=_params(("parallel", "parallel", "arbitrary"), 32),
        name="retention_prompt",
    )(qk_r, qk_r, vg_r, vg_r, s0, dmask, qdec, kdec, cdec)


def _retention_sample_kernel(q_ref, k_ref, v_ref, g_ref, s0_ref, dm_ref, qd_ref, kd_ref, cd_ref,
                             y_ref, sf_ref, *, ntok):
    for h in range(RET_HEADS):
        q = q_ref[h]
        k = k_ref[h]
        v = v_ref[h]
        s_prev = s0_ref[h]
        dmask = dm_ref[h]
        o = jnp.dot(q * qd_ref[h], s_prev, preferred_element_type=F32)
        for j in range(ntok):
            sj = jnp.sum(q * k[j:j + 1, :], axis=-1, keepdims=True) * dmask[:, j:j + 1]
            o = o + sj * v[j:j + 1, :]
        kd = k * kd_ref[h]
        kv = lax.dot_general(kd, v, (((0,), (0,)), ((), ())), preferred_element_type=F32)
        sf_ref[h] = s_prev * cd_ref[h] + kv
        mu = jnp.mean(o, axis=-1, keepdims=True)
        oc = o - mu
        var = jnp.mean(oc * oc, axis=-1, keepdims=True)
        y_ref[h] = g_ref[h] * (oc * lax.rsqrt(var + GN_EPS))


def _retention_sample(q, k, v, g, s0, ntok):
    bd, h, tp, _ = q.shape
    dmask, qdec, kdec, cdec = _retention_decay(tp, ntok)
    tok = pl.BlockSpec((None, h, tp, HEAD_DIM), lambda bi: (bi, 0, 0, 0))
    st = pl.BlockSpec((None, h, HEAD_DIM, HEAD_DIM), lambda bi: (bi, 0, 0, 0))
    const = lambda a: pl.BlockSpec(a.shape, lambda bi: (0,) * a.ndim)
    return pl.pallas_call(
        functools.partial(_retention_sample_kernel, ntok=ntok),
        out_shape=(jax.ShapeDtypeStruct((bd, h, tp, HEAD_DIM), F32),
                   jax.ShapeDtypeStruct((bd, h, HEAD_DIM, HEAD_DIM), F32)),
        grid=(bd,),
        in_specs=[tok, tok, tok, tok, st, const(dmask), const(qdec), const(kdec), const(cdec)],
        out_specs=(tok, st),
        compiler_params=_params(("parallel",), 32),
        name="retention_sample",
    )(q, k, v, g, s0, dmask, qdec, kdec, cdec)


def _t5_bucket(dist):
    max_exact = REL_BUCKETS // 2
    d = jnp.maximum(dist, 0)
    df = jnp.maximum(d, 1).astype(F32)
    large = max_exact + (jnp.log(df / max_exact) / math.log(REL_MAX_DIST / max_exact)
                         * (REL_BUCKETS - max_exact)).astype(jnp.int32)
    large = jnp.minimum(large, REL_BUCKETS - 1)
    return jnp.where(d < max_exact, d, large)


def _bias_lookup(tab, dist):
    onehot = _t5_bucket(dist)[..., None] == jnp.arange(REL_BUCKETS)
    return jnp.sum(jnp.where(onehot[..., None], tab.astype(F32), 0.0), axis=-2)


def _dilated_kernel(q_ref, kp_ref, kc_ref, vp_ref, vc_ref, bias_ref, o_ref, lse_ref, kfull, vfull, *, nsub):
    i = pl.program_id(1)
    blk = DIL_BLOCK
    kfull[0:blk, :] = kp_ref[...]
    kfull[blk:, :] = kc_ref[...]
    vfull[0:blk, :] = vp_ref[...]
    vfull[blk:, :] = vc_ref[...]
    col = lax.broadcasted_iota(jnp.int32, (blk, 2 * blk), 1)
    lane_head = lax.broadcasted_iota(jnp.int32, (blk, HEAD_DIM), 1) // LSE_LANES
    for s in range(nsub):
        rows = pl.ds(s * blk, blk)
        win = pl.ds(s * blk, 2 * blk)
        lse_tile = jnp.zeros((blk, HEAD_DIM), F32)
        for h in range(DIL_HPG):
            cols = pl.ds(h * HEAD_DIM, HEAD_DIM)
            q = q_ref[rows, cols]
            kw = kfull[win, cols]
            vw = vfull[win, cols]
            sc = lax.dot_general(q, kw, (((1,), (1,)), ((), ())), preferred_element_type=F32) + bias_ref[h]
            if s == 0:
                sc = jnp.where((col >= blk) | (i > 0), sc, -jnp.inf)
            m = jnp.max(sc, axis=-1, keepdims=True)
            p = jnp.exp(sc - m)
            l = jnp.sum(p, axis=-1, keepdims=True)
            o = jnp.dot(p.astype(BF16), vw, preferred_element_type=F32) / l
            o_ref[rows, cols] = o.astype(o_ref.dtype)
            lse_tile = jnp.where(lane_head == h, m + jnp.log(l), lse_tile)
        lse_ref[rows, :] = lse_tile


def _dilated_bias(bias_tab, r):
    blk = DIL_BLOCK
    qi = jnp.arange(blk)[:, None]
    kj = jnp.arange(2 * blk)[None, :]
    dc = blk + qi - kj
    band = (dc >= 0) & (dc <= blk)
    bias = _bias_lookup(bias_tab, dc * r).transpose(2, 0, 1)
    return jnp.where(band[None], bias, -jnp.inf)


def _dilated_prompt(q, k, v, bias, nsub):
    n, lc, w = q.shape
    blk = DIL_BLOCK
    tq = nsub * blk
    cur = pl.BlockSpec((None, tq, w), lambda ni, i: (ni, i, 0))
    prev = pl.BlockSpec((None, blk, w), lambda ni, i: (ni, jnp.maximum(i * nsub - 1, 0), 0))
    return pl.pallas_call(
        functools.partial(_dilated_kernel, nsub=nsub),
        out_shape=(jax.ShapeDtypeStruct((n, lc, w), F32),
                   jax.ShapeDtypeStruct((n, lc, HEAD_DIM), F32)),
        grid=(n, lc // tq),
        in_specs=[cur, prev, cur, prev, cur, pl.BlockSpec(bias.shape, lambda ni, i: (0, 0, 0))],
        out_specs=(cur, pl.BlockSpec((None, tq, HEAD_DIM), lambda ni, i: (ni, i, 0))),
        scratch_shapes=[pltpu.VMEM((tq + blk, w), BF16), pltpu.VMEM((tq + blk, w), BF16)],
        compiler_params=_params(("parallel", "parallel"), 32),
        name="dilated_prompt",
    )(q, k, k, v, v, bias)


def _dilated_sample_kernel(q_ref, kn_ref, vn_ref, c0_ref, c1_ref, c2_ref, bc_ref, bn_ref,
                           o_ref, lse_ref, *, ntok):
    caches = (c0_ref, c1_ref, c2_ref)
    for gi in range(DIL_GROUPS):
        cache = caches[gi]
        nclass = cache.shape[1]
        for t in range(ntok):
            cls = t if nclass > 1 else 0
            qt = q_ref[t, gi]
            kc = cache[:, cls, 0]
            vc = cache[:, cls, 1]
            sc = jnp.sum(kc * qt[None], axis=-1, keepdims=True) + bc_ref[gi, t]
            sn = jnp.sum(kn_ref[:, gi] * qt[None], axis=-1, keepdims=True) + bn_ref[gi, t]
            m = jnp.maximum(jnp.max(sc, axis=0), jnp.max(sn, axis=0))
            pc = jnp.exp(sc - m[None])
            pn = jnp.exp(sn - m[None])
            l = jnp.sum(pc, axis=0) + jnp.sum(pn, axis=0)
            o = jnp.sum(pc * vc, axis=0) + jnp.sum(pn * vn_ref[:, gi], axis=0)
            o_ref[t, gi] = o / l
            lse_ref[t, gi] = m + jnp.log(l)


def _dilated_sample(q, kn, vn, caches, bias_cache, bias_new, ntok):
    bd = q.shape[0]
    tok = pl.BlockSpec((None, ntok, DIL_GROUPS, DIL_HPG, HEAD_DIM), lambda bi: (bi, 0, 0, 0, 0))
    cspecs = []
    for c in caches:
        ncls = min(c.shape[2], ntok)
        cspecs.append(pl.BlockSpec((None, DIL_BLOCK, ncls, 2, DIL_HPG, HEAD_DIM),
                                   lambda bi: (bi, 0, 0, 0, 0, 0)))
    const = lambda a: pl.BlockSpec(a.shape, lambda bi: (0,) * a.ndim)
    return pl.pallas_call(
        functools.partial(_dilated_sample_kernel, ntok=ntok),
        out_shape=(jax.ShapeDtypeStruct(q.shape, F32),
                   jax.ShapeDtypeStruct((bd, ntok, DIL_GROUPS, DIL_HPG, 1), F32)),
        grid=(bd,),
        in_specs=[tok, tok, tok] + cspecs + [const(bias_cache), const(bias_new)],
        out_specs=(tok, pl.BlockSpec((None, ntok, DIL_GROUPS, DIL_HPG, 1), lambda bi: (bi, 0, 0, 0, 0))),
        compiler_params=_params(("parallel",), 32),
        name="dilated_sample",
    )(q, kn, vn, *caches, bias_cache, bias_new)


def _dilated_sample_bias(rel_bias, ntok):
    bc, bn = [], []
    m = jnp.arange(DIL_BLOCK)
    tn = jnp.arange(ntok)
    for gi, (_, r) in enumerate(DIL_PAIRS):
        tab = rel_bias[:, gi * DIL_HPG:(gi + 1) * DIL_HPG]
        rows_c, rows_n = [], []
        for t in range(ntok):
            if r == 1:
                jc = DIL_BLOCK + t - m
                okc = m >= t
                jn = t - tn
                okn = tn <= t
            else:
                jc = DIL_BLOCK - m
                okc = jnp.ones_like(m, bool)
                jn = jnp.zeros_like(tn)
                okn = tn == t
            rows_c.append(jnp.where(okc[:, None], _bias_lookup(tab, jc * r), -jnp.inf))
            rows_n.append(jnp.where(okn[:, None], _bias_lookup(tab, jn * r), -jnp.inf))
        bc.append(jnp.stack(rows_c))
        bn.append(jnp.stack(rows_n))
    return jnp.stack(bc)[..., None], jnp.stack(bn)[..., None]


def _cache_shift_kernel(c_ref, n_ref, o_ref, *, ntok):
    i = pl.program_id(1)
    last = pl.num_programs(1) - 1
    rows = o_ref.shape[0]

    @pl.when(i < last)
    def _():
        o_ref[...] = c_ref[0]

    @pl.when(i == last)
    def _():
        o_ref[0:rows - ntok] = c_ref[0, ntok:rows]
        o_ref[rows - ntok:rows] = n_ref[...]


def _cache_shift(cache, new, ntok):
    bd, lb = cache.shape[:2]
    tail = cache.shape[2:]
    rows = min(lb, 512)
    zeros = (0,) * len(tail)
    return pl.pallas_call(
        functools.partial(_cache_shift_kernel, ntok=ntok),
        out_shape=jax.ShapeDtypeStruct(cache.shape, cache.dtype),
        grid=(bd, lb // rows),
        in_specs=[pl.BlockSpec(tuple(pl.Element(s) for s in (1, rows) + tail),
                               lambda b, i: (b, jnp.minimum(i * rows + ntok, lb - rows)) + zeros),
                  pl.BlockSpec((None, ntok) + tail, lambda b, i: (b, 0) + zeros)],
        out_specs=pl.BlockSpec((None, rows) + tail, lambda b, i: (b, i) + zeros),
        compiler_params=_params(("parallel", "arbitrary"), 40),
        name="cache_shift",
    )(cache, new)


def _merge_kernel(o0_ref, o1_ref, o2_ref, l0_ref, l1_ref, l2_ref, yr_ref, gt_ref, wr_ref, wd_ref,
                  out_ref, oa_ref, *nat, dilations):
    tm = out_ref.shape[0]
    o_nat, l_nat = [], []
    for g, (o_ref, l_ref, r) in enumerate(zip((o0_ref, o1_ref, o2_ref), (l0_ref, l1_ref, l2_ref), dilations)):
        if r == 1:
            o_nat.append([o_ref[:, _head_cols(h)] for h in range(DIL_HPG)])
            l_nat.append(l_ref[...])
        else:
            on, ln = nat[2 * g], nat[2 * g + 1]
            for c in range(r):
                ln[pl.ds(c, tm // r, stride=r), :] = l_ref[c]
                for h in range(DIL_HPG):
                    on[h, pl.ds(c, tm // r, stride=r), :] = o_ref[c, :, _head_cols(h)]
            o_nat.append([on[h] for h in range(DIL_HPG)])
            l_nat.append(ln[...])
    m = jnp.maximum(jnp.maximum(l_nat[0], l_nat[1]), l_nat[2])
    es = [jnp.exp(l - m) for l in l_nat]
    den = es[0] + es[1] + es[2]
    for h in range(DIL_HPG):
        lane = h * LSE_LANES
        acc = None
        for g in range(DIL_GROUPS):
            term = (es[g][:, lane:lane + 1] / den[:, lane:lane + 1]) * o_nat[g][h]
            acc = term if acc is None else acc + term
        oa_ref[:, _head_cols(h)] = acc.astype(oa_ref.dtype)
    ya = jnp.dot(oa_ref[...], wd_ref[...], preferred_element_type=F32)
    yr = jnp.dot(yr_ref[...], wr_ref[...], preferred_element_type=F32)
    d = ya.shape[1]
    out_ref[...] = (gt_ref[:, :d].astype(F32) * yr + gt_ref[:, d:].astype(F32) * ya).astype(out_ref.dtype)


def _merge(o_g, lse_g, dilations, b, l, yr_in, gates, w_ret_b, w_dil_b, tm):
    n = yr_in.shape[0]
    d = w_ret_b.shape[1]
    bps = l // tm
    row = lambda w: pl.BlockSpec((tm, w), lambda i: (i, 0))
    const = lambda a: pl.BlockSpec(a.shape, lambda i: (0, 0))

    def group_specs(width):
        specs = []
        for r in dilations:
            if r == 1:
                specs.append(row(width))
            else:
                specs.append(pl.BlockSpec((None, r, tm // r, width), lambda i: (i // bps, 0, i % bps, 0)))
        return specs

    view = lambda t, r, width: t.reshape(n, width) if r == 1 else t.reshape(b, r, l // r, width)
    o_in = [view(t, r, DIL_OUT_WIDTH) for t, r in zip(o_g, dilations)]
    l_in = [view(t, r, HEAD_DIM) for t, r in zip(lse_g, dilations)]
    scratch = [pltpu.VMEM((tm, DIL_OUT_WIDTH), BF16)]
    for _ in dilations:
        scratch += [pltpu.VMEM((DIL_HPG, tm, HEAD_DIM), F32), pltpu.VMEM((tm, HEAD_DIM), F32)]
    return pl.pallas_call(
        functools.partial(_merge_kernel, dilations=tuple(dilations)),
        out_shape=jax.ShapeDtypeStruct((n, d), BF16),
        grid=(n // tm,),
        in_specs=group_specs(DIL_OUT_WIDTH) + group_specs(HEAD_DIM) + [row(RET_WIDTH), row(2 * d),
                                                                      const(w_ret_b), const(w_dil_b)],
        out_specs=row(d),
        scratch_shapes=scratch,
        compiler_params=_params(("parallel",), 48),
        name="merge_branches",
    )(*o_in, *l_in, yr_in, gates, w_ret_b, w_dil_b)


ROUTER_ROWS = 8 + N_EXPERTS


def _outproj_router_kernel(mg_ref, x_ref, wo_ref, g2_ref, wr_ref, br_ref, tri_ref, cin_ref,
                           h_ref, u2_ref, ei_ref, wt_ref, cnt_ref):
    @pl.when(pl.program_id(0) == 0)
    def _():
        cnt_ref[...] = cin_ref[...]

    h = x_ref[...] + jnp.dot(mg_ref[...], wo_ref[...], preferred_element_type=F32)
    h_ref[...] = h
    u2 = h * lax.rsqrt(jnp.mean(h * h, axis=-1, keepdims=True) + NORM_EPS) * g2_ref[...]
    u2_ref[...] = u2
    lt = lax.dot_general(wr_ref[...], u2.astype(BF16), (((1,), (1,)), ((), ())),
                         preferred_element_type=F32) + br_ref[...]
    lg = lt[0:N_GROUPS]
    gmax = jnp.max(lg, axis=0, keepdims=True)
    w_coarse = 1.0 / jnp.sum(jnp.exp(lg - gmax), axis=0, keepdims=True)
    gid = lax.broadcasted_iota(jnp.int32, lg.shape, 0)
    gsel = jnp.min(jnp.where(lg == gmax, gid, N_GROUPS), axis=0, keepdims=True)
    le = jnp.zeros((EXPERTS_PER_GROUP, lt.shape[1]), F32)
    for g in range(N_GROUPS):
        le = jnp.where(gsel == g, lt[8 + g * EXPERTS_PER_GROUP:8 + (g + 1) * EXPERTS_PER_GROUP], le)
    eid = lax.broadcasted_iota(jnp.int32, le.shape, 0)
    v1 = jnp.max(le, axis=0, keepdims=True)
    i1 = jnp.min(jnp.where(le == v1, eid, EXPERTS_PER_GROUP), axis=0, keepdims=True)
    le2 = jnp.where(eid == i1, -jnp.inf, le)
    v2 = jnp.max(le2, axis=0, keepdims=True)
    i2 = jnp.min(jnp.where(le2 == v2, eid, EXPERTS_PER_GROUP), axis=0, keepdims=True)
    e21 = jnp.exp(v2 - v1)
    w1 = w_coarse / (1.0 + e21)
    w2 = w_coarse * e21 / (1.0 + e21)
    e1 = gsel * EXPERTS_PER_GROUP + i1
    e2 = gsel * EXPERTS_PER_GROUP + i2

    xid = lax.broadcasted_iota(jnp.int32, (N_EXPERTS, lt.shape[1]), 0)
    oh1 = jnp.where(xid == e1, 1.0, 0.0)
    oh2 = jnp.where(xid == e2, 1.0, 0.0)
    n1 = jnp.sum(oh1, axis=1, keepdims=True)
    n2 = jnp.sum(oh2, axis=1, keepdims=True)
    before = cnt_ref[...]
    p1 = jnp.dot(oh1.astype(BF16), tri_ref[...], preferred_element_type=F32) + before
    p2 = jnp.dot(oh2.astype(BF16), tri_ref[...], preferred_element_type=F32) + before + n1
    rank1 = jnp.sum(oh1 * p1, axis=0, keepdims=True)
    rank2 = jnp.sum(oh2 * p2, axis=0, keepdims=True)
    cnt_ref[...] = before + n1 + n2

    row = lax.broadcasted_iota(jnp.int32, ei_ref.shape, 0)
    ei_ref[...] = jnp.where(row == 0, e1, jnp.where(row == 1, e2, jnp.where(
        row == 2, rank1.astype(jnp.int32), jnp.where(row == 3, rank2.astype(jnp.int32), 0))))
    wt_ref[...] = jnp.where(row == 0, w1, jnp.where(row == 1, w2, 0.0))


def _outproj_router(merged, x, w_o_b, norm_ffn, wr_t, br_t, counts_in, tm):
    n, d = x.shape
    tri = (jnp.arange(tm)[:, None] < jnp.arange(tm)[None, :]).astype(BF16)
    row = lambda: pl.BlockSpec((tm, d), lambda i: (i, 0))
    lane = lambda: pl.BlockSpec((8, tm), lambda i: (0, i))
    const = lambda a: pl.BlockSpec(a.shape, lambda i: (0, 0))
    return pl.pallas_call(
        _outproj_router_kernel,
        out_shape=(jax.ShapeDtypeStruct((n, d), F32), jax.ShapeDtypeStruct((n, d), F32),
                   jax.ShapeDtypeStruct((8, n), jnp.int32), jax.ShapeDtypeStruct((8, n), F32),
                   jax.ShapeDtypeStruct((N_EXPERTS, 1), F32)),
        grid=(n // tm,),
        in_specs=[row(), row(), const(w_o_b), pl.BlockSpec((1, d), lambda i: (0, 0)),
                  const(wr_t), const(br_t), const(tri), const(counts_in)],
        out_specs=(row(), row(), lane(), lane(), const(counts_in)),
        compiler_params=_params(("arbitrary",), 56),
        name="outproj_router",
    )(merged, x, w_o_b, norm_ffn.reshape(1, d), wr_t, br_t, tri, counts_in)


def _router_weights(w_rg, b_rg, w_re, b_re):
    d = w_rg.shape[0]
    wr = jnp.zeros((ROUTER_ROWS, d), F32)
    wr = wr.at[0:N_GROUPS].set(w_rg.T)
    wr = wr.at[8:].set(w_re.transpose(0, 2, 1).reshape(N_EXPERTS, d))
    br = jnp.zeros((ROUTER_ROWS, 1), F32)
    br = br.at[0:N_GROUPS, 0].set(b_rg)
    br = br.at[8:, 0].set(b_re.reshape(N_EXPERTS))
    return wr.astype(BF16), br


DMA_UNROLL = 8


def _dispatch_kernel(pos_p_ref, pos_s_ref, pad_start_ref, pad_count_ref, nt_ref, up_ref, us_ref, xs_ref,
                     zbuf, sem, *, tm):
    i = pl.program_id(0)
    last = pl.num_programs(0) - 1
    ntiles = xs_ref.shape[0] // MOE_TILE

    def pad_copy(e, r):
        return pltpu.make_async_copy(zbuf.at[pl.ds(0, 1)], xs_ref.at[pl.ds(pad_start_ref[e] + r, 1)], sem)

    def tile_copy(t):
        return pltpu.make_async_copy(zbuf, xs_ref.at[pl.ds(pl.multiple_of(t * MOE_TILE, MOE_TILE), MOE_TILE)], sem)

    @pl.when(i == 0)
    def _():
        zbuf[...] = jnp.zeros_like(zbuf)
        for op in ("start", "wait"):
            def per_expert(e, c, op=op):
                lax.fori_loop(0, pad_count_ref[e], lambda r, c2: (getattr(pad_copy(e, r), op)(), c2)[1], 0)
                return c

            lax.fori_loop(0, N_EXPERTS, per_expert, 0)
            lax.fori_loop(nt_ref[0], ntiles, lambda t, c, op=op: (getattr(tile_copy(t), op)(), c)[1], 0)

    def scatter_rows(u_ref, pos_ref, base, nrows, ntok_total):
        def row_copy(r, kk):
            slot = pos_ref[kk * ntok_total + base + r]
            return pltpu.make_async_copy(u_ref.at[pl.ds(r, 1)], xs_ref.at[pl.ds(slot, 1)], sem)

        for op in ("start", "wait"):
            def body(r, c, op=op):
                getattr(row_copy(r, 0), op)()
                getattr(row_copy(r, 1), op)()
                return c

            lax.fori_loop(0, nrows, body, 0, unroll=DMA_UNROLL)

    @pl.when(i < last)
    def _():
        scatter_rows(up_ref, pos_p_ref, i * tm, tm, last * tm)

    @pl.when(i == last)
    def _():
        scatter_rows(us_ref, pos_s_ref, 0, us_ref.shape[0], us_ref.shape[0])


def _dispatch(pos_p, pos_s, pad_start, pad_count, ntiles_used, u2p, u2s, ntiles, tm):
    n, d = u2p.shape
    nsteps = n // tm
    grid_spec = pltpu.PrefetchScalarGridSpec(
        num_scalar_prefetch=5,
        grid=(nsteps + 1,),
        in_specs=[pl.BlockSpec((tm, d), lambda i, *_: (jnp.minimum(i, nsteps - 1), 0)),
                  pl.BlockSpec(u2s.shape, lambda i, *_: (0, 0))],
        out_specs=pl.BlockSpec(memory_space=pl.ANY),
        scratch_shapes=[pltpu.VMEM((MOE_TILE, d), F32), pltpu.SemaphoreType.DMA(())],
    )
    return pl.pallas_call(
        functools.partial(_dispatch_kernel, tm=tm),
        out_shape=jax.ShapeDtypeStruct((ntiles * MOE_TILE, d), F32),
        grid_spec=grid_spec,
        compiler_params=_params(("arbitrary",), 32),
        name="moe_dispatch",
    )(pos_p, pos_s, pad_start, pad_count, ntiles_used, u2p, u2s)


def _gmm_kernel(te_ref, nt_ref, x_ref, wg_ref, wu_ref, wd_ref, o_ref, wgb, wub, wdb):
    i = pl.program_id(0)
    fresh = jnp.logical_or(i == 0, te_ref[i] != te_ref[jnp.maximum(i - 1, 0)])

    @pl.when(jnp.logical_and(fresh, i < nt_ref[0]))
    def _():
        wgb[...] = wg_ref[...].astype(BF16)
        wub[...] = wu_ref[...].astype(BF16)
        wdb[...] = wd_ref[...].astype(BF16)

    @pl.when(i < nt_ref[0])
    def _():
        x = x_ref[...].astype(BF16)
        acc = None
        for c0 in range(0, wgb.shape[1], PROJ_CHUNK):
            cols = slice(c0, c0 + PROJ_CHUNK)
            a = jnp.dot(x, wgb[:, cols], preferred_element_type=F32)
            b = jnp.dot(x, wub[:, cols], preferred_element_type=F32)
            hm = (a * jax.nn.sigmoid(a) * b).astype(BF16)
            part = jnp.dot(hm, wdb[cols, :], preferred_element_type=F32)
            acc = part if acc is None else acc + part
        o_ref[...] = acc

    @pl.when(i >= nt_ref[0])
    def _():
        o_ref[...] = jnp.zeros_like(o_ref)


def _gmm(tile_expert, ntiles_used, xs, w_gate, w_up, w_down):
    npad, d = xs.shape
    f = w_gate.shape[-1]
    ntiles = npad // MOE_TILE
    epg = w_gate.shape[1]

    def xmap(i, te, nt):
        return (jnp.minimum(i, nt[0] - 1), 0)

    def wmap(i, te, nt):
        e = te[i]
        return (e // epg, e % epg, 0, 0)

    grid_spec = pltpu.PrefetchScalarGridSpec(
        num_scalar_prefetch=2,
        grid=(ntiles,),
        in_specs=[pl.BlockSpec((MOE_TILE, d), xmap),
                  pl.BlockSpec((None, None, d, f), wmap),
                  pl.BlockSpec((None, None, d, f), wmap),
                  pl.BlockSpec((None, None, f, d), wmap)],
        out_specs=pl.BlockSpec((MOE_TILE, d), lambda i, te, nt: (i, 0)),
        scratch_shapes=[pltpu.VMEM((d, f), BF16), pltpu.VMEM((d, f), BF16), pltpu.VMEM((f, d), BF16)],
    )
    return pl.pallas_call(
        _gmm_kernel,
        out_shape=jax.ShapeDtypeStruct((npad, d), F32),
        grid_spec=grid_spec,
        compiler_params=_params(("arbitrary",), 56),
        name="moe_grouped_matmul",
    )(tile_expert, ntiles_used, xs, w_gate, w_up, w_down)


def _combine_kernel(pos_ref, h_ref, wt_ref, os_ref, y_ref, g0, g1, sem, *, tm, ntok_total):
    i = pl.program_id(0)
    nsteps = pl.num_programs(0)
    cur = i % 2

    def row_copy(step, r, kk, slot):
        src = pos_ref[kk * ntok_total + step * tm + r]
        buf = (g0, g1)[kk]
        return pltpu.make_async_copy(os_ref.at[pl.ds(src, 1)], buf.at[slot, pl.ds(r, 1)], sem.at[slot])

    def issue_step(step, slot):
        def body(r, c):
            row_copy(step, r, 0, slot).start()
            row_copy(step, r, 1, slot).start()
            return c

        lax.fori_loop(0, tm, body, 0, unroll=DMA_UNROLL)

    @pl.when(i == 0)
    def _():
        issue_step(0, 0)

    @pl.when(i + 1 < nsteps)
    def _():
        issue_step(i + 1, 1 - cur)

    def drain(r, c):
        row_copy(i, r, 0, cur).wait()
        row_copy(i, r, 1, cur).wait()
        return c

    lax.fori_loop(0, tm, drain, 0, unroll=DMA_UNROLL)
    y_ref[...] = h_ref[...] + wt_ref[:, 0:1] * g0[cur] + wt_ref[:, 1:2] * g1[cur]


def _combine(pos_flat, h, wt_cols, out_sorted, tm):
    n, d = h.shape
    grid_spec = pltpu.PrefetchScalarGridSpec(
        num_scalar_prefetch=1,
        grid=(n // tm,),
        in_specs=[pl.BlockSpec((tm, d), lambda i, pos: (i, 0)),
                  pl.BlockSpec((tm, 8), lambda i, pos: (i, 0)),
                  pl.BlockSpec(memory_space=pl.ANY)],
        out_specs=pl.BlockSpec((tm, d), lambda i, pos: (i, 0)),
        scratch_shapes=[pltpu.VMEM((2, tm, d), F32), pltpu.VMEM((2, tm, d), F32),
                        pltpu.SemaphoreType.DMA((2,))],
    )
    return pl.pallas_call(
        functools.partial(_combine_kernel, tm=tm, ntok_total=n),
        out_shape=jax.ShapeDtypeStruct((n, d), F32),
        grid_spec=grid_spec,
        compiler_params=_params(("arbitrary",), 32),
        name="moe_combine",
    )(pos_flat, h, wt_cols, out_sorted)


def _moe_plan(counts, route_sets):
    npairs = sum(e.shape[1] for e, _ in route_sets) * TOP_K
    tiles = (counts + MOE_TILE - 1) // MOE_TILE
    tile_end = jnp.cumsum(tiles)
    offs = (tile_end - tiles) * MOE_TILE
    ids = jnp.arange(N_EXPERTS)
    slots = []
    for experts, ranks in route_sets:
        base = jnp.sum(jnp.where(experts[..., None] == ids, offs, 0), axis=-1)
        slots.append((base + ranks).reshape(-1).astype(jnp.int32))
    ntiles = (npairs + N_EXPERTS * (MOE_TILE - 1)) // MOE_TILE
    tile_expert = jnp.sum(tile_end[None, :] <= jnp.arange(ntiles)[:, None], axis=1)
    tile_expert = jnp.minimum(tile_expert, N_EXPERTS - 1).astype(jnp.int32)
    pad_start = (offs + counts).astype(jnp.int32)
    pad_count = (tiles * MOE_TILE - counts).astype(jnp.int32)
    return slots, tile_expert, tile_end[-1:].astype(jnp.int32), ntiles, pad_start, pad_count


def _mixers_prompt(x, wts, counts_in):
    b, l, d = x.shape
    n = b * l
    tm = ROW_TILE
    xf = x.reshape(n, d)
    u = _rmsnorm(xf, wts["norm_attn"], tm)
    tab = _rotary_tables(jnp.arange(l), l)
    qk_r, vg_r, gates = _proj_retention_and_gates(u, wts["w_in"], tab, 2 * tm)

    s0 = jnp.zeros((b, RET_HEADS, HEAD_DIM, HEAD_DIM), F32)
    yr_in, s_fin = _retention_prompt(qk_r.reshape(b, l, -1), vg_r.reshape(b, l, -1), s0, 256, min(l, 1024))

    o_g, lse_g, bufs = [], [], []
    for gi, (w, r) in enumerate(DIL_PAIRS):
        qg, kg, vg, kt, vt = _proj_dilated_prompt(u, wts["w_in"], wts["q_norm"], wts["k_norm"], gi, b, l, tm)
        bias = _dilated_bias(wts["rel_bias"][:, gi * DIL_HPG:(gi + 1) * DIL_HPG], r)
        o, lse = _dilated_prompt(qg, kg, vg, bias, min(4, l // r // DIL_BLOCK))
        o_g.append(o)
        lse_g.append(lse)
        bufs.append(jnp.stack([kt, vt], axis=2).reshape(b, kt.shape[1], 2, DIL_HPG, HEAD_DIM))

    merged = _merge(o_g, lse_g, [r for _, r in DIL_PAIRS], b, l, yr_in.reshape(n, RET_WIDTH), gates,
                    wts["w_ret_out"], wts["w_dil_out"], tm)
    h, u2, ei, wt, counts = _outproj_router(merged, xf, wts["w_o"], wts["norm_ffn"], wts["wr_t"], wts["br_t"],
                                            counts_in, ROUTER_TILE)
    return h, u2, ei, wt, counts, s_fin, bufs


def _mixers_sample(x, caches, state, wts, counts_in):
    bd, t, d = x.shape
    n = bd * t
    xf = x.reshape(n, d)
    u = _rmsnorm(xf, wts["norm_attn"], n)
    tab = _rotary_tables(PAST_LEN + jnp.arange(t), n)
    qk_r, vg_r, gates = _proj_retention_and_gates(u, wts["w_in"], tab, n)
    qkv_a = _proj_dilated_sample(u, wts["w_in"], wts["q_norm"], wts["k_norm"])

    def heads(a):
        a = a.astype(F32).reshape(bd, t, RET_HEADS, HEAD_DIM).transpose(0, 2, 1, 3)
        return jnp.pad(a, ((0, 0), (0, 0), (0, 8 - t), (0, 0)))

    y_r, s_new = _retention_sample(heads(qk_r[:, :RET_WIDTH]), heads(qk_r[:, RET_WIDTH:]),
                                   heads(vg_r[:, :RET_WIDTH]), heads(vg_r[:, RET_WIDTH:]), state, t)
    yr_in = y_r[:, :, :t].transpose(0, 2, 1, 3).reshape(n, RET_WIDTH).astype(BF16)

    grp = lambda a: a.reshape(bd, t, DIL_GROUPS, DIL_HPG, HEAD_DIM)
    qa, ka, va = (grp(qkv_a[:, s * DIL_WIDTH:(s + 1) * DIL_WIDTH]) for s in range(3))
    cviews = [c.reshape(bd, DIL_BLOCK, r, 2, DIL_HPG, HEAD_DIM) for c, (_, r) in zip(caches, DIL_PAIRS)]
    bias_c, bias_n = _dilated_sample_bias(wts["rel_bias"], t)
    o, lse = _dilated_sample(qa, ka, va, cviews, bias_c, bias_n, t)
    o_g = [o[:, :, gi].reshape(n, 1, DIL_OUT_WIDTH) for gi in range(DIL_GROUPS)]
    lse_g = [jnp.repeat(lse[:, :, gi].reshape(n, DIL_HPG), LSE_LANES, axis=1).reshape(n, 1, HEAD_DIM)
             for gi in range(DIL_GROUPS)]

    news = [jnp.stack([ka[:, :, gi], va[:, :, gi]], axis=2) for gi in range(DIL_GROUPS)]
    bufs = [_cache_shift(c, nw, t) for c, nw in zip(caches, news)]

    merged = _merge(o_g, lse_g, [1] * DIL_GROUPS, 1, n, yr_in, gates, wts["w_ret_out"], wts["w_dil_out"], n)
    h, u2, ei, wt, counts = _outproj_router(merged, xf, wts["w_o"], wts["norm_ffn"], wts["wr_t"], wts["br_t"],
                                            counts_in, n)
    return h, u2, ei, wt, counts, s_new, bufs


def kernel(x_prompt, x_sample, cache_kv_g0, cache_kv_g1, cache_kv_g2, state_ret, norm_attn, w_in, q_norm,
           k_norm, rel_bias, w_ret_out, w_dil_out, w_o, norm_ffn, w_router_group, b_router_group,
           w_router_expert, b_router_expert, w_gate, w_up, w_down):
    caches = (cache_kv_g0, cache_kv_g1, cache_kv_g2)
    ntok = x_sample.shape[1]
    for c, (w, r) in zip(caches, DIL_PAIRS):
        assert c.shape[1] == w == DIL_BLOCK * r and (r == 1 or ntok <= r) and ntok <= 8
    wr_t, br_t = _router_weights(w_router_group, b_router_group, w_router_expert, b_router_expert)
    wts = dict(norm_attn=norm_attn, w_in=w_in, q_norm=q_norm, k_norm=k_norm, rel_bias=rel_bias,
               w_ret_out=w_ret_out.astype(BF16), w_dil_out=w_dil_out.astype(BF16), w_o=w_o.astype(BF16),
               norm_ffn=norm_ffn, wr_t=wr_t, br_t=br_t)

    zero_counts = jnp.zeros((N_EXPERTS, 1), F32)
    hp, u2p, eip, wtp, counts_p, s_p, bufs_p = _mixers_prompt(x_prompt, wts, zero_counts)
    hs, u2s, eis, wts_s, counts, s_s, bufs_s = _mixers_sample(x_sample, caches, state_ret, wts, counts_p)

    slots, tile_expert, ntiles_used, ntiles, pad_start, pad_count = _moe_plan(
        counts[:, 0].astype(jnp.int32), [(eip[0:2], eip[2:4]), (eis[0:2], eis[2:4])])
    xs = _dispatch(slots[0], slots[1], pad_start, pad_count, ntiles_used, u2p, u2s, ntiles, 256)
    out_sorted = _gmm(tile_expert, ntiles_used, xs, w_gate, w_up, w_down)
    yp = _combine(slots[0], hp, wtp.T, out_sorted, 256)
    ys = _combine(slots[1], hs, wts_s.T, out_sorted, hs.shape[0])

    return (yp.reshape(x_prompt.shape), ys.reshape(x_sample.shape), bufs_p[0], bufs_p[1], bufs_p[2], s_p,
            bufs_s[0], bufs_s[1], bufs_s[2], s_s)
```

```python
import functools
import math

import jax
import jax.numpy as jnp
from jax import lax
from jax.experimental import pallas as pl
from jax.experimental.pallas import tpu as pltpu

HEAD_DIM = 128
RET_HEADS = 8
RET_WIDTH = RET_HEADS * HEAD_DIM
ROPE_BASE = 10000.0
GN_EPS = 1e-5
DIL_PAIRS = ((128, 1), (512, 4), (2048, 16))
DIL_GROUPS = len(DIL_PAIRS)
DIL_HPG = 4
DIL_HEADS = DIL_HPG * DIL_GROUPS
DIL_WIDTH = DIL_HEADS * HEAD_DIM
DIL_OUT_WIDTH = DIL_HPG * HEAD_DIM
DIL_BLOCK = 128
LSE_LANES = HEAD_DIM // DIL_HPG
ATTN_SCALE = HEAD_DIM ** -0.5
REL_BUCKETS = 32
REL_MAX_DIST = 2048
N_GROUPS = 4
EXPERTS_PER_GROUP = 8
N_EXPERTS = N_GROUPS * EXPERTS_PER_GROUP
TOP_K = 2
NORM_EPS = 1e-6
PAST_LEN = 16384

COL_QR = 0
COL_KR = COL_QR + RET_WIDTH
COL_VR = COL_KR + RET_WIDTH
COL_GR = COL_VR + RET_WIDTH
COL_QA = COL_GR + RET_WIDTH
COL_KA = COL_QA + DIL_WIDTH
COL_VA = COL_KA + DIL_WIDTH
COL_GATES = COL_VA + DIL_WIDTH

MOE_TILE = 256
ROW_TILE = 512
PROJ_CHUNK = 256
ROUTER_TILE = 512
ROUTER_SUB = 256
MIB = 1 << 20
BF16 = jnp.bfloat16
F32 = jnp.float32


def _params(semantics, vmem_mib):
    return pltpu.CompilerParams(dimension_semantics=semantics, vmem_limit_bytes=vmem_mib * MIB)


def _head_cols(h):
    return slice(h * HEAD_DIM, (h + 1) * HEAD_DIM)


def _rmsnorm_kernel(x_ref, g_ref, o_ref):
    x = x_ref[...]
    y = x * lax.rsqrt(jnp.mean(x * x, axis=-1, keepdims=True) + NORM_EPS)
    o_ref[...] = (y * g_ref[...]).astype(o_ref.dtype)


def _rmsnorm(x, g, tm):
    n, d = x.shape
    return pl.pallas_call(
        _rmsnorm_kernel,
        out_shape=jax.ShapeDtypeStruct((n, d), BF16),
        grid=(n // tm,),
        in_specs=[pl.BlockSpec((tm, d), lambda i: (i, 0)), pl.BlockSpec((1, d), lambda i: (0, 0))],
        out_specs=pl.BlockSpec((tm, d), lambda i: (i, 0)),
        compiler_params=_params(("parallel",), 40),
        name="rmsnorm",
    )(x, g.reshape(1, d))


def _proj_kernel(u_ref, *refs, epilogue, n_w):
    w_refs = refs[:n_w]
    wb_refs = refs[len(refs) - n_w:]
    rest = refs[n_w:len(refs) - n_w]

    @pl.when(pl.program_id(1) == 0)
    def _():
        for w_ref, wb_ref in zip(w_refs, wb_refs):
            wb_ref[...] = w_ref[...].astype(BF16)

    tn = wb_refs[0].shape[1]
    for wi, wb_ref in enumerate(wb_refs):
        for c0 in range(0, tn, PROJ_CHUNK):
            acc = jnp.dot(u_ref[...], wb_ref[:, c0:c0 + PROJ_CHUNK], preferred_element_type=F32)
            epilogue(wi, c0, acc, *rest)


def _proj(u, w_in, col_offsets, ncol_blocks, tn, tm, epilogue, extra, extra_specs, out_shape, out_specs,
          scratch, name, vmem_mib=52):
    n, k = u.shape
    w_specs = [pl.BlockSpec((pl.Element(k), pl.Element(tn)),
                            functools.partial(lambda j, i, o: (0, pl.multiple_of(o + j * tn, HEAD_DIM)), o=o))
               for o in col_offsets]
    return pl.pallas_call(
        functools.partial(_proj_kernel, epilogue=epilogue, n_w=len(col_offsets)),
        out_shape=out_shape,
        grid=(ncol_blocks, n // tm),
        in_specs=[pl.BlockSpec((tm, k), lambda j, i: (i, 0))] + w_specs + list(extra_specs),
        out_specs=out_specs,
        scratch_shapes=list(scratch) + [pltpu.VMEM((k, tn), BF16) for _ in col_offsets],
        compiler_params=_params(("arbitrary", "arbitrary"), vmem_mib),
        name=name,
    )(u, *([w_in] * len(col_offsets)), *extra)


def _chunk_heads(c0, acc):
    return [(slice(c0 + h * HEAD_DIM, c0 + (h + 1) * HEAD_DIM), acc[:, _head_cols(h)])
            for h in range(acc.shape[1] // HEAD_DIM)]


def _epi_rotary(wi, c0, acc, tab_ref, o_ref):
    c = tab_ref[0]
    s = tab_ref[1]
    for cols, xh in _chunk_heads(c0, acc):
        o_ref[:, cols] = (xh * c + pltpu.roll(xh, HEAD_DIM // 2, 1) * s).astype(o_ref.dtype)


def _epi_value_gate(wi, c0, acc, o_ref):
    is_gate = pl.program_id(0) == 1
    o_ref[:, c0:c0 + acc.shape[1]] = jnp.where(is_gate, acc * jax.nn.sigmoid(acc), acc).astype(o_ref.dtype)


def _epi_sigmoid(wi, c0, acc, o_ref):
    o_ref[:, c0:c0 + acc.shape[1]] = jax.nn.sigmoid(acc).astype(o_ref.dtype)


def _head_rms(xh, g):
    return xh * lax.rsqrt(jnp.mean(xh * xh, axis=-1, keepdims=True) + NORM_EPS) * g


def _epi_dilated_prompt(wi, c0, acc, qn_ref, kn_ref, q_ref, k_ref, v_ref, kt_ref, vt_ref, scr, *, r):
    tm = scr.shape[2]
    tail_rows = kt_ref.shape[0]
    gain = (qn_ref[...] * ATTN_SCALE, kn_ref[...], None)[wi]
    out_ref = (q_ref, k_ref, v_ref)[wi]
    tail_ref = (None, kt_ref, vt_ref)[wi]
    for cols, xh in _chunk_heads(c0, acc):
        h = cols.start // HEAD_DIM
        y = xh if gain is None else _head_rms(xh, gain)
        slab = scr.at[wi, h]
        slab[...] = y
        if tail_ref is not None:
            tail_ref[:, cols] = y[tm - tail_rows:tm, :]
        if r == 1:
            out_ref[:, cols] = y.astype(out_ref.dtype)
        else:
            for c in range(r):
                out_ref[c, :, cols] = slab[pl.ds(c, tm // r, stride=r), :].astype(out_ref.dtype)


def _epi_dilated_sample(wi, c0, acc, qn_ref, kn_ref, o_ref):
    j = pl.program_id(0)
    gain = jnp.where(j < DIL_GROUPS, qn_ref[...] * ATTN_SCALE, kn_ref[...])
    for cols, xh in _chunk_heads(c0, acc):
        o_ref[:, cols] = jnp.where(j < 2 * DIL_GROUPS, _head_rms(xh, gain), xh)


def _rotary_tables(pos, rows):
    half = HEAD_DIM // 2
    inv = ROPE_BASE ** (-jnp.arange(half, dtype=F32) / half)
    ang = pos.astype(F32)[:, None] * inv[None, :]
    cos = jnp.cos(ang)
    sin = jnp.sin(ang)
    c = jnp.concatenate([cos, cos], axis=-1)
    s = jnp.concatenate([-sin, sin], axis=-1)
    tab = jnp.stack([jnp.stack([c, s]), jnp.stack([c, s]) * (HEAD_DIM ** -0.5)])
    reps = rows // pos.shape[0]
    return jnp.tile(tab, (1, 1, reps, 1))


def _proj_retention_and_gates(u, w_in, pos_tab, tm):
    n = u.shape[0]
    nblk = pos_tab.shape[2] // tm
    tile_out = lambda: pl.BlockSpec((tm, RET_WIDTH), lambda j, i: (i, j))
    qk_r = _proj(u, w_in, [COL_QR], 2, RET_WIDTH, tm, _epi_rotary, [pos_tab],
                 [pl.BlockSpec((None, 2, tm, HEAD_DIM), lambda j, i: (j, 0, i % nblk, 0))],
                 jax.ShapeDtypeStruct((n, 2 * RET_WIDTH), BF16), tile_out(), [], "proj_qk_ret")
    vg_r = _proj(u, w_in, [COL_VR], 2, RET_WIDTH, tm, _epi_value_gate, [], [],
                 jax.ShapeDtypeStruct((n, 2 * RET_WIDTH), BF16), tile_out(), [], "proj_vg_ret")
    ngate = w_in.shape[1] - COL_GATES
    gates = _proj(u, w_in, [COL_GATES], ngate // RET_WIDTH, RET_WIDTH, tm, _epi_sigmoid, [], [],
                  jax.ShapeDtypeStruct((n, ngate), BF16), tile_out(), [], "proj_gates")
    return qk_r, vg_r, gates


def _proj_dilated_prompt(u, w_in, q_norm, k_norm, gi, b, l, tm):
    w, r = DIL_PAIRS[gi]
    lw = min(w, l)
    bps = l // tm
    tail_rows = min(lw, tm)
    tail_blocks = lw // tail_rows
    cm = (b, r, l // r, DIL_OUT_WIDTH)
    if r == 1:
        cm_shape = jax.ShapeDtypeStruct((b * l, DIL_OUT_WIDTH), BF16)
        cm_spec = lambda: pl.BlockSpec((tm, DIL_OUT_WIDTH), lambda j, i: (i, 0))
    else:
        cm_shape = jax.ShapeDtypeStruct(cm, BF16)
        cm_spec = lambda: pl.BlockSpec((None, r, tm // r, DIL_OUT_WIDTH), lambda j, i: (i // bps, 0, i % bps, 0))
    tail_shape = jax.ShapeDtypeStruct((b, lw, DIL_OUT_WIDTH), F32)
    tail_spec = lambda: pl.BlockSpec(
        (None, tail_rows, DIL_OUT_WIDTH),
        lambda j, i: (i // bps, jnp.maximum(i % bps - (bps - tail_blocks), 0), 0))
    gspec = pl.BlockSpec((1, HEAD_DIM), lambda j, i: (0, 0))
    off = gi * DIL_OUT_WIDTH
    q, k, v, kt, vt = _proj(
        u, w_in, [COL_QA + off, COL_KA + off, COL_VA + off], 1, DIL_OUT_WIDTH, tm,
        functools.partial(_epi_dilated_prompt, r=r),
        [q_norm.reshape(1, HEAD_DIM), k_norm.reshape(1, HEAD_DIM)], [gspec, gspec],
        (cm_shape, cm_shape, cm_shape, tail_shape, tail_shape),
        (cm_spec(), cm_spec(), cm_spec(), tail_spec(), tail_spec()),
        [pltpu.VMEM((3, DIL_HPG, tm, HEAD_DIM), F32)], "proj_dilated_prompt")
    cls = lambda t: t.reshape(b * r, l // r, DIL_OUT_WIDTH)
    return cls(q), cls(k), cls(v), kt, vt


def _proj_dilated_sample(u, w_in, q_norm, k_norm):
    n = u.shape[0]
    gspec = pl.BlockSpec((1, HEAD_DIM), lambda j, i: (0, 0))
    return _proj(u, w_in, [COL_QA], 3 * DIL_GROUPS, DIL_OUT_WIDTH, n, _epi_dilated_sample,
                 [q_norm.reshape(1, HEAD_DIM), k_norm.reshape(1, HEAD_DIM)], [gspec, gspec],
                 jax.ShapeDtypeStruct((n, 3 * DIL_WIDTH), F32),
                 pl.BlockSpec((n, DIL_OUT_WIDTH), lambda j, i: (i, j)), [], "proj_dilated_sample")


def _retention_kernel(q_ref, k_ref, v_ref, g_ref, s0_ref, dm_ref, qd_ref, kd_ref, cd_ref,
                      y_ref, sf_ref, state, *, chunk, nchunks):
    t = pl.program_id(2)

    @pl.when(t == 0)
    def _():
        state[...] = s0_ref[...]

    dmask = dm_ref[...]
    qdec = qd_ref[...]
    kdec = kd_ref[...]
    cdec = cd_ref[...]
    s_prev = state[...]
    for ci in range(nchunks):
        rows = pl.ds(ci * chunk, chunk)
        q = q_ref[rows, :]
        k = k_ref[rows, :]
        v = v_ref[rows, :]
        sc = lax.dot_general(q, k, (((1,), (1,)), ((), ())), preferred_element_type=F32) * dmask
        intra = jnp.dot(sc.astype(BF16), v, preferred_element_type=F32)
        qd = (q.astype(F32) * qdec).astype(BF16)
        cross = jnp.dot(qd, s_prev.astype(BF16), preferred_element_type=F32)
        kd = (k.astype(F32) * kdec).astype(BF16)
        kv = lax.dot_general(kd, v, (((0,), (0,)), ((), ())), preferred_element_type=F32)
        s_prev = s_prev * cdec + kv
        o = intra + cross
        mu = jnp.mean(o, axis=-1, keepdims=True)
        oc = o - mu
        var = jnp.mean(oc * oc, axis=-1, keepdims=True)
        y = g_ref[rows, :].astype(F32) * (oc * lax.rsqrt(var + GN_EPS))
        y_ref[rows, :] = y.astype(y_ref.dtype)
    state[...] = s_prev

    @pl.when(t == pl.num_programs(2) - 1)
    def _():
        sf_ref[...] = state[...]


def _retention_decay(chunk, valid):
    lg = jnp.log1p(-jnp.exp2(-5.0 - jnp.arange(RET_HEADS, dtype=F32)))
    idx = jnp.arange(chunk, dtype=F32)
    rel = idx[:, None] - idx[None, :]
    dmask = jnp.where(rel[None] >= 0, jnp.exp(lg[:, None, None] * jnp.maximum(rel, 0.0)[None]), 0.0)
    qdec = jnp.exp(lg[:, None] * (idx + 1.0)[None, :])[..., None]
    kdec = jnp.exp(lg[:, None] * (valid - 1.0 - idx)[None, :])[..., None]
    kdec = jnp.where((idx < valid)[None, :, None], kdec, 0.0)
    cdec = jnp.exp(lg * valid)[:, None, None]
    return dmask, qdec, kdec, cdec


def _retention_prompt(qk_r, vg_r, s0, chunk, rows_per_step):
    b, l, _ = qk_r.shape
    h = RET_HEADS
    dmask, qdec, kdec, cdec = _retention_decay(chunk, chunk)
    nsteps = l // rows_per_step
    blk = lambda off: pl.BlockSpec((None, rows_per_step, HEAD_DIM), lambda bi, hi, ti: (bi, ti, hi + off))
    per_head = lambda shape: pl.BlockSpec((None,) + shape, lambda bi, hi, ti: (hi,) + (0,) * len(shape))
    state_spec = pl.BlockSpec((None, None, HEAD_DIM, HEAD_DIM), lambda bi, hi, ti: (bi, hi, 0, 0))
    return pl.pallas_call(
        functools.partial(_retention_kernel, chunk=chunk, nchunks=rows_per_step // chunk),
        out_shape=(jax.ShapeDtypeStruct((b, l, RET_WIDTH), BF16),
                   jax.ShapeDtypeStruct((b, h, HEAD_DIM, HEAD_DIM), F32)),
        grid=(b, h, nsteps),
        in_specs=[blk(0), blk(h), blk(0), blk(h), state_spec,
                  per_head((chunk, chunk)), per_head((chunk, 1)), per_head((chunk, 1)), per_head((1, 1))],
        out_specs=(blk(0), state_spec),
        scratch_shapes=[pltpu.VMEM((HEAD_DIM, HEAD_DIM), F32)],
        compiler_params=_params(("parallel", "parallel", "arbitrary"), 32),
        name="retention_prompt",
    )(qk_r, qk_r, vg_r, vg_r, s0, dmask, qdec, kdec, cdec)


def _retention_sample_kernel(q_ref, k_ref, v_ref, g_ref, s0_ref, dm_ref, qd_ref, kd_ref, cd_ref,
                             y_ref, sf_ref, *, ntok):
    for h in range(RET_HEADS):
        q = q_ref[h]
        k = k_ref[h]
        v = v_ref[h]
        s_prev = s0_ref[h]
        dmask = dm_ref[h]
        o = jnp.dot(q * qd_ref[h], s_prev, preferred_element_type=F32)
        for j in range(ntok):
            sj = jnp.sum(q * k[j:j + 1, :], axis=-1, keepdims=True) * dmask[:, j:j + 1]
            o = o + sj * v[j:j + 1, :]
        kd = k * kd_ref[h]
        kv = lax.dot_general(kd, v, (((0,), (0,)), ((), ())), preferred_element_type=F32)
        sf_ref[h] = s_prev * cd_ref[h] + kv
        mu = jnp.mean(o, axis=-1, keepdims=True)
        oc = o - mu
        var = jnp.mean(oc * oc, axis=-1, keepdims=True)
        y_ref[h] = g_ref[h] * (oc * lax.rsqrt(var + GN_EPS))


def _retention_sample(q, k, v, g, s0, ntok):
    bd, h, tp, _ = q.shape
    dmask, qdec, kdec, cdec = _retention_decay(tp, ntok)
    tok = pl.BlockSpec((None, h, tp, HEAD_DIM), lambda bi: (bi, 0, 0, 0))
    st = pl.BlockSpec((None, h, HEAD_DIM, HEAD_DIM), lambda bi: (bi, 0, 0, 0))
    const = lambda a: pl.BlockSpec(a.shape, lambda bi: (0,) * a.ndim)
    return pl.pallas_call(
        functools.partial(_retention_sample_kernel, ntok=ntok),
        out_shape=(jax.ShapeDtypeStruct((bd, h, tp, HEAD_DIM), F32),
                   jax.ShapeDtypeStruct((bd, h, HEAD_DIM, HEAD_DIM), F32)),
        grid=(bd,),
        in_specs=[tok, tok, tok, tok, st, const(dmask), const(qdec), const(kdec), const(cdec)],
        out_specs=(tok, st),
        compiler_params=_params(("parallel",), 32),
        name="retention_sample",
    )(q, k, v, g, s0, dmask, qdec, kdec, cdec)


def _t5_bucket(dist):
    max_exact = REL_BUCKETS // 2
    d = jnp.maximum(dist, 0)
    df = jnp.maximum(d, 1).astype(F32)
    large = max_exact + (jnp.log(df / max_exact) / math.log(REL_MAX_DIST / max_exact)
                         * (REL_BUCKETS - max_exact)).astype(jnp.int32)
    large = jnp.minimum(large, REL_BUCKETS - 1)
    return jnp.where(d < max_exact, d, large)


def _bias_lookup(tab, dist):
    onehot = _t5_bucket(dist)[..., None] == jnp.arange(REL_BUCKETS)
    return jnp.sum(jnp.where(onehot[..., None], tab.astype(F32), 0.0), axis=-2)


def _dilated_kernel(q_ref, kp_ref, kc_ref, vp_ref, vc_ref, bias_ref, o_ref, lse_ref, kfull, vfull, *, nsub):
    i = pl.program_id(1)
    blk = DIL_BLOCK
    kfull[0:blk, :] = kp_ref[...]
    kfull[blk:, :] = kc_ref[...]
    vfull[0:blk, :] = vp_ref[...]
    vfull[blk:, :] = vc_ref[...]
    col = lax.broadcasted_iota(jnp.int32, (blk, 2 * blk), 1)
    lane_head = lax.broadcasted_iota(jnp.int32, (blk, HEAD_DIM), 1) // LSE_LANES
    for s in range(nsub):
        rows = pl.ds(s * blk, blk)
        win = pl.ds(s * blk, 2 * blk)
        lse_tile = jnp.zeros((blk, HEAD_DIM), F32)
        for h in range(DIL_HPG):
            cols = pl.ds(h * HEAD_DIM, HEAD_DIM)
            q = q_ref[rows, cols]
            kw = kfull[win, cols]
            vw = vfull[win, cols]
            sc = lax.dot_general(q, kw, (((1,), (1,)), ((), ())), preferred_element_type=F32) + bias_ref[h]
            if s == 0:
                sc = jnp.where((col >= blk) | (i > 0), sc, -jnp.inf)
            m = jnp.max(sc, axis=-1, keepdims=True)
            p = jnp.exp(sc - m)
            l = jnp.sum(p, axis=-1, keepdims=True)
            o = jnp.dot(p.astype(BF16), vw, preferred_element_type=F32) / l
            o_ref[rows, cols] = o.astype(o_ref.dtype)
            lse_tile = jnp.where(lane_head == h, m + jnp.log(l), lse_tile)
        lse_ref[rows, :] = lse_tile


def _dilated_bias(bias_tab, r):
    blk = DIL_BLOCK
    qi = jnp.arange(blk)[:, None]
    kj = jnp.arange(2 * blk)[None, :]
    dc = blk + qi - kj
    band = (dc >= 0) & (dc <= blk)
    bias = _bias_lookup(bias_tab, dc * r).transpose(2, 0, 1)
    return jnp.where(band[None], bias, -jnp.inf)


def _dilated_prompt(q, k, v, bias, nsub):
    n, lc, w = q.shape
    blk = DIL_BLOCK
    tq = nsub * blk
    cur = pl.BlockSpec((None, tq, w), lambda ni, i: (ni, i, 0))
    prev = pl.BlockSpec((None, blk, w), lambda ni, i: (ni, jnp.maximum(i * nsub - 1, 0), 0))
    return pl.pallas_call(
        functools.partial(_dilated_kernel, nsub=nsub),
        out_shape=(jax.ShapeDtypeStruct((n, lc, w), F32),
                   jax.ShapeDtypeStruct((n, lc, HEAD_DIM), F32)),
        grid=(n, lc // tq),
        in_specs=[cur, prev, cur, prev, cur, pl.BlockSpec(bias.shape, lambda ni, i: (0, 0, 0))],
        out_specs=(cur, pl.BlockSpec((None, tq, HEAD_DIM), lambda ni, i: (ni, i, 0))),
        scratch_shapes=[pltpu.VMEM((tq + blk, w), BF16), pltpu.VMEM((tq + blk, w), BF16)],
        compiler_params=_params(("parallel", "parallel"), 32),
        name="dilated_prompt",
    )(q, k, k, v, v, bias)


def _dilated_sample_kernel(q_ref, kn_ref, vn_ref, c0_ref, c1_ref, c2_ref, bc_ref, bn_ref,
                           o_ref, lse_ref, *, ntok):
    caches = (c0_ref, c1_ref, c2_ref)
    for gi in range(DIL_GROUPS):
        cache = caches[gi]
        nclass = cache.shape[1]
        for t in range(ntok):
            cls = t if nclass > 1 else 0
            qt = q_ref[t, gi]
            kc = cache[:, cls, 0]
            vc = cache[:, cls, 1]
            sc = jnp.sum(kc * qt[None], axis=-1, keepdims=True) + bc_ref[gi, t]
            sn = jnp.sum(kn_ref[:, gi] * qt[None], axis=-1, keepdims=True) + bn_ref[gi, t]
            m = jnp.maximum(jnp.max(sc, axis=0), jnp.max(sn, axis=0))
            pc = jnp.exp(sc - m[None])
            pn = jnp.exp(sn - m[None])
            l = jnp.sum(pc, axis=0) + jnp.sum(pn, axis=0)
            o = jnp.sum(pc * vc, axis=0) + jnp.sum(pn * vn_ref[:, gi], axis=0)
            o_ref[t, gi] = o / l
            lse_ref[t, gi] = m + jnp.log(l)


def _dilated_sample(q, kn, vn, caches, bias_cache, bias_new, ntok):
    bd = q.shape[0]
    tok = pl.BlockSpec((None, ntok, DIL_GROUPS, DIL_HPG, HEAD_DIM), lambda bi: (bi, 0, 0, 0, 0))
    cspecs = []
    for c in caches:
        ncls = min(c.shape[2], ntok)
        cspecs.append(pl.BlockSpec((None, DIL_BLOCK, ncls, 2, DIL_HPG, HEAD_DIM),
                                   lambda bi: (bi, 0, 0, 0, 0, 0)))
    const = lambda a: pl.BlockSpec(a.shape, lambda bi: (0,) * a.ndim)
    return pl.pallas_call(
        functools.partial(_dilated_sample_kernel, ntok=ntok),
        out_shape=(jax.ShapeDtypeStruct(q.shape, F32),
                   jax.ShapeDtypeStruct((bd, ntok, DIL_GROUPS, DIL_HPG, 1), F32)),
        grid=(bd,),
        in_specs=[tok, tok, tok] + cspecs + [const(bias_cache), const(bias_new)],
        out_specs=(tok, pl.BlockSpec((None, ntok, DIL_GROUPS, DIL_HPG, 1), lambda bi: (bi, 0, 0, 0, 0))),
        compiler_params=_params(("parallel",), 32),
        name="dilated_sample",
    )(q, kn, vn, *caches, bias_cache, bias_new)


def _dilated_sample_bias(rel_bias, ntok):
    bc, bn = [], []
    m = jnp.arange(DIL_BLOCK)
    tn = jnp.arange(ntok)
    for gi, (_, r) in enumerate(DIL_PAIRS):
        tab = rel_bias[:, gi * DIL_HPG:(gi + 1) * DIL_HPG]
        rows_c, rows_n = [], []
        for t in range(ntok):
            if r == 1:
                jc = DIL_BLOCK + t - m
                okc = m >= t
                jn = t - tn
                okn = tn <= t
            else:
                jc = DIL_BLOCK - m
                okc = jnp.ones_like(m, bool)
                jn = jnp.zeros_like(tn)
                okn = tn == t
            rows_c.append(jnp.where(okc[:, None], _bias_lookup(tab, jc * r), -jnp.inf))
            rows_n.append(jnp.where(okn[:, None], _bias_lookup(tab, jn * r), -jnp.inf))
        bc.append(jnp.stack(rows_c))
        bn.append(jnp.stack(rows_n))
    return jnp.stack(bc)[..., None], jnp.stack(bn)[..., None]


def _cache_shift_kernel(c_ref, n_ref, o_ref, *, ntok):
    i = pl.program_id(1)
    last = pl.num_programs(1) - 1
    rows = o_ref.shape[0]

    @pl.when(i < last)
    def _():
        o_ref[...] = c_ref[0]

    @pl.when(i == last)
    def _():
        o_ref[0:rows - ntok] = c_ref[0, ntok:rows]
        o_ref[rows - ntok:rows] = n_ref[...]


def _cache_shift(cache, new, ntok):
    bd, lb = cache.shape[:2]
    tail = cache.shape[2:]
    rows = min(lb, 512)
    zeros = (0,) * len(tail)
    return pl.pallas_call(
        functools.partial(_cache_shift_kernel, ntok=ntok),
        out_shape=jax.ShapeDtypeStruct(cache.shape, cache.dtype),
        grid=(bd, lb // rows),
        in_specs=[pl.BlockSpec(tuple(pl.Element(s) for s in (1, rows) + tail),
                               lambda b, i: (b, jnp.minimum(i * rows + ntok, lb - rows)) + zeros),
                  pl.BlockSpec((None, ntok) + tail, lambda b, i: (b, 0) + zeros)],
        out_specs=pl.BlockSpec((None, rows) + tail, lambda b, i: (b, i) + zeros),
        compiler_params=_params(("parallel", "arbitrary"), 40),
        name="cache_shift",
    )(cache, new)


def _merge_kernel(o0_ref, o1_ref, o2_ref, l0_ref, l1_ref, l2_ref, yr_ref, gt_ref, wr_ref, wd_ref,
                  out_ref, oa_ref, *nat, dilations):
    tm = out_ref.shape[0]
    o_nat, l_nat = [], []
    for g, (o_ref, l_ref, r) in enumerate(zip((o0_ref, o1_ref, o2_ref), (l0_ref, l1_ref, l2_ref), dilations)):
        if r == 1:
            o_nat.append([o_ref[:, _head_cols(h)] for h in range(DIL_HPG)])
            l_nat.append(l_ref[...])
        else:
            on, ln = nat[2 * g], nat[2 * g + 1]
            for c in range(r):
                ln[pl.ds(c, tm // r, stride=r), :] = l_ref[c]
                for h in range(DIL_HPG):
                    on[h, pl.ds(c, tm // r, stride=r), :] = o_ref[c, :, _head_cols(h)]
            o_nat.append([on[h] for h in range(DIL_HPG)])
            l_nat.append(ln[...])
    m = jnp.maximum(jnp.maximum(l_nat[0], l_nat[1]), l_nat[2])
    es = [jnp.exp(l - m) for l in l_nat]
    den = es[0] + es[1] + es[2]
    for h in range(DIL_HPG):
        lane = h * LSE_LANES
        acc = None
        for g in range(DIL_GROUPS):
            term = (es[g][:, lane:lane + 1] / den[:, lane:lane + 1]) * o_nat[g][h]
            acc = term if acc is None else acc + term
        oa_ref[:, _head_cols(h)] = acc.astype(oa_ref.dtype)
    d = out_ref.shape[1]
    for c0 in range(0, d, PROJ_CHUNK):
        cols = slice(c0, c0 + PROJ_CHUNK)
        yr = jnp.dot(yr_ref[...], wr_ref[:, cols], preferred_element_type=F32)
        ya = jnp.dot(oa_ref[...], wd_ref[:, cols], preferred_element_type=F32)
        g_r = gt_ref[:, cols].astype(F32)
        g_a = gt_ref[:, d + c0:d + c0 + PROJ_CHUNK].astype(F32)
        out_ref[:, cols] = (g_r * yr + g_a * ya).astype(out_ref.dtype)


def _merge(o_g, lse_g, dilations, b, l, yr_in, gates, w_ret_b, w_dil_b, tm):
    n = yr_in.shape[0]
    d = w_ret_b.shape[1]
    bps = l // tm
    row = lambda w: pl.BlockSpec((tm, w), lambda i: (i, 0))
    const = lambda a: pl.BlockSpec(a.shape, lambda i: (0, 0))

    def group_specs(width):
        specs = []
        for r in dilations:
            if r == 1:
                specs.append(row(width))
            else:
                specs.append(pl.BlockSpec((None, r, tm // r, width), lambda i: (i // bps, 0, i % bps, 0)))
        return specs

    view = lambda t, r, width: t.reshape(n, width) if r == 1 else t.reshape(b, r, l // r, width)
    o_in = [view(t, r, DIL_OUT_WIDTH) for t, r in zip(o_g, dilations)]
    l_in = [view(t, r, HEAD_DIM) for t, r in zip(lse_g, dilations)]
    scratch = [pltpu.VMEM((tm, DIL_OUT_WIDTH), BF16)]
    for _ in dilations:
        scratch += [pltpu.VMEM((DIL_HPG, tm, HEAD_DIM), F32), pltpu.VMEM((tm, HEAD_DIM), F32)]
    return pl.pallas_call(
        functools.partial(_merge_kernel, dilations=tuple(dilations)),
        out_shape=jax.ShapeDtypeStruct((n, d), BF16),
        grid=(n // tm,),
        in_specs=group_specs(DIL_OUT_WIDTH) + group_specs(HEAD_DIM) + [row(RET_WIDTH), row(2 * d),
                                                                      const(w_ret_b), const(w_dil_b)],
        out_specs=row(d),
        scratch_shapes=scratch,
        compiler_params=_params(("parallel",), 48),
        name="merge_branches",
    )(*o_in, *l_in, yr_in, gates, w_ret_b, w_dil_b)


ROUTER_ROWS = 8 + N_EXPERTS


def _route_tile(lt, tri, before):
    lg = lt[0:N_GROUPS]
    gmax = jnp.max(lg, axis=0, keepdims=True)
    w_coarse = 1.0 / jnp.sum(jnp.exp(lg - gmax), axis=0, keepdims=True)
    gid = lax.broadcasted_iota(jnp.int32, lg.shape, 0)
    gsel = jnp.min(jnp.where(lg == gmax, gid, N_GROUPS), axis=0, keepdims=True)
    le = jnp.zeros((EXPERTS_PER_GROUP, lt.shape[1]), F32)
    for g in range(N_GROUPS):
        le = jnp.where(gsel == g, lt[8 + g * EXPERTS_PER_GROUP:8 + (g + 1) * EXPERTS_PER_GROUP], le)
    eid = lax.broadcasted_iota(jnp.int32, le.shape, 0)
    v1 = jnp.max(le, axis=0, keepdims=True)
    i1 = jnp.min(jnp.where(le == v1, eid, EXPERTS_PER_GROUP), axis=0, keepdims=True)
    le2 = jnp.where(eid == i1, -jnp.inf, le)
    v2 = jnp.max(le2, axis=0, keepdims=True)
    i2 = jnp.min(jnp.where(le2 == v2, eid, EXPERTS_PER_GROUP), axis=0, keepdims=True)
    e21 = jnp.exp(v2 - v1)
    w1 = w_coarse / (1.0 + e21)
    w2 = w_coarse * e21 / (1.0 + e21)
    e1 = gsel * EXPERTS_PER_GROUP + i1
    e2 = gsel * EXPERTS_PER_GROUP + i2

    xid = lax.broadcasted_iota(jnp.int32, (N_EXPERTS, lt.shape[1]), 0)
    oh1 = jnp.where(xid == e1, 1.0, 0.0)
    oh2 = jnp.where(xid == e2, 1.0, 0.0)
    n1 = jnp.sum(oh1, axis=1, keepdims=True)
    n2 = jnp.sum(oh2, axis=1, keepdims=True)
    p1 = jnp.dot(oh1.astype(BF16), tri, preferred_element_type=F32) + before
    p2 = jnp.dot(oh2.astype(BF16), tri, preferred_element_type=F32) + before + n1
    rank1 = jnp.sum(oh1 * p1, axis=0, keepdims=True)
    rank2 = jnp.sum(oh2 * p2, axis=0, keepdims=True)

    row = lax.broadcasted_iota(jnp.int32, (8, lt.shape[1]), 0)
    ei = jnp.where(row == 0, e1, jnp.where(row == 1, e2, jnp.where(
        row == 2, rank1.astype(jnp.int32), jnp.where(row == 3, rank2.astype(jnp.int32), 0))))
    wt = jnp.where(row == 0, w1, jnp.where(row == 1, w2, 0.0))
    return ei, wt, before + n1 + n2


def _outproj_router_kernel(mg_ref, x_ref, wo_ref, g2_ref, wr_ref, br_ref, tri_ref, cin_ref,
                           h_ref, u2_ref, ei_ref, wt_ref, cnt_ref):
    @pl.when(pl.program_id(0) == 0)
    def _():
        cnt_ref[...] = cin_ref[...]

    sub = tri_ref.shape[0]
    subtiles = [slice(s0, s0 + sub) for s0 in range(0, h_ref.shape[0], sub)]
    for rows in subtiles:
        h_ref[rows, :] = x_ref[rows, :] + jnp.dot(mg_ref[rows, :], wo_ref[...], preferred_element_type=F32)
    counts = cnt_ref[...]
    for rows in subtiles:
        h = h_ref[rows, :]
        u2 = h * lax.rsqrt(jnp.mean(h * h, axis=-1, keepdims=True) + NORM_EPS) * g2_ref[...]
        u2_ref[rows, :] = u2
        lt = lax.dot_general(wr_ref[...], u2.astype(BF16), (((1,), (1,)), ((), ())),
                             preferred_element_type=F32) + br_ref[...]
        ei, wt, counts = _route_tile(lt, tri_ref[...], counts)
        ei_ref[:, rows] = ei
        wt_ref[:, rows] = wt
    cnt_ref[...] = counts


def _outproj_router(merged, x, w_o_b, norm_ffn, wr_t, br_t, counts_in, tm):
    n, d = x.shape
    sub = min(tm, ROUTER_SUB)
    tri = (jnp.arange(sub)[:, None] < jnp.arange(sub)[None, :]).astype(BF16)
    row = lambda: pl.BlockSpec((tm, d), lambda i: (i, 0))
    lane = lambda: pl.BlockSpec((8, tm), lambda i: (0, i))
    const = lambda a: pl.BlockSpec(a.shape, lambda i: (0, 0))
    return pl.pallas_call(
        _outproj_router_kernel,
        out_shape=(jax.ShapeDtypeStruct((n, d), F32), jax.ShapeDtypeStruct((n, d), F32),
                   jax.ShapeDtypeStruct((8, n), jnp.int32), jax.ShapeDtypeStruct((8, n), F32),
                   jax.ShapeDtypeStruct((N_EXPERTS, 1), F32)),
        grid=(n // tm,),
        in_specs=[row(), row(), const(w_o_b), pl.BlockSpec((1, d), lambda i: (0, 0)),
                  const(wr_t), const(br_t), const(tri), const(counts_in)],
        out_specs=(row(), row(), lane(), lane(), const(counts_in)),
        compiler_params=_params(("arbitrary",), 60),
        name="outproj_router",
    )(merged, x, w_o_b, norm_ffn.reshape(1, d), wr_t, br_t, tri, counts_in)


def _router_weights(w_rg, b_rg, w_re, b_re):
    d = w_rg.shape[0]
    wr = jnp.zeros((ROUTER_ROWS, d), F32)
    wr = wr.at[0:N_GROUPS].set(w_rg.T)
    wr = wr.at[8:].set(w_re.transpose(0, 2, 1).reshape(N_EXPERTS, d))
    br = jnp.zeros((ROUTER_ROWS, 1), F32)
    br = br.at[0:N_GROUPS, 0].set(b_rg)
    br = br.at[8:, 0].set(b_re.reshape(N_EXPERTS))
    return wr.astype(BF16), br


DMA_UNROLL = 8


def _dispatch_kernel(pos_p_ref, pos_s_ref, zero_ref, up_ref, us_ref, xs_ref, zbuf, sem, *, tm):
    i = pl.program_id(0)
    last = pl.num_programs(0) - 1
    ntiles = xs_ref.shape[0] // MOE_TILE

    def tile_copy(t):
        return pltpu.make_async_copy(zbuf, xs_ref.at[pl.ds(pl.multiple_of(t * MOE_TILE, MOE_TILE), MOE_TILE)], sem)

    @pl.when(i == 0)
    def _():
        zbuf[...] = jnp.zeros_like(zbuf)
        for op in ("start", "wait"):
            def per_tile(t, c, op=op):
                @pl.when(zero_ref[t] != 0)
                def _():
                    getattr(tile_copy(t), op)()
                return c

            lax.fori_loop(0, ntiles, per_tile, 0)

    def scatter_rows(u_ref, pos_ref, base, nrows, ntok_total):
        def row_copy(r, kk):
            slot = pos_ref[kk * ntok_total + base + r]
            return pltpu.make_async_copy(u_ref.at[pl.ds(r, 1)], xs_ref.at[pl.ds(slot, 1)], sem)

        for op in ("start", "wait"):
            def body(r, c, op=op):
                getattr(row_copy(r, 0), op)()
                getattr(row_copy(r, 1), op)()
                return c

            lax.fori_loop(0, nrows, body, 0, unroll=DMA_UNROLL)

    @pl.when(i < last)
    def _():
        scatter_rows(up_ref, pos_p_ref, i * tm, tm, last * tm)

    @pl.when(i == last)
    def _():
        scatter_rows(us_ref, pos_s_ref, 0, us_ref.shape[0], us_ref.shape[0])


def _dispatch(pos_p, pos_s, zero_tiles, u2p, u2s, tm):
    n, d = u2p.shape
    nsteps = n // tm
    grid_spec = pltpu.PrefetchScalarGridSpec(
        num_scalar_prefetch=3,
        grid=(nsteps + 1,),
        in_specs=[pl.BlockSpec((tm, d), lambda i, *_: (jnp.minimum(i, nsteps - 1), 0)),
                  pl.BlockSpec(u2s.shape, lambda i, *_: (0, 0))],
        out_specs=pl.BlockSpec(memory_space=pl.ANY),
        scratch_shapes=[pltpu.VMEM((MOE_TILE, d), F32), pltpu.SemaphoreType.DMA(())],
    )
    return pl.pallas_call(
        functools.partial(_dispatch_kernel, tm=tm),
        out_shape=jax.ShapeDtypeStruct((zero_tiles.shape[0] * MOE_TILE, d), F32),
        grid_spec=grid_spec,
        compiler_params=_params(("arbitrary",), 32),
        name="moe_dispatch",
    )(pos_p, pos_s, zero_tiles, u2p, u2s)


def _gmm_kernel(te_ref, nt_ref, x_ref, wg_ref, wu_ref, wd_ref, o_ref, wgb, wub, wdb):
    i = pl.program_id(0)
    fresh = jnp.logical_or(i == 0, te_ref[i] != te_ref[jnp.maximum(i - 1, 0)])

    @pl.when(jnp.logical_and(fresh, i < nt_ref[0]))
    def _():
        wgb[...] = wg_ref[...].astype(BF16)
        wub[...] = wu_ref[...].astype(BF16)
        wdb[...] = wd_ref[...].astype(BF16)

    @pl.when(i < nt_ref[0])
    def _():
        x = x_ref[...].astype(BF16)
        acc = None
        for c0 in range(0, wgb.shape[1], PROJ_CHUNK):
            cols = slice(c0, c0 + PROJ_CHUNK)
            a = jnp.dot(x, wgb[:, cols], preferred_element_type=F32)
            b = jnp.dot(x, wub[:, cols], preferred_element_type=F32)
            hm = (a * jax.nn.sigmoid(a) * b).astype(BF16)
            part = jnp.dot(hm, wdb[cols, :], preferred_element_type=F32)
            acc = part if acc is None else acc + part
        o_ref[...] = acc

    @pl.when(i >= nt_ref[0])
    def _():
        o_ref[...] = jnp.zeros_like(o_ref)


def _gmm(tile_expert, ntiles_used, xs, w_gate, w_up, w_down):
    npad, d = xs.shape
    f = w_gate.shape[-1]
    ntiles = npad // MOE_TILE
    epg = w_gate.shape[1]

    def xmap(i, te, nt):
        return (jnp.minimum(i, nt[0] - 1), 0)

    def wmap(i, te, nt):
        e = te[i]
        return (e // epg, e % epg, 0, 0)

    grid_spec = pltpu.PrefetchScalarGridSpec(
        num_scalar_prefetch=2,
        grid=(ntiles,),
        in_specs=[pl.BlockSpec((MOE_TILE, d), xmap),
                  pl.BlockSpec((None, None, d, f), wmap),
                  pl.BlockSpec((None, None, d, f), wmap),
                  pl.BlockSpec((None, None, f, d), wmap)],
        out_specs=pl.BlockSpec((MOE_TILE, d), lambda i, te, nt: (i, 0)),
        scratch_shapes=[pltpu.VMEM((d, f), BF16), pltpu.VMEM((d, f), BF16), pltpu.VMEM((f, d), BF16)],
    )
    return pl.pallas_call(
        _gmm_kernel,
        out_shape=jax.ShapeDtypeStruct((npad, d), F32),
        grid_spec=grid_spec,
        compiler_params=_params(("arbitrary",), 56),
        name="moe_grouped_matmul",
    )(tile_expert, ntiles_used, xs, w_gate, w_up, w_down)


def _combine_kernel(pos_ref, h_ref, wt_ref, os_ref, y_ref, g0, g1, sem, *, tm, ntok_total):
    i = pl.program_id(0)
    nsteps = pl.num_programs(0)
    cur = i % 2

    def row_copy(step, r, kk, slot):
        src = pos_ref[kk * ntok_total + step * tm + r]
        buf = (g0, g1)[kk]
        return pltpu.make_async_copy(os_ref.at[pl.ds(src, 1)], buf.at[slot, pl.ds(r, 1)], sem.at[slot])

    def issue_step(step, slot):
        def body(r, c):
            row_copy(step, r, 0, slot).start()
            row_copy(step, r, 1, slot).start()
            return c

        lax.fori_loop(0, tm, body, 0, unroll=DMA_UNROLL)

    @pl.when(i == 0)
    def _():
        issue_step(0, 0)

    @pl.when(i + 1 < nsteps)
    def _():
        issue_step(i + 1, 1 - cur)

    def drain(r, c):
        row_copy(i, r, 0, cur).wait()
        row_copy(i, r, 1, cur).wait()
        return c

    lax.fori_loop(0, tm, drain, 0, unroll=DMA_UNROLL)
    y_ref[...] = h_ref[...] + wt_ref[:, 0:1] * g0[cur] + wt_ref[:, 1:2] * g1[cur]


def _combine(pos_flat, h, wt_cols, out_sorted, tm):
    n, d = h.shape
    grid_spec = pltpu.PrefetchScalarGridSpec(
        num_scalar_prefetch=1,
        grid=(n // tm,),
        in_specs=[pl.BlockSpec((tm, d), lambda i, pos: (i, 0)),
                  pl.BlockSpec((tm, 8), lambda i, pos: (i, 0)),
                  pl.BlockSpec(memory_space=pl.ANY)],
        out_specs=pl.BlockSpec((tm, d), lambda i, pos: (i, 0)),
        scratch_shapes=[pltpu.VMEM((2, tm, d), F32), pltpu.VMEM((2, tm, d), F32),
                        pltpu.SemaphoreType.DMA((2,))],
    )
    return pl.pallas_call(
        functools.partial(_combine_kernel, tm=tm, ntok_total=n),
        out_shape=jax.ShapeDtypeStruct((n, d), F32),
        grid_spec=grid_spec,
        compiler_params=_params(("arbitrary",), 32),
        name="moe_combine",
    )(pos_flat, h, wt_cols, out_sorted)


def _moe_plan(counts, route_sets):
    npairs = sum(e.shape[1] for e, _ in route_sets) * TOP_K
    tiles = (counts + MOE_TILE - 1) // MOE_TILE
    tile_end = jnp.cumsum(tiles)
    offs = (tile_end - tiles) * MOE_TILE
    ids = jnp.arange(N_EXPERTS)
    slots = []
    for experts, ranks in route_sets:
        base = jnp.sum(jnp.where(experts[..., None] == ids, offs, 0), axis=-1)
        slots.append((base + ranks).reshape(-1).astype(jnp.int32))
    ntiles = (npairs + N_EXPERTS * (MOE_TILE - 1)) // MOE_TILE
    tile_expert = jnp.sum(tile_end[None, :] <= jnp.arange(ntiles)[:, None], axis=1)
    tile_expert = jnp.minimum(tile_expert, N_EXPERTS - 1).astype(jnp.int32)
    tile_ids = jnp.arange(ntiles)
    is_last = jnp.any((tile_ids[:, None] == tile_end[None, :] - 1) & (tiles[None, :] > 0), axis=1)
    zero_tiles = (is_last | (tile_ids >= tile_end[-1])).astype(jnp.int32)
    return slots, tile_expert, tile_end[-1:].astype(jnp.int32), zero_tiles


def _mixers_prompt(x, wts, counts_in):
    b, l, d = x.shape
    n = b * l
    tm = ROW_TILE
    xf = x.reshape(n, d)
    u = _rmsnorm(xf, wts["norm_attn"], tm)
    tab = _rotary_tables(jnp.arange(l), l)
    qk_r, vg_r, gates = _proj_retention_and_gates(u, wts["w_in"], tab, 2 * tm)

    s0 = jnp.zeros((b, RET_HEADS, HEAD_DIM, HEAD_DIM), F32)
    yr_in, s_fin = _retention_prompt(qk_r.reshape(b, l, -1), vg_r.reshape(b, l, -1), s0, 256, min(l, 2048))

    o_g, lse_g, bufs = [], [], []
    for gi, (w, r) in enumerate(DIL_PAIRS):
        qg, kg, vg, kt, vt = _proj_dilated_prompt(u, wts["w_in"], wts["q_norm"], wts["k_norm"], gi, b, l, tm)
        bias = _dilated_bias(wts["rel_bias"][:, gi * DIL_HPG:(gi + 1) * DIL_HPG], r)
        o, lse = _dilated_prompt(qg, kg, vg, bias, min(8, l // r // DIL_BLOCK))
        o_g.append(o)
        lse_g.append(lse)
        bufs.append(jnp.stack([kt, vt], axis=2).reshape(b, kt.shape[1], 2, DIL_HPG, HEAD_DIM))

    merged = _merge(o_g, lse_g, [r for _, r in DIL_PAIRS], b, l, yr_in.reshape(n, RET_WIDTH), gates,
                    wts["w_ret_out"], wts["w_dil_out"], tm)
    h, u2, ei, wt, counts = _outproj_router(merged, xf, wts["w_o"], wts["norm_ffn"], wts["wr_t"], wts["br_t"],
                                            counts_in, ROUTER_TILE)
    return h, u2, ei, wt, counts, s_fin, bufs


def _mixers_sample(x, caches, state, wts, counts_in):
    bd, t, d = x.shape
    n = bd * t
    xf = x.reshape(n, d)
    u = _rmsnorm(xf, wts["norm_attn"], n)
    tab = _rotary_tables(PAST_LEN + jnp.arange(t), n)
    qk_r, vg_r, gates = _proj_retention_and_gates(u, wts["w_in"], tab, n)
    qkv_a = _proj_dilated_sample(u, wts["w_in"], wts["q_norm"], wts["k_norm"])

    def heads(a):
        a = a.astype(F32).reshape(bd, t, RET_HEADS, HEAD_DIM).transpose(0, 2, 1, 3)
        return jnp.pad(a, ((0, 0), (0, 0), (0, 8 - t), (0, 0)))

    y_r, s_new = _retention_sample(heads(qk_r[:, :RET_WIDTH]), heads(qk_r[:, RET_WIDTH:]),
                                   heads(vg_r[:, :RET_WIDTH]), heads(vg_r[:, RET_WIDTH:]), state, t)
    yr_in = y_r[:, :, :t].transpose(0, 2, 1, 3).reshape(n, RET_WIDTH).astype(BF16)

    grp = lambda a: a.reshape(bd, t, DIL_GROUPS, DIL_HPG, HEAD_DIM)
    qa, ka, va = (grp(qkv_a[:, s * DIL_WIDTH:(s + 1) * DIL_WIDTH]) for s in range(3))
    cviews = [c.reshape(bd, DIL_BLOCK, r, 2, DIL_HPG, HEAD_DIM) for c, (_, r) in zip(caches, DIL_PAIRS)]
    bias_c, bias_n = _dilated_sample_bias(wts["rel_bias"], t)
    o, lse = _dilated_sample(qa, ka, va, cviews, bias_c, bias_n, t)
    o_g = [o[:, :, gi].reshape(n, 1, DIL_OUT_WIDTH) for gi in range(DIL_GROUPS)]
    lse_g = [jnp.repeat(lse[:, :, gi].reshape(n, DIL_HPG), LSE_LANES, axis=1).reshape(n, 1, HEAD_DIM)
             for gi in range(DIL_GROUPS)]

    news = [jnp.stack([ka[:, :, gi], va[:, :, gi]], axis=2) for gi in range(DIL_GROUPS)]
    bufs = [_cache_shift(c, nw, t) for c, nw in zip(caches, news)]

    merged = _merge(o_g, lse_g, [1] * DIL_GROUPS, 1, n, yr_in, gates, wts["w_ret_out"], wts["w_dil_out"], n)
    h, u2, ei, wt, counts = _outproj_router(merged, xf, wts["w_o"], wts["norm_ffn"], wts["wr_t"], wts["br_t"],
                                            counts_in, n)
    return h, u2, ei, wt, counts, s_new, bufs


def kernel(x_prompt, x_sample, cache_kv_g0, cache_kv_g1, cache_kv_g2, state_ret, norm_attn, w_in, q_norm,
           k_norm, rel_bias, w_ret_out, w_dil_out, w_o, norm_ffn, w_router_group, b_router_group,
           w_router_expert, b_router_expert, w_gate, w_up, w_down):
    caches = (cache_kv_g0, cache_kv_g1, cache_kv_g2)
    ntok = x_sample.shape[1]
    for c, (w, r) in zip(caches, DIL_PAIRS):
        assert c.shape[1] == w == DIL_BLOCK * r and (r == 1 or ntok <= r) and ntok <= 8
    wr_t, br_t = _router_weights(w_router_group, b_router_group, w_router_expert, b_router_expert)
    wts = dict(norm_attn=norm_attn, w_in=w_in, q_norm=q_norm, k_norm=k_norm, rel_bias=rel_bias,
               w_ret_out=w_ret_out.astype(BF16), w_dil_out=w_dil_out.astype(BF16), w_o=w_o.astype(BF16),
               norm_ffn=norm_ffn, wr_t=wr_t, br_t=br_t)

    zero_counts = jnp.zeros((N_EXPERTS, 1), F32)
    hp, u2p, eip, wtp, counts_p, s_p, bufs_p = _mixers_prompt(x_prompt, wts, zero_counts)
    hs, u2s, eis, wts_s, counts, s_s, bufs_s = _mixers_sample(x_sample, caches, state_ret, wts, counts_p)

    slots, tile_expert, ntiles_used, zero_tiles = _moe_plan(
        counts[:, 0].astype(jnp.int32), [(eip[0:2], eip[2:4]), (eis[0:2], eis[2:4])])
    xs = _dispatch(slots[0], slots[1], zero_tiles, u2p, u2s, 256)
    out_sorted = _gmm(tile_expert, ntiles_used, xs, w_gate, w_up, w_down)
    yp = _combine(slots[0], hp, wtp.T, out_sorted, 256)
    ys = _combine(slots[1], hs, wts_s.T, out_sorted, hs.shape[0])

    return (yp.reshape(x_prompt.shape), ys.reshape(x_sample.shape), bufs_p[0], bufs_p[1], bufs_p[2], s_p,
            bufs_s[0], bufs_s[1], bufs_s[2], s_s)
```

```python
import functools
import math

import jax
import jax.numpy as jnp
from jax import lax
from jax.experimental import pallas as pl
from jax.experimental.pallas import tpu as pltpu

HEAD_DIM = 128
RET_HEADS = 8
RET_WIDTH = RET_HEADS * HEAD_DIM
ROPE_BASE = 10000.0
GN_EPS = 1e-5
DIL_PAIRS = ((128, 1), (512, 4), (2048, 16))
DIL_GROUPS = len(DIL_PAIRS)
DIL_HPG = 4
DIL_HEADS = DIL_HPG * DIL_GROUPS
DIL_WIDTH = DIL_HEADS * HEAD_DIM
DIL_OUT_WIDTH = DIL_HPG * HEAD_DIM
DIL_BLOCK = 128
LSE_LANES = HEAD_DIM // DIL_HPG
ATTN_SCALE = HEAD_DIM ** -0.5
REL_BUCKETS = 32
REL_MAX_DIST = 2048
N_GROUPS = 4
EXPERTS_PER_GROUP = 8
N_EXPERTS = N_GROUPS * EXPERTS_PER_GROUP
TOP_K = 2
NORM_EPS = 1e-6
PAST_LEN = 16384

COL_QR = 0
COL_KR = COL_QR + RET_WIDTH
COL_VR = COL_KR + RET_WIDTH
COL_GR = COL_VR + RET_WIDTH
COL_QA = COL_GR + RET_WIDTH
COL_KA = COL_QA + DIL_WIDTH
COL_VA = COL_KA + DIL_WIDTH
COL_GATES = COL_VA + DIL_WIDTH

MOE_TILE = 256
ROW_TILE = 512
PROJ_CHUNK = 256
ROUTER_TILE = 512
ROUTER_SUB = 256
SAMPLE_SEQS_PER_STEP = 1
MIB = 1 << 20
BF16 = jnp.bfloat16
F32 = jnp.float32


def _params(semantics, vmem_mib):
    return pltpu.CompilerParams(dimension_semantics=semantics, vmem_limit_bytes=vmem_mib * MIB)


def _head_cols(h):
    return slice(h * HEAD_DIM, (h + 1) * HEAD_DIM)


def _rmsnorm_kernel(x_ref, g_ref, o_ref):
    x = x_ref[...]
    y = x * lax.rsqrt(jnp.mean(x * x, axis=-1, keepdims=True) + NORM_EPS)
    o_ref[...] = (y * g_ref[...]).astype(o_ref.dtype)


def _rmsnorm(x, g, tm):
    n, d = x.shape
    return pl.pallas_call(
        _rmsnorm_kernel,
        out_shape=jax.ShapeDtypeStruct((n, d), BF16),
        grid=(n // tm,),
        in_specs=[pl.BlockSpec((tm, d), lambda i: (i, 0)), pl.BlockSpec((1, d), lambda i: (0, 0))],
        out_specs=pl.BlockSpec((tm, d), lambda i: (i, 0)),
        compiler_params=_params(("parallel",), 40),
        name="rmsnorm",
    )(x, g.reshape(1, d))


def _proj_kernel(u_ref, *refs, epilogue, n_w):
    w_refs = refs[:n_w]
    wb_refs = refs[len(refs) - n_w:]
    rest = refs[n_w:len(refs) - n_w]

    @pl.when(pl.program_id(1) == 0)
    def _():
        for w_ref, wb_ref in zip(w_refs, wb_refs):
            wb_ref[...] = w_ref[...].astype(BF16)

    tn = wb_refs[0].shape[1]
    for wi, wb_ref in enumerate(wb_refs):
        for c0 in range(0, tn, PROJ_CHUNK):
            acc = jnp.dot(u_ref[...], wb_ref[:, c0:c0 + PROJ_CHUNK], preferred_element_type=F32)
            epilogue(wi, c0, acc, *rest)


def _proj(u, w_in, col_offsets, ncol_blocks, tn, tm, epilogue, extra, extra_specs, out_shape, out_specs,
          scratch, name, vmem_mib=58):
    n, k = u.shape
    w_specs = [pl.BlockSpec((pl.Element(k), pl.Element(tn)),
                            functools.partial(lambda j, i, o: (0, pl.multiple_of(o + j * tn, HEAD_DIM)), o=o))
               for o in col_offsets]
    return pl.pallas_call(
        functools.partial(_proj_kernel, epilogue=epilogue, n_w=len(col_offsets)),
        out_shape=out_shape,
        grid=(ncol_blocks, n // tm),
        in_specs=[pl.BlockSpec((tm, k), lambda j, i: (i, 0))] + w_specs + list(extra_specs),
        out_specs=out_specs,
        scratch_shapes=list(scratch) + [pltpu.VMEM((k, tn), BF16) for _ in col_offsets],
        compiler_params=_params(("arbitrary", "arbitrary"), vmem_mib),
        name=name,
    )(u, *([w_in] * len(col_offsets)), *extra)


def _chunk_heads(c0, acc):
    return [(slice(c0 + h * HEAD_DIM, c0 + (h + 1) * HEAD_DIM), acc[:, _head_cols(h)])
            for h in range(acc.shape[1] // HEAD_DIM)]


def _epi_rotary(wi, c0, acc, tab_ref, o_ref):
    c = tab_ref[0]
    s = tab_ref[1]
    for cols, xh in _chunk_heads(c0, acc):
        o_ref[:, cols] = (xh * c + pltpu.roll(xh, HEAD_DIM // 2, 1) * s).astype(o_ref.dtype)


def _epi_value_gate(wi, c0, acc, o_ref):
    is_gate = pl.program_id(0) == 1
    o_ref[:, c0:c0 + acc.shape[1]] = jnp.where(is_gate, acc * jax.nn.sigmoid(acc), acc).astype(o_ref.dtype)


def _epi_sigmoid(wi, c0, acc, o_ref):
    o_ref[:, c0:c0 + acc.shape[1]] = jax.nn.sigmoid(acc).astype(o_ref.dtype)


def _head_rms(xh, g):
    return xh * lax.rsqrt(jnp.mean(xh * xh, axis=-1, keepdims=True) + NORM_EPS) * g


def _epi_dilated_prompt(wi, c0, acc, qn_ref, kn_ref, q_ref, k_ref, v_ref, kt_ref, vt_ref, scr, *, r):
    tm = scr.shape[2]
    tail_rows = kt_ref.shape[0]
    gain = (qn_ref[...] * ATTN_SCALE, kn_ref[...], None)[wi]
    out_ref = (q_ref, k_ref, v_ref)[wi]
    tail_ref = (None, kt_ref, vt_ref)[wi]
    for cols, xh in _chunk_heads(c0, acc):
        h = cols.start // HEAD_DIM
        y = xh if gain is None else _head_rms(xh, gain)
        slab = scr.at[wi, h]
        slab[...] = y
        if tail_ref is not None:
            tail_ref[:, cols] = y[tm - tail_rows:tm, :]
        if r == 1:
            out_ref[:, cols] = y.astype(out_ref.dtype)
        else:
            for c in range(r):
                out_ref[c, :, cols] = slab[pl.ds(c, tm // r, stride=r), :].astype(out_ref.dtype)


def _epi_dilated_sample(wi, c0, acc, qn_ref, kn_ref, o_ref):
    j = pl.program_id(0)
    gain = jnp.where(j < DIL_GROUPS, qn_ref[...] * ATTN_SCALE, kn_ref[...])
    for cols, xh in _chunk_heads(c0, acc):
        o_ref[:, cols] = jnp.where(j < 2 * DIL_GROUPS, _head_rms(xh, gain), xh)


def _rotary_tables(pos, rows):
    half = HEAD_DIM // 2
    inv = ROPE_BASE ** (-jnp.arange(half, dtype=F32) / half)
    ang = pos.astype(F32)[:, None] * inv[None, :]
    cos = jnp.cos(ang)
    sin = jnp.sin(ang)
    c = jnp.concatenate([cos, cos], axis=-1)
    s = jnp.concatenate([-sin, sin], axis=-1)
    tab = jnp.stack([jnp.stack([c, s]), jnp.stack([c, s]) * (HEAD_DIM ** -0.5)])
    reps = rows // pos.shape[0]
    return jnp.tile(tab, (1, 1, reps, 1))


def _proj_retention_and_gates(u, w_in, pos_tab, tm):
    n = u.shape[0]
    nblk = pos_tab.shape[2] // tm
    tile_out = lambda: pl.BlockSpec((tm, RET_WIDTH), lambda j, i: (i, j))
    qk_r = _proj(u, w_in, [COL_QR], 2, RET_WIDTH, tm, _epi_rotary, [pos_tab],
                 [pl.BlockSpec((None, 2, tm, HEAD_DIM), lambda j, i: (j, 0, i % nblk, 0))],
                 jax.ShapeDtypeStruct((n, 2 * RET_WIDTH), BF16), tile_out(), [], "proj_qk_ret")
    vg_r = _proj(u, w_in, [COL_VR], 2, RET_WIDTH, tm, _epi_value_gate, [], [],
                 jax.ShapeDtypeStruct((n, 2 * RET_WIDTH), BF16), tile_out(), [], "proj_vg_ret")
    ngate = w_in.shape[1] - COL_GATES
    gates = _proj(u, w_in, [COL_GATES], ngate // RET_WIDTH, RET_WIDTH, tm, _epi_sigmoid, [], [],
                  jax.ShapeDtypeStruct((n, ngate), BF16), tile_out(), [], "proj_gates")
    return qk_r, vg_r, gates


def _proj_dilated_prompt(u, w_in, q_norm, k_norm, gi, b, l, tm):
    w, r = DIL_PAIRS[gi]
    lw = min(w, l)
    bps = l // tm
    tail_rows = min(lw, tm)
    tail_blocks = lw // tail_rows
    cm = (b, r, l // r, DIL_OUT_WIDTH)
    if r == 1:
        cm_shape = jax.ShapeDtypeStruct((b * l, DIL_OUT_WIDTH), BF16)
        cm_spec = lambda: pl.BlockSpec((tm, DIL_OUT_WIDTH), lambda j, i: (i, 0))
    else:
        cm_shape = jax.ShapeDtypeStruct(cm, BF16)
        cm_spec = lambda: pl.BlockSpec((None, r, tm // r, DIL_OUT_WIDTH), lambda j, i: (i // bps, 0, i % bps, 0))
    tail_shape = jax.ShapeDtypeStruct((b, lw, DIL_OUT_WIDTH), F32)
    tail_spec = lambda: pl.BlockSpec(
        (None, tail_rows, DIL_OUT_WIDTH),
        lambda j, i: (i // bps, jnp.maximum(i % bps - (bps - tail_blocks), 0), 0))
    gspec = pl.BlockSpec((1, HEAD_DIM), lambda j, i: (0, 0))
    off = gi * DIL_OUT_WIDTH
    q, k, v, kt, vt = _proj(
        u, w_in, [COL_QA + off, COL_KA + off, COL_VA + off], 1, DIL_OUT_WIDTH, tm,
        functools.partial(_epi_dilated_prompt, r=r),
        [q_norm.reshape(1, HEAD_DIM), k_norm.reshape(1, HEAD_DIM)], [gspec, gspec],
        (cm_shape, cm_shape, cm_shape, tail_shape, tail_shape),
        (cm_spec(), cm_spec(), cm_spec(), tail_spec(), tail_spec()),
        [pltpu.VMEM((3, DIL_HPG, tm, HEAD_DIM), F32)], "proj_dilated_prompt")
    cls = lambda t: t.reshape(b * r, l // r, DIL_OUT_WIDTH)
    return cls(q), cls(k), cls(v), kt, vt


def _proj_dilated_sample(u, w_in, q_norm, k_norm):
    n = u.shape[0]
    gspec = pl.BlockSpec((1, HEAD_DIM), lambda j, i: (0, 0))
    return _proj(u, w_in, [COL_QA], 3 * DIL_GROUPS, DIL_OUT_WIDTH, n, _epi_dilated_sample,
                 [q_norm.reshape(1, HEAD_DIM), k_norm.reshape(1, HEAD_DIM)], [gspec, gspec],
                 jax.ShapeDtypeStruct((n, 3 * DIL_WIDTH), F32),
                 pl.BlockSpec((n, DIL_OUT_WIDTH), lambda j, i: (i, j)), [], "proj_dilated_sample")


def _retention_kernel(q_ref, k_ref, v_ref, g_ref, s0_ref, dm_ref, qd_ref, kd_ref, cd_ref,
                      y_ref, sf_ref, state, *, chunk, nchunks):
    t = pl.program_id(2)

    @pl.when(t == 0)
    def _():
        state[...] = s0_ref[...]

    dmask = dm_ref[...]
    qdec = qd_ref[...]
    kdec = kd_ref[...]
    cdec = cd_ref[...]
    s_prev = state[...]
    for ci in range(nchunks):
        rows = pl.ds(ci * chunk, chunk)
        q = q_ref[rows, :]
        k = k_ref[rows, :]
        v = v_ref[rows, :]
        sc = lax.dot_general(q, k, (((1,), (1,)), ((), ())), preferred_element_type=F32) * dmask
        intra = jnp.dot(sc.astype(BF16), v, preferred_element_type=F32)
        qd = (q.astype(F32) * qdec).astype(BF16)
        cross = jnp.dot(qd, s_prev.astype(BF16), preferred_element_type=F32)
        kd = (k.astype(F32) * kdec).astype(BF16)
        kv = lax.dot_general(kd, v, (((0,), (0,)), ((), ())), preferred_element_type=F32)
        s_prev = s_prev * cdec + kv
        o = intra + cross
        mu = jnp.mean(o, axis=-1, keepdims=True)
        oc = o - mu
        var = jnp.mean(oc * oc, axis=-1, keepdims=True)
        y = g_ref[rows, :].astype(F32) * (oc * lax.rsqrt(var + GN_EPS))
        y_ref[rows, :] = y.astype(y_ref.dtype)
    state[...] = s_prev

    @pl.when(t == pl.num_programs(2) - 1)
    def _():
        sf_ref[...] = state[...]


def _retention_decay(chunk, valid):
    lg = jnp.log1p(-jnp.exp2(-5.0 - jnp.arange(RET_HEADS, dtype=F32)))
    idx = jnp.arange(chunk, dtype=F32)
    rel = idx[:, None] - idx[None, :]
    dmask = jnp.where(rel[None] >= 0, jnp.exp(lg[:, None, None] * jnp.maximum(rel, 0.0)[None]), 0.0)
    qdec = jnp.exp(lg[:, None] * (idx + 1.0)[None, :])[..., None]
    kdec = jnp.exp(lg[:, None] * (valid - 1.0 - idx)[None, :])[..., None]
    kdec = jnp.where((idx < valid)[None, :, None], kdec, 0.0)
    cdec = jnp.exp(lg * valid)[:, None, None]
    return dmask, qdec, kdec, cdec


def _retention_prompt(qk_r, vg_r, s0, chunk, rows_per_step):
    b, l, _ = qk_r.shape
    h = RET_HEADS
    dmask, qdec, kdec, cdec = _retention_decay(chunk, chunk)
    nsteps = l // rows_per_step
    blk = lambda off: pl.BlockSpec((None, rows_per_step, HEAD_DIM), lambda bi, hi, ti: (bi, ti, hi + off))
    per_head = lambda shape: pl.BlockSpec((None,) + shape, lambda bi, hi, ti: (hi,) + (0,) * len(shape))
    state_spec = pl.BlockSpec((None, None, HEAD_DIM, HEAD_DIM), lambda bi, hi, ti: (bi, hi, 0, 0))
    return pl.pallas_call(
        functools.partial(_retention_kernel, chunk=chunk, nchunks=rows_per_step // chunk),
        out_shape=(jax.ShapeDtypeStruct((b, l, RET_WIDTH), BF16),
                   jax.ShapeDtypeStruct((b, h, HEAD_DIM, HEAD_DIM), F32)),
        grid=(b, h, nsteps),
        in_specs=[blk(0), blk(h), blk(0), blk(h), state_spec,
                  per_head((chunk, chunk)), per_head((chunk, 1)), per_head((chunk, 1)), per_head((1, 1))],
        out_specs=(blk(0), state_spec),
        scratch_shapes=[pltpu.VMEM((HEAD_DIM, HEAD_DIM), F32)],
        compiler_params=_params(("parallel", "parallel", "arbitrary"), 32),
        name="retention_prompt",
    )(qk_r, qk_r, vg_r, vg_r, s0, dmask, qdec, kdec, cdec)


def _retention_sample_kernel(q_ref, k_ref, v_ref, g_ref, s0_ref, dm_ref, qd_ref, kd_ref, cd_ref,
                             y_ref, sf_ref, *, ntok):
    for bi in range(q_ref.shape[0]):
        for h in range(RET_HEADS):
            q = q_ref[bi, h]
            k = k_ref[bi, h]
            v = v_ref[bi, h]
            s_prev = s0_ref[bi, h]
            dmask = dm_ref[h]
            o = jnp.dot(q * qd_ref[h], s_prev, preferred_element_type=F32)
            for j in range(ntok):
                sj = jnp.sum(q * k[j:j + 1, :], axis=-1, keepdims=True) * dmask[:, j:j + 1]
                o = o + sj * v[j:j + 1, :]
            kd = k * kd_ref[h]
            kv = lax.dot_general(kd, v, (((0,), (0,)), ((), ())), preferred_element_type=F32)
            sf_ref[bi, h] = s_prev * cd_ref[h] + kv
            mu = jnp.mean(o, axis=-1, keepdims=True)
            oc = o - mu
            var = jnp.mean(oc * oc, axis=-1, keepdims=True)
            y_ref[bi, h] = g_ref[bi, h] * (oc * lax.rsqrt(var + GN_EPS))


def _retention_sample(q, k, v, g, s0, ntok):
    bd, h, tp, _ = q.shape
    dmask, qdec, kdec, cdec = _retention_decay(tp, ntok)
    nb = math.gcd(bd, SAMPLE_SEQS_PER_STEP)
    tok = pl.BlockSpec((nb, h, tp, HEAD_DIM), lambda bi: (bi, 0, 0, 0))
    st = pl.BlockSpec((nb, h, HEAD_DIM, HEAD_DIM), lambda bi: (bi, 0, 0, 0))
    const = lambda a: pl.BlockSpec(a.shape, lambda bi: (0,) * a.ndim)
    return pl.pallas_call(
        functools.partial(_retention_sample_kernel, ntok=ntok),
        out_shape=(jax.ShapeDtypeStruct((bd, h, tp, HEAD_DIM), F32),
                   jax.ShapeDtypeStruct((bd, h, HEAD_DIM, HEAD_DIM), F32)),
        grid=(bd // nb,),
        in_specs=[tok, tok, tok, tok, st, const(dmask), const(qdec), const(kdec), const(cdec)],
        out_specs=(tok, st),
        compiler_params=_params(("parallel",), 32),
        name="retention_sample",
    )(q, k, v, g, s0, dmask, qdec, kdec, cdec)


def _t5_bucket(dist):
    max_exact = REL_BUCKETS // 2
    d = jnp.maximum(dist, 0)
    df = jnp.maximum(d, 1).astype(F32)
    large = max_exact + (jnp.log(df / max_exact) / math.log(REL_MAX_DIST / max_exact)
                         * (REL_BUCKETS - max_exact)).astype(jnp.int32)
    large = jnp.minimum(large, REL_BUCKETS - 1)
    return jnp.where(d < max_exact, d, large)


def _bias_lookup(tab, dist):
    onehot = _t5_bucket(dist)[..., None] == jnp.arange(REL_BUCKETS)
    return jnp.sum(jnp.where(onehot[..., None], tab.astype(F32), 0.0), axis=-2)


def _dilated_kernel(q_ref, kp_ref, kc_ref, vp_ref, vc_ref, bias_ref, o_ref, lse_ref, kfull, vfull, *, nsub):
    i = pl.program_id(1)
    blk = DIL_BLOCK
    kfull[0:blk, :] = kp_ref[...]
    kfull[blk:, :] = kc_ref[...]
    vfull[0:blk, :] = vp_ref[...]
    vfull[blk:, :] = vc_ref[...]
    col = lax.broadcasted_iota(jnp.int32, (blk, 2 * blk), 1)
    lane_head = lax.broadcasted_iota(jnp.int32, (blk, HEAD_DIM), 1) // LSE_LANES
    for s in range(nsub):
        rows = pl.ds(s * blk, blk)
        win = pl.ds(s * blk, 2 * blk)
        lse_tile = jnp.zeros((blk, HEAD_DIM), F32)
        for h in range(DIL_HPG):
            cols = pl.ds(h * HEAD_DIM, HEAD_DIM)
            q = q_ref[rows, cols]
            kw = kfull[win, cols]
            vw = vfull[win, cols]
            sc = lax.dot_general(q, kw, (((1,), (1,)), ((), ())), preferred_element_type=F32) + bias_ref[h]
            if s == 0:
                sc = jnp.where((col >= blk) | (i > 0), sc, -jnp.inf)
            m = jnp.max(sc, axis=-1, keepdims=True)
            p = jnp.exp(sc - m)
            l = jnp.sum(p, axis=-1, keepdims=True)
            o = jnp.dot(p.astype(BF16), vw, preferred_element_type=F32) / l
            o_ref[rows, cols] = o.astype(o_ref.dtype)
            lse_tile = jnp.where(lane_head == h, m + jnp.log(l), lse_tile)
        lse_ref[rows, :] = lse_tile


def _dilated_bias(bias_tab, r):
    blk = DIL_BLOCK
    qi = jnp.arange(blk)[:, None]
    kj = jnp.arange(2 * blk)[None, :]
    dc = blk + qi - kj
    band = (dc >= 0) & (dc <= blk)
    bias = _bias_lookup(bias_tab, dc * r).transpose(2, 0, 1)
    return jnp.where(band[None], bias, -jnp.inf)


def _dilated_prompt(q, k, v, bias, nsub):
    n, lc, w = q.shape
    blk = DIL_BLOCK
    tq = nsub * blk
    cur = pl.BlockSpec((None, tq, w), lambda ni, i: (ni, i, 0))
    prev = pl.BlockSpec((None, blk, w), lambda ni, i: (ni, jnp.maximum(i * nsub - 1, 0), 0))
    return pl.pallas_call(
        functools.partial(_dilated_kernel, nsub=nsub),
        out_shape=(jax.ShapeDtypeStruct((n, lc, w), F32),
                   jax.ShapeDtypeStruct((n, lc, HEAD_DIM), F32)),
        grid=(n, lc // tq),
        in_specs=[cur, prev, cur, prev, cur, pl.BlockSpec(bias.shape, lambda ni, i: (0, 0, 0))],
        out_specs=(cur, pl.BlockSpec((None, tq, HEAD_DIM), lambda ni, i: (ni, i, 0))),
        scratch_shapes=[pltpu.VMEM((tq + blk, w), BF16), pltpu.VMEM((tq + blk, w), BF16)],
        compiler_params=_params(("parallel", "parallel"), 32),
        name="dilated_prompt",
    )(q, k, k, v, v, bias)


def _dilated_sample_kernel(q_ref, kn_ref, vn_ref, c0_ref, c1_ref, c2_ref, bc_ref, bn_ref,
                           o_ref, lse_ref, *, ntok):
    caches = (c0_ref, c1_ref, c2_ref)
    for gi in range(DIL_GROUPS):
        cache = caches[gi]
        nclass = cache.shape[1]
        for t in range(ntok):
            cls = t if nclass > 1 else 0
            qt = q_ref[t, gi]
            kc = cache[:, cls, 0]
            vc = cache[:, cls, 1]
            sc = jnp.sum(kc * qt[None], axis=-1, keepdims=True) + bc_ref[gi, t]
            sn = jnp.sum(kn_ref[:, gi] * qt[None], axis=-1, keepdims=True) + bn_ref[gi, t]
            m = jnp.maximum(jnp.max(sc, axis=0), jnp.max(sn, axis=0))
            pc = jnp.exp(sc - m[None])
            pn = jnp.exp(sn - m[None])
            l = jnp.sum(pc, axis=0) + jnp.sum(pn, axis=0)
            o = jnp.sum(pc * vc, axis=0) + jnp.sum(pn * vn_ref[:, gi], axis=0)
            o_ref[t, gi] = o / l
            lse_ref[t, gi] = m + jnp.log(l)


def _dilated_sample(q, kn, vn, caches, bias_cache, bias_new, ntok):
    bd = q.shape[0]
    tok = pl.BlockSpec((None, ntok, DIL_GROUPS, DIL_HPG, HEAD_DIM), lambda bi: (bi, 0, 0, 0, 0))
    cspecs = []
    for c in caches:
        ncls = min(c.shape[2], ntok)
        cspecs.append(pl.BlockSpec((None, DIL_BLOCK, ncls, 2, DIL_HPG, HEAD_DIM),
                                   lambda bi: (bi, 0, 0, 0, 0, 0)))
    const = lambda a: pl.BlockSpec(a.shape, lambda bi: (0,) * a.ndim)
    return pl.pallas_call(
        functools.partial(_dilated_sample_kernel, ntok=ntok),
        out_shape=(jax.ShapeDtypeStruct(q.shape, F32),
                   jax.ShapeDtypeStruct((bd, ntok, DIL_GROUPS, DIL_HPG, 1), F32)),
        grid=(bd,),
        in_specs=[tok, tok, tok] + cspecs + [const(bias_cache), const(bias_new)],
        out_specs=(tok, pl.BlockSpec((None, ntok, DIL_GROUPS, DIL_HPG, 1), lambda bi: (bi, 0, 0, 0, 0))),
        compiler_params=_params(("parallel",), 32),
        name="dilated_sample",
    )(q, kn, vn, *caches, bias_cache, bias_new)


def _dilated_sample_bias(rel_bias, ntok):
    bc, bn = [], []
    m = jnp.arange(DIL_BLOCK)
    tn = jnp.arange(ntok)
    for gi, (_, r) in enumerate(DIL_PAIRS):
        tab = rel_bias[:, gi * DIL_HPG:(gi + 1) * DIL_HPG]
        rows_c, rows_n = [], []
        for t in range(ntok):
            if r == 1:
                jc = DIL_BLOCK + t - m
                okc = m >= t
                jn = t - tn
                okn = tn <= t
            else:
                jc = DIL_BLOCK - m
                okc = jnp.ones_like(m, bool)
                jn = jnp.zeros_like(tn)
                okn = tn == t
            rows_c.append(jnp.where(okc[:, None], _bias_lookup(tab, jc * r), -jnp.inf))
            rows_n.append(jnp.where(okn[:, None], _bias_lookup(tab, jn * r), -jnp.inf))
        bc.append(jnp.stack(rows_c))
        bn.append(jnp.stack(rows_n))
    return jnp.stack(bc)[..., None], jnp.stack(bn)[..., None]


def _cache_shift_kernel(c_ref, n_ref, o_ref, *, ntok):
    i = pl.program_id(1)
    last = pl.num_programs(1) - 1
    rows = o_ref.shape[0]

    @pl.when(i < last)
    def _():
        o_ref[...] = c_ref[0]

    @pl.when(i == last)
    def _():
        o_ref[0:rows - ntok] = c_ref[0, ntok:rows]
        o_ref[rows - ntok:rows] = n_ref[...]


def _cache_shift_dma_kernel(c_ref, n_ref, o_ref, sem, *, nsplit):
    bd, lb, _ = c_ref.shape
    ntok = n_ref.shape[1]
    per = bd // nsplit
    copies = []
    for sp in range(nsplit):
        bs = pl.ds(sp * per, per)
        copies.append(pltpu.make_async_copy(c_ref.at[bs, pl.ds(ntok, lb - ntok)],
                                            o_ref.at[bs, pl.ds(0, lb - ntok)], sem.at[2 * sp]))
        copies.append(pltpu.make_async_copy(n_ref.at[bs], o_ref.at[bs, pl.ds(lb - ntok, ntok)],
                                            sem.at[2 * sp + 1]))
    for c in copies:
        c.start()
    for c in copies:
        c.wait()


def _cache_shift_dma(cache, new):
    shape = cache.shape
    bd = shape[0]
    cache = cache.reshape(bd, -1, HEAD_DIM)
    new = new.reshape(bd, -1, HEAD_DIM)
    nsplit = math.gcd(bd, 8)
    any_spec = pl.BlockSpec(memory_space=pl.ANY)
    out = pl.pallas_call(
        functools.partial(_cache_shift_dma_kernel, nsplit=nsplit),
        out_shape=jax.ShapeDtypeStruct(cache.shape, cache.dtype),
        in_specs=[any_spec, any_spec],
        out_specs=any_spec,
        scratch_shapes=[pltpu.SemaphoreType.DMA((2 * nsplit,))],
        name="cache_shift_dma",
    )(cache, new)
    return out.reshape(shape)


def _cache_shift(cache, new):
    shape = cache.shape
    bd, lb = shape[:2]
    per_tok = math.prod(shape[2:-1])
    ntok = new.shape[1] * per_tok
    lb = lb * per_tok
    cache = cache.reshape(bd, lb, HEAD_DIM)
    new = new.reshape(bd, ntok, HEAD_DIM)
    rows = min(lb, 4096)
    out = pl.pallas_call(
        functools.partial(_cache_shift_kernel, ntok=ntok),
        out_shape=jax.ShapeDtypeStruct(cache.shape, cache.dtype),
        grid=(bd, lb // rows),
        in_specs=[pl.BlockSpec(tuple(pl.Element(s) for s in (1, rows, HEAD_DIM)),
                               lambda b, i: (b, pl.multiple_of(jnp.minimum(i * rows + ntok, lb - rows), 8), 0)),
                  pl.BlockSpec((None, ntok, HEAD_DIM), lambda b, i: (b, 0, 0))],
        out_specs=pl.BlockSpec((None, rows, HEAD_DIM), lambda b, i: (b, i, 0)),
        compiler_params=_params(("parallel", "arbitrary"), 40),
        name="cache_shift",
    )(cache, new)
    return out.reshape(shape)


def _merge_kernel(o0_ref, o1_ref, o2_ref, l0_ref, l1_ref, l2_ref, yr_ref, gt_ref, wr_ref, wd_ref,
                  out_ref, oa_ref, *nat, dilations):
    tm = out_ref.shape[0]
    o_nat, l_nat = [], []
    for g, (o_ref, l_ref, r) in enumerate(zip((o0_ref, o1_ref, o2_ref), (l0_ref, l1_ref, l2_ref), dilations)):
        if r == 1:
            o_nat.append([o_ref[:, _head_cols(h)] for h in range(DIL_HPG)])
            l_nat.append(l_ref[...])
        else:
            on, ln = nat[2 * g], nat[2 * g + 1]
            for c in range(r):
                ln[pl.ds(c, tm // r, stride=r), :] = l_ref[c]
                for h in range(DIL_HPG):
                    on[h, pl.ds(c, tm // r, stride=r), :] = o_ref[c, :, _head_cols(h)]
            o_nat.append([on[h] for h in range(DIL_HPG)])
            l_nat.append(ln[...])
    m = jnp.maximum(jnp.maximum(l_nat[0], l_nat[1]), l_nat[2])
    es = [jnp.exp(l - m) for l in l_nat]
    den = es[0] + es[1] + es[2]
    for h in range(DIL_HPG):
        lane = h * LSE_LANES
        acc = None
        for g in range(DIL_GROUPS):
            term = (es[g][:, lane:lane + 1] / den[:, lane:lane + 1]) * o_nat[g][h]
            acc = term if acc is None else acc + term
        oa_ref[:, _head_cols(h)] = acc.astype(oa_ref.dtype)
    d = out_ref.shape[1]
    for c0 in range(0, d, PROJ_CHUNK):
        cols = slice(c0, c0 + PROJ_CHUNK)
        yr = jnp.dot(yr_ref[...], wr_ref[:, cols], preferred_element_type=F32)
        ya = jnp.dot(oa_ref[...], wd_ref[:, cols], preferred_element_type=F32)
        g_r = gt_ref[:, cols].astype(F32)
        g_a = gt_ref[:, d + c0:d + c0 + PROJ_CHUNK].astype(F32)
        out_ref[:, cols] = (g_r * yr + g_a * ya).astype(out_ref.dtype)


def _merge(o_g, lse_g, dilations, b, l, yr_in, gates, w_ret_b, w_dil_b, tm):
    n = yr_in.shape[0]
    d = w_ret_b.shape[1]
    bps = l // tm
    row = lambda w: pl.BlockSpec((tm, w), lambda i: (i, 0))
    const = lambda a: pl.BlockSpec(a.shape, lambda i: (0, 0))

    def group_specs(width):
        specs = []
        for r in dilations:
            if r == 1:
                specs.append(row(width))
            else:
                specs.append(pl.BlockSpec((None, r, tm // r, width), lambda i: (i // bps, 0, i % bps, 0)))
        return specs

    view = lambda t, r, width: t.reshape(n, width) if r == 1 else t.reshape(b, r, l // r, width)
    o_in = [view(t, r, DIL_OUT_WIDTH) for t, r in zip(o_g, dilations)]
    l_in = [view(t, r, HEAD_DIM) for t, r in zip(lse_g, dilations)]
    scratch = [pltpu.VMEM((tm, DIL_OUT_WIDTH), BF16)]
    for _ in dilations:
        scratch += [pltpu.VMEM((DIL_HPG, tm, HEAD_DIM), F32), pltpu.VMEM((tm, HEAD_DIM), F32)]
    return pl.pallas_call(
        functools.partial(_merge_kernel, dilations=tuple(dilations)),
        out_shape=jax.ShapeDtypeStruct((n, d), BF16),
        grid=(n // tm,),
        in_specs=group_specs(DIL_OUT_WIDTH) + group_specs(HEAD_DIM) + [row(RET_WIDTH), row(2 * d),
                                                                      const(w_ret_b), const(w_dil_b)],
        out_specs=row(d),
        scratch_shapes=scratch,
        compiler_params=_params(("parallel",), 48),
        name="merge_branches",
    )(*o_in, *l_in, yr_in, gates, w_ret_b, w_dil_b)


ROUTER_ROWS = 8 + N_EXPERTS


def _route_tile(lt, tri, before):
    lg = lt[0:N_GROUPS]
    gmax = jnp.max(lg, axis=0, keepdims=True)
    w_coarse = 1.0 / jnp.sum(jnp.exp(lg - gmax), axis=0, keepdims=True)
    gid = lax.broadcasted_iota(jnp.int32, lg.shape, 0)
    gsel = jnp.min(jnp.where(lg == gmax, gid, N_GROUPS), axis=0, keepdims=True)
    le = jnp.zeros((EXPERTS_PER_GROUP, lt.shape[1]), F32)
    for g in range(N_GROUPS):
        le = jnp.where(gsel == g, lt[8 + g * EXPERTS_PER_GROUP:8 + (g + 1) * EXPERTS_PER_GROUP], le)
    eid = lax.broadcasted_iota(jnp.int32, le.shape, 0)
    v1 = jnp.max(le, axis=0, keepdims=True)
    i1 = jnp.min(jnp.where(le == v1, eid, EXPERTS_PER_GROUP), axis=0, keepdims=True)
    le2 = jnp.where(eid == i1, -jnp.inf, le)
    v2 = jnp.max(le2, axis=0, keepdims=True)
    i2 = jnp.min(jnp.where(le2 == v2, eid, EXPERTS_PER_GROUP), axis=0, keepdims=True)
    e21 = jnp.exp(v2 - v1)
    w1 = w_coarse / (1.0 + e21)
    w2 = w_coarse * e21 / (1.0 + e21)
    e1 = gsel * EXPERTS_PER_GROUP + i1
    e2 = gsel * EXPERTS_PER_GROUP + i2

    xid = lax.broadcasted_iota(jnp.int32, (N_EXPERTS, lt.shape[1]), 0)
    oh1 = jnp.where(xid == e1, 1.0, 0.0)
    oh2 = jnp.where(xid == e2, 1.0, 0.0)
    n1 = jnp.sum(oh1, axis=1, keepdims=True)
    n2 = jnp.sum(oh2, axis=1, keepdims=True)
    p1 = jnp.dot(oh1.astype(BF16), tri, preferred_element_type=F32) + before
    p2 = jnp.dot(oh2.astype(BF16), tri, preferred_element_type=F32) + before + n1
    rank1 = jnp.sum(oh1 * p1, axis=0, keepdims=True)
    rank2 = jnp.sum(oh2 * p2, axis=0, keepdims=True)

    row = lax.broadcasted_iota(jnp.int32, (8, lt.shape[1]), 0)
    ei = jnp.where(row == 0, e1, jnp.where(row == 1, e2, jnp.where(
        row == 2, rank1.astype(jnp.int32), jnp.where(row == 3, rank2.astype(jnp.int32), 0))))
    wt = jnp.where(row == 0, w1, jnp.where(row == 1, w2, 0.0))
    return ei, wt, before + n1 + n2


def _outproj_router_kernel(mg_ref, x_ref, wo_ref, g2_ref, wr_ref, br_ref, tri_ref, cin_ref,
                           h_ref, u2_ref, ei_ref, wt_ref, cnt_ref):
    @pl.when(pl.program_id(0) == 0)
    def _():
        cnt_ref[...] = cin_ref[...]

    sub = tri_ref.shape[0]
    subtiles = [slice(s0, s0 + sub) for s0 in range(0, h_ref.shape[0], sub)]
    for rows in subtiles:
        h_ref[rows, :] = x_ref[rows, :] + jnp.dot(mg_ref[rows, :], wo_ref[...], preferred_element_type=F32)
    counts = cnt_ref[...]
    for rows in subtiles:
        h = h_ref[rows, :]
        u2 = h * lax.rsqrt(jnp.mean(h * h, axis=-1, keepdims=True) + NORM_EPS) * g2_ref[...]
        u2_ref[rows, :] = u2
        lt = lax.dot_general(wr_ref[...], u2.astype(BF16), (((1,), (1,)), ((), ())),
                             preferred_element_type=F32) + br_ref[...]
        ei, wt, counts = _route_tile(lt, tri_ref[...], counts)
        ei_ref[:, rows] = ei
        wt_ref[:, rows] = wt
    cnt_ref[...] = counts


def _outproj_router(merged, x, w_o_b, norm_ffn, wr_t, br_t, counts_in, tm):
    n, d = x.shape
    sub = min(tm, ROUTER_SUB)
    tri = (jnp.arange(sub)[:, None] < jnp.arange(sub)[None, :]).astype(BF16)
    row = lambda: pl.BlockSpec((tm, d), lambda i: (i, 0))
    lane = lambda: pl.BlockSpec((8, tm), lambda i: (0, i))
    const = lambda a: pl.BlockSpec(a.shape, lambda i: (0, 0))
    return pl.pallas_call(
        _outproj_router_kernel,
        out_shape=(jax.ShapeDtypeStruct((n, d), F32), jax.ShapeDtypeStruct((n, d), F32),
                   jax.ShapeDtypeStruct((8, n), jnp.int32), jax.ShapeDtypeStruct((8, n), F32),
                   jax.ShapeDtypeStruct((N_EXPERTS, 1), F32)),
        grid=(n // tm,),
        in_specs=[row(), row(), const(w_o_b), pl.BlockSpec((1, d), lambda i: (0, 0)),
                  const(wr_t), const(br_t), const(tri), const(counts_in)],
        out_specs=(row(), row(), lane(), lane(), const(counts_in)),
        compiler_params=_params(("arbitrary",), 60),
        name="outproj_router",
    )(merged, x, w_o_b, norm_ffn.reshape(1, d), wr_t, br_t, tri, counts_in)


def _router_weights(w_rg, b_rg, w_re, b_re):
    d = w_rg.shape[0]
    wr = jnp.zeros((ROUTER_ROWS, d), F32)
    wr = wr.at[0:N_GROUPS].set(w_rg.T)
    wr = wr.at[8:].set(w_re.transpose(0, 2, 1).reshape(N_EXPERTS, d))
    br = jnp.zeros((ROUTER_ROWS, 1), F32)
    br = br.at[0:N_GROUPS, 0].set(b_rg)
    br = br.at[8:, 0].set(b_re.reshape(N_EXPERTS))
    return wr.astype(BF16), br


DMA_UNROLL = 8


def _dispatch_kernel(pos_p_ref, pos_s_ref, zero_ref, up_ref, us_ref, xs_ref, zbuf, stage, sem, *, tm):
    i = pl.program_id(0)
    last = pl.num_programs(0) - 1
    ntiles = xs_ref.shape[0] // MOE_TILE

    def tile_copy(t):
        return pltpu.make_async_copy(zbuf, xs_ref.at[pl.ds(pl.multiple_of(t * MOE_TILE, MOE_TILE), MOE_TILE)],
                                     sem.at[2])

    @pl.when(i == 0)
    def _():
        zbuf[...] = jnp.zeros_like(zbuf)
        for op in ("start", "wait"):
            def per_tile(t, c, op=op):
                @pl.when(zero_ref[t] != 0)
                def _():
                    getattr(tile_copy(t), op)()
                return c

            lax.fori_loop(0, ntiles, per_tile, 0)

    def scatter_rows(op, src_ref, pos_ref, base, nrows, ntok_total, row_sem):
        def body(r, c):
            for kk in range(TOP_K):
                slot = pos_ref[kk * ntok_total + base + r]
                copy = pltpu.make_async_copy(src_ref.at[pl.ds(r, 1)], xs_ref.at[pl.ds(slot, 1)], row_sem)
                getattr(copy, op)()
            return c

        lax.fori_loop(0, nrows, body, 0, unroll=DMA_UNROLL)

    prompt_rows = last * tm
    cur = i % 2

    @pl.when(i < last)
    def _():
        stage[cur] = up_ref[...]
        scatter_rows("start", stage.at[cur], pos_p_ref, i * tm, tm, prompt_rows, sem.at[cur])

    @pl.when(i > 0)
    def _():
        scatter_rows("wait", stage.at[1 - cur], pos_p_ref, (i - 1) * tm, tm, prompt_rows, sem.at[1 - cur])

    @pl.when(i == last)
    def _():
        for op in ("start", "wait"):
            scatter_rows(op, us_ref, pos_s_ref, 0, us_ref.shape[0], us_ref.shape[0], sem.at[2])


def _dispatch(pos_p, pos_s, zero_tiles, u2p, u2s, tm):
    n, d = u2p.shape
    nsteps = n // tm
    grid_spec = pltpu.PrefetchScalarGridSpec(
        num_scalar_prefetch=3,
        grid=(nsteps + 1,),
        in_specs=[pl.BlockSpec((tm, d), lambda i, *_: (jnp.minimum(i, nsteps - 1), 0)),
                  pl.BlockSpec(u2s.shape, lambda i, *_: (0, 0))],
        out_specs=pl.BlockSpec(memory_space=pl.ANY),
        scratch_shapes=[pltpu.VMEM((MOE_TILE, d), F32), pltpu.VMEM((2, tm, d), F32),
                        pltpu.SemaphoreType.DMA((3,))],
    )
    return pl.pallas_call(
        functools.partial(_dispatch_kernel, tm=tm),
        out_shape=jax.ShapeDtypeStruct((zero_tiles.shape[0] * MOE_TILE, d), F32),
        grid_spec=grid_spec,
        compiler_params=_params(("arbitrary",), 32),
        name="moe_dispatch",
    )(pos_p, pos_s, zero_tiles, u2p, u2s)


def _gmm_kernel(te_ref, nt_ref, x_ref, wg_ref, wu_ref, wd_ref, o_ref, wgb, wub, wdb):
    i = pl.program_id(0)
    fresh = jnp.logical_or(i == 0, te_ref[i] != te_ref[jnp.maximum(i - 1, 0)])

    @pl.when(jnp.logical_and(fresh, i < nt_ref[0]))
    def _():
        wgb[...] = wg_ref[...].astype(BF16)
        wub[...] = wu_ref[...].astype(BF16)
        wdb[...] = wd_ref[...].astype(BF16)

    @pl.when(i < nt_ref[0])
    def _():
        x = x_ref[...].astype(BF16)
        acc = None
        for c0 in range(0, wgb.shape[1], PROJ_CHUNK):
            cols = slice(c0, c0 + PROJ_CHUNK)
            a = jnp.dot(x, wgb[:, cols], preferred_element_type=F32)
            b = jnp.dot(x, wub[:, cols], preferred_element_type=F32)
            hm = (a * jax.nn.sigmoid(a) * b).astype(BF16)
            part = jnp.dot(hm, wdb[cols, :], preferred_element_type=F32)
            acc = part if acc is None else acc + part
        o_ref[...] = acc

    @pl.when(i >= nt_ref[0])
    def _():
        o_ref[...] = jnp.zeros_like(o_ref)


def _gmm(tile_expert, ntiles_used, xs, w_gate, w_up, w_down):
    npad, d = xs.shape
    f = w_gate.shape[-1]
    ntiles = npad // MOE_TILE
    epg = w_gate.shape[1]

    def xmap(i, te, nt):
        return (jnp.minimum(i, nt[0] - 1), 0)

    def wmap(i, te, nt):
        e = te[i]
        return (e // epg, e % epg, 0, 0)

    grid_spec = pltpu.PrefetchScalarGridSpec(
        num_scalar_prefetch=2,
        grid=(ntiles,),
        in_specs=[pl.BlockSpec((MOE_TILE, d), xmap),
                  pl.BlockSpec((None, None, d, f), wmap),
                  pl.BlockSpec((None, None, d, f), wmap),
                  pl.BlockSpec((None, None, f, d), wmap)],
        out_specs=pl.BlockSpec((MOE_TILE, d), lambda i, te, nt: (i, 0)),
        scratch_shapes=[pltpu.VMEM((d, f), BF16), pltpu.VMEM((d, f), BF16), pltpu.VMEM((f, d), BF16)],
    )
    return pl.pallas_call(
        _gmm_kernel,
        out_shape=jax.ShapeDtypeStruct((npad, d), F32),
        grid_spec=grid_spec,
        compiler_params=_params(("arbitrary",), 56),
        name="moe_grouped_matmul",
    )(tile_expert, ntiles_used, xs, w_gate, w_up, w_down)


def _combine_kernel(pos_ref, h_ref, wt_ref, os_ref, y_ref, g0, g1, sem, *, tm, ntok_total):
    i = pl.program_id(0)
    nsteps = pl.num_programs(0)
    cur = i % 2

    def row_copy(step, r, kk, slot):
        src = pos_ref[kk * ntok_total + step * tm + r]
        buf = (g0, g1)[kk]
        return pltpu.make_async_copy(os_ref.at[pl.ds(src, 1)], buf.at[slot, pl.ds(r, 1)], sem.at[slot])

    def issue_step(step, slot):
        def body(r, c):
            row_copy(step, r, 0, slot).start()
            row_copy(step, r, 1, slot).start()
            return c

        lax.fori_loop(0, tm, body, 0, unroll=DMA_UNROLL)

    @pl.when(i == 0)
    def _():
        issue_step(0, 0)

    @pl.when(i + 1 < nsteps)
    def _():
        issue_step(i + 1, 1 - cur)

    def drain(r, c):
        row_copy(i, r, 0, cur).wait()
        row_copy(i, r, 1, cur).wait()
        return c

    lax.fori_loop(0, tm, drain, 0, unroll=DMA_UNROLL)
    y_ref[...] = h_ref[...] + wt_ref[:, 0:1] * g0[cur] + wt_ref[:, 1:2] * g1[cur]


def _combine(pos_flat, h, wt_cols, out_sorted, tm):
    n, d = h.shape
    grid_spec = pltpu.PrefetchScalarGridSpec(
        num_scalar_prefetch=1,
        grid=(n // tm,),
        in_specs=[pl.BlockSpec((tm, d), lambda i, pos: (i, 0)),
                  pl.BlockSpec((tm, 8), lambda i, pos: (i, 0)),
                  pl.BlockSpec(memory_space=pl.ANY)],
        out_specs=pl.BlockSpec((tm, d), lambda i, pos: (i, 0)),
        scratch_shapes=[pltpu.VMEM((2, tm, d), F32), pltpu.VMEM((2, tm, d), F32),
                        pltpu.SemaphoreType.DMA((2,))],
    )
    return pl.pallas_call(
        functools.partial(_combine_kernel, tm=tm, ntok_total=n),
        out_shape=jax.ShapeDtypeStruct((n, d), F32),
        grid_spec=grid_spec,
        compiler_params=_params(("arbitrary",), 32),
        name="moe_combine",
    )(pos_flat, h, wt_cols, out_sorted)


def _moe_plan(counts, route_sets):
    npairs = sum(e.shape[1] for e, _ in route_sets) * TOP_K
    tiles = (counts + MOE_TILE - 1) // MOE_TILE
    tile_end = jnp.cumsum(tiles)
    offs = (tile_end - tiles) * MOE_TILE
    ids = jnp.arange(N_EXPERTS)
    slots = []
    for experts, ranks in route_sets:
        base = jnp.sum(jnp.where(experts[..., None] == ids, offs, 0), axis=-1)
        slots.append((base + ranks).reshape(-1).astype(jnp.int32))
    ntiles = (npairs + N_EXPERTS * (MOE_TILE - 1)) // MOE_TILE
    tile_expert = jnp.sum(tile_end[None, :] <= jnp.arange(ntiles)[:, None], axis=1)
    tile_expert = jnp.minimum(tile_expert, N_EXPERTS - 1).astype(jnp.int32)
    tile_ids = jnp.arange(ntiles)
    is_last = jnp.any((tile_ids[:, None] == tile_end[None, :] - 1) & (tiles[None, :] > 0), axis=1)
    zero_tiles = (is_last | (tile_ids >= tile_end[-1])).astype(jnp.int32)
    return slots, tile_expert, tile_end[-1:].astype(jnp.int32), zero_tiles


def _mixers_prompt(x, wts, counts_in):
    b, l, d = x.shape
    n = b * l
    tm = ROW_TILE
    xf = x.reshape(n, d)
    u = _rmsnorm(xf, wts["norm_attn"], tm)
    tab = _rotary_tables(jnp.arange(l), l)
    qk_r, vg_r, gates = _proj_retention_and_gates(u, wts["w_in"], tab, 4 * tm)

    s0 = jnp.zeros((b, RET_HEADS, HEAD_DIM, HEAD_DIM), F32)
    yr_in, s_fin = _retention_prompt(qk_r.reshape(b, l, -1), vg_r.reshape(b, l, -1), s0, 256, min(l, 2048))

    o_g, lse_g, bufs = [], [], []
    for gi, (w, r) in enumerate(DIL_PAIRS):
        qg, kg, vg, kt, vt = _proj_dilated_prompt(u, wts["w_in"], wts["q_norm"], wts["k_norm"], gi, b, l, tm)
        bias = _dilated_bias(wts["rel_bias"][:, gi * DIL_HPG:(gi + 1) * DIL_HPG], r)
        o, lse = _dilated_prompt(qg, kg, vg, bias, min(8, l // r // DIL_BLOCK))
        o_g.append(o)
        lse_g.append(lse)
        bufs.append(jnp.stack([kt, vt], axis=2).reshape(b, kt.shape[1], 2, DIL_HPG, HEAD_DIM))

    merged = _merge(o_g, lse_g, [r for _, r in DIL_PAIRS], b, l, yr_in.reshape(n, RET_WIDTH), gates,
                    wts["w_ret_out"], wts["w_dil_out"], tm)
    h, u2, ei, wt, counts = _outproj_router(merged, xf, wts["w_o"], wts["norm_ffn"], wts["wr_t"], wts["br_t"],
                                            counts_in, ROUTER_TILE)
    return h, u2, ei, wt, counts, s_fin, bufs


def _mixers_sample(x, caches, state, wts, counts_in):
    bd, t, d = x.shape
    n = bd * t
    xf = x.reshape(n, d)
    u = _rmsnorm(xf, wts["norm_attn"], n)
    tab = _rotary_tables(PAST_LEN + jnp.arange(t), n)
    qk_r, vg_r, gates = _proj_retention_and_gates(u, wts["w_in"], tab, n)
    qkv_a = _proj_dilated_sample(u, wts["w_in"], wts["q_norm"], wts["k_norm"])

    def heads(a):
        a = a.astype(F32).reshape(bd, t, RET_HEADS, HEAD_DIM).transpose(0, 2, 1, 3)
        return jnp.pad(a, ((0, 0), (0, 0), (0, 8 - t), (0, 0)))

    y_r, s_new = _retention_sample(heads(qk_r[:, :RET_WIDTH]), heads(qk_r[:, RET_WIDTH:]),
                                   heads(vg_r[:, :RET_WIDTH]), heads(vg_r[:, RET_WIDTH:]), state, t)
    yr_in = y_r[:, :, :t].transpose(0, 2, 1, 3).reshape(n, RET_WIDTH).astype(BF16)

    grp = lambda a: a.reshape(bd, t, DIL_GROUPS, DIL_HPG, HEAD_DIM)
    qa, ka, va = (grp(qkv_a[:, s * DIL_WIDTH:(s + 1) * DIL_WIDTH]) for s in range(3))
    cviews = [c.reshape(bd, DIL_BLOCK, r, 2, DIL_HPG, HEAD_DIM) for c, (_, r) in zip(caches, DIL_PAIRS)]
    bias_c, bias_n = _dilated_sample_bias(wts["rel_bias"], t)
    o, lse = _dilated_sample(qa, ka, va, cviews, bias_c, bias_n, t)
    o_g = [o[:, :, gi].reshape(n, 1, DIL_OUT_WIDTH) for gi in range(DIL_GROUPS)]
    lse_g = [jnp.repeat(lse[:, :, gi].reshape(n, DIL_HPG), LSE_LANES, axis=1).reshape(n, 1, HEAD_DIM)
             for gi in range(DIL_GROUPS)]

    news = [jnp.stack([ka[:, :, gi], va[:, :, gi]], axis=2) for gi in range(DIL_GROUPS)]
    bufs = [_cache_shift(c, nw) for c, nw in zip(caches[:2], news[:2])]
    bufs.append(_cache_shift_dma(caches[2], news[2]))

    merged = _merge(o_g, lse_g, [1] * DIL_GROUPS, 1, n, yr_in, gates, wts["w_ret_out"], wts["w_dil_out"], n)
    h, u2, ei, wt, counts = _outproj_router(merged, xf, wts["w_o"], wts["norm_ffn"], wts["wr_t"], wts["br_t"],
                                            counts_in, n)
    return h, u2, ei, wt, counts, s_new, bufs


def kernel(x_prompt, x_sample, cache_kv_g0, cache_kv_g1, cache_kv_g2, state_ret, norm_attn, w_in, q_norm,
           k_norm, rel_bias, w_ret_out, w_dil_out, w_o, norm_ffn, w_router_group, b_router_group,
           w_router_expert, b_router_expert, w_gate, w_up, w_down):
    caches = (cache_kv_g0, cache_kv_g1, cache_kv_g2)
    ntok = x_sample.shape[1]
    for c, (w, r) in zip(caches, DIL_PAIRS):
        assert c.shape[1] == w == DIL_BLOCK * r and (r == 1 or ntok <= r) and ntok <= 8
    wr_t, br_t = _router_weights(w_router_group, b_router_group, w_router_expert, b_router_expert)
    wts = dict(norm_attn=norm_attn, w_in=w_in, q_norm=q_norm, k_norm=k_norm, rel_bias=rel_bias,
               w_ret_out=w_ret_out.astype(BF16), w_dil_out=w_dil_out.astype(BF16), w_o=w_o.astype(BF16),
               norm_ffn=norm_ffn, wr_t=wr_t, br_t=br_t)

    zero_counts = jnp.zeros((N_EXPERTS, 1), F32)
    hp, u2p, eip, wtp, counts_p, s_p, bufs_p = _mixers_prompt(x_prompt, wts, zero_counts)
    hs, u2s, eis, wts_s, counts, s_s, bufs_s = _mixers_sample(x_sample, caches, state_ret, wts, counts_p)

    slots, tile_expert, ntiles_used, zero_tiles = _moe_plan(
        counts[:, 0].astype(jnp.int32), [(eip[0:2], eip[2:4]), (eis[0:2], eis[2:4])])
    xs = _dispatch(slots[0], slots[1], zero_tiles, u2p, u2s, 256)
    out_sorted = _gmm(tile_expert, ntiles_used, xs, w_gate, w_up, w_down)
    yp = _combine(slots[0], hp, wtp.T, out_sorted, 256)
    ys = _combine(slots[1], hs, wts_s.T, out_sorted, hs.shape[0])

    return (yp.reshape(x_prompt.shape), ys.reshape(x_sample.shape), bufs_p[0], bufs_p[1], bufs_p[2], s_p,
            bufs_s[0], bufs_s[1], bufs_s[2], s_s)
```

```python
import functools
import math

import jax
import jax.numpy as jnp
from jax import lax
from jax.experimental import pallas as pl
from jax.experimental.pallas import tpu as pltpu

HEAD_DIM = 128
RET_HEADS = 8
RET_WIDTH = RET_HEADS * HEAD_DIM
ROPE_BASE = 10000.0
GN_EPS = 1e-5
DIL_PAIRS = ((128, 1), (512, 4), (2048, 16))
DIL_GROUPS = len(DIL_PAIRS)
DIL_HPG = 4
DIL_HEADS = DIL_HPG * DIL_GROUPS
DIL_WIDTH = DIL_HEADS * HEAD_DIM
DIL_OUT_WIDTH = DIL_HPG * HEAD_DIM
DIL_BLOCK = 128
LSE_LANES = HEAD_DIM // DIL_HPG
ATTN_SCALE = HEAD_DIM ** -0.5
REL_BUCKETS = 32
REL_MAX_DIST = 2048
N_GROUPS = 4
EXPERTS_PER_GROUP = 8
N_EXPERTS = N_GROUPS * EXPERTS_PER_GROUP
TOP_K = 2
NORM_EPS = 1e-6
PAST_LEN = 16384

COL_QR = 0
COL_KR = COL_QR + RET_WIDTH
COL_VR = COL_KR + RET_WIDTH
COL_GR = COL_VR + RET_WIDTH
COL_QA = COL_GR + RET_WIDTH
COL_KA = COL_QA + DIL_WIDTH
COL_VA = COL_KA + DIL_WIDTH
COL_GATES = COL_VA + DIL_WIDTH

MOE_TILE = 256
ROW_TILE = 512
PROJ_CHUNK = 256
ROUTER_TILE = 512
ROUTER_SUB = 256
SAMPLE_SEQS_PER_STEP = 1
MIB = 1 << 20
BF16 = jnp.bfloat16
F32 = jnp.float32


def _params(semantics, vmem_mib):
    return pltpu.CompilerParams(dimension_semantics=semantics, vmem_limit_bytes=vmem_mib * MIB)


def _head_cols(h):
    return slice(h * HEAD_DIM, (h + 1) * HEAD_DIM)


def _rmsnorm_kernel(x_ref, g_ref, o_ref):
    x = x_ref[...]
    y = x * lax.rsqrt(jnp.mean(x * x, axis=-1, keepdims=True) + NORM_EPS)
    o_ref[...] = (y * g_ref[...]).astype(o_ref.dtype)


def _rmsnorm(x, g, tm):
    n, d = x.shape
    return pl.pallas_call(
        _rmsnorm_kernel,
        out_shape=jax.ShapeDtypeStruct((n, d), BF16),
        grid=(n // tm,),
        in_specs=[pl.BlockSpec((tm, d), lambda i: (i, 0)), pl.BlockSpec((1, d), lambda i: (0, 0))],
        out_specs=pl.BlockSpec((tm, d), lambda i: (i, 0)),
        compiler_params=_params(("parallel",), 40),
        name="rmsnorm",
    )(x, g.reshape(1, d))


def _proj_kernel(u_ref, *refs, epilogue, n_w):
    w_refs = refs[:n_w]
    wb_refs = refs[len(refs) - n_w:]
    rest = refs[n_w:len(refs) - n_w]

    @pl.when(pl.program_id(1) == 0)
    def _():
        for w_ref, wb_ref in zip(w_refs, wb_refs):
            wb_ref[...] = w_ref[...].astype(BF16)

    tn = wb_refs[0].shape[1]
    for wi, wb_ref in enumerate(wb_refs):
        for c0 in range(0, tn, PROJ_CHUNK):
            acc = jnp.dot(u_ref[...], wb_ref[:, c0:c0 + PROJ_CHUNK], preferred_element_type=F32)
            epilogue(wi, c0, acc, *rest)


def _proj(u, w_in, col_offsets, ncol_blocks, tn, tm, epilogue, extra, extra_specs, out_shape, out_specs,
          scratch, name, vmem_mib=58):
    n, k = u.shape
    w_specs = [pl.BlockSpec((pl.Element(k), pl.Element(tn)),
                            functools.partial(lambda j, i, o: (0, pl.multiple_of(o + j * tn, HEAD_DIM)), o=o))
               for o in col_offsets]
    return pl.pallas_call(
        functools.partial(_proj_kernel, epilogue=epilogue, n_w=len(col_offsets)),
        out_shape=out_shape,
        grid=(ncol_blocks, n // tm),
        in_specs=[pl.BlockSpec((tm, k), lambda j, i: (i, 0))] + w_specs + list(extra_specs),
        out_specs=out_specs,
        scratch_shapes=list(scratch) + [pltpu.VMEM((k, tn), BF16) for _ in col_offsets],
        compiler_params=_params(("arbitrary", "arbitrary"), vmem_mib),
        name=name,
    )(u, *([w_in] * len(col_offsets)), *extra)


def _chunk_heads(c0, acc):
    return [(slice(c0 + h * HEAD_DIM, c0 + (h + 1) * HEAD_DIM), acc[:, _head_cols(h)])
            for h in range(acc.shape[1] // HEAD_DIM)]


def _epi_rotary(wi, c0, acc, tab_ref, o_ref):
    c = tab_ref[0]
    s = tab_ref[1]
    for cols, xh in _chunk_heads(c0, acc):
        o_ref[:, cols] = (xh * c + pltpu.roll(xh, HEAD_DIM // 2, 1) * s).astype(o_ref.dtype)


def _epi_value_gate(wi, c0, acc, o_ref):
    is_gate = pl.program_id(0) == 1
    o_ref[:, c0:c0 + acc.shape[1]] = jnp.where(is_gate, acc * jax.nn.sigmoid(acc), acc).astype(o_ref.dtype)


def _epi_sigmoid(wi, c0, acc, o_ref):
    o_ref[:, c0:c0 + acc.shape[1]] = jax.nn.sigmoid(acc).astype(o_ref.dtype)


def _head_rms(xh, g):
    return xh * lax.rsqrt(jnp.mean(xh * xh, axis=-1, keepdims=True) + NORM_EPS) * g


def _epi_dilated_prompt(wi, c0, acc, qn_ref, kn_ref, q_ref, k_ref, v_ref, kt_ref, vt_ref, scr, *, r):
    tm = scr.shape[2]
    tail_rows = kt_ref.shape[0]
    gain = (qn_ref[...] * ATTN_SCALE, kn_ref[...], None)[wi]
    out_ref = (q_ref, k_ref, v_ref)[wi]
    tail_ref = (None, kt_ref, vt_ref)[wi]
    for cols, xh in _chunk_heads(c0, acc):
        h = cols.start // HEAD_DIM
        y = xh if gain is None else _head_rms(xh, gain)
        slab = scr.at[wi, h]
        slab[...] = y
        if tail_ref is not None:
            tail_ref[:, cols] = y[tm - tail_rows:tm, :]
        if r == 1:
            out_ref[:, cols] = y.astype(out_ref.dtype)
        else:
            for c in range(r):
                out_ref[c, :, cols] = slab[pl.ds(c, tm // r, stride=r), :].astype(out_ref.dtype)


def _epi_dilated_sample(wi, c0, acc, qn_ref, kn_ref, o_ref):
    j = pl.program_id(0)
    gain = jnp.where(j < DIL_GROUPS, qn_ref[...] * ATTN_SCALE, kn_ref[...])
    for cols, xh in _chunk_heads(c0, acc):
        o_ref[:, cols] = jnp.where(j < 2 * DIL_GROUPS, _head_rms(xh, gain), xh)


def _rotary_tables(pos, rows):
    half = HEAD_DIM // 2
    inv = ROPE_BASE ** (-jnp.arange(half, dtype=F32) / half)
    ang = pos.astype(F32)[:, None] * inv[None, :]
    cos = jnp.cos(ang)
    sin = jnp.sin(ang)
    c = jnp.concatenate([cos, cos], axis=-1)
    s = jnp.concatenate([-sin, sin], axis=-1)
    tab = jnp.stack([jnp.stack([c, s]), jnp.stack([c, s]) * (HEAD_DIM ** -0.5)])
    reps = rows // pos.shape[0]
    return jnp.tile(tab, (1, 1, reps, 1))


def _background_shift_step(step, nsteps, src_ref, new_ref, dst_ref, ring, sems, *, chunks_per_seq, rows):
    shift = new_ref.shape[1]

    def chunk_copy(s, slot, inbound):
        b = s // chunks_per_seq
        r0 = (s % chunks_per_seq) * rows
        if inbound:
            return pltpu.make_async_copy(src_ref.at[b, pl.ds(pl.multiple_of(r0 + shift, 8), rows)],
                                         ring.at[slot], sems.at[0, slot])
        return pltpu.make_async_copy(ring.at[slot], dst_ref.at[b, pl.ds(pl.multiple_of(r0, 8), rows)],
                                     sems.at[1, slot])

    cur = step % 2

    @pl.when(step >= 2)
    def _():
        chunk_copy(step - 2, cur, False).wait()

    chunk_copy(step, cur, True).start()

    @pl.when(step >= 1)
    def _():
        chunk_copy(step - 1, 1 - cur, True).wait()
        chunk_copy(step - 1, 1 - cur, False).start()

    @pl.when(step == nsteps - 1)
    def _():
        tail = pltpu.make_async_copy(new_ref, dst_ref.at[:, pl.ds(dst_ref.shape[1] - shift, shift)], sems.at[2, 0])
        tail.start()
        chunk_copy(step, cur, True).wait()
        chunk_copy(step, cur, False).start()
        chunk_copy(step - 1, 1 - cur, False).wait()
        chunk_copy(step, cur, False).wait()
        tail.wait()


def _proj_gates_shift_kernel(u_ref, w_ref, src_ref, new_ref, o_ref, dst_ref, wb_ref, ring, sems, *,
                             chunks_per_seq, rows):
    nrow_blocks = pl.num_programs(1)
    step = pl.program_id(0) * nrow_blocks + pl.program_id(1)
    _background_shift_step(step, pl.num_programs(0) * nrow_blocks, src_ref, new_ref, dst_ref, ring, sems,
                           chunks_per_seq=chunks_per_seq, rows=rows)

    @pl.when(pl.program_id(1) == 0)
    def _():
        wb_ref[...] = w_ref[...].astype(BF16)

    for c0 in range(0, wb_ref.shape[1], PROJ_CHUNK):
        acc = jnp.dot(u_ref[...], wb_ref[:, c0:c0 + PROJ_CHUNK], preferred_element_type=F32)
        _epi_sigmoid(0, c0, acc, o_ref)


def _proj_gates_shift(u, w_in, tm, cache, new):
    n, k = u.shape
    tn = RET_WIDTH
    ngate = w_in.shape[1] - COL_GATES
    shape = cache.shape
    bd = shape[0]
    cache = cache.reshape(bd, -1, HEAD_DIM)
    new = new.reshape(bd, -1, HEAD_DIM)
    nsteps = (ngate // tn) * (n // tm)
    chunks_per_seq = nsteps // bd
    rows = (cache.shape[1] - new.shape[1]) // chunks_per_seq
    assert chunks_per_seq * bd == nsteps and rows * chunks_per_seq == cache.shape[1] - new.shape[1] and rows % 8 == 0
    any_spec = pl.BlockSpec(memory_space=pl.ANY)
    gates, shifted = pl.pallas_call(
        functools.partial(_proj_gates_shift_kernel, chunks_per_seq=chunks_per_seq, rows=rows),
        out_shape=(jax.ShapeDtypeStruct((n, ngate), BF16), jax.ShapeDtypeStruct(cache.shape, cache.dtype)),
        grid=(ngate // tn, n // tm),
        in_specs=[pl.BlockSpec((tm, k), lambda j, i: (i, 0)),
                  pl.BlockSpec((pl.Element(k), pl.Element(tn)),
                               lambda j, i: (0, pl.multiple_of(COL_GATES + j * tn, HEAD_DIM))),
                  any_spec, pl.BlockSpec(new.shape, lambda j, i: (0, 0, 0))],
        out_specs=(pl.BlockSpec((tm, tn), lambda j, i: (i, j)), any_spec),
        scratch_shapes=[pltpu.VMEM((k, tn), BF16), pltpu.VMEM((2, rows, HEAD_DIM), F32),
                        pltpu.SemaphoreType.DMA((3, 2))],
        compiler_params=_params(("arbitrary", "arbitrary"), 58),
        name="proj_gates_cache_shift",
    )(u, w_in, cache, new)
    return gates, shifted.reshape(shape)


def _proj_retention_and_gates(u, w_in, pos_tab, tm, shift_job=None):
    n = u.shape[0]
    nblk = pos_tab.shape[2] // tm
    tile_out = lambda: pl.BlockSpec((tm, RET_WIDTH), lambda j, i: (i, j))
    qk_r = _proj(u, w_in, [COL_QR], 2, RET_WIDTH, tm, _epi_rotary, [pos_tab],
                 [pl.BlockSpec((None, 2, tm, HEAD_DIM), lambda j, i: (j, 0, i % nblk, 0))],
                 jax.ShapeDtypeStruct((n, 2 * RET_WIDTH), BF16), tile_out(), [], "proj_qk_ret")
    vg_r = _proj(u, w_in, [COL_VR], 2, RET_WIDTH, tm, _epi_value_gate, [], [],
                 jax.ShapeDtypeStruct((n, 2 * RET_WIDTH), BF16), tile_out(), [], "proj_vg_ret")
    ngate = w_in.shape[1] - COL_GATES
    if shift_job is not None:
        gates, shifted = _proj_gates_shift(u, w_in, tm, *shift_job)
        return qk_r, vg_r, gates, shifted
    gates = _proj(u, w_in, [COL_GATES], ngate // RET_WIDTH, RET_WIDTH, tm, _epi_sigmoid, [], [],
                  jax.ShapeDtypeStruct((n, ngate), BF16), tile_out(), [], "proj_gates")
    return qk_r, vg_r, gates


def _proj_dilated_prompt(u, w_in, q_norm, k_norm, gi, b, l, tm):
    w, r = DIL_PAIRS[gi]
    lw = min(w, l)
    bps = l // tm
    tail_rows = min(lw, tm)
    tail_blocks = lw // tail_rows
    cm = (b, r, l // r, DIL_OUT_WIDTH)
    if r == 1:
        cm_shape = jax.ShapeDtypeStruct((b * l, DIL_OUT_WIDTH), BF16)
        cm_spec = lambda: pl.BlockSpec((tm, DIL_OUT_WIDTH), lambda j, i: (i, 0))
    else:
        cm_shape = jax.ShapeDtypeStruct(cm, BF16)
        cm_spec = lambda: pl.BlockSpec((None, r, tm // r, DIL_OUT_WIDTH), lambda j, i: (i // bps, 0, i % bps, 0))
    tail_shape = jax.ShapeDtypeStruct((b, lw, DIL_OUT_WIDTH), F32)
    tail_spec = lambda: pl.BlockSpec(
        (None, tail_rows, DIL_OUT_WIDTH),
        lambda j, i: (i // bps, jnp.maximum(i % bps - (bps - tail_blocks), 0), 0))
    gspec = pl.BlockSpec((1, HEAD_DIM), lambda j, i: (0, 0))
    off = gi * DIL_OUT_WIDTH
    q, k, v, kt, vt = _proj(
        u, w_in, [COL_QA + off, COL_KA + off, COL_VA + off], 1, DIL_OUT_WIDTH, tm,
        functools.partial(_epi_dilated_prompt, r=r),
        [q_norm.reshape(1, HEAD_DIM), k_norm.reshape(1, HEAD_DIM)], [gspec, gspec],
        (cm_shape, cm_shape, cm_shape, tail_shape, tail_shape),
        (cm_spec(), cm_spec(), cm_spec(), tail_spec(), tail_spec()),
        [pltpu.VMEM((3, DIL_HPG, tm, HEAD_DIM), F32)], "proj_dilated_prompt")
    cls = lambda t: t.reshape(b * r, l // r, DIL_OUT_WIDTH)
    return cls(q), cls(k), cls(v), kt, vt


def _proj_dilated_sample(u, w_in, q_norm, k_norm):
    n = u.shape[0]
    gspec = pl.BlockSpec((1, HEAD_DIM), lambda j, i: (0, 0))
    return _proj(u, w_in, [COL_QA], 3 * DIL_GROUPS, DIL_OUT_WIDTH, n, _epi_dilated_sample,
                 [q_norm.reshape(1, HEAD_DIM), k_norm.reshape(1, HEAD_DIM)], [gspec, gspec],
                 jax.ShapeDtypeStruct((n, 3 * DIL_WIDTH), F32),
                 pl.BlockSpec((n, DIL_OUT_WIDTH), lambda j, i: (i, j)), [], "proj_dilated_sample")


def _retention_kernel(q_ref, k_ref, v_ref, g_ref, s0_ref, dm_ref, qd_ref, kd_ref, cd_ref,
                      y_ref, sf_ref, state, *, chunk, nchunks):
    t = pl.program_id(2)

    @pl.when(t == 0)
    def _():
        state[...] = s0_ref[...]

    dmask = dm_ref[...]
    qdec = qd_ref[...]
    kdec = kd_ref[...]
    cdec = cd_ref[...]
    s_prev = state[...]
    for ci in range(nchunks):
        rows = pl.ds(ci * chunk, chunk)
        q = q_ref[rows, :]
        k = k_ref[rows, :]
        v = v_ref[rows, :]
        sc = lax.dot_general(q, k, (((1,), (1,)), ((), ())), preferred_element_type=F32) * dmask
        intra = jnp.dot(sc.astype(BF16), v, preferred_element_type=F32)
        qd = (q.astype(F32) * qdec).astype(BF16)
        cross = jnp.dot(qd, s_prev.astype(BF16), preferred_element_type=F32)
        kd = (k.astype(F32) * kdec).astype(BF16)
        kv = lax.dot_general(kd, v, (((0,), (0,)), ((), ())), preferred_element_type=F32)
        s_prev = s_prev * cdec + kv
        o = intra + cross
        mu = jnp.mean(o, axis=-1, keepdims=True)
        oc = o - mu
        var = jnp.mean(oc * oc, axis=-1, keepdims=True)
        y = g_ref[rows, :].astype(F32) * (oc * lax.rsqrt(var + GN_EPS))
        y_ref[rows, :] = y.astype(y_ref.dtype)
    state[...] = s_prev

    @pl.when(t == pl.num_programs(2) - 1)
    def _():
        sf_ref[...] = state[...]


def _retention_decay(chunk, valid):
    lg = jnp.log1p(-jnp.exp2(-5.0 - jnp.arange(RET_HEADS, dtype=F32)))
    idx = jnp.arange(chunk, dtype=F32)
    rel = idx[:, None] - idx[None, :]
    dmask = jnp.where(rel[None] >= 0, jnp.exp(lg[:, None, None] * jnp.maximum(rel, 0.0)[None]), 0.0)
    qdec = jnp.exp(lg[:, None] * (idx + 1.0)[None, :])[..., None]
    kdec = jnp.exp(lg[:, None] * (valid - 1.0 - idx)[None, :])[..., None]
    kdec = jnp.where((idx < valid)[None, :, None], kdec, 0.0)
    cdec = jnp.exp(lg * valid)[:, None, None]
    return dmask, qdec, kdec, cdec


def _retention_prompt(qk_r, vg_r, s0, chunk, rows_per_step):
    b, l, _ = qk_r.shape
    h = RET_HEADS
    dmask, qdec, kdec, cdec = _retention_decay(chunk, chunk)
    nsteps = l // rows_per_step
    blk = lambda off: pl.BlockSpec((None, rows_per_step, HEAD_DIM), lambda bi, hi, ti: (bi, ti, hi + off))
    per_head = lambda shape: pl.BlockSpec((None,) + shape, lambda bi, hi, ti: (hi,) + (0,) * len(shape))
    state_spec = pl.BlockSpec((None, None, HEAD_DIM, HEAD_DIM), lambda bi, hi, ti: (bi, hi, 0, 0))
    return pl.pallas_call(
        functools.partial(_retention_kernel, chunk=chunk, nchunks=rows_per_step // chunk),
        out_shape=(jax.ShapeDtypeStruct((b, l, RET_WIDTH), BF16),
                   jax.ShapeDtypeStruct((b, h, HEAD_DIM, HEAD_DIM), F32)),
        grid=(b, h, nsteps),
        in_specs=[blk(0), blk(h), blk(0), blk(h), state_spec,
                  per_head((chunk, chunk)), per_head((chunk, 1)), per_head((chunk, 1)), per_head((1, 1))],
        out_specs=(blk(0), state_spec),
        scratch_shapes=[pltpu.VMEM((HEAD_DIM, HEAD_DIM), F32)],
        compiler_params=_params(("parallel", "parallel", "arbitrary"), 32),
        name="retention_prompt",
    )(qk_r, qk_r, vg_r, vg_r, s0, dmask, qdec, kdec, cdec)


def _retention_sample_kernel(q_ref, k_ref, v_ref, g_ref, s0_ref, dm_ref, qd_ref, kd_ref, cd_ref,
                             y_ref, sf_ref, *, ntok):
    for bi in range(q_ref.shape[0]):
        for h in range(RET_HEADS):
            q = q_ref[bi, h]
            k = k_ref[bi, h]
            v = v_ref[bi, h]
            s_prev = s0_ref[bi, h]
            dmask = dm_ref[h]
            o = jnp.dot(q * qd_ref[h], s_prev, preferred_element_type=F32)
            for j in range(ntok):
                sj = jnp.sum(q * k[j:j + 1, :], axis=-1, keepdims=True) * dmask[:, j:j + 1]
                o = o + sj * v[j:j + 1, :]
            kd = k * kd_ref[h]
            kv = lax.dot_general(kd, v, (((0,), (0,)), ((), ())), preferred_element_type=F32)
            sf_ref[bi, h] = s_prev * cd_ref[h] + kv
            mu = jnp.mean(o, axis=-1, keepdims=True)
            oc = o - mu
            var = jnp.mean(oc * oc, axis=-1, keepdims=True)
            y_ref[bi, h] = g_ref[bi, h] * (oc * lax.rsqrt(var + GN_EPS))


def _retention_sample(q, k, v, g, s0, ntok):
    bd, h, tp, _ = q.shape
    dmask, qdec, kdec, cdec = _retention_decay(tp, ntok)
    nb = math.gcd(bd, SAMPLE_SEQS_PER_STEP)
    tok = pl.BlockSpec((nb, h, tp, HEAD_DIM), lambda bi: (bi, 0, 0, 0))
    st = pl.BlockSpec((nb, h, HEAD_DIM, HEAD_DIM), lambda bi: (bi, 0, 0, 0))
    const = lambda a: pl.BlockSpec(a.shape, lambda bi: (0,) * a.ndim)
    return pl.pallas_call(
        functools.partial(_retention_sample_kernel, ntok=ntok),
        out_shape=(jax.ShapeDtypeStruct((bd, h, tp, HEAD_DIM), F32),
                   jax.ShapeDtypeStruct((bd, h, HEAD_DIM, HEAD_DIM), F32)),
        grid=(bd // nb,),
        in_specs=[tok, tok, tok, tok, st, const(dmask), const(qdec), const(kdec), const(cdec)],
        out_specs=(tok, st),
        compiler_params=_params(("parallel",), 32),
        name="retention_sample",
    )(q, k, v, g, s0, dmask, qdec, kdec, cdec)


def _t5_bucket(dist):
    max_exact = REL_BUCKETS // 2
    d = jnp.maximum(dist, 0)
    df = jnp.maximum(d, 1).astype(F32)
    large = max_exact + (jnp.log(df / max_exact) / math.log(REL_MAX_DIST / max_exact)
                         * (REL_BUCKETS - max_exact)).astype(jnp.int32)
    large = jnp.minimum(large, REL_BUCKETS - 1)
    return jnp.where(d < max_exact, d, large)


def _bias_lookup(tab, dist):
    onehot = _t5_bucket(dist)[..., None] == jnp.arange(REL_BUCKETS)
    return jnp.sum(jnp.where(onehot[..., None], tab.astype(F32), 0.0), axis=-2)


def _dilated_kernel(q_ref, kp_ref, kc_ref, vp_ref, vc_ref, bias_ref, o_ref, lse_ref, kfull, vfull, *, nsub):
    i = pl.program_id(1)
    blk = DIL_BLOCK
    kfull[0:blk, :] = kp_ref[...]
    kfull[blk:, :] = kc_ref[...]
    vfull[0:blk, :] = vp_ref[...]
    vfull[blk:, :] = vc_ref[...]
    col = lax.broadcasted_iota(jnp.int32, (blk, 2 * blk), 1)
    lane_head = lax.broadcasted_iota(jnp.int32, (blk, HEAD_DIM), 1) // LSE_LANES
    for s in range(nsub):
        rows = pl.ds(s * blk, blk)
        win = pl.ds(s * blk, 2 * blk)
        lse_tile = jnp.zeros((blk, HEAD_DIM), F32)
        for h in range(DIL_HPG):
            cols = pl.ds(h * HEAD_DIM, HEAD_DIM)
            q = q_ref[rows, cols]
            kw = kfull[win, cols]
            vw = vfull[win, cols]
            sc = lax.dot_general(q, kw, (((1,), (1,)), ((), ())), preferred_element_type=F32) + bias_ref[h]
            if s == 0:
                sc = jnp.where((col >= blk) | (i > 0), sc, -jnp.inf)
            m = jnp.max(sc, axis=-1, keepdims=True)
            p = jnp.exp(sc - m)
            l = jnp.sum(p, axis=-1, keepdims=True)
            o = jnp.dot(p.astype(BF16), vw, preferred_element_type=F32) / l
            o_ref[rows, cols] = o.astype(o_ref.dtype)
            lse_tile = jnp.where(lane_head == h, m + jnp.log(l), lse_tile)
        lse_ref[rows, :] = lse_tile


def _dilated_bias(bias_tab, r):
    blk = DIL_BLOCK
    qi = jnp.arange(blk)[:, None]
    kj = jnp.arange(2 * blk)[None, :]
    dc = blk + qi - kj
    band = (dc >= 0) & (dc <= blk)
    bias = _bias_lookup(bias_tab, dc * r).transpose(2, 0, 1)
    return jnp.where(band[None], bias, -jnp.inf)


def _dilated_prompt(q, k, v, bias, nsub):
    n, lc, w = q.shape
    blk = DIL_BLOCK
    tq = nsub * blk
    cur = pl.BlockSpec((None, tq, w), lambda ni, i: (ni, i, 0))
    prev = pl.BlockSpec((None, blk, w), lambda ni, i: (ni, jnp.maximum(i * nsub - 1, 0), 0))
    return pl.pallas_call(
        functools.partial(_dilated_kernel, nsub=nsub),
        out_shape=(jax.ShapeDtypeStruct((n, lc, w), F32),
                   jax.ShapeDtypeStruct((n, lc, HEAD_DIM), F32)),
        grid=(n, lc // tq),
        in_specs=[cur, prev, cur, prev, cur, pl.BlockSpec(bias.shape, lambda ni, i: (0, 0, 0))],
        out_specs=(cur, pl.BlockSpec((None, tq, HEAD_DIM), lambda ni, i: (ni, i, 0))),
        scratch_shapes=[pltpu.VMEM((tq + blk, w), BF16), pltpu.VMEM((tq + blk, w), BF16)],
        compiler_params=_params(("parallel", "parallel"), 32),
        name="dilated_prompt",
    )(q, k, k, v, v, bias)


def _dilated_sample_kernel(q_ref, kn_ref, vn_ref, c0_ref, c1_ref, c2_ref, bc_ref, bn_ref,
                           o_ref, lse_ref, *, ntok):
    caches = (c0_ref, c1_ref, c2_ref)
    for gi in range(DIL_GROUPS):
        cache = caches[gi]
        nclass = cache.shape[1]
        for t in range(ntok):
            cls = t if nclass > 1 else 0
            qt = q_ref[t, gi]
            kc = cache[:, cls, 0]
            vc = cache[:, cls, 1]
            sc = jnp.sum(kc * qt[None], axis=-1, keepdims=True) + bc_ref[gi, t]
            sn = jnp.sum(kn_ref[:, gi] * qt[None], axis=-1, keepdims=True) + bn_ref[gi, t]
            m = jnp.maximum(jnp.max(sc, axis=0), jnp.max(sn, axis=0))
            pc = jnp.exp(sc - m[None])
            pn = jnp.exp(sn - m[None])
            l = jnp.sum(pc, axis=0) + jnp.sum(pn, axis=0)
            o = jnp.sum(pc * vc, axis=0) + jnp.sum(pn * vn_ref[:, gi], axis=0)
            o_ref[t, gi] = o / l
            lse_ref[t, gi] = m + jnp.log(l)


def _dilated_sample(q, kn, vn, caches, bias_cache, bias_new, ntok):
    bd = q.shape[0]
    tok = pl.BlockSpec((None, ntok, DIL_GROUPS, DIL_HPG, HEAD_DIM), lambda bi: (bi, 0, 0, 0, 0))
    cspecs = []
    for c in caches:
        ncls = min(c.shape[2], ntok)
        cspecs.append(pl.BlockSpec((None, DIL_BLOCK, ncls, 2, DIL_HPG, HEAD_DIM),
                                   lambda bi: (bi, 0, 0, 0, 0, 0)))
    const = lambda a: pl.BlockSpec(a.shape, lambda bi: (0,) * a.ndim)
    return pl.pallas_call(
        functools.partial(_dilated_sample_kernel, ntok=ntok),
        out_shape=(jax.ShapeDtypeStruct(q.shape, F32),
                   jax.ShapeDtypeStruct((bd, ntok, DIL_GROUPS, DIL_HPG, 1), F32)),
        grid=(bd,),
        in_specs=[tok, tok, tok] + cspecs + [const(bias_cache), const(bias_new)],
        out_specs=(tok, pl.BlockSpec((None, ntok, DIL_GROUPS, DIL_HPG, 1), lambda bi: (bi, 0, 0, 0, 0))),
        compiler_params=_params(("parallel",), 32),
        name="dilated_sample",
    )(q, kn, vn, *caches, bias_cache, bias_new)


def _dilated_sample_bias(rel_bias, ntok):
    bc, bn = [], []
    m = jnp.arange(DIL_BLOCK)
    tn = jnp.arange(ntok)
    for gi, (_, r) in enumerate(DIL_PAIRS):
        tab = rel_bias[:, gi * DIL_HPG:(gi + 1) * DIL_HPG]
        rows_c, rows_n = [], []
        for t in range(ntok):
            if r == 1:
                jc = DIL_BLOCK + t - m
                okc = m >= t
                jn = t - tn
                okn = tn <= t
            else:
                jc = DIL_BLOCK - m
                okc = jnp.ones_like(m, bool)
                jn = jnp.zeros_like(tn)
                okn = tn == t
            rows_c.append(jnp.where(okc[:, None], _bias_lookup(tab, jc * r), -jnp.inf))
            rows_n.append(jnp.where(okn[:, None], _bias_lookup(tab, jn * r), -jnp.inf))
        bc.append(jnp.stack(rows_c))
        bn.append(jnp.stack(rows_n))
    return jnp.stack(bc)[..., None], jnp.stack(bn)[..., None]


def _cache_shift_kernel(c_ref, n_ref, o_ref, *, ntok):
    i = pl.program_id(1)
    last = pl.num_programs(1) - 1
    rows = o_ref.shape[0]

    @pl.when(i < last)
    def _():
        o_ref[...] = c_ref[0]

    @pl.when(i == last)
    def _():
        o_ref[0:rows - ntok] = c_ref[0, ntok:rows]
        o_ref[rows - ntok:rows] = n_ref[...]


def _cache_shift(cache, new):
    shape = cache.shape
    bd, lb = shape[:2]
    per_tok = math.prod(shape[2:-1])
    ntok = new.shape[1] * per_tok
    lb = lb * per_tok
    cache = cache.reshape(bd, lb, HEAD_DIM)
    new = new.reshape(bd, ntok, HEAD_DIM)
    rows = min(lb, 4096)
    out = pl.pallas_call(
        functools.partial(_cache_shift_kernel, ntok=ntok),
        out_shape=jax.ShapeDtypeStruct(cache.shape, cache.dtype),
        grid=(bd, lb // rows),
        in_specs=[pl.BlockSpec(tuple(pl.Element(s) for s in (1, rows, HEAD_DIM)),
                               lambda b, i: (b, pl.multiple_of(jnp.minimum(i * rows + ntok, lb - rows), 8), 0)),
                  pl.BlockSpec((None, ntok, HEAD_DIM), lambda b, i: (b, 0, 0))],
        out_specs=pl.BlockSpec((None, rows, HEAD_DIM), lambda b, i: (b, i, 0)),
        compiler_params=_params(("parallel", "arbitrary"), 40),
        name="cache_shift",
    )(cache, new)
    return out.reshape(shape)


def _merge_kernel(o0_ref, o1_ref, o2_ref, l0_ref, l1_ref, l2_ref, yr_ref, gt_ref, wr_ref, wd_ref,
                  out_ref, oa_ref, *nat, dilations):
    tm = out_ref.shape[0]
    o_nat, l_nat = [], []
    for g, (o_ref, l_ref, r) in enumerate(zip((o0_ref, o1_ref, o2_ref), (l0_ref, l1_ref, l2_ref), dilations)):
        if r == 1:
            o_nat.append([o_ref[:, _head_cols(h)] for h in range(DIL_HPG)])
            l_nat.append(l_ref[...])
        else:
            on, ln = nat[2 * g], nat[2 * g + 1]
            for c in range(r):
                ln[pl.ds(c, tm // r, stride=r), :] = l_ref[c]
                for h in range(DIL_HPG):
                    on[h, pl.ds(c, tm // r, stride=r), :] = o_ref[c, :, _head_cols(h)]
            o_nat.append([on[h] for h in range(DIL_HPG)])
            l_nat.append(ln[...])
    m = jnp.maximum(jnp.maximum(l_nat[0], l_nat[1]), l_nat[2])
    es = [jnp.exp(l - m) for l in l_nat]
    den = es[0] + es[1] + es[2]
    for h in range(DIL_HPG):
        lane = h * LSE_LANES
        acc = None
        for g in range(DIL_GROUPS):
            term = (es[g][:, lane:lane + 1] / den[:, lane:lane + 1]) * o_nat[g][h]
            acc = term if acc is None else acc + term
        oa_ref[:, _head_cols(h)] = acc.astype(oa_ref.dtype)
    d = out_ref.shape[1]
    for c0 in range(0, d, PROJ_CHUNK):
        cols = slice(c0, c0 + PROJ_CHUNK)
        yr = jnp.dot(yr_ref[...], wr_ref[:, cols], preferred_element_type=F32)
        ya = jnp.dot(oa_ref[...], wd_ref[:, cols], preferred_element_type=F32)
        g_r = gt_ref[:, cols].astype(F32)
        g_a = gt_ref[:, d + c0:d + c0 + PROJ_CHUNK].astype(F32)
        out_ref[:, cols] = (g_r * yr + g_a * ya).astype(out_ref.dtype)


def _merge(o_g, lse_g, dilations, b, l, yr_in, gates, w_ret_b, w_dil_b, tm):
    n = yr_in.shape[0]
    d = w_ret_b.shape[1]
    bps = l // tm
    row = lambda w: pl.BlockSpec((tm, w), lambda i: (i, 0))
    const = lambda a: pl.BlockSpec(a.shape, lambda i: (0, 0))

    def group_specs(width):
        specs = []
        for r in dilations:
            if r == 1:
                specs.append(row(width))
            else:
                specs.append(pl.BlockSpec((None, r, tm // r, width), lambda i: (i // bps, 0, i % bps, 0)))
        return specs

    view = lambda t, r, width: t.reshape(n, width) if r == 1 else t.reshape(b, r, l // r, width)
    o_in = [view(t, r, DIL_OUT_WIDTH) for t, r in zip(o_g, dilations)]
    l_in = [view(t, r, HEAD_DIM) for t, r in zip(lse_g, dilations)]
    scratch = [pltpu.VMEM((tm, DIL_OUT_WIDTH), BF16)]
    for _ in dilations:
        scratch += [pltpu.VMEM((DIL_HPG, tm, HEAD_DIM), F32), pltpu.VMEM((tm, HEAD_DIM), F32)]
    return pl.pallas_call(
        functools.partial(_merge_kernel, dilations=tuple(dilations)),
        out_shape=jax.ShapeDtypeStruct((n, d), BF16),
        grid=(n // tm,),
        in_specs=group_specs(DIL_OUT_WIDTH) + group_specs(HEAD_DIM) + [row(RET_WIDTH), row(2 * d),
                                                                      const(w_ret_b), const(w_dil_b)],
        out_specs=row(d),
        scratch_shapes=scratch,
        compiler_params=_params(("parallel",), 48),
        name="merge_branches",
    )(*o_in, *l_in, yr_in, gates, w_ret_b, w_dil_b)


ROUTER_ROWS = 8 + N_EXPERTS


def _route_tile(lt, tri, before):
    lg = lt[0:N_GROUPS]
    gmax = jnp.max(lg, axis=0, keepdims=True)
    w_coarse = 1.0 / jnp.sum(jnp.exp(lg - gmax), axis=0, keepdims=True)
    gid = lax.broadcasted_iota(jnp.int32, lg.shape, 0)
    gsel = jnp.min(jnp.where(lg == gmax, gid, N_GROUPS), axis=0, keepdims=True)
    le = jnp.zeros((EXPERTS_PER_GROUP, lt.shape[1]), F32)
    for g in range(N_GROUPS):
        le = jnp.where(gsel == g, lt[8 + g * EXPERTS_PER_GROUP:8 + (g + 1) * EXPERTS_PER_GROUP], le)
    eid = lax.broadcasted_iota(jnp.int32, le.shape, 0)
    v1 = jnp.max(le, axis=0, keepdims=True)
    i1 = jnp.min(jnp.where(le == v1, eid, EXPERTS_PER_GROUP), axis=0, keepdims=True)
    le2 = jnp.where(eid == i1, -jnp.inf, le)
    v2 = jnp.max(le2, axis=0, keepdims=True)
    i2 = jnp.min(jnp.where(le2 == v2, eid, EXPERTS_PER_GROUP), axis=0, keepdims=True)
    e21 = jnp.exp(v2 - v1)
    w1 = w_coarse / (1.0 + e21)
    w2 = w_coarse * e21 / (1.0 + e21)
    e1 = gsel * EXPERTS_PER_GROUP + i1
    e2 = gsel * EXPERTS_PER_GROUP + i2

    xid = lax.broadcasted_iota(jnp.int32, (N_EXPERTS, lt.shape[1]), 0)
    oh1 = jnp.where(xid == e1, 1.0, 0.0)
    oh2 = jnp.where(xid == e2, 1.0, 0.0)
    n1 = jnp.sum(oh1, axis=1, keepdims=True)
    n2 = jnp.sum(oh2, axis=1, keepdims=True)
    p1 = jnp.dot(oh1.astype(BF16), tri, preferred_element_type=F32) + before
    p2 = jnp.dot(oh2.astype(BF16), tri, preferred_element_type=F32) + before + n1
    rank1 = jnp.sum(oh1 * p1, axis=0, keepdims=True)
    rank2 = jnp.sum(oh2 * p2, axis=0, keepdims=True)

    row = lax.broadcasted_iota(jnp.int32, (8, lt.shape[1]), 0)
    ei = jnp.where(row == 0, e1, jnp.where(row == 1, e2, jnp.where(
        row == 2, rank1.astype(jnp.int32), jnp.where(row == 3, rank2.astype(jnp.int32), 0))))
    wt = jnp.where(row == 0, w1, jnp.where(row == 1, w2, 0.0))
    return ei, wt, before + n1 + n2


def _outproj_router_kernel(mg_ref, x_ref, wo_ref, g2_ref, wr_ref, br_ref, tri_ref, cin_ref,
                           h_ref, u2_ref, ei_ref, wt_ref, cnt_ref):
    @pl.when(pl.program_id(0) == 0)
    def _():
        cnt_ref[...] = cin_ref[...]

    sub = tri_ref.shape[0]
    subtiles = [slice(s0, s0 + sub) for s0 in range(0, h_ref.shape[0], sub)]
    for rows in subtiles:
        h_ref[rows, :] = x_ref[rows, :] + jnp.dot(mg_ref[rows, :], wo_ref[...], preferred_element_type=F32)
    counts = cnt_ref[...]
    for rows in subtiles:
        h = h_ref[rows, :]
        u2 = h * lax.rsqrt(jnp.mean(h * h, axis=-1, keepdims=True) + NORM_EPS) * g2_ref[...]
        u2_ref[rows, :] = u2
        lt = lax.dot_general(wr_ref[...], u2.astype(BF16), (((1,), (1,)), ((), ())),
                             preferred_element_type=F32) + br_ref[...]
        ei, wt, counts = _route_tile(lt, tri_ref[...], counts)
        ei_ref[:, rows] = ei
        wt_ref[:, rows] = wt
    cnt_ref[...] = counts


def _outproj_router(merged, x, w_o_b, norm_ffn, wr_t, br_t, counts_in, tm):
    n, d = x.shape
    sub = min(tm, ROUTER_SUB)
    tri = (jnp.arange(sub)[:, None] < jnp.arange(sub)[None, :]).astype(BF16)
    row = lambda: pl.BlockSpec((tm, d), lambda i: (i, 0))
    lane = lambda: pl.BlockSpec((8, tm), lambda i: (0, i))
    const = lambda a: pl.BlockSpec(a.shape, lambda i: (0, 0))
    return pl.pallas_call(
        _outproj_router_kernel,
        out_shape=(jax.ShapeDtypeStruct((n, d), F32), jax.ShapeDtypeStruct((n, d), F32),
                   jax.ShapeDtypeStruct((8, n), jnp.int32), jax.ShapeDtypeStruct((8, n), F32),
                   jax.ShapeDtypeStruct((N_EXPERTS, 1), F32)),
        grid=(n // tm,),
        in_specs=[row(), row(), const(w_o_b), pl.BlockSpec((1, d), lambda i: (0, 0)),
                  const(wr_t), const(br_t), const(tri), const(counts_in)],
        out_specs=(row(), row(), lane(), lane(), const(counts_in)),
        compiler_params=_params(("arbitrary",), 60),
        name="outproj_router",
    )(merged, x, w_o_b, norm_ffn.reshape(1, d), wr_t, br_t, tri, counts_in)


def _router_weights(w_rg, b_rg, w_re, b_re):
    d = w_rg.shape[0]
    wr = jnp.zeros((ROUTER_ROWS, d), F32)
    wr = wr.at[0:N_GROUPS].set(w_rg.T)
    wr = wr.at[8:].set(w_re.transpose(0, 2, 1).reshape(N_EXPERTS, d))
    br = jnp.zeros((ROUTER_ROWS, 1), F32)
    br = br.at[0:N_GROUPS, 0].set(b_rg)
    br = br.at[8:, 0].set(b_re.reshape(N_EXPERTS))
    return wr.astype(BF16), br


DMA_UNROLL = 8


def _dispatch_kernel(pos_p_ref, pos_s_ref, zero_ref, up_ref, us_ref, xs_ref, zbuf, stage, sem, *, tm):
    i = pl.program_id(0)
    last = pl.num_programs(0) - 1
    ntiles = xs_ref.shape[0] // MOE_TILE

    def tile_copy(t):
        return pltpu.make_async_copy(zbuf, xs_ref.at[pl.ds(pl.multiple_of(t * MOE_TILE, MOE_TILE), MOE_TILE)],
                                     sem.at[2])

    @pl.when(i == 0)
    def _():
        zbuf[...] = jnp.zeros_like(zbuf)
        for op in ("start", "wait"):
            def per_tile(t, c, op=op):
                @pl.when(zero_ref[t] != 0)
                def _():
                    getattr(tile_copy(t), op)()
                return c

            lax.fori_loop(0, ntiles, per_tile, 0)

    def scatter_rows(op, src_ref, pos_ref, base, nrows, ntok_total, row_sem):
        def body(r, c):
            for kk in range(TOP_K):
                slot = pos_ref[kk * ntok_total + base + r]
                copy = pltpu.make_async_copy(src_ref.at[pl.ds(r, 1)], xs_ref.at[pl.ds(slot, 1)], row_sem)
                getattr(copy, op)()
            return c

        lax.fori_loop(0, nrows, body, 0, unroll=DMA_UNROLL)

    prompt_rows = last * tm
    cur = i % 2

    @pl.when(i < last)
    def _():
        stage[cur] = up_ref[...]
        scatter_rows("start", stage.at[cur], pos_p_ref, i * tm, tm, prompt_rows, sem.at[cur])

    @pl.when(i > 0)
    def _():
        scatter_rows("wait", stage.at[1 - cur], pos_p_ref, (i - 1) * tm, tm, prompt_rows, sem.at[1 - cur])

    @pl.when(i == last)
    def _():
        for op in ("start", "wait"):
            scatter_rows(op, us_ref, pos_s_ref, 0, us_ref.shape[0], us_ref.shape[0], sem.at[2])


def _dispatch(pos_p, pos_s, zero_tiles, u2p, u2s, tm):
    n, d = u2p.shape
    nsteps = n // tm
    grid_spec = pltpu.PrefetchScalarGridSpec(
        num_scalar_prefetch=3,
        grid=(nsteps + 1,),
        in_specs=[pl.BlockSpec((tm, d), lambda i, *_: (jnp.minimum(i, nsteps - 1), 0)),
                  pl.BlockSpec(u2s.shape, lambda i, *_: (0, 0))],
        out_specs=pl.BlockSpec(memory_space=pl.ANY),
        scratch_shapes=[pltpu.VMEM((MOE_TILE, d), F32), pltpu.VMEM((2, tm, d), F32),
                        pltpu.SemaphoreType.DMA((3,))],
    )
    return pl.pallas_call(
        functools.partial(_dispatch_kernel, tm=tm),
        out_shape=jax.ShapeDtypeStruct((zero_tiles.shape[0] * MOE_TILE, d), F32),
        grid_spec=grid_spec,
        compiler_params=_params(("arbitrary",), 32),
        name="moe_dispatch",
    )(pos_p, pos_s, zero_tiles, u2p, u2s)


def _gmm_kernel(te_ref, nt_ref, x_ref, wg_ref, wu_ref, wd_ref, o_ref, wgb, wub, wdb):
    i = pl.program_id(0)
    fresh = jnp.logical_or(i == 0, te_ref[i] != te_ref[jnp.maximum(i - 1, 0)])

    @pl.when(jnp.logical_and(fresh, i < nt_ref[0]))
    def _():
        wgb[...] = wg_ref[...].astype(BF16)
        wub[...] = wu_ref[...].astype(BF16)
        wdb[...] = wd_ref[...].astype(BF16)

    @pl.when(i < nt_ref[0])
    def _():
        x = x_ref[...].astype(BF16)
        acc = None
        for c0 in range(0, wgb.shape[1], PROJ_CHUNK):
            cols = slice(c0, c0 + PROJ_CHUNK)
            a = jnp.dot(x, wgb[:, cols], preferred_element_type=F32)
            b = jnp.dot(x, wub[:, cols], preferred_element_type=F32)
            hm = (a * jax.nn.sigmoid(a) * b).astype(BF16)
            part = jnp.dot(hm, wdb[cols, :], preferred_element_type=F32)
            acc = part if acc is None else acc + part
        o_ref[...] = acc

    @pl.when(i >= nt_ref[0])
    def _():
        o_ref[...] = jnp.zeros_like(o_ref)


def _gmm(tile_expert, ntiles_used, xs, w_gate, w_up, w_down):
    npad, d = xs.shape
    f = w_gate.shape[-1]
    ntiles = npad // MOE_TILE
    epg = w_gate.shape[1]

    def xmap(i, te, nt):
        return (jnp.minimum(i, nt[0] - 1), 0)

    def wmap(i, te, nt):
        e = te[i]
        return (e // epg, e % epg, 0, 0)

    grid_spec = pltpu.PrefetchScalarGridSpec(
        num_scalar_prefetch=2,
        grid=(ntiles,),
        in_specs=[pl.BlockSpec((MOE_TILE, d), xmap),
                  pl.BlockSpec((None, None, d, f), wmap),
                  pl.BlockSpec((None, None, d, f), wmap),
                  pl.BlockSpec((None, None, f, d), wmap)],
        out_specs=pl.BlockSpec((MOE_TILE, d), lambda i, te, nt: (i, 0)),
        scratch_shapes=[pltpu.VMEM((d, f), BF16), pltpu.VMEM((d, f), BF16), pltpu.VMEM((f, d), BF16)],
    )
    return pl.pallas_call(
        _gmm_kernel,
        out_shape=jax.ShapeDtypeStruct((npad, d), F32),
        grid_spec=grid_spec,
        compiler_params=_params(("arbitrary",), 56),
        name="moe_grouped_matmul",
    )(tile_expert, ntiles_used, xs, w_gate, w_up, w_down)


def _combine_kernel(pos_ref, h_ref, wt_ref, os_ref, y_ref, g0, g1, sem, *, tm, ntok_total):
    i = pl.program_id(0)
    nsteps = pl.num_programs(0)
    cur = i % 2

    def row_copy(step, r, kk, slot):
        src = pos_ref[kk * ntok_total + step * tm + r]
        buf = (g0, g1)[kk]
        return pltpu.make_async_copy(os_ref.at[pl.ds(src, 1)], buf.at[slot, pl.ds(r, 1)], sem.at[slot])

    def issue_step(step, slot):
        def body(r, c):
            row_copy(step, r, 0, slot).start()
            row_copy(step, r, 1, slot).start()
            return c

        lax.fori_loop(0, tm, body, 0, unroll=DMA_UNROLL)

    @pl.when(i == 0)
    def _():
        issue_step(0, 0)

    @pl.when(i + 1 < nsteps)
    def _():
        issue_step(i + 1, 1 - cur)

    def drain(r, c):
        row_copy(i, r, 0, cur).wait()
        row_copy(i, r, 1, cur).wait()
        return c

    lax.fori_loop(0, tm, drain, 0, unroll=DMA_UNROLL)
    y_ref[...] = h_ref[...] + wt_ref[:, 0:1] * g0[cur] + wt_ref[:, 1:2] * g1[cur]


def _combine(pos_flat, h, wt_cols, out_sorted, tm):
    n, d = h.shape
    grid_spec = pltpu.PrefetchScalarGridSpec(
        num_scalar_prefetch=1,
        grid=(n // tm,),
        in_specs=[pl.BlockSpec((tm, d), lambda i, pos: (i, 0)),
                  pl.BlockSpec((tm, 8), lambda i, pos: (i, 0)),
                  pl.BlockSpec(memory_space=pl.ANY)],
        out_specs=pl.BlockSpec((tm, d), lambda i, pos: (i, 0)),
        scratch_shapes=[pltpu.VMEM((2, tm, d), F32), pltpu.VMEM((2, tm, d), F32),
                        pltpu.SemaphoreType.DMA((2,))],
    )
    return pl.pallas_call(
        functools.partial(_combine_kernel, tm=tm, ntok_total=n),
        out_shape=jax.ShapeDtypeStruct((n, d), F32),
        grid_spec=grid_spec,
        compiler_params=_params(("arbitrary",), 32),
        name="moe_combine",
    )(pos_flat, h, wt_cols, out_sorted)


def _moe_plan(counts, route_sets):
    npairs = sum(e.shape[1] for e, _ in route_sets) * TOP_K
    tiles = (counts + MOE_TILE - 1) // MOE_TILE
    tile_end = jnp.cumsum(tiles)
    offs = (tile_end - tiles) * MOE_TILE
    ids = jnp.arange(N_EXPERTS)
    slots = []
    for experts, ranks in route_sets:
        base = jnp.sum(jnp.where(experts[..., None] == ids, offs, 0), axis=-1)
        slots.append((base + ranks).reshape(-1).astype(jnp.int32))
    ntiles = (npairs + N_EXPERTS * (MOE_TILE - 1)) // MOE_TILE
    tile_expert = jnp.sum(tile_end[None, :] <= jnp.arange(ntiles)[:, None], axis=1)
    tile_expert = jnp.minimum(tile_expert, N_EXPERTS - 1).astype(jnp.int32)
    tile_ids = jnp.arange(ntiles)
    is_last = jnp.any((tile_ids[:, None] == tile_end[None, :] - 1) & (tiles[None, :] > 0), axis=1)
    zero_tiles = (is_last | (tile_ids >= tile_end[-1])).astype(jnp.int32)
    return slots, tile_expert, tile_end[-1:].astype(jnp.int32), zero_tiles


def _mixers_prompt(x, wts, counts_in, shift_job):
    b, l, d = x.shape
    n = b * l
    tm = ROW_TILE
    xf = x.reshape(n, d)
    u = _rmsnorm(xf, wts["norm_attn"], tm)
    tab = _rotary_tables(jnp.arange(l), l)
    qk_r, vg_r, gates, shifted = _proj_retention_and_gates(u, wts["w_in"], tab, 2 * tm, shift_job)

    s0 = jnp.zeros((b, RET_HEADS, HEAD_DIM, HEAD_DIM), F32)
    yr_in, s_fin = _retention_prompt(qk_r.reshape(b, l, -1), vg_r.reshape(b, l, -1), s0, 256, min(l, 2048))

    o_g, lse_g, bufs = [], [], []
    for gi, (w, r) in enumerate(DIL_PAIRS):
        qg, kg, vg, kt, vt = _proj_dilated_prompt(u, wts["w_in"], wts["q_norm"], wts["k_norm"], gi, b, l, tm)
        bias = _dilated_bias(wts["rel_bias"][:, gi * DIL_HPG:(gi + 1) * DIL_HPG], r)
        o, lse = _dilated_prompt(qg, kg, vg, bias, min(8, l // r // DIL_BLOCK))
        o_g.append(o)
        lse_g.append(lse)
        bufs.append(jnp.stack([kt, vt], axis=2).reshape(b, kt.shape[1], 2, DIL_HPG, HEAD_DIM))

    merged = _merge(o_g, lse_g, [r for _, r in DIL_PAIRS], b, l, yr_in.reshape(n, RET_WIDTH), gates,
                    wts["w_ret_out"], wts["w_dil_out"], tm)
    h, u2, ei, wt, counts = _outproj_router(merged, xf, wts["w_o"], wts["norm_ffn"], wts["wr_t"], wts["br_t"],
                                            counts_in, ROUTER_TILE)
    return h, u2, ei, wt, counts, s_fin, bufs, shifted


def _mixers_sample(x, caches, state, wts, counts_in):
    bd, t, d = x.shape
    n = bd * t
    xf = x.reshape(n, d)
    u = _rmsnorm(xf, wts["norm_attn"], n)
    tab = _rotary_tables(PAST_LEN + jnp.arange(t), n)
    qk_r, vg_r, gates = _proj_retention_and_gates(u, wts["w_in"], tab, n)
    qkv_a = _proj_dilated_sample(u, wts["w_in"], wts["q_norm"], wts["k_norm"])

    def heads(a):
        a = a.astype(F32).reshape(bd, t, RET_HEADS, HEAD_DIM).transpose(0, 2, 1, 3)
        return jnp.pad(a, ((0, 0), (0, 0), (0, 8 - t), (0, 0)))

    y_r, s_new = _retention_sample(heads(qk_r[:, :RET_WIDTH]), heads(qk_r[:, RET_WIDTH:]),
                                   heads(vg_r[:, :RET_WIDTH]), heads(vg_r[:, RET_WIDTH:]), state, t)
    yr_in = y_r[:, :, :t].transpose(0, 2, 1, 3).reshape(n, RET_WIDTH).astype(BF16)

    grp = lambda a: a.reshape(bd, t, DIL_GROUPS, DIL_HPG, HEAD_DIM)
    qa, ka, va = (grp(qkv_a[:, s * DIL_WIDTH:(s + 1) * DIL_WIDTH]) for s in range(3))
    cviews = [c.reshape(bd, DIL_BLOCK, r, 2, DIL_HPG, HEAD_DIM) for c, (_, r) in zip(caches, DIL_PAIRS)]
    bias_c, bias_n = _dilated_sample_bias(wts["rel_bias"], t)
    o, lse = _dilated_sample(qa, ka, va, cviews, bias_c, bias_n, t)
    o_g = [o[:, :, gi].reshape(n, 1, DIL_OUT_WIDTH) for gi in range(DIL_GROUPS)]
    lse_g = [jnp.repeat(lse[:, :, gi].reshape(n, DIL_HPG), LSE_LANES, axis=1).reshape(n, 1, HEAD_DIM)
             for gi in range(DIL_GROUPS)]

    news = [jnp.stack([ka[:, :, gi], va[:, :, gi]], axis=2) for gi in range(DIL_GROUPS)]
    bufs = [_cache_shift(c, nw) for c, nw in zip(caches[:-1], news[:-1])]
    shift_job = (caches[-1], news[-1])

    merged = _merge(o_g, lse_g, [1] * DIL_GROUPS, 1, n, yr_in, gates, wts["w_ret_out"], wts["w_dil_out"], n)
    h, u2, ei, wt, counts = _outproj_router(merged, xf, wts["w_o"], wts["norm_ffn"], wts["wr_t"], wts["br_t"],
                                            counts_in, n)
    return h, u2, ei, wt, counts, s_new, bufs, shift_job


def kernel(x_prompt, x_sample, cache_kv_g0, cache_kv_g1, cache_kv_g2, state_ret, norm_attn, w_in, q_norm,
           k_norm, rel_bias, w_ret_out, w_dil_out, w_o, norm_ffn, w_router_group, b_router_group,
           w_router_expert, b_router_expert, w_gate, w_up, w_down):
    caches = (cache_kv_g0, cache_kv_g1, cache_kv_g2)
    ntok = x_sample.shape[1]
    for c, (w, r) in zip(caches, DIL_PAIRS):
        assert c.shape[1] == w == DIL_BLOCK * r and (r == 1 or ntok <= r) and ntok <= 8
    wr_t, br_t = _router_weights(w_router_group, b_router_group, w_router_expert, b_router_expert)
    wts = dict(norm_attn=norm_attn, w_in=w_in, q_norm=q_norm, k_norm=k_norm, rel_bias=rel_bias,
               w_ret_out=w_ret_out.astype(BF16), w_dil_out=w_dil_out.astype(BF16), w_o=w_o.astype(BF16),
               norm_ffn=norm_ffn, wr_t=wr_t, br_t=br_t)

    zero_counts = jnp.zeros((N_EXPERTS, 1), F32)
    hs, u2s, eis, wts_s, counts_s, s_s, bufs_s, shift_job = _mixers_sample(x_sample, caches, state_ret, wts,
                                                                           zero_counts)
    hp, u2p, eip, wtp, counts, s_p, bufs_p, shifted = _mixers_prompt(x_prompt, wts, counts_s, shift_job)
    bufs_s.append(shifted)

    slots, tile_expert, ntiles_used, zero_tiles = _moe_plan(
        counts[:, 0].astype(jnp.int32), [(eip[0:2], eip[2:4]), (eis[0:2], eis[2:4])])
    xs = _dispatch(slots[0], slots[1], zero_tiles, u2p, u2s, 256)
    out_sorted = _gmm(tile_expert, ntiles_used, xs, w_gate, w_up, w_down)
    yp = _combine(slots[0], hp, wtp.T, out_sorted, 256)
    ys = _combine(slots[1], hs, wts_s.T, out_sorted, hs.shape[0])

    return (yp.reshape(x_prompt.shape), ys.reshape(x_sample.shape), bufs_p[0], bufs_p[1], bufs_p[2], s_p,
            bufs_s[0], bufs_s[1], bufs_s[2], s_s)
```

```python
import functools
import math

import jax
import jax.numpy as jnp
from jax import lax
from jax.experimental import pallas as pl
from jax.experimental.pallas import tpu as pltpu

HEAD_DIM = 128
RET_HEADS = 8
RET_WIDTH = RET_HEADS * HEAD_DIM
ROPE_BASE = 10000.0
GN_EPS = 1e-5
DIL_PAIRS = ((128, 1), (512, 4), (2048, 16))
DIL_GROUPS = len(DIL_PAIRS)
DIL_HPG = 4
DIL_HEADS = DIL_HPG * DIL_GROUPS
DIL_WIDTH = DIL_HEADS * HEAD_DIM
DIL_OUT_WIDTH = DIL_HPG * HEAD_DIM
DIL_BLOCK = 128
LSE_LANES = HEAD_DIM // DIL_HPG
ATTN_SCALE = HEAD_DIM ** -0.5
REL_BUCKETS = 32
REL_MAX_DIST = 2048
N_GROUPS = 4
EXPERTS_PER_GROUP = 8
N_EXPERTS = N_GROUPS * EXPERTS_PER_GROUP
TOP_K = 2
NORM_EPS = 1e-6
PAST_LEN = 16384

COL_QR = 0
COL_KR = COL_QR + RET_WIDTH
COL_VR = COL_KR + RET_WIDTH
COL_GR = COL_VR + RET_WIDTH
COL_QA = COL_GR + RET_WIDTH
COL_KA = COL_QA + DIL_WIDTH
COL_VA = COL_KA + DIL_WIDTH
COL_GATES = COL_VA + DIL_WIDTH

MOE_TILE = 256
ROW_TILE = 512
PROJ_CHUNK = 256
ROUTER_TILE = 512
ROUTER_SUB = 256
SAMPLE_SEQS_PER_STEP = 1
MIB = 1 << 20
BF16 = jnp.bfloat16
F32 = jnp.float32


def _params(semantics, vmem_mib):
    return pltpu.CompilerParams(dimension_semantics=semantics, vmem_limit_bytes=vmem_mib * MIB)


def _head_cols(h):
    return slice(h * HEAD_DIM, (h + 1) * HEAD_DIM)


def _rmsnorm_kernel(x_ref, g_ref, o_ref):
    x = x_ref[...]
    y = x * lax.rsqrt(jnp.mean(x * x, axis=-1, keepdims=True) + NORM_EPS)
    o_ref[...] = (y * g_ref[...]).astype(o_ref.dtype)


def _rmsnorm(x, g, tm):
    n, d = x.shape
    return pl.pallas_call(
        _rmsnorm_kernel,
        out_shape=jax.ShapeDtypeStruct((n, d), BF16),
        grid=(n // tm,),
        in_specs=[pl.BlockSpec((tm, d), lambda i: (i, 0)), pl.BlockSpec((1, d), lambda i: (0, 0))],
        out_specs=pl.BlockSpec((tm, d), lambda i: (i, 0)),
        compiler_params=_params(("parallel",), 40),
        name="rmsnorm",
    )(x, g.reshape(1, d))


def _background_shift_step(step, nsteps, src_ref, new_ref, dst_ref, ring, sems, *, chunks_per_seq, rows):
    shift = new_ref.shape[1]

    def chunk_copy(s, slot, inbound):
        b = s // chunks_per_seq
        r0 = (s % chunks_per_seq) * rows
        if inbound:
            return pltpu.make_async_copy(src_ref.at[b, pl.ds(pl.multiple_of(r0 + shift, 8), rows)],
                                         ring.at[slot], sems.at[0, slot])
        return pltpu.make_async_copy(ring.at[slot], dst_ref.at[b, pl.ds(pl.multiple_of(r0, 8), rows)],
                                     sems.at[1, slot])

    cur = step % 2

    @pl.when(step >= 2)
    def _():
        chunk_copy(step - 2, cur, False).wait()

    chunk_copy(step, cur, True).start()

    @pl.when(step >= 1)
    def _():
        chunk_copy(step - 1, 1 - cur, True).wait()
        chunk_copy(step - 1, 1 - cur, False).start()

    @pl.when(step == nsteps - 1)
    def _():
        tail = pltpu.make_async_copy(new_ref, dst_ref.at[:, pl.ds(dst_ref.shape[1] - shift, shift)], sems.at[2, 0])
        tail.start()
        chunk_copy(step, cur, True).wait()
        chunk_copy(step, cur, False).start()
        chunk_copy(step - 1, 1 - cur, False).wait()
        chunk_copy(step, cur, False).wait()
        tail.wait()


def _proj_kernel(u_ref, *refs, epilogue, n_w, n_extra, n_out, shift):
    w_refs, refs = refs[:n_w], refs[n_w:]
    extras, refs = refs[:n_extra], refs[n_extra:]
    if shift is not None:
        (src_ref, new_ref), refs = refs[:2], refs[2:]
    outs, refs = refs[:n_out], refs[n_out:]
    if shift is not None:
        dst_ref, (ring, sems), refs = refs[0], refs[-2:], refs[1:-2]
        nrow_blocks = pl.num_programs(1)
        _background_shift_step(pl.program_id(0) * nrow_blocks + pl.program_id(1), pl.num_programs(0) * nrow_blocks,
                               src_ref, new_ref, dst_ref, ring, sems, **shift)
    scratch, wb_refs = refs[:len(refs) - n_w], refs[len(refs) - n_w:]

    @pl.when(pl.program_id(1) == 0)
    def _():
        for w_ref, wb_ref in zip(w_refs, wb_refs):
            wb_ref[...] = w_ref[...].astype(BF16)

    tn = wb_refs[0].shape[1]
    for wi, wb_ref in enumerate(wb_refs):
        for c0 in range(0, tn, PROJ_CHUNK):
            acc = jnp.dot(u_ref[...], wb_ref[:, c0:c0 + PROJ_CHUNK], preferred_element_type=F32)
            epilogue(wi, c0, acc, *extras, *outs, *scratch)


def _proj(u, w_in, col_offsets, ncol_blocks, tn, tm, epilogue, extra, extra_specs, out_shape, out_specs,
          scratch, name, shift_job=None, vmem_mib=58):
    n, k = u.shape
    w_specs = [pl.BlockSpec((pl.Element(k), pl.Element(tn)),
                            functools.partial(lambda j, i, o: (0, pl.multiple_of(o + j * tn, HEAD_DIM)), o=o))
               for o in col_offsets]
    multi = isinstance(out_shape, (tuple, list))
    out_shapes = list(out_shape) if multi else [out_shape]
    out_spec_list = list(out_specs) if multi else [out_specs]
    n_out = len(out_shapes)
    side_in, side_in_specs, side_scratch, shift = [], [], [], None
    if shift_job is not None:
        cache, new = shift_job
        shape = cache.shape
        bd = shape[0]
        cache = cache.reshape(bd, -1, HEAD_DIM)
        new = new.reshape(bd, -1, HEAD_DIM)
        nsteps = ncol_blocks * (n // tm)
        chunks_per_seq = nsteps // bd
        rows = (cache.shape[1] - new.shape[1]) // chunks_per_seq
        assert chunks_per_seq * bd == nsteps and rows % 8 == 0
        assert rows * chunks_per_seq == cache.shape[1] - new.shape[1]
        shift = dict(chunks_per_seq=chunks_per_seq, rows=rows)
        any_spec = pl.BlockSpec(memory_space=pl.ANY)
        side_in, side_in_specs = [cache, new], [any_spec, pl.BlockSpec(new.shape, lambda j, i: (0, 0, 0))]
        out_shapes.append(jax.ShapeDtypeStruct(cache.shape, cache.dtype))
        out_spec_list.append(any_spec)
        side_scratch = [pltpu.VMEM((2, rows, HEAD_DIM), F32), pltpu.SemaphoreType.DMA((3, 2))]
    outs = pl.pallas_call(
        functools.partial(_proj_kernel, epilogue=epilogue, n_w=len(col_offsets), n_extra=len(extra), n_out=n_out,
                          shift=shift),
        out_shape=tuple(out_shapes) if (multi or shift) else out_shapes[0],
        grid=(ncol_blocks, n // tm),
        in_specs=[pl.BlockSpec((tm, k), lambda j, i: (i, 0))] + w_specs + list(extra_specs) + side_in_specs,
        out_specs=tuple(out_spec_list) if (multi or shift) else out_spec_list[0],
        scratch_shapes=list(scratch) + [pltpu.VMEM((k, tn), BF16) for _ in col_offsets] + side_scratch,
        compiler_params=_params(("arbitrary", "arbitrary"), vmem_mib),
        name=name,
    )(u, *([w_in] * len(col_offsets)), *extra, *side_in)
    if shift is None:
        return outs
    shifted = outs[-1].reshape(shape)
    return (tuple(outs[:-1]) if multi else outs[0]), shifted


def _chunk_heads(c0, acc):
    return [(slice(c0 + h * HEAD_DIM, c0 + (h + 1) * HEAD_DIM), acc[:, _head_cols(h)])
            for h in range(acc.shape[1] // HEAD_DIM)]


def _epi_rotary(wi, c0, acc, tab_ref, o_ref):
    c = tab_ref[0]
    s = tab_ref[1]
    for cols, xh in _chunk_heads(c0, acc):
        o_ref[:, cols] = (xh * c + pltpu.roll(xh, HEAD_DIM // 2, 1) * s).astype(o_ref.dtype)


def _epi_value_gate(wi, c0, acc, o_ref):
    is_gate = pl.program_id(0) == 1
    o_ref[:, c0:c0 + acc.shape[1]] = jnp.where(is_gate, acc * jax.nn.sigmoid(acc), acc).astype(o_ref.dtype)


def _epi_sigmoid(wi, c0, acc, o_ref):
    o_ref[:, c0:c0 + acc.shape[1]] = jax.nn.sigmoid(acc).astype(o_ref.dtype)


def _head_rms(xh, g):
    return xh * lax.rsqrt(jnp.mean(xh * xh, axis=-1, keepdims=True) + NORM_EPS) * g


def _epi_dilated_prompt(wi, c0, acc, qn_ref, kn_ref, q_ref, k_ref, v_ref, kt_ref, vt_ref, scr, *, r):
    tm = scr.shape[2]
    tail_rows = kt_ref.shape[0]
    gain = (qn_ref[...] * ATTN_SCALE, kn_ref[...], None)[wi]
    out_ref = (q_ref, k_ref, v_ref)[wi]
    tail_ref = (None, kt_ref, vt_ref)[wi]
    for cols, xh in _chunk_heads(c0, acc):
        h = cols.start // HEAD_DIM
        y = xh if gain is None else _head_rms(xh, gain)
        slab = scr.at[wi, h]
        slab[...] = y
        if tail_ref is not None:
            tail_ref[:, cols] = y[tm - tail_rows:tm, :]
        if r == 1:
            out_ref[:, cols] = y.astype(out_ref.dtype)
        else:
            for c in range(r):
                out_ref[c, :, cols] = slab[pl.ds(c, tm // r, stride=r), :].astype(out_ref.dtype)


def _epi_dilated_sample(wi, c0, acc, qn_ref, kn_ref, o_ref):
    j = pl.program_id(0)
    gain = jnp.where(j < DIL_GROUPS, qn_ref[...] * ATTN_SCALE, kn_ref[...])
    for cols, xh in _chunk_heads(c0, acc):
        o_ref[:, cols] = jnp.where(j < 2 * DIL_GROUPS, _head_rms(xh, gain), xh)


def _rotary_tables(pos, rows):
    half = HEAD_DIM // 2
    inv = ROPE_BASE ** (-jnp.arange(half, dtype=F32) / half)
    ang = pos.astype(F32)[:, None] * inv[None, :]
    cos = jnp.cos(ang)
    sin = jnp.sin(ang)
    c = jnp.concatenate([cos, cos], axis=-1)
    s = jnp.concatenate([-sin, sin], axis=-1)
    tab = jnp.stack([jnp.stack([c, s]), jnp.stack([c, s]) * (HEAD_DIM ** -0.5)])
    reps = rows // pos.shape[0]
    return jnp.tile(tab, (1, 1, reps, 1))


def _proj_retention_and_gates(u, w_in, pos_tab, tm, shift_jobs=(None, None, None)):
    n = u.shape[0]
    nblk = pos_tab.shape[2] // tm
    tile_out = lambda: pl.BlockSpec((tm, RET_WIDTH), lambda j, i: (i, j))
    ngate = w_in.shape[1] - COL_GATES
    qk_r = _proj(u, w_in, [COL_QR], 2, RET_WIDTH, tm, _epi_rotary, [pos_tab],
                 [pl.BlockSpec((None, 2, tm, HEAD_DIM), lambda j, i: (j, 0, i % nblk, 0))],
                 jax.ShapeDtypeStruct((n, 2 * RET_WIDTH), BF16), tile_out(), [], "proj_qk_ret", shift_jobs[0])
    vg_r = _proj(u, w_in, [COL_VR], 2, RET_WIDTH, tm, _epi_value_gate, [], [],
                 jax.ShapeDtypeStruct((n, 2 * RET_WIDTH), BF16), tile_out(), [], "proj_vg_ret", shift_jobs[1])
    gates = _proj(u, w_in, [COL_GATES], ngate // RET_WIDTH, RET_WIDTH, tm, _epi_sigmoid, [], [],
                  jax.ShapeDtypeStruct((n, ngate), BF16), tile_out(), [], "proj_gates", shift_jobs[2])
    results = (qk_r, vg_r, gates)
    if any(job is not None for job in shift_jobs):
        return tuple(r[0] for r in results), [r[1] for r in results]
    return results, []


def _proj_dilated_prompt(u, w_in, q_norm, k_norm, gi, b, l, tm):
    w, r = DIL_PAIRS[gi]
    lw = min(w, l)
    bps = l // tm
    tail_rows = min(lw, tm)
    tail_blocks = lw // tail_rows
    cm = (b, r, l // r, DIL_OUT_WIDTH)
    if r == 1:
        cm_shape = jax.ShapeDtypeStruct((b * l, DIL_OUT_WIDTH), BF16)
        cm_spec = lambda: pl.BlockSpec((tm, DIL_OUT_WIDTH), lambda j, i: (i, 0))
    else:
        cm_shape = jax.ShapeDtypeStruct(cm, BF16)
        cm_spec = lambda: pl.BlockSpec((None, r, tm // r, DIL_OUT_WIDTH), lambda j, i: (i // bps, 0, i % bps, 0))
    tail_shape = jax.ShapeDtypeStruct((b, lw, DIL_OUT_WIDTH), F32)
    tail_spec = lambda: pl.BlockSpec(
        (None, tail_rows, DIL_OUT_WIDTH),
        lambda j, i: (i // bps, jnp.maximum(i % bps - (bps - tail_blocks), 0), 0))
    gspec = pl.BlockSpec((1, HEAD_DIM), lambda j, i: (0, 0))
    off = gi * DIL_OUT_WIDTH
    q, k, v, kt, vt = _proj(
        u, w_in, [COL_QA + off, COL_KA + off, COL_VA + off], 1, DIL_OUT_WIDTH, tm,
        functools.partial(_epi_dilated_prompt, r=r),
        [q_norm.reshape(1, HEAD_DIM), k_norm.reshape(1, HEAD_DIM)], [gspec, gspec],
        (cm_shape, cm_shape, cm_shape, tail_shape, tail_shape),
        (cm_spec(), cm_spec(), cm_spec(), tail_spec(), tail_spec()),
        [pltpu.VMEM((3, DIL_HPG, tm, HEAD_DIM), F32)], "proj_dilated_prompt")
    cls = lambda t: t.reshape(b * r, l // r, DIL_OUT_WIDTH)
    return cls(q), cls(k), cls(v), kt, vt


def _proj_dilated_sample(u, w_in, q_norm, k_norm):
    n = u.shape[0]
    gspec = pl.BlockSpec((1, HEAD_DIM), lambda j, i: (0, 0))
    return _proj(u, w_in, [COL_QA], 3 * DIL_GROUPS, DIL_OUT_WIDTH, n, _epi_dilated_sample,
                 [q_norm.reshape(1, HEAD_DIM), k_norm.reshape(1, HEAD_DIM)], [gspec, gspec],
                 jax.ShapeDtypeStruct((n, 3 * DIL_WIDTH), F32),
                 pl.BlockSpec((n, DIL_OUT_WIDTH), lambda j, i: (i, j)), [], "proj_dilated_sample")


def _retention_kernel(q_ref, k_ref, v_ref, g_ref, s0_ref, dm_ref, qd_ref, kd_ref, cd_ref,
                      y_ref, sf_ref, state, *, chunk, nchunks):
    t = pl.program_id(2)

    @pl.when(t == 0)
    def _():
        state[...] = s0_ref[...]

    dmask = dm_ref[...]
    qdec = qd_ref[...]
    kdec = kd_ref[...]
    cdec = cd_ref[...]
    s_prev = state[...]
    for ci in range(nchunks):
        rows = pl.ds(ci * chunk, chunk)
        q = q_ref[rows, :]
        k = k_ref[rows, :]
        v = v_ref[rows, :]
        sc = lax.dot_general(q, k, (((1,), (1,)), ((), ())), preferred_element_type=F32) * dmask
        intra = jnp.dot(sc.astype(BF16), v, preferred_element_type=F32)
        qd = (q.astype(F32) * qdec).astype(BF16)
        cross = jnp.dot(qd, s_prev.astype(BF16), preferred_element_type=F32)
        kd = (k.astype(F32) * kdec).astype(BF16)
        kv = lax.dot_general(kd, v, (((0,), (0,)), ((), ())), preferred_element_type=F32)
        s_prev = s_prev * cdec + kv
        o = intra + cross
        mu = jnp.mean(o, axis=-1, keepdims=True)
        oc = o - mu
        var = jnp.mean(oc * oc, axis=-1, keepdims=True)
        y = g_ref[rows, :].astype(F32) * (oc * lax.rsqrt(var + GN_EPS))
        y_ref[rows, :] = y.astype(y_ref.dtype)
    state[...] = s_prev

    @pl.when(t == pl.num_programs(2) - 1)
    def _():
        sf_ref[...] = state[...]


def _retention_decay(chunk, valid):
    lg = jnp.log1p(-jnp.exp2(-5.0 - jnp.arange(RET_HEADS, dtype=F32)))
    idx = jnp.arange(chunk, dtype=F32)
    rel = idx[:, None] - idx[None, :]
    dmask = jnp.where(rel[None] >= 0, jnp.exp(lg[:, None, None] * jnp.maximum(rel, 0.0)[None]), 0.0)
    qdec = jnp.exp(lg[:, None] * (idx + 1.0)[None, :])[..., None]
    kdec = jnp.exp(lg[:, None] * (valid - 1.0 - idx)[None, :])[..., None]
    kdec = jnp.where((idx < valid)[None, :, None], kdec, 0.0)
    cdec = jnp.exp(lg * valid)[:, None, None]
    return dmask, qdec, kdec, cdec


def _retention_prompt(qk_r, vg_r, s0, chunk, rows_per_step):
    b, l, _ = qk_r.shape
    h = RET_HEADS
    dmask, qdec, kdec, cdec = _retention_decay(chunk, chunk)
    nsteps = l // rows_per_step
    blk = lambda off: pl.BlockSpec((None, rows_per_step, HEAD_DIM), lambda bi, hi, ti: (bi, ti, hi + off))
    per_head = lambda shape: pl.BlockSpec((None,) + shape, lambda bi, hi, ti: (hi,) + (0,) * len(shape))
    state_spec = pl.BlockSpec((None, None, HEAD_DIM, HEAD_DIM), lambda bi, hi, ti: (bi, hi, 0, 0))
    return pl.pallas_call(
        functools.partial(_retention_kernel, chunk=chunk, nchunks=rows_per_step // chunk),
        out_shape=(jax.ShapeDtypeStruct((b, l, RET_WIDTH), BF16),
                   jax.ShapeDtypeStruct((b, h, HEAD_DIM, HEAD_DIM), F32)),
        grid=(b, h, nsteps),
        in_specs=[blk(0), blk(h), blk(0), blk(h), state_spec,
                  per_head((chunk, chunk)), per_head((chunk, 1)), per_head((chunk, 1)), per_head((1, 1))],
        out_specs=(blk(0), state_spec),
        scratch_shapes=[pltpu.VMEM((HEAD_DIM, HEAD_DIM), F32)],
        compiler_params=_params(("parallel", "parallel", "arbitrary"), 32),
        name="retention_prompt",
    )(qk_r, qk_r, vg_r, vg_r, s0, dmask, qdec, kdec, cdec)


def _retention_sample_kernel(q_ref, k_ref, v_ref, g_ref, s0_ref, dm_ref, qd_ref, kd_ref, cd_ref,
                             y_ref, sf_ref, *, ntok):
    for bi in range(q_ref.shape[0]):
        for h in range(RET_HEADS):
            q = q_ref[bi, h]
            k = k_ref[bi, h]
            v = v_ref[bi, h]
            s_prev = s0_ref[bi, h]
            dmask = dm_ref[h]
            o = jnp.dot(q * qd_ref[h], s_prev, preferred_element_type=F32)
            for j in range(ntok):
                sj = jnp.sum(q * k[j:j + 1, :], axis=-1, keepdims=True) * dmask[:, j:j + 1]
                o = o + sj * v[j:j + 1, :]
            kd = k * kd_ref[h]
            kv = lax.dot_general(kd, v, (((0,), (0,)), ((), ())), preferred_element_type=F32)
            sf_ref[bi, h] = s_prev * cd_ref[h] + kv
            mu = jnp.mean(o, axis=-1, keepdims=True)
            oc = o - mu
            var = jnp.mean(oc * oc, axis=-1, keepdims=True)
            y_ref[bi, h] = g_ref[bi, h] * (oc * lax.rsqrt(var + GN_EPS))


def _retention_sample(q, k, v, g, s0, ntok):
    bd, h, tp, _ = q.shape
    dmask, qdec, kdec, cdec = _retention_decay(tp, ntok)
    nb = math.gcd(bd, SAMPLE_SEQS_PER_STEP)
    tok = pl.BlockSpec((nb, h, tp, HEAD_DIM), lambda bi: (bi, 0, 0, 0))
    st = pl.BlockSpec((nb, h, HEAD_DIM, HEAD_DIM), lambda bi: (bi, 0, 0, 0))
    const = lambda a: pl.BlockSpec(a.shape, lambda bi: (0,) * a.ndim)
    return pl.pallas_call(
        functools.partial(_retention_sample_kernel, ntok=ntok),
        out_shape=(jax.ShapeDtypeStruct((bd, h, tp, HEAD_DIM), F32),
                   jax.ShapeDtypeStruct((bd, h, HEAD_DIM, HEAD_DIM), F32)),
        grid=(bd // nb,),
        in_specs=[tok, tok, tok, tok, st, const(dmask), const(qdec), const(kdec), const(cdec)],
        out_specs=(tok, st),
        compiler_params=_params(("parallel",), 32),
        name="retention_sample",
    )(q, k, v, g, s0, dmask, qdec, kdec, cdec)


def _t5_bucket(dist):
    max_exact = REL_BUCKETS // 2
    d = jnp.maximum(dist, 0)
    df = jnp.maximum(d, 1).astype(F32)
    large = max_exact + (jnp.log(df / max_exact) / math.log(REL_MAX_DIST / max_exact)
                         * (REL_BUCKETS - max_exact)).astype(jnp.int32)
    large = jnp.minimum(large, REL_BUCKETS - 1)
    return jnp.where(d < max_exact, d, large)


def _bias_lookup(tab, dist):
    onehot = _t5_bucket(dist)[..., None] == jnp.arange(REL_BUCKETS)
    return jnp.sum(jnp.where(onehot[..., None], tab.astype(F32), 0.0), axis=-2)


def _dilated_kernel(q_ref, kp_ref, kc_ref, vp_ref, vc_ref, bias_ref, o_ref, lse_ref, kfull, vfull, *, nsub):
    i = pl.program_id(1)
    blk = DIL_BLOCK
    kfull[0:blk, :] = kp_ref[...]
    kfull[blk:, :] = kc_ref[...]
    vfull[0:blk, :] = vp_ref[...]
    vfull[blk:, :] = vc_ref[...]
    col = lax.broadcasted_iota(jnp.int32, (blk, 2 * blk), 1)
    lane_head = lax.broadcasted_iota(jnp.int32, (blk, HEAD_DIM), 1) // LSE_LANES
    for s in range(nsub):
        rows = pl.ds(s * blk, blk)
        win = pl.ds(s * blk, 2 * blk)
        lse_tile = jnp.zeros((blk, HEAD_DIM), F32)
        for h in range(DIL_HPG):
            cols = pl.ds(h * HEAD_DIM, HEAD_DIM)
            q = q_ref[rows, cols]
            kw = kfull[win, cols]
            vw = vfull[win, cols]
            sc = lax.dot_general(q, kw, (((1,), (1,)), ((), ())), preferred_element_type=F32) + bias_ref[h]
            if s == 0:
                sc = jnp.where((col >= blk) | (i > 0), sc, -jnp.inf)
            m = jnp.max(sc, axis=-1, keepdims=True)
            p = jnp.exp(sc - m)
            l = jnp.sum(p, axis=-1, keepdims=True)
            o = jnp.dot(p.astype(BF16), vw, preferred_element_type=F32) / l
            o_ref[rows, cols] = o.astype(o_ref.dtype)
            lse_tile = jnp.where(lane_head == h, m + jnp.log(l), lse_tile)
        lse_ref[rows, :] = lse_tile


def _dilated_bias(bias_tab, r):
    blk = DIL_BLOCK
    qi = jnp.arange(blk)[:, None]
    kj = jnp.arange(2 * blk)[None, :]
    dc = blk + qi - kj
    band = (dc >= 0) & (dc <= blk)
    bias = _bias_lookup(bias_tab, dc * r).transpose(2, 0, 1)
    return jnp.where(band[None], bias, -jnp.inf)


def _dilated_prompt(q, k, v, bias, nsub):
    n, lc, w = q.shape
    blk = DIL_BLOCK
    tq = nsub * blk
    cur = pl.BlockSpec((None, tq, w), lambda ni, i: (ni, i, 0))
    prev = pl.BlockSpec((None, blk, w), lambda ni, i: (ni, jnp.maximum(i * nsub - 1, 0), 0))
    return pl.pallas_call(
        functools.partial(_dilated_kernel, nsub=nsub),
        out_shape=(jax.ShapeDtypeStruct((n, lc, w), F32),
                   jax.ShapeDtypeStruct((n, lc, HEAD_DIM), F32)),
        grid=(n, lc // tq),
        in_specs=[cur, prev, cur, prev, cur, pl.BlockSpec(bias.shape, lambda ni, i: (0, 0, 0))],
        out_specs=(cur, pl.BlockSpec((None, tq, HEAD_DIM), lambda ni, i: (ni, i, 0))),
        scratch_shapes=[pltpu.VMEM((tq + blk, w), BF16), pltpu.VMEM((tq + blk, w), BF16)],
        compiler_params=_params(("parallel", "parallel"), 32),
        name="dilated_prompt",
    )(q, k, k, v, v, bias)


def _dilated_sample_kernel(q_ref, kn_ref, vn_ref, c0_ref, c1_ref, c2_ref, bc_ref, bn_ref,
                           o_ref, lse_ref, *, ntok):
    caches = (c0_ref, c1_ref, c2_ref)
    for gi in range(DIL_GROUPS):
        cache = caches[gi]
        nclass = cache.shape[1]
        for t in range(ntok):
            cls = t if nclass > 1 else 0
            qt = q_ref[t, gi]
            kc = cache[:, cls, 0]
            vc = cache[:, cls, 1]
            sc = jnp.sum(kc * qt[None], axis=-1, keepdims=True) + bc_ref[gi, t]
            sn = jnp.sum(kn_ref[:, gi] * qt[None], axis=-1, keepdims=True) + bn_ref[gi, t]
            m = jnp.maximum(jnp.max(sc, axis=0), jnp.max(sn, axis=0))
            pc = jnp.exp(sc - m[None])
            pn = jnp.exp(sn - m[None])
            l = jnp.sum(pc, axis=0) + jnp.sum(pn, axis=0)
            o = jnp.sum(pc * vc, axis=0) + jnp.sum(pn * vn_ref[:, gi], axis=0)
            o_ref[t, gi] = o / l
            lse_ref[t, gi] = m + jnp.log(l)


def _dilated_sample(q, kn, vn, caches, bias_cache, bias_new, ntok):
    bd = q.shape[0]
    tok = pl.BlockSpec((None, ntok, DIL_GROUPS, DIL_HPG, HEAD_DIM), lambda bi: (bi, 0, 0, 0, 0))
    cspecs = []
    for c in caches:
        ncls = min(c.shape[2], ntok)
        cspecs.append(pl.BlockSpec((None, DIL_BLOCK, ncls, 2, DIL_HPG, HEAD_DIM),
                                   lambda bi: (bi, 0, 0, 0, 0, 0)))
    const = lambda a: pl.BlockSpec(a.shape, lambda bi: (0,) * a.ndim)
    return pl.pallas_call(
        functools.partial(_dilated_sample_kernel, ntok=ntok),
        out_shape=(jax.ShapeDtypeStruct(q.shape, F32),
                   jax.ShapeDtypeStruct((bd, ntok, DIL_GROUPS, DIL_HPG, 1), F32)),
        grid=(bd,),
        in_specs=[tok, tok, tok] + cspecs + [const(bias_cache), const(bias_new)],
        out_specs=(tok, pl.BlockSpec((None, ntok, DIL_GROUPS, DIL_HPG, 1), lambda bi: (bi, 0, 0, 0, 0))),
        compiler_params=_params(("parallel",), 32),
        name="dilated_sample",
    )(q, kn, vn, *caches, bias_cache, bias_new)


def _dilated_sample_bias(rel_bias, ntok):
    bc, bn = [], []
    m = jnp.arange(DIL_BLOCK)
    tn = jnp.arange(ntok)
    for gi, (_, r) in enumerate(DIL_PAIRS):
        tab = rel_bias[:, gi * DIL_HPG:(gi + 1) * DIL_HPG]
        rows_c, rows_n = [], []
        for t in range(ntok):
            if r == 1:
                jc = DIL_BLOCK + t - m
                okc = m >= t
                jn = t - tn
                okn = tn <= t
            else:
                jc = DIL_BLOCK - m
                okc = jnp.ones_like(m, bool)
                jn = jnp.zeros_like(tn)
                okn = tn == t
            rows_c.append(jnp.where(okc[:, None], _bias_lookup(tab, jc * r), -jnp.inf))
            rows_n.append(jnp.where(okn[:, None], _bias_lookup(tab, jn * r), -jnp.inf))
        bc.append(jnp.stack(rows_c))
        bn.append(jnp.stack(rows_n))
    return jnp.stack(bc)[..., None], jnp.stack(bn)[..., None]


def _merge_kernel(o0_ref, o1_ref, o2_ref, l0_ref, l1_ref, l2_ref, yr_ref, gt_ref, wr_ref, wd_ref,
                  out_ref, oa_ref, *nat, dilations):
    tm = out_ref.shape[0]
    o_nat, l_nat = [], []
    for g, (o_ref, l_ref, r) in enumerate(zip((o0_ref, o1_ref, o2_ref), (l0_ref, l1_ref, l2_ref), dilations)):
        if r == 1:
            o_nat.append([o_ref[:, _head_cols(h)] for h in range(DIL_HPG)])
            l_nat.append(l_ref[...])
        else:
            on, ln = nat[2 * g], nat[2 * g + 1]
            for c in range(r):
                ln[pl.ds(c, tm // r, stride=r), :] = l_ref[c]
                for h in range(DIL_HPG):
                    on[h, pl.ds(c, tm // r, stride=r), :] = o_ref[c, :, _head_cols(h)]
            o_nat.append([on[h] for h in range(DIL_HPG)])
            l_nat.append(ln[...])
    m = jnp.maximum(jnp.maximum(l_nat[0], l_nat[1]), l_nat[2])
    es = [jnp.exp(l - m) for l in l_nat]
    den = es[0] + es[1] + es[2]
    for h in range(DIL_HPG):
        lane = h * LSE_LANES
        acc = None
        for g in range(DIL_GROUPS):
            term = (es[g][:, lane:lane + 1] / den[:, lane:lane + 1]) * o_nat[g][h]
            acc = term if acc is None else acc + term
        oa_ref[:, _head_cols(h)] = acc.astype(oa_ref.dtype)
    d = out_ref.shape[1]
    for c0 in range(0, d, PROJ_CHUNK):
        cols = slice(c0, c0 + PROJ_CHUNK)
        yr = jnp.dot(yr_ref[...], wr_ref[:, cols], preferred_element_type=F32)
        ya = jnp.dot(oa_ref[...], wd_ref[:, cols], preferred_element_type=F32)
        g_r = gt_ref[:, cols].astype(F32)
        g_a = gt_ref[:, d + c0:d + c0 + PROJ_CHUNK].astype(F32)
        out_ref[:, cols] = (g_r * yr + g_a * ya).astype(out_ref.dtype)


def _merge(o_g, lse_g, dilations, b, l, yr_in, gates, w_ret_b, w_dil_b, tm):
    n = yr_in.shape[0]
    d = w_ret_b.shape[1]
    bps = l // tm
    row = lambda w: pl.BlockSpec((tm, w), lambda i: (i, 0))
    const = lambda a: pl.BlockSpec(a.shape, lambda i: (0, 0))

    def group_specs(width):
        specs = []
        for r in dilations:
            if r == 1:
                specs.append(row(width))
            else:
                specs.append(pl.BlockSpec((None, r, tm // r, width), lambda i: (i // bps, 0, i % bps, 0)))
        return specs

    view = lambda t, r, width: t.reshape(n, width) if r == 1 else t.reshape(b, r, l // r, width)
    o_in = [view(t, r, DIL_OUT_WIDTH) for t, r in zip(o_g, dilations)]
    l_in = [view(t, r, HEAD_DIM) for t, r in zip(lse_g, dilations)]
    scratch = [pltpu.VMEM((tm, DIL_OUT_WIDTH), BF16)]
    for _ in dilations:
        scratch += [pltpu.VMEM((DIL_HPG, tm, HEAD_DIM), F32), pltpu.VMEM((tm, HEAD_DIM), F32)]
    return pl.pallas_call(
        functools.partial(_merge_kernel, dilations=tuple(dilations)),
        out_shape=jax.ShapeDtypeStruct((n, d), BF16),
        grid=(n // tm,),
        in_specs=group_specs(DIL_OUT_WIDTH) + group_specs(HEAD_DIM) + [row(RET_WIDTH), row(2 * d),
                                                                      const(w_ret_b), const(w_dil_b)],
        out_specs=row(d),
        scratch_shapes=scratch,
        compiler_params=_params(("parallel",), 48),
        name="merge_branches",
    )(*o_in, *l_in, yr_in, gates, w_ret_b, w_dil_b)


ROUTER_ROWS = 8 + N_EXPERTS


def _route_tile(lt, tri, before):
    lg = lt[0:N_GROUPS]
    gmax = jnp.max(lg, axis=0, keepdims=True)
    w_coarse = 1.0 / jnp.sum(jnp.exp(lg - gmax), axis=0, keepdims=True)
    gid = lax.broadcasted_iota(jnp.int32, lg.shape, 0)
    gsel = jnp.min(jnp.where(lg == gmax, gid, N_GROUPS), axis=0, keepdims=True)
    le = jnp.zeros((EXPERTS_PER_GROUP, lt.shape[1]), F32)
    for g in range(N_GROUPS):
        le = jnp.where(gsel == g, lt[8 + g * EXPERTS_PER_GROUP:8 + (g + 1) * EXPERTS_PER_GROUP], le)
    eid = lax.broadcasted_iota(jnp.int32, le.shape, 0)
    v1 = jnp.max(le, axis=0, keepdims=True)
    i1 = jnp.min(jnp.where(le == v1, eid, EXPERTS_PER_GROUP), axis=0, keepdims=True)
    le2 = jnp.where(eid == i1, -jnp.inf, le)
    v2 = jnp.max(le2, axis=0, keepdims=True)
    i2 = jnp.min(jnp.where(le2 == v2, eid, EXPERTS_PER_GROUP), axis=0, keepdims=True)
    e21 = jnp.exp(v2 - v1)
    w1 = w_coarse / (1.0 + e21)
    w2 = w_coarse * e21 / (1.0 + e21)
    e1 = gsel * EXPERTS_PER_GROUP + i1
    e2 = gsel * EXPERTS_PER_GROUP + i2

    xid = lax.broadcasted_iota(jnp.int32, (N_EXPERTS, lt.shape[1]), 0)
    oh1 = jnp.where(xid == e1, 1.0, 0.0)
    oh2 = jnp.where(xid == e2, 1.0, 0.0)
    n1 = jnp.sum(oh1, axis=1, keepdims=True)
    n2 = jnp.sum(oh2, axis=1, keepdims=True)
    p1 = jnp.dot(oh1.astype(BF16), tri, preferred_element_type=F32) + before
    p2 = jnp.dot(oh2.astype(BF16), tri, preferred_element_type=F32) + before + n1
    rank1 = jnp.sum(oh1 * p1, axis=0, keepdims=True)
    rank2 = jnp.sum(oh2 * p2, axis=0, keepdims=True)

    row = lax.broadcasted_iota(jnp.int32, (8, lt.shape[1]), 0)
    ei = jnp.where(row == 0, e1, jnp.where(row == 1, e2, jnp.where(
        row == 2, rank1.astype(jnp.int32), jnp.where(row == 3, rank2.astype(jnp.int32), 0))))
    wt = jnp.where(row == 0, w1, jnp.where(row == 1, w2, 0.0))
    return ei, wt, before + n1 + n2


def _outproj_router_kernel(mg_ref, x_ref, wo_ref, g2_ref, wr_ref, br_ref, tri_ref, cin_ref,
                           h_ref, u2_ref, ei_ref, wt_ref, cnt_ref):
    @pl.when(pl.program_id(0) == 0)
    def _():
        cnt_ref[...] = cin_ref[...]

    sub = tri_ref.shape[0]
    subtiles = [slice(s0, s0 + sub) for s0 in range(0, h_ref.shape[0], sub)]
    for rows in subtiles:
        h_ref[rows, :] = x_ref[rows, :] + jnp.dot(mg_ref[rows, :], wo_ref[...], preferred_element_type=F32)
    counts = cnt_ref[...]
    for rows in subtiles:
        h = h_ref[rows, :]
        u2 = h * lax.rsqrt(jnp.mean(h * h, axis=-1, keepdims=True) + NORM_EPS) * g2_ref[...]
        u2_ref[rows, :] = u2
        lt = lax.dot_general(wr_ref[...], u2.astype(BF16), (((1,), (1,)), ((), ())),
                             preferred_element_type=F32) + br_ref[...]
        ei, wt, counts = _route_tile(lt, tri_ref[...], counts)
        ei_ref[:, rows] = ei
        wt_ref[:, rows] = wt
    cnt_ref[...] = counts


def _outproj_router(merged, x, w_o_b, norm_ffn, wr_t, br_t, counts_in, tm):
    n, d = x.shape
    sub = min(tm, ROUTER_SUB)
    tri = (jnp.arange(sub)[:, None] < jnp.arange(sub)[None, :]).astype(BF16)
    row = lambda: pl.BlockSpec((tm, d), lambda i: (i, 0))
    lane = lambda: pl.BlockSpec((8, tm), lambda i: (0, i))
    const = lambda a: pl.BlockSpec(a.shape, lambda i: (0, 0))
    return pl.pallas_call(
        _outproj_router_kernel,
        out_shape=(jax.ShapeDtypeStruct((n, d), F32), jax.ShapeDtypeStruct((n, d), F32),
                   jax.ShapeDtypeStruct((8, n), jnp.int32), jax.ShapeDtypeStruct((8, n), F32),
                   jax.ShapeDtypeStruct((N_EXPERTS, 1), F32)),
        grid=(n // tm,),
        in_specs=[row(), row(), const(w_o_b), pl.BlockSpec((1, d), lambda i: (0, 0)),
                  const(wr_t), const(br_t), const(tri), const(counts_in)],
        out_specs=(row(), row(), lane(), lane(), const(counts_in)),
        compiler_params=_params(("arbitrary",), 60),
        name="outproj_router",
    )(merged, x, w_o_b, norm_ffn.reshape(1, d), wr_t, br_t, tri, counts_in)


def _router_weights(w_rg, b_rg, w_re, b_re):
    d = w_rg.shape[0]
    wr = jnp.zeros((ROUTER_ROWS, d), F32)
    wr = wr.at[0:N_GROUPS].set(w_rg.T)
    wr = wr.at[8:].set(w_re.transpose(0, 2, 1).reshape(N_EXPERTS, d))
    br = jnp.zeros((ROUTER_ROWS, 1), F32)
    br = br.at[0:N_GROUPS, 0].set(b_rg)
    br = br.at[8:, 0].set(b_re.reshape(N_EXPERTS))
    return wr.astype(BF16), br


DMA_UNROLL = 8


def _dispatch_kernel(pos_p_ref, pos_s_ref, zero_ref, up_ref, us_ref, xs_ref, zbuf, stage, sem, *, tm):
    i = pl.program_id(0)
    last = pl.num_programs(0) - 1
    ntiles = xs_ref.shape[0] // MOE_TILE

    def tile_copy(t):
        return pltpu.make_async_copy(zbuf, xs_ref.at[pl.ds(pl.multiple_of(t * MOE_TILE, MOE_TILE), MOE_TILE)],
                                     sem.at[2])

    @pl.when(i == 0)
    def _():
        zbuf[...] = jnp.zeros_like(zbuf)
        for op in ("start", "wait"):
            def per_tile(t, c, op=op):
                @pl.when(zero_ref[t] != 0)
                def _():
                    getattr(tile_copy(t), op)()
                return c

            lax.fori_loop(0, ntiles, per_tile, 0)

    def scatter_rows(op, src_ref, pos_ref, base, nrows, ntok_total, row_sem):
        if op == "wait":
            for _ in range(TOP_K):
                pltpu.make_async_copy(src_ref, xs_ref.at[pl.ds(0, nrows)], row_sem).wait()
            return

        def body(r, c):
            for kk in range(TOP_K):
                slot = pos_ref[kk * ntok_total + base + r]
                pltpu.make_async_copy(src_ref.at[pl.ds(r, 1)], xs_ref.at[pl.ds(slot, 1)], row_sem).start()
            return c

        lax.fori_loop(0, nrows, body, 0, unroll=DMA_UNROLL)

    prompt_rows = last * tm
    cur = i % 2

    @pl.when(i < last)
    def _():
        stage[cur] = up_ref[...]
        scatter_rows("start", stage.at[cur], pos_p_ref, i * tm, tm, prompt_rows, sem.at[cur])

    @pl.when(i > 0)
    def _():
        scatter_rows("wait", stage.at[1 - cur], pos_p_ref, (i - 1) * tm, tm, prompt_rows, sem.at[1 - cur])

    @pl.when(i == last)
    def _():
        for op in ("start", "wait"):
            scatter_rows(op, us_ref, pos_s_ref, 0, us_ref.shape[0], us_ref.shape[0], sem.at[2])


def _dispatch(pos_p, pos_s, zero_tiles, u2p, u2s, tm):
    n, d = u2p.shape
    nsteps = n // tm
    grid_spec = pltpu.PrefetchScalarGridSpec(
        num_scalar_prefetch=3,
        grid=(nsteps + 1,),
        in_specs=[pl.BlockSpec((tm, d), lambda i, *_: (jnp.minimum(i, nsteps - 1), 0)),
                  pl.BlockSpec(u2s.shape, lambda i, *_: (0, 0))],
        out_specs=pl.BlockSpec(memory_space=pl.ANY),
        scratch_shapes=[pltpu.VMEM((MOE_TILE, d), F32), pltpu.VMEM((2, tm, d), F32),
                        pltpu.SemaphoreType.DMA((3,))],
    )
    return pl.pallas_call(
        functools.partial(_dispatch_kernel, tm=tm),
        out_shape=jax.ShapeDtypeStruct((zero_tiles.shape[0] * MOE_TILE, d), F32),
        grid_spec=grid_spec,
        compiler_params=_params(("arbitrary",), 32),
        name="moe_dispatch",
    )(pos_p, pos_s, zero_tiles, u2p, u2s)


def _gmm_kernel(te_ref, nt_ref, x_ref, wg_ref, wu_ref, wd_ref, o_ref, wgb, wub, wdb):
    i = pl.program_id(0)
    fresh = jnp.logical_or(i == 0, te_ref[i] != te_ref[jnp.maximum(i - 1, 0)])

    @pl.when(jnp.logical_and(fresh, i < nt_ref[0]))
    def _():
        wgb[...] = wg_ref[...].astype(BF16)
        wub[...] = wu_ref[...].astype(BF16)
        wdb[...] = wd_ref[...].astype(BF16)

    @pl.when(i < nt_ref[0])
    def _():
        x = x_ref[...].astype(BF16)
        acc = None
        for c0 in range(0, wgb.shape[1], PROJ_CHUNK):
            cols = slice(c0, c0 + PROJ_CHUNK)
            a = jnp.dot(x, wgb[:, cols], preferred_element_type=F32)
            b = jnp.dot(x, wub[:, cols], preferred_element_type=F32)
            hm = (a * jax.nn.sigmoid(a) * b).astype(BF16)
            part = jnp.dot(hm, wdb[cols, :], preferred_element_type=F32)
            acc = part if acc is None else acc + part
        o_ref[...] = acc

    @pl.when(i >= nt_ref[0])
    def _():
        o_ref[...] = jnp.zeros_like(o_ref)


def _gmm(tile_expert, ntiles_used, xs, w_gate, w_up, w_down):
    npad, d = xs.shape
    f = w_gate.shape[-1]
    ntiles = npad // MOE_TILE
    epg = w_gate.shape[1]

    def xmap(i, te, nt):
        return (jnp.minimum(i, nt[0] - 1), 0)

    def wmap(i, te, nt):
        e = te[i]
        return (e // epg, e % epg, 0, 0)

    grid_spec = pltpu.PrefetchScalarGridSpec(
        num_scalar_prefetch=2,
        grid=(ntiles,),
        in_specs=[pl.BlockSpec((MOE_TILE, d), xmap),
                  pl.BlockSpec((None, None, d, f), wmap),
                  pl.BlockSpec((None, None, d, f), wmap),
                  pl.BlockSpec((None, None, f, d), wmap)],
        out_specs=pl.BlockSpec((MOE_TILE, d), lambda i, te, nt: (i, 0)),
        scratch_shapes=[pltpu.VMEM((d, f), BF16), pltpu.VMEM((d, f), BF16), pltpu.VMEM((f, d), BF16)],
    )
    return pl.pallas_call(
        _gmm_kernel,
        out_shape=jax.ShapeDtypeStruct((npad, d), F32),
        grid_spec=grid_spec,
        compiler_params=_params(("arbitrary",), 56),
        name="moe_grouped_matmul",
    )(tile_expert, ntiles_used, xs, w_gate, w_up, w_down)


def _combine_kernel(pos_ref, h_ref, wt_ref, os_ref, y_ref, g0, g1, sem, *, tm, ntok_total):
    i = pl.program_id(0)
    nsteps = pl.num_programs(0)
    cur = i % 2

    def row_copy(step, r, kk, slot):
        src = pos_ref[kk * ntok_total + step * tm + r]
        buf = (g0, g1)[kk]
        return pltpu.make_async_copy(os_ref.at[pl.ds(src, 1)], buf.at[slot, pl.ds(r, 1)], sem.at[slot])

    def issue_step(step, slot):
        def body(r, c):
            row_copy(step, r, 0, slot).start()
            row_copy(step, r, 1, slot).start()
            return c

        lax.fori_loop(0, tm, body, 0, unroll=DMA_UNROLL)

    @pl.when(i == 0)
    def _():
        issue_step(0, 0)

    @pl.when(i + 1 < nsteps)
    def _():
        issue_step(i + 1, 1 - cur)

    for buf in (g0, g1):
        pltpu.make_async_copy(os_ref.at[pl.ds(0, tm)], buf.at[cur], sem.at[cur]).wait()
    y_ref[...] = h_ref[...] + wt_ref[:, 0:1] * g0[cur] + wt_ref[:, 1:2] * g1[cur]


def _combine(pos_flat, h, wt_cols, out_sorted, tm):
    n, d = h.shape
    grid_spec = pltpu.PrefetchScalarGridSpec(
        num_scalar_prefetch=1,
        grid=(n // tm,),
        in_specs=[pl.BlockSpec((tm, d), lambda i, pos: (i, 0)),
                  pl.BlockSpec((tm, 8), lambda i, pos: (i, 0)),
                  pl.BlockSpec(memory_space=pl.ANY)],
        out_specs=pl.BlockSpec((tm, d), lambda i, pos: (i, 0)),
        scratch_shapes=[pltpu.VMEM((2, tm, d), F32), pltpu.VMEM((2, tm, d), F32),
                        pltpu.SemaphoreType.DMA((2,))],
    )
    return pl.pallas_call(
        functools.partial(_combine_kernel, tm=tm, ntok_total=n),
        out_shape=jax.ShapeDtypeStruct((n, d), F32),
        grid_spec=grid_spec,
        compiler_params=_params(("arbitrary",), 32),
        name="moe_combine",
    )(pos_flat, h, wt_cols, out_sorted)


def _moe_plan(counts, route_sets):
    npairs = sum(e.shape[1] for e, _ in route_sets) * TOP_K
    tiles = (counts + MOE_TILE - 1) // MOE_TILE
    tile_end = jnp.cumsum(tiles)
    offs = (tile_end - tiles) * MOE_TILE
    ids = jnp.arange(N_EXPERTS)
    slots = []
    for experts, ranks in route_sets:
        base = jnp.sum(jnp.where(experts[..., None] == ids, offs, 0), axis=-1)
        slots.append((base + ranks).reshape(-1).astype(jnp.int32))
    ntiles = (npairs + N_EXPERTS * (MOE_TILE - 1)) // MOE_TILE
    tile_expert = jnp.sum(tile_end[None, :] <= jnp.arange(ntiles)[:, None], axis=1)
    tile_expert = jnp.minimum(tile_expert, N_EXPERTS - 1).astype(jnp.int32)
    tile_ids = jnp.arange(ntiles)
    is_last = jnp.any((tile_ids[:, None] == tile_end[None, :] - 1) & (tiles[None, :] > 0), axis=1)
    zero_tiles = (is_last | (tile_ids >= tile_end[-1])).astype(jnp.int32)
    return slots, tile_expert, tile_end[-1:].astype(jnp.int32), zero_tiles


def _mixers_prompt(x, wts, counts_in, shift_jobs):
    b, l, d = x.shape
    n = b * l
    tm = ROW_TILE
    xf = x.reshape(n, d)
    u = _rmsnorm(xf, wts["norm_attn"], tm)
    tab = _rotary_tables(jnp.arange(l), l)
    (qk_r, vg_r, gates), shifted = _proj_retention_and_gates(u, wts["w_in"], tab, 2 * tm, shift_jobs)

    s0 = jnp.zeros((b, RET_HEADS, HEAD_DIM, HEAD_DIM), F32)
    yr_in, s_fin = _retention_prompt(qk_r.reshape(b, l, -1), vg_r.reshape(b, l, -1), s0, 256, min(l, 2048))

    o_g, lse_g, bufs = [], [], []
    for gi, (w, r) in enumerate(DIL_PAIRS):
        qg, kg, vg, kt, vt = _proj_dilated_prompt(u, wts["w_in"], wts["q_norm"], wts["k_norm"], gi, b, l, tm)
        bias = _dilated_bias(wts["rel_bias"][:, gi * DIL_HPG:(gi + 1) * DIL_HPG], r)
        o, lse = _dilated_prompt(qg, kg, vg, bias, min(8, l // r // DIL_BLOCK))
        o_g.append(o)
        lse_g.append(lse)
        bufs.append(jnp.stack([kt, vt], axis=2).reshape(b, kt.shape[1], 2, DIL_HPG, HEAD_DIM))

    merged = _merge(o_g, lse_g, [r for _, r in DIL_PAIRS], b, l, yr_in.reshape(n, RET_WIDTH), gates,
                    wts["w_ret_out"], wts["w_dil_out"], tm)
    h, u2, ei, wt, counts = _outproj_router(merged, xf, wts["w_o"], wts["norm_ffn"], wts["wr_t"], wts["br_t"],
                                            counts_in, ROUTER_TILE)
    return h, u2, ei, wt, counts, s_fin, bufs, shifted


def _mixers_sample(x, caches, state, wts, counts_in):
    bd, t, d = x.shape
    n = bd * t
    xf = x.reshape(n, d)
    u = _rmsnorm(xf, wts["norm_attn"], n)
    tab = _rotary_tables(PAST_LEN + jnp.arange(t), n)
    (qk_r, vg_r, gates), _ = _proj_retention_and_gates(u, wts["w_in"], tab, n)
    qkv_a = _proj_dilated_sample(u, wts["w_in"], wts["q_norm"], wts["k_norm"])

    def heads(a):
        a = a.astype(F32).reshape(bd, t, RET_HEADS, HEAD_DIM).transpose(0, 2, 1, 3)
        return jnp.pad(a, ((0, 0), (0, 0), (0, 8 - t), (0, 0)))

    y_r, s_new = _retention_sample(heads(qk_r[:, :RET_WIDTH]), heads(qk_r[:, RET_WIDTH:]),
                                   heads(vg_r[:, :RET_WIDTH]), heads(vg_r[:, RET_WIDTH:]), state, t)
    yr_in = y_r[:, :, :t].transpose(0, 2, 1, 3).reshape(n, RET_WIDTH).astype(BF16)

    grp = lambda a: a.reshape(bd, t, DIL_GROUPS, DIL_HPG, HEAD_DIM)
    qa, ka, va = (grp(qkv_a[:, s * DIL_WIDTH:(s + 1) * DIL_WIDTH]) for s in range(3))
    cviews = [c.reshape(bd, DIL_BLOCK, r, 2, DIL_HPG, HEAD_DIM) for c, (_, r) in zip(caches, DIL_PAIRS)]
    bias_c, bias_n = _dilated_sample_bias(wts["rel_bias"], t)
    o, lse = _dilated_sample(qa, ka, va, cviews, bias_c, bias_n, t)
    o_g = [o[:, :, gi].reshape(n, 1, DIL_OUT_WIDTH) for gi in range(DIL_GROUPS)]
    lse_g = [jnp.repeat(lse[:, :, gi].reshape(n, DIL_HPG), LSE_LANES, axis=1).reshape(n, 1, HEAD_DIM)
             for gi in range(DIL_GROUPS)]

    news = [jnp.stack([ka[:, :, gi], va[:, :, gi]], axis=2) for gi in range(DIL_GROUPS)]
    shift_jobs = list(zip(caches, news))

    merged = _merge(o_g, lse_g, [1] * DIL_GROUPS, 1, n, yr_in, gates, wts["w_ret_out"], wts["w_dil_out"], n)
    h, u2, ei, wt, counts = _outproj_router(merged, xf, wts["w_o"], wts["norm_ffn"], wts["wr_t"], wts["br_t"],
                                            counts_in, n)
    return h, u2, ei, wt, counts, s_new, shift_jobs


def kernel(x_prompt, x_sample, cache_kv_g0, cache_kv_g1, cache_kv_g2, state_ret, norm_attn, w_in, q_norm,
           k_norm, rel_bias, w_ret_out, w_dil_out, w_o, norm_ffn, w_router_group, b_router_group,
           w_router_expert, b_router_expert, w_gate, w_up, w_down):
    caches = (cache_kv_g0, cache_kv_g1, cache_kv_g2)
    ntok = x_sample.shape[1]
    for c, (w, r) in zip(caches, DIL_PAIRS):
        assert c.shape[1] == w == DIL_BLOCK * r and (r == 1 or ntok <= r) and ntok <= 8
    wr_t, br_t = _router_weights(w_router_group, b_router_group, w_router_expert, b_router_expert)
    wts = dict(norm_attn=norm_attn, w_in=w_in, q_norm=q_norm, k_norm=k_norm, rel_bias=rel_bias,
               w_ret_out=w_ret_out.astype(BF16), w_dil_out=w_dil_out.astype(BF16), w_o=w_o.astype(BF16),
               norm_ffn=norm_ffn, wr_t=wr_t, br_t=br_t)

    zero_counts = jnp.zeros((N_EXPERTS, 1), F32)
    hs, u2s, eis, wts_s, counts_s, s_s, shift_jobs = _mixers_sample(x_sample, caches, state_ret, wts, zero_counts)
    hp, u2p, eip, wtp, counts, s_p, bufs_p, bufs_s = _mixers_prompt(x_prompt, wts, counts_s, shift_jobs)

    slots, tile_expert, ntiles_used, zero_tiles = _moe_plan(
        counts[:, 0].astype(jnp.int32), [(eip[0:2], eip[2:4]), (eis[0:2], eis[2:4])])
    xs = _dispatch(slots[0], slots[1], zero_tiles, u2p, u2s, 256)
    out_sorted = _gmm(tile_expert, ntiles_used, xs, w_gate, w_up, w_down)
    yp = _combine(slots[0], hp, wtp.T, out_sorted, 256)
    ys = _combine(slots[1], hs, wts_s.T, out_sorted, hs.shape[0])

    return (yp.reshape(x_prompt.shape), ys.reshape(x_sample.shape), bufs_p[0], bufs_p[1], bufs_p[2], s_p,
            bufs_s[0], bufs_s[1], bufs_s[2], s_s)
```

```python
import functools
import math

import jax
import jax.numpy as jnp
from jax import lax
from jax.experimental import pallas as pl
from jax.experimental.pallas import tpu as pltpu

HEAD_DIM = 128
RET_HEADS = 8
RET_WIDTH = RET_HEADS * HEAD_DIM
ROPE_BASE = 10000.0
GN_EPS = 1e-5
DIL_PAIRS = ((128, 1), (512, 4), (2048, 16))
DIL_GROUPS = len(DIL_PAIRS)
DIL_HPG = 4
DIL_HEADS = DIL_HPG * DIL_GROUPS
DIL_WIDTH = DIL_HEADS * HEAD_DIM
DIL_OUT_WIDTH = DIL_HPG * HEAD_DIM
DIL_BLOCK = 128
LSE_LANES = HEAD_DIM // DIL_HPG
ATTN_SCALE = HEAD_DIM ** -0.5
REL_BUCKETS = 32
REL_MAX_DIST = 2048
N_GROUPS = 4
EXPERTS_PER_GROUP = 8
N_EXPERTS = N_GROUPS * EXPERTS_PER_GROUP
TOP_K = 2
NORM_EPS = 1e-6
PAST_LEN = 16384

COL_QR = 0
COL_KR = COL_QR + RET_WIDTH
COL_VR = COL_KR + RET_WIDTH
COL_GR = COL_VR + RET_WIDTH
COL_QA = COL_GR + RET_WIDTH
COL_KA = COL_QA + DIL_WIDTH
COL_VA = COL_KA + DIL_WIDTH
COL_GATES = COL_VA + DIL_WIDTH

MOE_TILE = 256
ROW_TILE = 512
PROJ_CHUNK = 256
ROUTER_TILE = 512
ROUTER_SUB = 256
SAMPLE_SEQS_PER_STEP = 1
MIB = 1 << 20
BF16 = jnp.bfloat16
F32 = jnp.float32


def _params(semantics, vmem_mib):
    return pltpu.CompilerParams(dimension_semantics=semantics, vmem_limit_bytes=vmem_mib * MIB)


def _head_cols(h):
    return slice(h * HEAD_DIM, (h + 1) * HEAD_DIM)


def _rmsnorm_kernel(x_ref, g_ref, o_ref):
    x = x_ref[...]
    y = x * lax.rsqrt(jnp.mean(x * x, axis=-1, keepdims=True) + NORM_EPS)
    o_ref[...] = (y * g_ref[...]).astype(o_ref.dtype)


def _rmsnorm(x, g, tm):
    n, d = x.shape
    return pl.pallas_call(
        _rmsnorm_kernel,
        out_shape=jax.ShapeDtypeStruct((n, d), BF16),
        grid=(n // tm,),
        in_specs=[pl.BlockSpec((tm, d), lambda i: (i, 0)), pl.BlockSpec((1, d), lambda i: (0, 0))],
        out_specs=pl.BlockSpec((tm, d), lambda i: (i, 0)),
        compiler_params=_params(("parallel",), 40),
        name="rmsnorm",
    )(x, g.reshape(1, d))


def _background_shift_step(step, nsteps, src_ref, new_ref, dst_ref, ring, sems, *, chunks_per_seq, rows):
    shift = new_ref.shape[1]

    def chunk_copy(s, slot, inbound):
        b = s // chunks_per_seq
        r0 = (s % chunks_per_seq) * rows
        if inbound:
            return pltpu.make_async_copy(src_ref.at[b, pl.ds(pl.multiple_of(r0 + shift, 8), rows)],
                                         ring.at[slot], sems.at[0, slot])
        return pltpu.make_async_copy(ring.at[slot], dst_ref.at[b, pl.ds(pl.multiple_of(r0, 8), rows)],
                                     sems.at[1, slot])

    cur = step % 2

    @pl.when(step >= 2)
    def _():
        chunk_copy(step - 2, cur, False).wait()

    chunk_copy(step, cur, True).start()

    @pl.when(step >= 1)
    def _():
        chunk_copy(step - 1, 1 - cur, True).wait()
        chunk_copy(step - 1, 1 - cur, False).start()

    @pl.when(step == nsteps - 1)
    def _():
        tail = pltpu.make_async_copy(new_ref, dst_ref.at[:, pl.ds(dst_ref.shape[1] - shift, shift)], sems.at[2, 0])
        tail.start()
        chunk_copy(step, cur, True).wait()
        chunk_copy(step, cur, False).start()
        chunk_copy(step - 1, 1 - cur, False).wait()
        chunk_copy(step, cur, False).wait()
        tail.wait()


def _proj_kernel(u_ref, *refs, epilogue, n_w, n_extra, n_out, shift):
    w_refs, refs = refs[:n_w], refs[n_w:]
    extras, refs = refs[:n_extra], refs[n_extra:]
    if shift is not None:
        (src_ref, new_ref), refs = refs[:2], refs[2:]
    outs, refs = refs[:n_out], refs[n_out:]
    if shift is not None:
        dst_ref, (ring, sems), refs = refs[0], refs[-2:], refs[1:-2]
        nrow_blocks = pl.num_programs(1)
        _background_shift_step(pl.program_id(0) * nrow_blocks + pl.program_id(1), pl.num_programs(0) * nrow_blocks,
                               src_ref, new_ref, dst_ref, ring, sems, **shift)
    scratch, wb_refs = refs[:len(refs) - n_w], refs[len(refs) - n_w:]

    @pl.when(pl.program_id(1) == 0)
    def _():
        for w_ref, wb_ref in zip(w_refs, wb_refs):
            wb_ref[...] = w_ref[...].astype(BF16)

    tn = wb_refs[0].shape[1]
    for wi, wb_ref in enumerate(wb_refs):
        for c0 in range(0, tn, PROJ_CHUNK):
            acc = jnp.dot(u_ref[...], wb_ref[:, c0:c0 + PROJ_CHUNK], preferred_element_type=F32)
            epilogue(wi, c0, acc, *extras, *outs, *scratch)


def _proj(u, w_in, col_offsets, ncol_blocks, tn, tm, epilogue, extra, extra_specs, out_shape, out_specs,
          scratch, name, shift_job=None, vmem_mib=58):
    n, k = u.shape
    w_specs = [pl.BlockSpec((pl.Element(k), pl.Element(tn)),
                            functools.partial(lambda j, i, o: (0, pl.multiple_of(o + j * tn, HEAD_DIM)), o=o))
               for o in col_offsets]
    multi = isinstance(out_shape, (tuple, list))
    out_shapes = list(out_shape) if multi else [out_shape]
    out_spec_list = list(out_specs) if multi else [out_specs]
    n_out = len(out_shapes)
    side_in, side_in_specs, side_scratch, shift = [], [], [], None
    if shift_job is not None:
        cache, new = shift_job
        shape = cache.shape
        bd = shape[0]
        cache = cache.reshape(bd, -1, HEAD_DIM)
        new = new.reshape(bd, -1, HEAD_DIM)
        nsteps = ncol_blocks * (n // tm)
        chunks_per_seq = nsteps // bd
        rows = (cache.shape[1] - new.shape[1]) // chunks_per_seq
        assert chunks_per_seq * bd == nsteps and rows % 8 == 0
        assert rows * chunks_per_seq == cache.shape[1] - new.shape[1]
        shift = dict(chunks_per_seq=chunks_per_seq, rows=rows)
        any_spec = pl.BlockSpec(memory_space=pl.ANY)
        side_in, side_in_specs = [cache, new], [any_spec, pl.BlockSpec(new.shape, lambda j, i: (0, 0, 0))]
        out_shapes.append(jax.ShapeDtypeStruct(cache.shape, cache.dtype))
        out_spec_list.append(any_spec)
        side_scratch = [pltpu.VMEM((2, rows, HEAD_DIM), F32), pltpu.SemaphoreType.DMA((3, 2))]
    outs = pl.pallas_call(
        functools.partial(_proj_kernel, epilogue=epilogue, n_w=len(col_offsets), n_extra=len(extra), n_out=n_out,
                          shift=shift),
        out_shape=tuple(out_shapes) if (multi or shift) else out_shapes[0],
        grid=(ncol_blocks, n // tm),
        in_specs=[pl.BlockSpec((tm, k), lambda j, i: (i, 0))] + w_specs + list(extra_specs) + side_in_specs,
        out_specs=tuple(out_spec_list) if (multi or shift) else out_spec_list[0],
        scratch_shapes=list(scratch) + [pltpu.VMEM((k, tn), BF16) for _ in col_offsets] + side_scratch,
        compiler_params=_params(("arbitrary", "arbitrary"), vmem_mib),
        name=name,
    )(u, *([w_in] * len(col_offsets)), *extra, *side_in)
    if shift is None:
        return outs
    shifted = outs[-1].reshape(shape)
    return (tuple(outs[:-1]) if multi else outs[0]), shifted


def _chunk_heads(c0, acc):
    return [(slice(c0 + h * HEAD_DIM, c0 + (h + 1) * HEAD_DIM), acc[:, _head_cols(h)])
            for h in range(acc.shape[1] // HEAD_DIM)]


def _epi_rotary(wi, c0, acc, tab_ref, o_ref):
    c = tab_ref[0]
    s = tab_ref[1]
    for cols, xh in _chunk_heads(c0, acc):
        o_ref[:, cols] = (xh * c + pltpu.roll(xh, HEAD_DIM // 2, 1) * s).astype(o_ref.dtype)


def _epi_value_gate(wi, c0, acc, o_ref):
    is_gate = pl.program_id(0) == 1
    o_ref[:, c0:c0 + acc.shape[1]] = jnp.where(is_gate, acc * jax.nn.sigmoid(acc), acc).astype(o_ref.dtype)


def _epi_sigmoid(wi, c0, acc, o_ref):
    o_ref[:, c0:c0 + acc.shape[1]] = jax.nn.sigmoid(acc).astype(o_ref.dtype)


def _head_rms(xh, g):
    return xh * lax.rsqrt(jnp.mean(xh * xh, axis=-1, keepdims=True) + NORM_EPS) * g


def _epi_dilated_prompt(wi, c0, acc, qn_ref, kn_ref, q_ref, k_ref, v_ref, kt_ref, vt_ref, scr, *, r):
    tm = scr.shape[2]
    tail_rows = kt_ref.shape[0]
    gain = (qn_ref[...] * ATTN_SCALE, kn_ref[...], None)[wi]
    out_ref = (q_ref, k_ref, v_ref)[wi]
    tail_ref = (None, kt_ref, vt_ref)[wi]
    for cols, xh in _chunk_heads(c0, acc):
        h = cols.start // HEAD_DIM
        y = xh if gain is None else _head_rms(xh, gain)
        slab = scr.at[wi, h]
        slab[...] = y
        if tail_ref is not None:
            tail_ref[:, cols] = y[tm - tail_rows:tm, :]
        if r == 1:
            out_ref[:, cols] = y.astype(out_ref.dtype)
        else:
            for c in range(r):
                out_ref[c, :, cols] = slab[pl.ds(c, tm // r, stride=r), :].astype(out_ref.dtype)


def _epi_dilated_sample(wi, c0, acc, qn_ref, kn_ref, o_ref):
    j = pl.program_id(0)
    gain = jnp.where(j < DIL_GROUPS, qn_ref[...] * ATTN_SCALE, kn_ref[...])
    for cols, xh in _chunk_heads(c0, acc):
        o_ref[:, cols] = jnp.where(j < 2 * DIL_GROUPS, _head_rms(xh, gain), xh)


def _rotary_tables(pos, rows):
    half = HEAD_DIM // 2
    inv = ROPE_BASE ** (-jnp.arange(half, dtype=F32) / half)
    ang = pos.astype(F32)[:, None] * inv[None, :]
    cos = jnp.cos(ang)
    sin = jnp.sin(ang)
    c = jnp.concatenate([cos, cos], axis=-1)
    s = jnp.concatenate([-sin, sin], axis=-1)
    tab = jnp.stack([jnp.stack([c, s]), jnp.stack([c, s]) * (HEAD_DIM ** -0.5)])
    reps = rows // pos.shape[0]
    return jnp.tile(tab, (1, 1, reps, 1))


def _proj_retention_and_gates(u, w_in, pos_tab, tm, shift_jobs=(None, None, None)):
    n = u.shape[0]
    nblk = pos_tab.shape[2] // tm
    tile_out = lambda: pl.BlockSpec((tm, RET_WIDTH), lambda j, i: (i, j))
    ngate = w_in.shape[1] - COL_GATES
    qk_r = _proj(u, w_in, [COL_QR], 2, RET_WIDTH, tm, _epi_rotary, [pos_tab],
                 [pl.BlockSpec((None, 2, tm, HEAD_DIM), lambda j, i: (j, 0, i % nblk, 0))],
                 jax.ShapeDtypeStruct((n, 2 * RET_WIDTH), BF16), tile_out(), [], "proj_qk_ret", shift_jobs[0])
    vg_r = _proj(u, w_in, [COL_VR], 2, RET_WIDTH, tm, _epi_value_gate, [], [],
                 jax.ShapeDtypeStruct((n, 2 * RET_WIDTH), BF16), tile_out(), [], "proj_vg_ret", shift_jobs[1])
    gates = _proj(u, w_in, [COL_GATES], ngate // RET_WIDTH, RET_WIDTH, tm, _epi_sigmoid, [], [],
                  jax.ShapeDtypeStruct((n, ngate), BF16), tile_out(), [], "proj_gates", shift_jobs[2])
    results = (qk_r, vg_r, gates)
    if any(job is not None for job in shift_jobs):
        return tuple(r[0] for r in results), [r[1] for r in results]
    return results, []


def _proj_dilated_prompt(u, w_in, q_norm, k_norm, gi, b, l, tm):
    w, r = DIL_PAIRS[gi]
    lw = min(w, l)
    bps = l // tm
    tail_rows = min(lw, tm)
    tail_blocks = lw // tail_rows
    cm = (b, r, l // r, DIL_OUT_WIDTH)
    if r == 1:
        cm_shape = jax.ShapeDtypeStruct((b * l, DIL_OUT_WIDTH), BF16)
        cm_spec = lambda: pl.BlockSpec((tm, DIL_OUT_WIDTH), lambda j, i: (i, 0))
    else:
        cm_shape = jax.ShapeDtypeStruct(cm, BF16)
        cm_spec = lambda: pl.BlockSpec((None, r, tm // r, DIL_OUT_WIDTH), lambda j, i: (i // bps, 0, i % bps, 0))
    tail_shape = jax.ShapeDtypeStruct((b, lw, DIL_OUT_WIDTH), F32)
    tail_spec = lambda: pl.BlockSpec(
        (None, tail_rows, DIL_OUT_WIDTH),
        lambda j, i: (i // bps, jnp.maximum(i % bps - (bps - tail_blocks), 0), 0))
    gspec = pl.BlockSpec((1, HEAD_DIM), lambda j, i: (0, 0))
    off = gi * DIL_OUT_WIDTH
    q, k, v, kt, vt = _proj(
        u, w_in, [COL_QA + off, COL_KA + off, COL_VA + off], 1, DIL_OUT_WIDTH, tm,
        functools.partial(_epi_dilated_prompt, r=r),
        [q_norm.reshape(1, HEAD_DIM), k_norm.reshape(1, HEAD_DIM)], [gspec, gspec],
        (cm_shape, cm_shape, cm_shape, tail_shape, tail_shape),
        (cm_spec(), cm_spec(), cm_spec(), tail_spec(), tail_spec()),
        [pltpu.VMEM((3, DIL_HPG, tm, HEAD_DIM), F32)], "proj_dilated_prompt")
    cls = lambda t: t.reshape(b * r, l // r, DIL_OUT_WIDTH)
    return cls(q), cls(k), cls(v), kt, vt


def _proj_dilated_sample(u, w_in, q_norm, k_norm):
    n = u.shape[0]
    gspec = pl.BlockSpec((1, HEAD_DIM), lambda j, i: (0, 0))
    return _proj(u, w_in, [COL_QA], 3 * DIL_GROUPS, DIL_OUT_WIDTH, n, _epi_dilated_sample,
                 [q_norm.reshape(1, HEAD_DIM), k_norm.reshape(1, HEAD_DIM)], [gspec, gspec],
                 jax.ShapeDtypeStruct((n, 3 * DIL_WIDTH), F32),
                 pl.BlockSpec((n, DIL_OUT_WIDTH), lambda j, i: (i, j)), [], "proj_dilated_sample")


def _retention_kernel(q_ref, k_ref, v_ref, g_ref, s0_ref, dm_ref, qd_ref, kd_ref, cd_ref,
                      y_ref, sf_ref, state, *, chunk, nchunks):
    t = pl.program_id(2)

    @pl.when(t == 0)
    def _():
        state[...] = s0_ref[...]

    dmask = dm_ref[...]
    qdec = qd_ref[...]
    kdec = kd_ref[...]
    cdec = cd_ref[...]
    s_prev = state[...]
    for ci in range(nchunks):
        rows = pl.ds(ci * chunk, chunk)
        q = q_ref[rows, :]
        k = k_ref[rows, :]
        v = v_ref[rows, :]
        sc = lax.dot_general(q, k, (((1,), (1,)), ((), ())), preferred_element_type=F32) * dmask
        intra = jnp.dot(sc.astype(BF16), v, preferred_element_type=F32)
        qd = (q.astype(F32) * qdec).astype(BF16)
        cross = jnp.dot(qd, s_prev.astype(BF16), preferred_element_type=F32)
        kd = (k.astype(F32) * kdec).astype(BF16)
        kv = lax.dot_general(kd, v, (((0,), (0,)), ((), ())), preferred_element_type=F32)
        s_prev = s_prev * cdec + kv
        o = intra + cross
        mu = jnp.mean(o, axis=-1, keepdims=True)
        oc = o - mu
        var = jnp.mean(oc * oc, axis=-1, keepdims=True)
        y = g_ref[rows, :].astype(F32) * (oc * lax.rsqrt(var + GN_EPS))
        y_ref[rows, :] = y.astype(y_ref.dtype)
    state[...] = s_prev

    @pl.when(t == pl.num_programs(2) - 1)
    def _():
        sf_ref[...] = state[...]


def _retention_decay(chunk, valid):
    lg = jnp.log1p(-jnp.exp2(-5.0 - jnp.arange(RET_HEADS, dtype=F32)))
    idx = jnp.arange(chunk, dtype=F32)
    rel = idx[:, None] - idx[None, :]
    dmask = jnp.where(rel[None] >= 0, jnp.exp(lg[:, None, None] * jnp.maximum(rel, 0.0)[None]), 0.0)
    qdec = jnp.exp(lg[:, None] * (idx + 1.0)[None, :])[..., None]
    kdec = jnp.exp(lg[:, None] * (valid - 1.0 - idx)[None, :])[..., None]
    kdec = jnp.where((idx < valid)[None, :, None], kdec, 0.0)
    cdec = jnp.exp(lg * valid)[:, None, None]
    return dmask, qdec, kdec, cdec


def _retention_prompt(qk_r, vg_r, s0, chunk, rows_per_step):
    b, l, _ = qk_r.shape
    h = RET_HEADS
    dmask, qdec, kdec, cdec = _retention_decay(chunk, chunk)
    nsteps = l // rows_per_step
    blk = lambda off: pl.BlockSpec((None, rows_per_step, HEAD_DIM), lambda bi, hi, ti: (bi, ti, hi + off))
    per_head = lambda shape: pl.BlockSpec((None,) + shape, lambda bi, hi, ti: (hi,) + (0,) * len(shape))
    state_spec = pl.BlockSpec((None, None, HEAD_DIM, HEAD_DIM), lambda bi, hi, ti: (bi, hi, 0, 0))
    return pl.pallas_call(
        functools.partial(_retention_kernel, chunk=chunk, nchunks=rows_per_step // chunk),
        out_shape=(jax.ShapeDtypeStruct((b, l, RET_WIDTH), BF16),
                   jax.ShapeDtypeStruct((b, h, HEAD_DIM, HEAD_DIM), F32)),
        grid=(b, h, nsteps),
        in_specs=[blk(0), blk(h), blk(0), blk(h), state_spec,
                  per_head((chunk, chunk)), per_head((chunk, 1)), per_head((chunk, 1)), per_head((1, 1))],
        out_specs=(blk(0), state_spec),
        scratch_shapes=[pltpu.VMEM((HEAD_DIM, HEAD_DIM), F32)],
        compiler_params=_params(("parallel", "parallel", "arbitrary"), 32),
        name="retention_prompt",
    )(qk_r, qk_r, vg_r, vg_r, s0, dmask, qdec, kdec, cdec)


def _retention_sample_kernel(q_ref, k_ref, v_ref, g_ref, s0_ref, dm_ref, qd_ref, kd_ref, cd_ref,
                             y_ref, sf_ref, *, ntok):
    for bi in range(q_ref.shape[0]):
        for h in range(RET_HEADS):
            q = q_ref[bi, h]
            k = k_ref[bi, h]
            v = v_ref[bi, h]
            s_prev = s0_ref[bi, h]
            dmask = dm_ref[h]
            o = jnp.dot(q * qd_ref[h], s_prev, preferred_element_type=F32)
            for j in range(ntok):
                sj = jnp.sum(q * k[j:j + 1, :], axis=-1, keepdims=True) * dmask[:, j:j + 1]
                o = o + sj * v[j:j + 1, :]
            kd = k * kd_ref[h]
            kv = lax.dot_general(kd, v, (((0,), (0,)), ((), ())), preferred_element_type=F32)
            sf_ref[bi, h] = s_prev * cd_ref[h] + kv
            mu = jnp.mean(o, axis=-1, keepdims=True)
            oc = o - mu
            var = jnp.mean(oc * oc, axis=-1, keepdims=True)
            y_ref[bi, h] = g_ref[bi, h] * (oc * lax.rsqrt(var + GN_EPS))


def _retention_sample(q, k, v, g, s0, ntok):
    bd, h, tp, _ = q.shape
    dmask, qdec, kdec, cdec = _retention_decay(tp, ntok)
    nb = math.gcd(bd, SAMPLE_SEQS_PER_STEP)
    tok = pl.BlockSpec((nb, h, tp, HEAD_DIM), lambda bi: (bi, 0, 0, 0))
    st = pl.BlockSpec((nb, h, HEAD_DIM, HEAD_DIM), lambda bi: (bi, 0, 0, 0))
    const = lambda a: pl.BlockSpec(a.shape, lambda bi: (0,) * a.ndim)
    return pl.pallas_call(
        functools.partial(_retention_sample_kernel, ntok=ntok),
        out_shape=(jax.ShapeDtypeStruct((bd, h, tp, HEAD_DIM), F32),
                   jax.ShapeDtypeStruct((bd, h, HEAD_DIM, HEAD_DIM), F32)),
        grid=(bd // nb,),
        in_specs=[tok, tok, tok, tok, st, const(dmask), const(qdec), const(kdec), const(cdec)],
        out_specs=(tok, st),
        compiler_params=_params(("parallel",), 32),
        name="retention_sample",
    )(q, k, v, g, s0, dmask, qdec, kdec, cdec)


def _t5_bucket(dist):
    max_exact = REL_BUCKETS // 2
    d = jnp.maximum(dist, 0)
    df = jnp.maximum(d, 1).astype(F32)
    large = max_exact + (jnp.log(df / max_exact) / math.log(REL_MAX_DIST / max_exact)
                         * (REL_BUCKETS - max_exact)).astype(jnp.int32)
    large = jnp.minimum(large, REL_BUCKETS - 1)
    return jnp.where(d < max_exact, d, large)


def _bias_lookup(tab, dist):
    onehot = _t5_bucket(dist)[..., None] == jnp.arange(REL_BUCKETS)
    return jnp.sum(jnp.where(onehot[..., None], tab.astype(F32), 0.0), axis=-2)


def _dilated_kernel(q_ref, kp_ref, kc_ref, vp_ref, vc_ref, bias_ref, o_ref, lse_ref, kfull, vfull, *, nsub):
    i = pl.program_id(1)
    blk = DIL_BLOCK
    kfull[0:blk, :] = kp_ref[...]
    kfull[blk:, :] = kc_ref[...]
    vfull[0:blk, :] = vp_ref[...]
    vfull[blk:, :] = vc_ref[...]
    col = lax.broadcasted_iota(jnp.int32, (blk, 2 * blk), 1)
    lane_head = lax.broadcasted_iota(jnp.int32, (blk, HEAD_DIM), 1) // LSE_LANES
    for s in range(nsub):
        rows = pl.ds(s * blk, blk)
        win = pl.ds(s * blk, 2 * blk)
        lse_tile = jnp.zeros((blk, HEAD_DIM), F32)
        for h in range(DIL_HPG):
            cols = pl.ds(h * HEAD_DIM, HEAD_DIM)
            q = q_ref[rows, cols]
            kw = kfull[win, cols]
            vw = vfull[win, cols]
            sc = lax.dot_general(q, kw, (((1,), (1,)), ((), ())), preferred_element_type=F32) + bias_ref[h]
            if s == 0:
                sc = jnp.where((col >= blk) | (i > 0), sc, -jnp.inf)
            m = jnp.max(sc, axis=-1, keepdims=True)
            p = jnp.exp(sc - m)
            l = jnp.sum(p, axis=-1, keepdims=True)
            o = jnp.dot(p.astype(BF16), vw, preferred_element_type=F32) / l
            o_ref[rows, cols] = o.astype(o_ref.dtype)
            lse_tile = jnp.where(lane_head == h, m + jnp.log(l), lse_tile)
        lse_ref[rows, :] = lse_tile


def _dilated_bias(bias_tab, r):
    blk = DIL_BLOCK
    qi = jnp.arange(blk)[:, None]
    kj = jnp.arange(2 * blk)[None, :]
    dc = blk + qi - kj
    band = (dc >= 0) & (dc <= blk)
    bias = _bias_lookup(bias_tab, dc * r).transpose(2, 0, 1)
    return jnp.where(band[None], bias, -jnp.inf)


def _dilated_prompt(q, k, v, bias, nsub):
    n, lc, w = q.shape
    blk = DIL_BLOCK
    tq = nsub * blk
    cur = pl.BlockSpec((None, tq, w), lambda ni, i: (ni, i, 0))
    prev = pl.BlockSpec((None, blk, w), lambda ni, i: (ni, jnp.maximum(i * nsub - 1, 0), 0))
    return pl.pallas_call(
        functools.partial(_dilated_kernel, nsub=nsub),
        out_shape=(jax.ShapeDtypeStruct((n, lc, w), F32),
                   jax.ShapeDtypeStruct((n, lc, HEAD_DIM), F32)),
        grid=(n, lc // tq),
        in_specs=[cur, prev, cur, prev, cur, pl.BlockSpec(bias.shape, lambda ni, i: (0, 0, 0))],
        out_specs=(cur, pl.BlockSpec((None, tq, HEAD_DIM), lambda ni, i: (ni, i, 0))),
        scratch_shapes=[pltpu.VMEM((tq + blk, w), BF16), pltpu.VMEM((tq + blk, w), BF16)],
        compiler_params=_params(("parallel", "parallel"), 32),
        name="dilated_prompt",
    )(q, k, k, v, v, bias)


def _dilated_sample_kernel(q_ref, kn_ref, vn_ref, c0_ref, c1_ref, c2_ref, bc_ref, bn_ref,
                           o_ref, lse_ref, *, ntok):
    caches = (c0_ref, c1_ref, c2_ref)
    for gi in range(DIL_GROUPS):
        cache = caches[gi]
        nclass = cache.shape[1]
        for t in range(ntok):
            cls = t if nclass > 1 else 0
            qt = q_ref[t, gi]
            kc = cache[:, cls, 0]
            vc = cache[:, cls, 1]
            sc = jnp.sum(kc * qt[None], axis=-1, keepdims=True) + bc_ref[gi, t]
            sn = jnp.sum(kn_ref[:, gi] * qt[None], axis=-1, keepdims=True) + bn_ref[gi, t]
            m = jnp.maximum(jnp.max(sc, axis=0), jnp.max(sn, axis=0))
            pc = jnp.exp(sc - m[None])
            pn = jnp.exp(sn - m[None])
            l = jnp.sum(pc, axis=0) + jnp.sum(pn, axis=0)
            o = jnp.sum(pc * vc, axis=0) + jnp.sum(pn * vn_ref[:, gi], axis=0)
            o_ref[t, gi] = o / l
            lse_ref[t, gi] = m + jnp.log(l)


def _dilated_sample(q, kn, vn, caches, bias_cache, bias_new, ntok):
    bd = q.shape[0]
    tok = pl.BlockSpec((None, ntok, DIL_GROUPS, DIL_HPG, HEAD_DIM), lambda bi: (bi, 0, 0, 0, 0))
    cspecs = []
    for c in caches:
        ncls = min(c.shape[2], ntok)
        cspecs.append(pl.BlockSpec((None, DIL_BLOCK, ncls, 2, DIL_HPG, HEAD_DIM),
                                   lambda bi: (bi, 0, 0, 0, 0, 0)))
    const = lambda a: pl.BlockSpec(a.shape, lambda bi: (0,) * a.ndim)
    return pl.pallas_call(
        functools.partial(_dilated_sample_kernel, ntok=ntok),
        out_shape=(jax.ShapeDtypeStruct(q.shape, F32),
                   jax.ShapeDtypeStruct((bd, ntok, DIL_GROUPS, DIL_HPG, 1), F32)),
        grid=(bd,),
        in_specs=[tok, tok, tok] + cspecs + [const(bias_cache), const(bias_new)],
        out_specs=(tok, pl.BlockSpec((None, ntok, DIL_GROUPS, DIL_HPG, 1), lambda bi: (bi, 0, 0, 0, 0))),
        compiler_params=_params(("parallel",), 32),
        name="dilated_sample",
    )(q, kn, vn, *caches, bias_cache, bias_new)


def _dilated_sample_bias(rel_bias, ntok):
    bc, bn = [], []
    m = jnp.arange(DIL_BLOCK)
    tn = jnp.arange(ntok)
    for gi, (_, r) in enumerate(DIL_PAIRS):
        tab = rel_bias[:, gi * DIL_HPG:(gi + 1) * DIL_HPG]
        rows_c, rows_n = [], []
        for t in range(ntok):
            if r == 1:
                jc = DIL_BLOCK + t - m
                okc = m >= t
                jn = t - tn
                okn = tn <= t
            else:
                jc = DIL_BLOCK - m
                okc = jnp.ones_like(m, bool)
                jn = jnp.zeros_like(tn)
                okn = tn == t
            rows_c.append(jnp.where(okc[:, None], _bias_lookup(tab, jc * r), -jnp.inf))
            rows_n.append(jnp.where(okn[:, None], _bias_lookup(tab, jn * r), -jnp.inf))
        bc.append(jnp.stack(rows_c))
        bn.append(jnp.stack(rows_n))
    return jnp.stack(bc)[..., None], jnp.stack(bn)[..., None]


def _merge_kernel(o0_ref, o1_ref, o2_ref, l0_ref, l1_ref, l2_ref, yr_ref, gt_ref, wr_ref, wd_ref,
                  out_ref, oa_ref, *nat, dilations):
    tm = out_ref.shape[0]
    o_nat, l_nat = [], []
    for g, (o_ref, l_ref, r) in enumerate(zip((o0_ref, o1_ref, o2_ref), (l0_ref, l1_ref, l2_ref), dilations)):
        if r == 1:
            o_nat.append([o_ref[:, _head_cols(h)] for h in range(DIL_HPG)])
            l_nat.append(l_ref[...])
        else:
            on, ln = nat[2 * g], nat[2 * g + 1]
            for c in range(r):
                ln[pl.ds(c, tm // r, stride=r), :] = l_ref[c]
                for h in range(DIL_HPG):
                    on[h, pl.ds(c, tm // r, stride=r), :] = o_ref[c, :, _head_cols(h)]
            o_nat.append([on[h] for h in range(DIL_HPG)])
            l_nat.append(ln[...])
    m = jnp.maximum(jnp.maximum(l_nat[0], l_nat[1]), l_nat[2])
    es = [jnp.exp(l - m) for l in l_nat]
    den = es[0] + es[1] + es[2]
    for h in range(DIL_HPG):
        lane = h * LSE_LANES
        acc = None
        for g in range(DIL_GROUPS):
            term = (es[g][:, lane:lane + 1] / den[:, lane:lane + 1]) * o_nat[g][h]
            acc = term if acc is None else acc + term
        oa_ref[:, _head_cols(h)] = acc.astype(oa_ref.dtype)
    d = out_ref.shape[1]
    for c0 in range(0, d, PROJ_CHUNK):
        cols = slice(c0, c0 + PROJ_CHUNK)
        yr = jnp.dot(yr_ref[...], wr_ref[:, cols], preferred_element_type=F32)
        ya = jnp.dot(oa_ref[...], wd_ref[:, cols], preferred_element_type=F32)
        g_r = gt_ref[:, cols].astype(F32)
        g_a = gt_ref[:, d + c0:d + c0 + PROJ_CHUNK].astype(F32)
        out_ref[:, cols] = (g_r * yr + g_a * ya).astype(out_ref.dtype)


def _merge(o_g, lse_g, dilations, b, l, yr_in, gates, w_ret_b, w_dil_b, tm):
    n = yr_in.shape[0]
    d = w_ret_b.shape[1]
    bps = l // tm
    row = lambda w: pl.BlockSpec((tm, w), lambda i: (i, 0))
    const = lambda a: pl.BlockSpec(a.shape, lambda i: (0, 0))

    def group_specs(width):
        specs = []
        for r in dilations:
            if r == 1:
                specs.append(row(width))
            else:
                specs.append(pl.BlockSpec((None, r, tm // r, width), lambda i: (i // bps, 0, i % bps, 0)))
        return specs

    view = lambda t, r, width: t.reshape(n, width) if r == 1 else t.reshape(b, r, l // r, width)
    o_in = [view(t, r, DIL_OUT_WIDTH) for t, r in zip(o_g, dilations)]
    l_in = [view(t, r, HEAD_DIM) for t, r in zip(lse_g, dilations)]
    scratch = [pltpu.VMEM((tm, DIL_OUT_WIDTH), BF16)]
    for _ in dilations:
        scratch += [pltpu.VMEM((DIL_HPG, tm, HEAD_DIM), F32), pltpu.VMEM((tm, HEAD_DIM), F32)]
    return pl.pallas_call(
        functools.partial(_merge_kernel, dilations=tuple(dilations)),
        out_shape=jax.ShapeDtypeStruct((n, d), BF16),
        grid=(n // tm,),
        in_specs=group_specs(DIL_OUT_WIDTH) + group_specs(HEAD_DIM) + [row(RET_WIDTH), row(2 * d),
                                                                      const(w_ret_b), const(w_dil_b)],
        out_specs=row(d),
        scratch_shapes=scratch,
        compiler_params=_params(("parallel",), 48),
        name="merge_branches",
    )(*o_in, *l_in, yr_in, gates, w_ret_b, w_dil_b)


ROUTER_ROWS = 8 + N_EXPERTS


def _route_select(lt):
    lg = lt[0:N_GROUPS]
    gmax = jnp.max(lg, axis=0, keepdims=True)
    w_coarse = 1.0 / jnp.sum(jnp.exp(lg - gmax), axis=0, keepdims=True)
    gid = lax.broadcasted_iota(jnp.int32, lg.shape, 0)
    gsel = jnp.min(jnp.where(lg == gmax, gid, N_GROUPS), axis=0, keepdims=True)
    le = jnp.zeros((EXPERTS_PER_GROUP, lt.shape[1]), F32)
    for g in range(N_GROUPS):
        le = jnp.where(gsel == g, lt[8 + g * EXPERTS_PER_GROUP:8 + (g + 1) * EXPERTS_PER_GROUP], le)
    eid = lax.broadcasted_iota(jnp.int32, le.shape, 0)
    v1 = jnp.max(le, axis=0, keepdims=True)
    i1 = jnp.min(jnp.where(le == v1, eid, EXPERTS_PER_GROUP), axis=0, keepdims=True)
    le2 = jnp.where(eid == i1, -jnp.inf, le)
    v2 = jnp.max(le2, axis=0, keepdims=True)
    i2 = jnp.min(jnp.where(le2 == v2, eid, EXPERTS_PER_GROUP), axis=0, keepdims=True)
    e21 = jnp.exp(v2 - v1)
    w1 = w_coarse / (1.0 + e21)
    w2 = w_coarse * e21 / (1.0 + e21)
    e1 = gsel * EXPERTS_PER_GROUP + i1
    e2 = gsel * EXPERTS_PER_GROUP + i2
    xid = lax.broadcasted_iota(jnp.int32, (N_EXPERTS, lt.shape[1]), 0)
    oh1 = jnp.where(xid == e1, 1.0, 0.0)
    oh2 = jnp.where(xid == e2, 1.0, 0.0)
    return e1, e2, w1, w2, oh1, oh2


def _route_rank(sel, p1, p2, before):
    e1, e2, w1, w2, oh1, oh2 = sel
    n1 = jnp.sum(oh1, axis=1, keepdims=True)
    n2 = jnp.sum(oh2, axis=1, keepdims=True)
    rank1 = jnp.sum(oh1 * (p1 + before), axis=0, keepdims=True)
    rank2 = jnp.sum(oh2 * (p2 + before + n1), axis=0, keepdims=True)
    row = lax.broadcasted_iota(jnp.int32, (8, e1.shape[1]), 0)
    ei = jnp.where(row == 0, e1, jnp.where(row == 1, e2, jnp.where(
        row == 2, rank1.astype(jnp.int32), jnp.where(row == 3, rank2.astype(jnp.int32), 0))))
    wt = jnp.where(row == 0, w1, jnp.where(row == 1, w2, 0.0))
    return ei, wt, before + n1 + n2


def _outproj_router_kernel(mg_ref, x_ref, wo_ref, g2_ref, wr_ref, br_ref, tri_ref, cin_ref,
                           h_ref, u2_ref, ei_ref, wt_ref, cnt_ref, h_keep):
    i = pl.program_id(0)

    @pl.when(i == 0)
    def _():
        cnt_ref[...] = cin_ref[...]
        h_keep[...] = jnp.zeros_like(h_keep)

    sub = tri_ref.shape[0]
    subtiles = [slice(s0, s0 + sub) for s0 in range(0, h_ref.shape[0], sub)]

    def project(rows):
        h_ref[rows, :] = x_ref[rows, :] + jnp.dot(mg_ref[rows, :], wo_ref[...], preferred_element_type=F32)

    project(subtiles[0])
    logits = []
    for rows in subtiles:
        h = h_keep[rows, :]
        u2 = h * lax.rsqrt(jnp.mean(h * h, axis=-1, keepdims=True) + NORM_EPS) * g2_ref[...]
        u2_ref[rows, :] = u2
        logits.append(lax.dot_general(wr_ref[...], u2.astype(BF16), (((1,), (1,)), ((), ())),
                                      preferred_element_type=F32) + br_ref[...])
    for rows in subtiles[1:]:
        project(rows)
    selected = [_route_select(lt) for lt in logits]
    prefix = [(jnp.dot(sel[4].astype(BF16), tri_ref[...], preferred_element_type=F32),
               jnp.dot(sel[5].astype(BF16), tri_ref[...], preferred_element_type=F32)) for sel in selected]
    h_keep[...] = h_ref[...]
    counts = cnt_ref[...]
    for rows, sel, (p1, p2) in zip(subtiles, selected, prefix):
        ei, wt, routed = _route_rank(sel, p1, p2, counts)
        counts = jnp.where(i > 0, routed, counts)
        ei_ref[:, rows] = ei
        wt_ref[:, rows] = wt
    cnt_ref[...] = counts


def _outproj_router(merged, x, w_o_b, norm_ffn, wr_t, br_t, counts_in, tm):
    n, d = x.shape
    nsteps = n // tm
    sub = min(tm, ROUTER_SUB)
    tri = (jnp.arange(sub)[:, None] < jnp.arange(sub)[None, :]).astype(BF16)
    matmul_tile = lambda: pl.BlockSpec((tm, d), lambda i: (jnp.minimum(i, nsteps - 1), 0))
    routed_rows = lambda: pl.BlockSpec((tm, d), lambda i: (jnp.maximum(i - 1, 0), 0))
    routed_lanes = lambda: pl.BlockSpec((8, tm), lambda i: (0, jnp.maximum(i - 1, 0)))
    const = lambda a: pl.BlockSpec(a.shape, lambda i: (0, 0))
    return pl.pallas_call(
        _outproj_router_kernel,
        out_shape=(jax.ShapeDtypeStruct((n, d), F32), jax.ShapeDtypeStruct((n, d), F32),
                   jax.ShapeDtypeStruct((8, n), jnp.int32), jax.ShapeDtypeStruct((8, n), F32),
                   jax.ShapeDtypeStruct((N_EXPERTS, 1), F32)),
        grid=(nsteps + 1,),
        in_specs=[matmul_tile(), matmul_tile(), const(w_o_b), pl.BlockSpec((1, d), lambda i: (0, 0)),
                  const(wr_t), const(br_t), const(tri), const(counts_in)],
        out_specs=(matmul_tile(), routed_rows(), routed_lanes(), routed_lanes(), const(counts_in)),
        scratch_shapes=[pltpu.VMEM((tm, d), F32)],
        compiler_params=_params(("arbitrary",), 60),
        name="outproj_router",
    )(merged, x, w_o_b, norm_ffn.reshape(1, d), wr_t, br_t, tri, counts_in)


def _router_weights(w_rg, b_rg, w_re, b_re):
    d = w_rg.shape[0]
    wr = jnp.zeros((ROUTER_ROWS, d), F32)
    wr = wr.at[0:N_GROUPS].set(w_rg.T)
    wr = wr.at[8:].set(w_re.transpose(0, 2, 1).reshape(N_EXPERTS, d))
    br = jnp.zeros((ROUTER_ROWS, 1), F32)
    br = br.at[0:N_GROUPS, 0].set(b_rg)
    br = br.at[8:, 0].set(b_re.reshape(N_EXPERTS))
    return wr.astype(BF16), br


DMA_UNROLL = 8


def _dispatch_kernel(pos_p_ref, pos_s_ref, zero_ref, up_ref, us_ref, xs_ref, zbuf, stage, sem, *, tm):
    i = pl.program_id(0)
    last = pl.num_programs(0) - 1
    ntiles = xs_ref.shape[0] // MOE_TILE

    def tile_copy(t):
        return pltpu.make_async_copy(zbuf, xs_ref.at[pl.ds(pl.multiple_of(t * MOE_TILE, MOE_TILE), MOE_TILE)],
                                     sem.at[2])

    @pl.when(i == 0)
    def _():
        zbuf[...] = jnp.zeros_like(zbuf)
        for op in ("start", "wait"):
            def per_tile(t, c, op=op):
                @pl.when(zero_ref[t] != 0)
                def _():
                    getattr(tile_copy(t), op)()
                return c

            lax.fori_loop(0, ntiles, per_tile, 0)

    def scatter_rows(op, src_ref, pos_ref, base, nrows, ntok_total, row_sem):
        if op == "wait":
            for _ in range(TOP_K):
                pltpu.make_async_copy(src_ref, xs_ref.at[pl.ds(0, nrows)], row_sem).wait()
            return

        def body(r, c):
            for kk in range(TOP_K):
                slot = pos_ref[kk * ntok_total + base + r]
                pltpu.make_async_copy(src_ref.at[pl.ds(r, 1)], xs_ref.at[pl.ds(slot, 1)],
                                      row_sem).start(priority=kk % 2)
            return c

        lax.fori_loop(0, nrows, body, 0, unroll=DMA_UNROLL)

    prompt_rows = last * tm
    cur = i % 2

    @pl.when(i < last)
    def _():
        stage[cur] = up_ref[...]
        scatter_rows("start", stage.at[cur], pos_p_ref, i * tm, tm, prompt_rows, sem.at[cur])

    @pl.when(i > 0)
    def _():
        scatter_rows("wait", stage.at[1 - cur], pos_p_ref, (i - 1) * tm, tm, prompt_rows, sem.at[1 - cur])

    @pl.when(i == last)
    def _():
        for op in ("start", "wait"):
            scatter_rows(op, us_ref, pos_s_ref, 0, us_ref.shape[0], us_ref.shape[0], sem.at[2])


def _dispatch(pos_p, pos_s, zero_tiles, u2p, u2s, tm):
    n, d = u2p.shape
    nsteps = n // tm
    grid_spec = pltpu.PrefetchScalarGridSpec(
        num_scalar_prefetch=3,
        grid=(nsteps + 1,),
        in_specs=[pl.BlockSpec((tm, d), lambda i, *_: (jnp.minimum(i, nsteps - 1), 0)),
                  pl.BlockSpec(u2s.shape, lambda i, *_: (0, 0))],
        out_specs=pl.BlockSpec(memory_space=pl.ANY),
        scratch_shapes=[pltpu.VMEM((MOE_TILE, d), F32), pltpu.VMEM((2, tm, d), F32),
                        pltpu.SemaphoreType.DMA((3,))],
    )
    return pl.pallas_call(
        functools.partial(_dispatch_kernel, tm=tm),
        out_shape=jax.ShapeDtypeStruct((zero_tiles.shape[0] * MOE_TILE, d), F32),
        grid_spec=grid_spec,
        compiler_params=_params(("arbitrary",), 32),
        name="moe_dispatch",
    )(pos_p, pos_s, zero_tiles, u2p, u2s)


def _gmm_kernel(te_ref, nt_ref, x_ref, wg_ref, wu_ref, wd_ref, o_ref, wgb, wub, wdb):
    i = pl.program_id(0)
    fresh = jnp.logical_or(i == 0, te_ref[i] != te_ref[jnp.maximum(i - 1, 0)])

    @pl.when(jnp.logical_and(fresh, i < nt_ref[0]))
    def _():
        wgb[...] = wg_ref[...].astype(BF16)
        wub[...] = wu_ref[...].astype(BF16)
        wdb[...] = wd_ref[...].astype(BF16)

    @pl.when(i < nt_ref[0])
    def _():
        x = x_ref[...].astype(BF16)
        acc = None
        for c0 in range(0, wgb.shape[1], PROJ_CHUNK):
            cols = slice(c0, c0 + PROJ_CHUNK)
            a = jnp.dot(x, wgb[:, cols], preferred_element_type=F32)
            b = jnp.dot(x, wub[:, cols], preferred_element_type=F32)
            hm = (a * jax.nn.sigmoid(a) * b).astype(BF16)
            part = jnp.dot(hm, wdb[cols, :], preferred_element_type=F32)
            acc = part if acc is None else acc + part
        o_ref[...] = acc

    @pl.when(i >= nt_ref[0])
    def _():
        o_ref[...] = jnp.zeros_like(o_ref)


def _gmm(tile_expert, ntiles_used, xs, w_gate, w_up, w_down):
    npad, d = xs.shape
    f = w_gate.shape[-1]
    ntiles = npad // MOE_TILE
    epg = w_gate.shape[1]

    def xmap(i, te, nt):
        return (jnp.minimum(i, nt[0] - 1), 0)

    def wmap(i, te, nt):
        e = te[i]
        return (e // epg, e % epg, 0, 0)

    grid_spec = pltpu.PrefetchScalarGridSpec(
        num_scalar_prefetch=2,
        grid=(ntiles,),
        in_specs=[pl.BlockSpec((MOE_TILE, d), xmap),
                  pl.BlockSpec((None, None, d, f), wmap),
                  pl.BlockSpec((None, None, d, f), wmap),
                  pl.BlockSpec((None, None, f, d), wmap)],
        out_specs=pl.BlockSpec((MOE_TILE, d), lambda i, te, nt: (i, 0)),
        scratch_shapes=[pltpu.VMEM((d, f), BF16), pltpu.VMEM((d, f), BF16), pltpu.VMEM((f, d), BF16)],
    )
    return pl.pallas_call(
        _gmm_kernel,
        out_shape=jax.ShapeDtypeStruct((npad, d), F32),
        grid_spec=grid_spec,
        compiler_params=_params(("arbitrary",), 56),
        name="moe_grouped_matmul",
    )(tile_expert, ntiles_used, xs, w_gate, w_up, w_down)


def _combine_kernel(pos_ref, h_ref, wt_ref, os_ref, y_ref, g0, g1, sem, *, tm, ntok_total):
    i = pl.program_id(0)
    nsteps = pl.num_programs(0)
    cur = i % 2

    def row_copy(step, r, kk, slot):
        src = pos_ref[kk * ntok_total + step * tm + r]
        buf = (g0, g1)[kk]
        return pltpu.make_async_copy(os_ref.at[pl.ds(src, 1)], buf.at[slot, pl.ds(r, 1)], sem.at[slot])

    def issue_step(step, slot):
        def body(r, c):
            row_copy(step, r, 0, slot).start(priority=0)
            row_copy(step, r, 1, slot).start(priority=1)
            return c

        lax.fori_loop(0, tm, body, 0, unroll=DMA_UNROLL)

    @pl.when(i == 0)
    def _():
        issue_step(0, 0)

    @pl.when(i + 1 < nsteps)
    def _():
        issue_step(i + 1, 1 - cur)

    for buf in (g0, g1):
        pltpu.make_async_copy(os_ref.at[pl.ds(0, tm)], buf.at[cur], sem.at[cur]).wait()
    y_ref[...] = h_ref[...] + wt_ref[:, 0:1] * g0[cur] + wt_ref[:, 1:2] * g1[cur]


def _combine(pos_flat, h, wt_cols, out_sorted, tm):
    n, d = h.shape
    grid_spec = pltpu.PrefetchScalarGridSpec(
        num_scalar_prefetch=1,
        grid=(n // tm,),
        in_specs=[pl.BlockSpec((tm, d), lambda i, pos: (i, 0)),
                  pl.BlockSpec((tm, 8), lambda i, pos: (i, 0)),
                  pl.BlockSpec(memory_space=pl.ANY)],
        out_specs=pl.BlockSpec((tm, d), lambda i, pos: (i, 0)),
        scratch_shapes=[pltpu.VMEM((2, tm, d), F32), pltpu.VMEM((2, tm, d), F32),
                        pltpu.SemaphoreType.DMA((2,))],
    )
    return pl.pallas_call(
        functools.partial(_combine_kernel, tm=tm, ntok_total=n),
        out_shape=jax.ShapeDtypeStruct((n, d), F32),
        grid_spec=grid_spec,
        compiler_params=_params(("arbitrary",), 32),
        name="moe_combine",
    )(pos_flat, h, wt_cols, out_sorted)


def _moe_plan(counts, route_sets):
    npairs = sum(e.shape[1] for e, _ in route_sets) * TOP_K
    tiles = (counts + MOE_TILE - 1) // MOE_TILE
    tile_end = jnp.cumsum(tiles)
    offs = (tile_end - tiles) * MOE_TILE
    ids = jnp.arange(N_EXPERTS)
    slots = []
    for experts, ranks in route_sets:
        base = jnp.sum(jnp.where(experts[..., None] == ids, offs, 0), axis=-1)
        slots.append((base + ranks).reshape(-1).astype(jnp.int32))
    ntiles = (npairs + N_EXPERTS * (MOE_TILE - 1)) // MOE_TILE
    tile_expert = jnp.sum(tile_end[None, :] <= jnp.arange(ntiles)[:, None], axis=1)
    tile_expert = jnp.minimum(tile_expert, N_EXPERTS - 1).astype(jnp.int32)
    tile_ids = jnp.arange(ntiles)
    is_last = jnp.any((tile_ids[:, None] == tile_end[None, :] - 1) & (tiles[None, :] > 0), axis=1)
    zero_tiles = (is_last | (tile_ids >= tile_end[-1])).astype(jnp.int32)
    return slots, tile_expert, tile_end[-1:].astype(jnp.int32), zero_tiles


def _mixers_prompt(x, wts, counts_in, shift_jobs):
    b, l, d = x.shape
    n = b * l
    tm = ROW_TILE
    xf = x.reshape(n, d)
    u = _rmsnorm(xf, wts["norm_attn"], tm)
    tab = _rotary_tables(jnp.arange(l), l)
    (qk_r, vg_r, gates), shifted = _proj_retention_and_gates(u, wts["w_in"], tab, 2 * tm, shift_jobs)

    s0 = jnp.zeros((b, RET_HEADS, HEAD_DIM, HEAD_DIM), F32)
    yr_in, s_fin = _retention_prompt(qk_r.reshape(b, l, -1), vg_r.reshape(b, l, -1), s0, 256, min(l, 2048))

    o_g, lse_g, bufs = [], [], []
    for gi, (w, r) in enumerate(DIL_PAIRS):
        qg, kg, vg, kt, vt = _proj_dilated_prompt(u, wts["w_in"], wts["q_norm"], wts["k_norm"], gi, b, l, tm)
        bias = _dilated_bias(wts["rel_bias"][:, gi * DIL_HPG:(gi + 1) * DIL_HPG], r)
        o, lse = _dilated_prompt(qg, kg, vg, bias, min(8, l // r // DIL_BLOCK))
        o_g.append(o)
        lse_g.append(lse)
        bufs.append(jnp.stack([kt, vt], axis=2).reshape(b, kt.shape[1], 2, DIL_HPG, HEAD_DIM))

    merged = _merge(o_g, lse_g, [r for _, r in DIL_PAIRS], b, l, yr_in.reshape(n, RET_WIDTH), gates,
                    wts["w_ret_out"], wts["w_dil_out"], tm)
    h, u2, ei, wt, counts = _outproj_router(merged, xf, wts["w_o"], wts["norm_ffn"], wts["wr_t"], wts["br_t"],
                                            counts_in, ROUTER_TILE)
    return h, u2, ei, wt, counts, s_fin, bufs, shifted


def _mixers_sample(x, caches, state, wts, counts_in):
    bd, t, d = x.shape
    n = bd * t
    xf = x.reshape(n, d)
    u = _rmsnorm(xf, wts["norm_attn"], n)
    tab = _rotary_tables(PAST_LEN + jnp.arange(t), n)
    (qk_r, vg_r, gates), _ = _proj_retention_and_gates(u, wts["w_in"], tab, n)
    qkv_a = _proj_dilated_sample(u, wts["w_in"], wts["q_norm"], wts["k_norm"])

    def heads(a):
        a = a.astype(F32).reshape(bd, t, RET_HEADS, HEAD_DIM).transpose(0, 2, 1, 3)
        return jnp.pad(a, ((0, 0), (0, 0), (0, 8 - t), (0, 0)))

    y_r, s_new = _retention_sample(heads(qk_r[:, :RET_WIDTH]), heads(qk_r[:, RET_WIDTH:]),
                                   heads(vg_r[:, :RET_WIDTH]), heads(vg_r[:, RET_WIDTH:]), state, t)
    yr_in = y_r[:, :, :t].transpose(0, 2, 1, 3).reshape(n, RET_WIDTH).astype(BF16)

    grp = lambda a: a.reshape(bd, t, DIL_GROUPS, DIL_HPG, HEAD_DIM)
    qa, ka, va = (grp(qkv_a[:, s * DIL_WIDTH:(s + 1) * DIL_WIDTH]) for s in range(3))
    cviews = [c.reshape(bd, DIL_BLOCK, r, 2, DIL_HPG, HEAD_DIM) for c, (_, r) in zip(caches, DIL_PAIRS)]
    bias_c, bias_n = _dilated_sample_bias(wts["rel_bias"], t)
    o, lse = _dilated_sample(qa, ka, va, cviews, bias_c, bias_n, t)
    o_g = [o[:, :, gi].reshape(n, 1, DIL_OUT_WIDTH) for gi in range(DIL_GROUPS)]
    lse_g = [jnp.repeat(lse[:, :, gi].reshape(n, DIL_HPG), LSE_LANES, axis=1).reshape(n, 1, HEAD_DIM)
             for gi in range(DIL_GROUPS)]

    news = [jnp.stack([ka[:, :, gi], va[:, :, gi]], axis=2) for gi in range(DIL_GROUPS)]
    shift_jobs = list(zip(caches, news))

    merged = _merge(o_g, lse_g, [1] * DIL_GROUPS, 1, n, yr_in, gates, wts["w_ret_out"], wts["w_dil_out"], n)
    h, u2, ei, wt, counts = _outproj_router(merged, xf, wts["w_o"], wts["norm_ffn"], wts["wr_t"], wts["br_t"],
                                            counts_in, n)
    return h, u2, ei, wt, counts, s_new, shift_jobs


def kernel(x_prompt, x_sample, cache_kv_g0, cache_kv_g1, cache_kv_g2, state_ret, norm_attn, w_in, q_norm,
           k_norm, rel_bias, w_ret_out, w_dil_out, w_o, norm_ffn, w_router_group, b_router_group,
           w_router_expert, b_router_expert, w_gate, w_up, w_down):
    caches = (cache_kv_g0, cache_kv_g1, cache_kv_g2)
    ntok = x_sample.shape[1]
    for c, (w, r) in zip(caches, DIL_PAIRS):
        assert c.shape[1] == w == DIL_BLOCK * r and (r == 1 or ntok <= r) and ntok <= 8
    wr_t, br_t = _router_weights(w_router_group, b_router_group, w_router_expert, b_router_expert)
    wts = dict(norm_attn=norm_attn, w_in=w_in, q_norm=q_norm, k_norm=k_norm, rel_bias=rel_bias,
               w_ret_out=w_ret_out.astype(BF16), w_dil_out=w_dil_out.astype(BF16), w_o=w_o.astype(BF16),
               norm_ffn=norm_ffn, wr_t=wr_t, br_t=br_t)

    zero_counts = jnp.zeros((N_EXPERTS, 1), F32)
    hs, u2s, eis, wts_s, counts_s, s_s, shift_jobs = _mixers_sample(x_sample, caches, state_ret, wts, zero_counts)
    hp, u2p, eip, wtp, counts, s_p, bufs_p, bufs_s = _mixers_prompt(x_prompt, wts, counts_s, shift_jobs)

    slots, tile_expert, ntiles_used, zero_tiles = _moe_plan(
        counts[:, 0].astype(jnp.int32), [(eip[0:2], eip[2:4]), (eis[0:2], eis[2:4])])
    xs = _dispatch(slots[0], slots[1], zero_tiles, u2p, u2s, 256)
    out_sorted = _gmm(tile_expert, ntiles_used, xs, w_gate, w_up, w_down)
    yp = _combine(slots[0], hp, wtp.T, out_sorted, 256)
    ys = _combine(slots[1], hs, wts_s.T, out_sorted, hs.shape[0])

    return (yp.reshape(x_prompt.shape), ys.reshape(x_sample.shape), bufs_p[0], bufs_p[1], bufs_p[2], s_p,
            bufs_s[0], bufs_s[1], bufs_s[2], s_s)
```

```python
import functools
import math

import jax
import jax.numpy as jnp
from jax import lax
from jax.experimental import pallas as pl
from jax.experimental.pallas import tpu as pltpu

HEAD_DIM = 128
RET_HEADS = 8
RET_WIDTH = RET_HEADS * HEAD_DIM
ROPE_BASE = 10000.0
GN_EPS = 1e-5
DIL_PAIRS = ((128, 1), (512, 4), (2048, 16))
DIL_GROUPS = len(DIL_PAIRS)
DIL_HPG = 4
DIL_HEADS = DIL_HPG * DIL_GROUPS
DIL_WIDTH = DIL_HEADS * HEAD_DIM
DIL_OUT_WIDTH = DIL_HPG * HEAD_DIM
DIL_BLOCK = 128
LSE_LANES = HEAD_DIM // DIL_HPG
ATTN_SCALE = HEAD_DIM ** -0.5
REL_BUCKETS = 32
REL_MAX_DIST = 2048
N_GROUPS = 4
EXPERTS_PER_GROUP = 8
N_EXPERTS = N_GROUPS * EXPERTS_PER_GROUP
TOP_K = 2
NORM_EPS = 1e-6
PAST_LEN = 16384

COL_QR = 0
COL_KR = COL_QR + RET_WIDTH
COL_VR = COL_KR + RET_WIDTH
COL_GR = COL_VR + RET_WIDTH
COL_QA = COL_GR + RET_WIDTH
COL_KA = COL_QA + DIL_WIDTH
COL_VA = COL_KA + DIL_WIDTH
COL_GATES = COL_VA + DIL_WIDTH

MOE_TILE = 256
ROW_TILE = 512
PROJ_CHUNK = 256
ROUTER_TILE = 512
ROUTER_SUB = 256
SAMPLE_SEQS_PER_STEP = 1
MIB = 1 << 20
BF16 = jnp.bfloat16
F32 = jnp.float32


def _params(semantics, vmem_mib):
    return pltpu.CompilerParams(dimension_semantics=semantics, vmem_limit_bytes=vmem_mib * MIB)


def _head_cols(h):
    return slice(h * HEAD_DIM, (h + 1) * HEAD_DIM)


def _rmsnorm_kernel(x_ref, g_ref, o_ref):
    x = x_ref[...]
    y = x * lax.rsqrt(jnp.mean(x * x, axis=-1, keepdims=True) + NORM_EPS)
    o_ref[...] = (y * g_ref[...]).astype(o_ref.dtype)


def _rmsnorm(x, g, tm):
    n, d = x.shape
    return pl.pallas_call(
        _rmsnorm_kernel,
        out_shape=jax.ShapeDtypeStruct((n, d), BF16),
        grid=(n // tm,),
        in_specs=[pl.BlockSpec((tm, d), lambda i: (i, 0)), pl.BlockSpec((1, d), lambda i: (0, 0))],
        out_specs=pl.BlockSpec((tm, d), lambda i: (i, 0)),
        compiler_params=_params(("parallel",), 40),
        name="rmsnorm",
    )(x, g.reshape(1, d))


def _background_shift_step(step, nsteps, src_ref, new_ref, dst_ref, ring, sems, *, chunks_per_seq, rows):
    shift = new_ref.shape[1]

    def chunk_copy(s, slot, inbound):
        b = s // chunks_per_seq
        r0 = (s % chunks_per_seq) * rows
        if inbound:
            return pltpu.make_async_copy(src_ref.at[b, pl.ds(pl.multiple_of(r0 + shift, 8), rows)],
                                         ring.at[slot], sems.at[0, slot])
        return pltpu.make_async_copy(ring.at[slot], dst_ref.at[b, pl.ds(pl.multiple_of(r0, 8), rows)],
                                     sems.at[1, slot])

    cur = step % 2

    @pl.when(step >= 2)
    def _():
        chunk_copy(step - 2, cur, False).wait()

    chunk_copy(step, cur, True).start()

    @pl.when(step >= 1)
    def _():
        chunk_copy(step - 1, 1 - cur, True).wait()
        chunk_copy(step - 1, 1 - cur, False).start()

    @pl.when(step == nsteps - 1)
    def _():
        tail = pltpu.make_async_copy(new_ref, dst_ref.at[:, pl.ds(dst_ref.shape[1] - shift, shift)], sems.at[2, 0])
        tail.start()
        chunk_copy(step, cur, True).wait()
        chunk_copy(step, cur, False).start()
        chunk_copy(step - 1, 1 - cur, False).wait()
        chunk_copy(step, cur, False).wait()
        tail.wait()


def _proj_kernel(u_ref, *refs, epilogue, n_w, n_extra, n_out, shift):
    w_refs, refs = refs[:n_w], refs[n_w:]
    extras, refs = refs[:n_extra], refs[n_extra:]
    if shift is not None:
        (src_ref, new_ref), refs = refs[:2], refs[2:]
    outs, refs = refs[:n_out], refs[n_out:]
    if shift is not None:
        dst_ref, (ring, sems), refs = refs[0], refs[-2:], refs[1:-2]
        nrow_blocks = pl.num_programs(1)
        _background_shift_step(pl.program_id(0) * nrow_blocks + pl.program_id(1), pl.num_programs(0) * nrow_blocks,
                               src_ref, new_ref, dst_ref, ring, sems, **shift)
    scratch, wb_refs = refs[:len(refs) - n_w], refs[len(refs) - n_w:]

    @pl.when(pl.program_id(1) == 0)
    def _():
        for w_ref, wb_ref in zip(w_refs, wb_refs):
            wb_ref[...] = w_ref[...].astype(BF16)

    tn = wb_refs[0].shape[1]
    for wi, wb_ref in enumerate(wb_refs):
        for c0 in range(0, tn, PROJ_CHUNK):
            acc = jnp.dot(u_ref[...], wb_ref[:, c0:c0 + PROJ_CHUNK], preferred_element_type=F32)
            epilogue(wi, c0, acc, *extras, *outs, *scratch)


def _proj(u, w_in, col_offsets, ncol_blocks, tn, tm, epilogue, extra, extra_specs, out_shape, out_specs,
          scratch, name, shift_job=None, vmem_mib=58):
    n, k = u.shape
    w_specs = [pl.BlockSpec((pl.Element(k), pl.Element(tn)),
                            functools.partial(lambda j, i, o: (0, pl.multiple_of(o + j * tn, HEAD_DIM)), o=o))
               for o in col_offsets]
    multi = isinstance(out_shape, (tuple, list))
    out_shapes = list(out_shape) if multi else [out_shape]
    out_spec_list = list(out_specs) if multi else [out_specs]
    n_out = len(out_shapes)
    side_in, side_in_specs, side_scratch, shift = [], [], [], None
    if shift_job is not None:
        cache, new = shift_job
        shape = cache.shape
        bd = shape[0]
        cache = cache.reshape(bd, -1, HEAD_DIM)
        new = new.reshape(bd, -1, HEAD_DIM)
        nsteps = ncol_blocks * (n // tm)
        chunks_per_seq = nsteps // bd
        rows = (cache.shape[1] - new.shape[1]) // chunks_per_seq
        assert chunks_per_seq * bd == nsteps and rows % 8 == 0
        assert rows * chunks_per_seq == cache.shape[1] - new.shape[1]
        shift = dict(chunks_per_seq=chunks_per_seq, rows=rows)
        any_spec = pl.BlockSpec(memory_space=pl.ANY)
        side_in, side_in_specs = [cache, new], [any_spec, pl.BlockSpec(new.shape, lambda j, i: (0, 0, 0))]
        out_shapes.append(jax.ShapeDtypeStruct(cache.shape, cache.dtype))
        out_spec_list.append(any_spec)
        side_scratch = [pltpu.VMEM((2, rows, HEAD_DIM), F32), pltpu.SemaphoreType.DMA((3, 2))]
    outs = pl.pallas_call(
        functools.partial(_proj_kernel, epilogue=epilogue, n_w=len(col_offsets), n_extra=len(extra), n_out=n_out,
                          shift=shift),
        out_shape=tuple(out_shapes) if (multi or shift) else out_shapes[0],
        grid=(ncol_blocks, n // tm),
        in_specs=[pl.BlockSpec((tm, k), lambda j, i: (i, 0))] + w_specs + list(extra_specs) + side_in_specs,
        out_specs=tuple(out_spec_list) if (multi or shift) else out_spec_list[0],
        scratch_shapes=list(scratch) + [pltpu.VMEM((k, tn), BF16) for _ in col_offsets] + side_scratch,
        compiler_params=_params(("arbitrary", "arbitrary"), vmem_mib),
        name=name,
    )(u, *([w_in] * len(col_offsets)), *extra, *side_in)
    if shift is None:
        return outs
    shifted = outs[-1].reshape(shape)
    return (tuple(outs[:-1]) if multi else outs[0]), shifted


def _chunk_heads(c0, acc):
    return [(slice(c0 + h * HEAD_DIM, c0 + (h + 1) * HEAD_DIM), acc[:, _head_cols(h)])
            for h in range(acc.shape[1] // HEAD_DIM)]


def _epi_rotary(wi, c0, acc, tab_ref, o_ref):
    c = tab_ref[0]
    s = tab_ref[1]
    for cols, xh in _chunk_heads(c0, acc):
        o_ref[:, cols] = (xh * c + pltpu.roll(xh, HEAD_DIM // 2, 1) * s).astype(o_ref.dtype)


def _epi_value_gate(wi, c0, acc, o_ref):
    is_gate = pl.program_id(0) == 1
    o_ref[:, c0:c0 + acc.shape[1]] = jnp.where(is_gate, acc * jax.nn.sigmoid(acc), acc).astype(o_ref.dtype)


def _epi_sigmoid(wi, c0, acc, o_ref):
    o_ref[:, c0:c0 + acc.shape[1]] = jax.nn.sigmoid(acc).astype(o_ref.dtype)


def _head_rms(xh, g):
    return xh * lax.rsqrt(jnp.mean(xh * xh, axis=-1, keepdims=True) + NORM_EPS) * g


def _epi_dilated_prompt(wi, c0, acc, qn_ref, kn_ref, q_ref, k_ref, v_ref, kt_ref, vt_ref, scr, *, r):
    tm = scr.shape[2]
    tail_rows = kt_ref.shape[0]
    gain = (qn_ref[...] * ATTN_SCALE, kn_ref[...], None)[wi]
    out_ref = (q_ref, k_ref, v_ref)[wi]
    tail_ref = (None, kt_ref, vt_ref)[wi]
    for cols, xh in _chunk_heads(c0, acc):
        h = cols.start // HEAD_DIM
        y = xh if gain is None else _head_rms(xh, gain)
        slab = scr.at[wi, h]
        slab[...] = y
        if tail_ref is not None:
            tail_ref[:, cols] = y[tm - tail_rows:tm, :]
        if r == 1:
            out_ref[:, cols] = y.astype(out_ref.dtype)
        else:
            for c in range(r):
                out_ref[c, :, cols] = slab[pl.ds(c, tm // r, stride=r), :].astype(out_ref.dtype)


def _epi_dilated_sample(wi, c0, acc, qn_ref, kn_ref, o_ref):
    j = pl.program_id(0)
    gain = jnp.where(j < DIL_GROUPS, qn_ref[...] * ATTN_SCALE, kn_ref[...])
    for cols, xh in _chunk_heads(c0, acc):
        o_ref[:, cols] = jnp.where(j < 2 * DIL_GROUPS, _head_rms(xh, gain), xh)


def _rotary_tables(pos, rows):
    half = HEAD_DIM // 2
    inv = ROPE_BASE ** (-jnp.arange(half, dtype=F32) / half)
    ang = pos.astype(F32)[:, None] * inv[None, :]
    cos = jnp.cos(ang)
    sin = jnp.sin(ang)
    c = jnp.concatenate([cos, cos], axis=-1)
    s = jnp.concatenate([-sin, sin], axis=-1)
    tab = jnp.stack([jnp.stack([c, s]), jnp.stack([c, s]) * (HEAD_DIM ** -0.5)])
    reps = rows // pos.shape[0]
    return jnp.tile(tab, (1, 1, reps, 1))


def _proj_retention_and_gates(u, w_in, pos_tab, tm, shift_jobs=(None, None, None)):
    n = u.shape[0]
    nblk = pos_tab.shape[2] // tm
    tile_out = lambda: pl.BlockSpec((tm, RET_WIDTH), lambda j, i: (i, j))
    ngate = w_in.shape[1] - COL_GATES
    qk_r = _proj(u, w_in, [COL_QR], 2, RET_WIDTH, tm, _epi_rotary, [pos_tab],
                 [pl.BlockSpec((None, 2, tm, HEAD_DIM), lambda j, i: (j, 0, i % nblk, 0))],
                 jax.ShapeDtypeStruct((n, 2 * RET_WIDTH), BF16), tile_out(), [], "proj_qk_ret", shift_jobs[0])
    vg_r = _proj(u, w_in, [COL_VR], 2, RET_WIDTH, tm, _epi_value_gate, [], [],
                 jax.ShapeDtypeStruct((n, 2 * RET_WIDTH), BF16), tile_out(), [], "proj_vg_ret", shift_jobs[1])
    gates = _proj(u, w_in, [COL_GATES], ngate // RET_WIDTH, RET_WIDTH, tm, _epi_sigmoid, [], [],
                  jax.ShapeDtypeStruct((n, ngate), BF16), tile_out(), [], "proj_gates", shift_jobs[2])
    results = (qk_r, vg_r, gates)
    if any(job is not None for job in shift_jobs):
        return tuple(r[0] for r in results), [r[1] for r in results]
    return results, []


def _proj_dilated_prompt(u, w_in, q_norm, k_norm, gi, b, l, tm):
    w, r = DIL_PAIRS[gi]
    lw = min(w, l)
    bps = l // tm
    tail_rows = min(lw, tm)
    tail_blocks = lw // tail_rows
    cm = (b, r, l // r, DIL_OUT_WIDTH)
    if r == 1:
        cm_shape = jax.ShapeDtypeStruct((b * l, DIL_OUT_WIDTH), BF16)
        cm_spec = lambda: pl.BlockSpec((tm, DIL_OUT_WIDTH), lambda j, i: (i, 0))
    else:
        cm_shape = jax.ShapeDtypeStruct(cm, BF16)
        cm_spec = lambda: pl.BlockSpec((None, r, tm // r, DIL_OUT_WIDTH), lambda j, i: (i // bps, 0, i % bps, 0))
    tail_shape = jax.ShapeDtypeStruct((b, lw, DIL_OUT_WIDTH), F32)
    tail_spec = lambda: pl.BlockSpec(
        (None, tail_rows, DIL_OUT_WIDTH),
        lambda j, i: (i // bps, jnp.maximum(i % bps - (bps - tail_blocks), 0), 0))
    gspec = pl.BlockSpec((1, HEAD_DIM), lambda j, i: (0, 0))
    off = gi * DIL_OUT_WIDTH
    q, k, v, kt, vt = _proj(
        u, w_in, [COL_QA + off, COL_KA + off, COL_VA + off], 1, DIL_OUT_WIDTH, tm,
        functools.partial(_epi_dilated_prompt, r=r),
        [q_norm.reshape(1, HEAD_DIM), k_norm.reshape(1, HEAD_DIM)], [gspec, gspec],
        (cm_shape, cm_shape, cm_shape, tail_shape, tail_shape),
        (cm_spec(), cm_spec(), cm_spec(), tail_spec(), tail_spec()),
        [pltpu.VMEM((3, DIL_HPG, tm, HEAD_DIM), F32)], "proj_dilated_prompt")
    cls = lambda t: t.reshape(b * r, l // r, DIL_OUT_WIDTH)
    return cls(q), cls(k), cls(v), kt, vt


def _proj_dilated_sample(u, w_in, q_norm, k_norm):
    n = u.shape[0]
    gspec = pl.BlockSpec((1, HEAD_DIM), lambda j, i: (0, 0))
    return _proj(u, w_in, [COL_QA], 3 * DIL_GROUPS, DIL_OUT_WIDTH, n, _epi_dilated_sample,
                 [q_norm.reshape(1, HEAD_DIM), k_norm.reshape(1, HEAD_DIM)], [gspec, gspec],
                 jax.ShapeDtypeStruct((n, 3 * DIL_WIDTH), F32),
                 pl.BlockSpec((n, DIL_OUT_WIDTH), lambda j, i: (i, j)), [], "proj_dilated_sample")


def _retention_kernel(q_ref, k_ref, v_ref, g_ref, s0_ref, dm_ref, qd_ref, kd_ref, cd_ref,
                      y_ref, sf_ref, state, *, chunk, nchunks):
    t = pl.program_id(2)

    @pl.when(t == 0)
    def _():
        state[...] = s0_ref[...]

    dmask = dm_ref[...]
    qdec = qd_ref[...]
    kdec = kd_ref[...]
    cdec = cd_ref[...]
    s_prev = state[...]
    for ci in range(nchunks):
        rows = pl.ds(ci * chunk, chunk)
        q = q_ref[rows, :]
        k = k_ref[rows, :]
        v = v_ref[rows, :]
        sc = lax.dot_general(q, k, (((1,), (1,)), ((), ())), preferred_element_type=F32) * dmask
        intra = jnp.dot(sc.astype(BF16), v, preferred_element_type=F32)
        qd = (q.astype(F32) * qdec).astype(BF16)
        cross = jnp.dot(qd, s_prev.astype(BF16), preferred_element_type=F32)
        kd = (k.astype(F32) * kdec).astype(BF16)
        kv = lax.dot_general(kd, v, (((0,), (0,)), ((), ())), preferred_element_type=F32)
        s_prev = s_prev * cdec + kv
        o = intra + cross
        mu = jnp.mean(o, axis=-1, keepdims=True)
        oc = o - mu
        var = jnp.mean(oc * oc, axis=-1, keepdims=True)
        y = g_ref[rows, :].astype(F32) * (oc * lax.rsqrt(var + GN_EPS))
        y_ref[rows, :] = y.astype(y_ref.dtype)
    state[...] = s_prev

    @pl.when(t == pl.num_programs(2) - 1)
    def _():
        sf_ref[...] = state[...]


def _retention_decay(chunk, valid):
    lg = jnp.log1p(-jnp.exp2(-5.0 - jnp.arange(RET_HEADS, dtype=F32)))
    idx = jnp.arange(chunk, dtype=F32)
    rel = idx[:, None] - idx[None, :]
    dmask = jnp.where(rel[None] >= 0, jnp.exp(lg[:, None, None] * jnp.maximum(rel, 0.0)[None]), 0.0)
    qdec = jnp.exp(lg[:, None] * (idx + 1.0)[None, :])[..., None]
    kdec = jnp.exp(lg[:, None] * (valid - 1.0 - idx)[None, :])[..., None]
    kdec = jnp.where((idx < valid)[None, :, None], kdec, 0.0)
    cdec = jnp.exp(lg * valid)[:, None, None]
    return dmask, qdec, kdec, cdec


def _retention_prompt(qk_r, vg_r, s0, chunk, rows_per_step):
    b, l, _ = qk_r.shape
    h = RET_HEADS
    dmask, qdec, kdec, cdec = _retention_decay(chunk, chunk)
    nsteps = l // rows_per_step
    blk = lambda off: pl.BlockSpec((None, rows_per_step, HEAD_DIM), lambda bi, hi, ti: (bi, ti, hi + off))
    per_head = lambda shape: pl.BlockSpec((None,) + shape, lambda bi, hi, ti: (hi,) + (0,) * len(shape))
    state_spec = pl.BlockSpec((None, None, HEAD_DIM, HEAD_DIM), lambda bi, hi, ti: (bi, hi, 0, 0))
    return pl.pallas_call(
        functools.partial(_retention_kernel, chunk=chunk, nchunks=rows_per_step // chunk),
        out_shape=(jax.ShapeDtypeStruct((b, l, RET_WIDTH), BF16),
                   jax.ShapeDtypeStruct((b, h, HEAD_DIM, HEAD_DIM), F32)),
        grid=(b, h, nsteps),
        in_specs=[blk(0), blk(h), blk(0), blk(h), state_spec,
                  per_head((chunk, chunk)), per_head((chunk, 1)), per_head((chunk, 1)), per_head((1, 1))],
        out_specs=(blk(0), state_spec),
        scratch_shapes=[pltpu.VMEM((HEAD_DIM, HEAD_DIM), F32)],
        compiler_params=_params(("parallel", "parallel", "arbitrary"), 32),
        name="retention_prompt",
    )(qk_r, qk_r, vg_r, vg_r, s0, dmask, qdec, kdec, cdec)


def _retention_sample_kernel(q_ref, k_ref, v_ref, g_ref, s0_ref, dm_ref, qd_ref, kd_ref, cd_ref,
                             y_ref, sf_ref, *, ntok):
    for bi in range(q_ref.shape[0]):
        for h in range(RET_HEADS):
            q = q_ref[bi, h]
            k = k_ref[bi, h]
            v = v_ref[bi, h]
            s_prev = s0_ref[bi, h]
            dmask = dm_ref[h]
            o = jnp.dot(q * qd_ref[h], s_prev, preferred_element_type=F32)
            for j in range(ntok):
                sj = jnp.sum(q * k[j:j + 1, :], axis=-1, keepdims=True) * dmask[:, j:j + 1]
                o = o + sj * v[j:j + 1, :]
            kd = k * kd_ref[h]
            kv = lax.dot_general(kd, v, (((0,), (0,)), ((), ())), preferred_element_type=F32)
            sf_ref[bi, h] = s_prev * cd_ref[h] + kv
            mu = jnp.mean(o, axis=-1, keepdims=True)
            oc = o - mu
            var = jnp.mean(oc * oc, axis=-1, keepdims=True)
            y_ref[bi, h] = g_ref[bi, h] * (oc * lax.rsqrt(var + GN_EPS))


def _retention_sample(q, k, v, g, s0, ntok):
    bd, h, tp, _ = q.shape
    dmask, qdec, kdec, cdec = _retention_decay(tp, ntok)
    nb = math.gcd(bd, SAMPLE_SEQS_PER_STEP)
    tok = pl.BlockSpec((nb, h, tp, HEAD_DIM), lambda bi: (bi, 0, 0, 0))
    st = pl.BlockSpec((nb, h, HEAD_DIM, HEAD_DIM), lambda bi: (bi, 0, 0, 0))
    const = lambda a: pl.BlockSpec(a.shape, lambda bi: (0,) * a.ndim)
    return pl.pallas_call(
        functools.partial(_retention_sample_kernel, ntok=ntok),
        out_shape=(jax.ShapeDtypeStruct((bd, h, tp, HEAD_DIM), F32),
                   jax.ShapeDtypeStruct((bd, h, HEAD_DIM, HEAD_DIM), F32)),
        grid=(bd // nb,),
        in_specs=[tok, tok, tok, tok, st, const(dmask), const(qdec), const(kdec), const(cdec)],
        out_specs=(tok, st),
        compiler_params=_params(("parallel",), 32),
        name="retention_sample",
    )(q, k, v, g, s0, dmask, qdec, kdec, cdec)


def _t5_bucket(dist):
    max_exact = REL_BUCKETS // 2
    d = jnp.maximum(dist, 0)
    df = jnp.maximum(d, 1).astype(F32)
    large = max_exact + (jnp.log(df / max_exact) / math.log(REL_MAX_DIST / max_exact)
                         * (REL_BUCKETS - max_exact)).astype(jnp.int32)
    large = jnp.minimum(large, REL_BUCKETS - 1)
    return jnp.where(d < max_exact, d, large)


def _bias_lookup(tab, dist):
    onehot = _t5_bucket(dist)[..., None] == jnp.arange(REL_BUCKETS)
    return jnp.sum(jnp.where(onehot[..., None], tab.astype(F32), 0.0), axis=-2)


def _dilated_kernel(q_ref, kp_ref, kc_ref, vp_ref, vc_ref, bias_ref, o_ref, lse_ref, kfull, vfull, *, nsub):
    i = pl.program_id(1)
    blk = DIL_BLOCK
    kfull[0:blk, :] = kp_ref[...]
    kfull[blk:, :] = kc_ref[...]
    vfull[0:blk, :] = vp_ref[...]
    vfull[blk:, :] = vc_ref[...]
    col = lax.broadcasted_iota(jnp.int32, (blk, 2 * blk), 1)
    lane_head = lax.broadcasted_iota(jnp.int32, (blk, HEAD_DIM), 1) // LSE_LANES
    for s in range(nsub):
        rows = pl.ds(s * blk, blk)
        win = pl.ds(s * blk, 2 * blk)
        lse_tile = jnp.zeros((blk, HEAD_DIM), F32)
        for h in range(DIL_HPG):
            cols = pl.ds(h * HEAD_DIM, HEAD_DIM)
            q = q_ref[rows, cols]
            kw = kfull[win, cols]
            vw = vfull[win, cols]
            sc = lax.dot_general(q, kw, (((1,), (1,)), ((), ())), preferred_element_type=F32) + bias_ref[h]
            if s == 0:
                sc = jnp.where((col >= blk) | (i > 0), sc, -jnp.inf)
            m = jnp.max(sc, axis=-1, keepdims=True)
            p = jnp.exp(sc - m)
            l = jnp.sum(p, axis=-1, keepdims=True)
            o = jnp.dot(p.astype(BF16), vw, preferred_element_type=F32) / l
            o_ref[rows, cols] = o.astype(o_ref.dtype)
            lse_tile = jnp.where(lane_head == h, m + jnp.log(l), lse_tile)
        lse_ref[rows, :] = lse_tile


def _dilated_bias(bias_tab, r):
    blk = DIL_BLOCK
    qi = jnp.arange(blk)[:, None]
    kj = jnp.arange(2 * blk)[None, :]
    dc = blk + qi - kj
    band = (dc >= 0) & (dc <= blk)
    bias = _bias_lookup(bias_tab, dc * r).transpose(2, 0, 1)
    return jnp.where(band[None], bias, -jnp.inf)


def _dilated_prompt(q, k, v, bias, nsub):
    n, lc, w = q.shape
    blk = DIL_BLOCK
    tq = nsub * blk
    cur = pl.BlockSpec((None, tq, w), lambda ni, i: (ni, i, 0))
    prev = pl.BlockSpec((None, blk, w), lambda ni, i: (ni, jnp.maximum(i * nsub - 1, 0), 0))
    return pl.pallas_call(
        functools.partial(_dilated_kernel, nsub=nsub),
        out_shape=(jax.ShapeDtypeStruct((n, lc, w), F32),
                   jax.ShapeDtypeStruct((n, lc, HEAD_DIM), F32)),
        grid=(n, lc // tq),
        in_specs=[cur, prev, cur, prev, cur, pl.BlockSpec(bias.shape, lambda ni, i: (0, 0, 0))],
        out_specs=(cur, pl.BlockSpec((None, tq, HEAD_DIM), lambda ni, i: (ni, i, 0))),
        scratch_shapes=[pltpu.VMEM((tq + blk, w), BF16), pltpu.VMEM((tq + blk, w), BF16)],
        compiler_params=_params(("parallel", "parallel"), 32),
        name="dilated_prompt",
    )(q, k, k, v, v, bias)


def _dilated_sample_kernel(q_ref, kn_ref, vn_ref, c0_ref, c1_ref, c2_ref, bc_ref, bn_ref,
                           o_ref, lse_ref, *, ntok):
    caches = (c0_ref, c1_ref, c2_ref)
    for gi in range(DIL_GROUPS):
        cache = caches[gi]
        nclass = cache.shape[1]
        for t in range(ntok):
            cls = t if nclass > 1 else 0
            qt = q_ref[t, gi]
            kc = cache[:, cls, 0]
            vc = cache[:, cls, 1]
            sc = jnp.sum(kc * qt[None], axis=-1, keepdims=True) + bc_ref[gi, t]
            sn = jnp.sum(kn_ref[:, gi] * qt[None], axis=-1, keepdims=True) + bn_ref[gi, t]
            m = jnp.maximum(jnp.max(sc, axis=0), jnp.max(sn, axis=0))
            pc = jnp.exp(sc - m[None])
            pn = jnp.exp(sn - m[None])
            l = jnp.sum(pc, axis=0) + jnp.sum(pn, axis=0)
            o = jnp.sum(pc * vc, axis=0) + jnp.sum(pn * vn_ref[:, gi], axis=0)
            o_ref[t, gi] = o / l
            lse_ref[t, gi] = m + jnp.log(l)


def _dilated_sample(q, kn, vn, caches, bias_cache, bias_new, ntok):
    bd = q.shape[0]
    tok = pl.BlockSpec((None, ntok, DIL_GROUPS, DIL_HPG, HEAD_DIM), lambda bi: (bi, 0, 0, 0, 0))
    cspecs = []
    for c in caches:
        ncls = min(c.shape[2], ntok)
        cspecs.append(pl.BlockSpec((None, DIL_BLOCK, ncls, 2, DIL_HPG, HEAD_DIM),
                                   lambda bi: (bi, 0, 0, 0, 0, 0)))
    const = lambda a: pl.BlockSpec(a.shape, lambda bi: (0,) * a.ndim)
    return pl.pallas_call(
        functools.partial(_dilated_sample_kernel, ntok=ntok),
        out_shape=(jax.ShapeDtypeStruct(q.shape, F32),
                   jax.ShapeDtypeStruct((bd, ntok, DIL_GROUPS, DIL_HPG, 1), F32)),
        grid=(bd,),
        in_specs=[tok, tok, tok] + cspecs + [const(bias_cache), const(bias_new)],
        out_specs=(tok, pl.BlockSpec((None, ntok, DIL_GROUPS, DIL_HPG, 1), lambda bi: (bi, 0, 0, 0, 0))),
        compiler_params=_params(("parallel",), 32),
        name="dilated_sample",
    )(q, kn, vn, *caches, bias_cache, bias_new)


def _dilated_sample_bias(rel_bias, ntok):
    bc, bn = [], []
    m = jnp.arange(DIL_BLOCK)
    tn = jnp.arange(ntok)
    for gi, (_, r) in enumerate(DIL_PAIRS):
        tab = rel_bias[:, gi * DIL_HPG:(gi + 1) * DIL_HPG]
        rows_c, rows_n = [], []
        for t in range(ntok):
            if r == 1:
                jc = DIL_BLOCK + t - m
                okc = m >= t
                jn = t - tn
                okn = tn <= t
            else:
                jc = DIL_BLOCK - m
                okc = jnp.ones_like(m, bool)
                jn = jnp.zeros_like(tn)
                okn = tn == t
            rows_c.append(jnp.where(okc[:, None], _bias_lookup(tab, jc * r), -jnp.inf))
            rows_n.append(jnp.where(okn[:, None], _bias_lookup(tab, jn * r), -jnp.inf))
        bc.append(jnp.stack(rows_c))
        bn.append(jnp.stack(rows_n))
    return jnp.stack(bc)[..., None], jnp.stack(bn)[..., None]


def _merge_kernel(o0_ref, o1_ref, o2_ref, l0_ref, l1_ref, l2_ref, yr_ref, gt_ref, wr_ref, wd_ref,
                  out_ref, oa_ref, *nat, dilations):
    tm = out_ref.shape[0]
    o_nat, l_nat = [], []
    for g, (o_ref, l_ref, r) in enumerate(zip((o0_ref, o1_ref, o2_ref), (l0_ref, l1_ref, l2_ref), dilations)):
        if r == 1:
            o_nat.append([o_ref[:, _head_cols(h)] for h in range(DIL_HPG)])
            l_nat.append(l_ref[...])
        else:
            on, ln = nat[2 * g], nat[2 * g + 1]
            for c in range(r):
                ln[pl.ds(c, tm // r, stride=r), :] = l_ref[c]
                for h in range(DIL_HPG):
                    on[h, pl.ds(c, tm // r, stride=r), :] = o_ref[c, :, _head_cols(h)]
            o_nat.append([on[h] for h in range(DIL_HPG)])
            l_nat.append(ln[...])
    m = jnp.maximum(jnp.maximum(l_nat[0], l_nat[1]), l_nat[2])
    es = [jnp.exp(l - m) for l in l_nat]
    den = es[0] + es[1] + es[2]
    for h in range(DIL_HPG):
        lane = h * LSE_LANES
        acc = None
        for g in range(DIL_GROUPS):
            term = (es[g][:, lane:lane + 1] / den[:, lane:lane + 1]) * o_nat[g][h]
            acc = term if acc is None else acc + term
        oa_ref[:, _head_cols(h)] = acc.astype(oa_ref.dtype)
    d = out_ref.shape[1]
    for c0 in range(0, d, PROJ_CHUNK):
        cols = slice(c0, c0 + PROJ_CHUNK)
        yr = jnp.dot(yr_ref[...], wr_ref[:, cols], preferred_element_type=F32)
        ya = jnp.dot(oa_ref[...], wd_ref[:, cols], preferred_element_type=F32)
        g_r = gt_ref[:, cols].astype(F32)
        g_a = gt_ref[:, d + c0:d + c0 + PROJ_CHUNK].astype(F32)
        out_ref[:, cols] = (g_r * yr + g_a * ya).astype(out_ref.dtype)


def _merge(o_g, lse_g, dilations, b, l, yr_in, gates, w_ret_b, w_dil_b, tm):
    n = yr_in.shape[0]
    d = w_ret_b.shape[1]
    bps = l // tm
    row = lambda w: pl.BlockSpec((tm, w), lambda i: (i, 0))
    const = lambda a: pl.BlockSpec(a.shape, lambda i: (0, 0))

    def group_specs(width):
        specs = []
        for r in dilations:
            if r == 1:
                specs.append(row(width))
            else:
                specs.append(pl.BlockSpec((None, r, tm // r, width), lambda i: (i // bps, 0, i % bps, 0)))
        return specs

    view = lambda t, r, width: t.reshape(n, width) if r == 1 else t.reshape(b, r, l // r, width)
    o_in = [view(t, r, DIL_OUT_WIDTH) for t, r in zip(o_g, dilations)]
    l_in = [view(t, r, HEAD_DIM) for t, r in zip(lse_g, dilations)]
    scratch = [pltpu.VMEM((tm, DIL_OUT_WIDTH), BF16)]
    for _ in dilations:
        scratch += [pltpu.VMEM((DIL_HPG, tm, HEAD_DIM), F32), pltpu.VMEM((tm, HEAD_DIM), F32)]
    return pl.pallas_call(
        functools.partial(_merge_kernel, dilations=tuple(dilations)),
        out_shape=jax.ShapeDtypeStruct((n, d), BF16),
        grid=(n // tm,),
        in_specs=group_specs(DIL_OUT_WIDTH) + group_specs(HEAD_DIM) + [row(RET_WIDTH), row(2 * d),
                                                                      const(w_ret_b), const(w_dil_b)],
        out_specs=row(d),
        scratch_shapes=scratch,
        compiler_params=_params(("parallel",), 48),
        name="merge_branches",
    )(*o_in, *l_in, yr_in, gates, w_ret_b, w_dil_b)


ROUTER_ROWS = 8 + N_EXPERTS


def _route_select(lt):
    lg = lt[0:N_GROUPS]
    gmax = jnp.max(lg, axis=0, keepdims=True)
    w_coarse = 1.0 / jnp.sum(jnp.exp(lg - gmax), axis=0, keepdims=True)
    gid = lax.broadcasted_iota(jnp.int32, lg.shape, 0)
    gsel = jnp.min(jnp.where(lg == gmax, gid, N_GROUPS), axis=0, keepdims=True)
    le = jnp.zeros((EXPERTS_PER_GROUP, lt.shape[1]), F32)
    for g in range(N_GROUPS):
        le = jnp.where(gsel == g, lt[8 + g * EXPERTS_PER_GROUP:8 + (g + 1) * EXPERTS_PER_GROUP], le)
    eid = lax.broadcasted_iota(jnp.int32, le.shape, 0)
    v1 = jnp.max(le, axis=0, keepdims=True)
    i1 = jnp.min(jnp.where(le == v1, eid, EXPERTS_PER_GROUP), axis=0, keepdims=True)
    le2 = jnp.where(eid == i1, -jnp.inf, le)
    v2 = jnp.max(le2, axis=0, keepdims=True)
    i2 = jnp.min(jnp.where(le2 == v2, eid, EXPERTS_PER_GROUP), axis=0, keepdims=True)
    e21 = jnp.exp(v2 - v1)
    w1 = w_coarse / (1.0 + e21)
    w2 = w_coarse * e21 / (1.0 + e21)
    e1 = gsel * EXPERTS_PER_GROUP + i1
    e2 = gsel * EXPERTS_PER_GROUP + i2
    xid = lax.broadcasted_iota(jnp.int32, (N_EXPERTS, lt.shape[1]), 0)
    oh1 = jnp.where(xid == e1, 1.0, 0.0)
    oh2 = jnp.where(xid == e2, 1.0, 0.0)
    return e1, e2, w1, w2, oh1, oh2


def _route_rank(sel, p1, p2, before):
    e1, e2, w1, w2, oh1, oh2 = sel
    n1 = jnp.sum(oh1, axis=1, keepdims=True)
    n2 = jnp.sum(oh2, axis=1, keepdims=True)
    rank1 = jnp.sum(oh1 * (p1 + before), axis=0, keepdims=True)
    rank2 = jnp.sum(oh2 * (p2 + before + n1), axis=0, keepdims=True)
    row = lax.broadcasted_iota(jnp.int32, (8, e1.shape[1]), 0)
    ei = jnp.where(row == 0, e1, jnp.where(row == 1, e2, jnp.where(
        row == 2, rank1.astype(jnp.int32), jnp.where(row == 3, rank2.astype(jnp.int32), 0))))
    wt = jnp.where(row == 0, w1, jnp.where(row == 1, w2, 0.0))
    return ei, wt, before + n1 + n2


def _outproj_router_kernel(mg_ref, x_ref, wo_ref, g2_ref, wr_ref, br_ref, tri_ref, cin_ref,
                           h_ref, u2_ref, ei_ref, wt_ref, cnt_ref, h_keep):
    i = pl.program_id(0)

    @pl.when(i == 0)
    def _():
        cnt_ref[...] = cin_ref[...]
        h_keep[...] = jnp.zeros_like(h_keep)

    sub = tri_ref.shape[0]
    subtiles = [slice(s0, s0 + sub) for s0 in range(0, h_ref.shape[0], sub)]

    def project(rows):
        h_ref[rows, :] = x_ref[rows, :] + jnp.dot(mg_ref[rows, :], wo_ref[...], preferred_element_type=F32)

    project(subtiles[0])
    logits = []
    for rows in subtiles:
        h = h_keep[rows, :]
        u2 = h * lax.rsqrt(jnp.mean(h * h, axis=-1, keepdims=True) + NORM_EPS) * g2_ref[...]
        u2_ref[rows, :] = u2
        logits.append(lax.dot_general(wr_ref[...], u2.astype(BF16), (((1,), (1,)), ((), ())),
                                      preferred_element_type=F32) + br_ref[...])
    for rows in subtiles[1:]:
        project(rows)
    selected = [_route_select(lt) for lt in logits]
    prefix = [(jnp.dot(sel[4].astype(BF16), tri_ref[...], preferred_element_type=F32),
               jnp.dot(sel[5].astype(BF16), tri_ref[...], preferred_element_type=F32)) for sel in selected]
    h_keep[...] = h_ref[...]
    counts = cnt_ref[...]
    for rows, sel, (p1, p2) in zip(subtiles, selected, prefix):
        ei, wt, routed = _route_rank(sel, p1, p2, counts)
        counts = jnp.where(i > 0, routed, counts)
        ei_ref[:, rows] = ei
        wt_ref[:, rows] = wt
    cnt_ref[...] = counts


def _outproj_router(merged, x, w_o_b, norm_ffn, wr_t, br_t, counts_in, tm):
    n, d = x.shape
    nsteps = n // tm
    sub = min(tm, ROUTER_SUB)
    tri = (jnp.arange(sub)[:, None] < jnp.arange(sub)[None, :]).astype(BF16)
    matmul_tile = lambda: pl.BlockSpec((tm, d), lambda i: (jnp.minimum(i, nsteps - 1), 0))
    routed_rows = lambda: pl.BlockSpec((tm, d), lambda i: (jnp.maximum(i - 1, 0), 0))
    routed_lanes = lambda: pl.BlockSpec((8, tm), lambda i: (0, jnp.maximum(i - 1, 0)))
    const = lambda a: pl.BlockSpec(a.shape, lambda i: (0, 0))
    return pl.pallas_call(
        _outproj_router_kernel,
        out_shape=(jax.ShapeDtypeStruct((n, d), F32), jax.ShapeDtypeStruct((n, d), F32),
                   jax.ShapeDtypeStruct((8, n), jnp.int32), jax.ShapeDtypeStruct((8, n), F32),
                   jax.ShapeDtypeStruct((N_EXPERTS, 1), F32)),
        grid=(nsteps + 1,),
        in_specs=[matmul_tile(), matmul_tile(), const(w_o_b), pl.BlockSpec((1, d), lambda i: (0, 0)),
                  const(wr_t), const(br_t), const(tri), const(counts_in)],
        out_specs=(matmul_tile(), routed_rows(), routed_lanes(), routed_lanes(), const(counts_in)),
        scratch_shapes=[pltpu.VMEM((tm, d), F32)],
        compiler_params=_params(("arbitrary",), 60),
        name="outproj_router",
    )(merged, x, w_o_b, norm_ffn.reshape(1, d), wr_t, br_t, tri, counts_in)


def _router_weights(w_rg, b_rg, w_re, b_re):
    d = w_rg.shape[0]
    wr = jnp.zeros((ROUTER_ROWS, d), F32)
    wr = wr.at[0:N_GROUPS].set(w_rg.T)
    wr = wr.at[8:].set(w_re.transpose(0, 2, 1).reshape(N_EXPERTS, d))
    br = jnp.zeros((ROUTER_ROWS, 1), F32)
    br = br.at[0:N_GROUPS, 0].set(b_rg)
    br = br.at[8:, 0].set(b_re.reshape(N_EXPERTS))
    return wr.astype(BF16), br


DMA_UNROLL = 8


def _dispatch_kernel(pos_p_ref, pos_s_ref, zero_ref, up_ref, us_ref, xs_ref, zbuf, stage, sem, *, tm):
    i = pl.program_id(0)
    last = pl.num_programs(0) - 1
    ntiles = xs_ref.shape[0] // MOE_TILE

    def tile_copy(t):
        return pltpu.make_async_copy(zbuf, xs_ref.at[pl.ds(pl.multiple_of(t * MOE_TILE, MOE_TILE), MOE_TILE)],
                                     sem.at[2])

    @pl.when(i == 0)
    def _():
        zbuf[...] = jnp.zeros_like(zbuf)
        for op in ("start", "wait"):
            def per_tile(t, c, op=op):
                @pl.when(zero_ref[t] != 0)
                def _():
                    getattr(tile_copy(t), op)()
                return c

            lax.fori_loop(0, ntiles, per_tile, 0)

    def scatter_rows(op, src_ref, pos_ref, base, nrows, ntok_total, row_sem):
        if op == "wait":
            for _ in range(TOP_K):
                pltpu.make_async_copy(src_ref, xs_ref.at[pl.ds(0, nrows)], row_sem).wait()
            return

        def body(r, c):
            for kk in range(TOP_K):
                slot = pos_ref[kk * ntok_total + base + r]
                pltpu.make_async_copy(src_ref.at[pl.ds(r, 1)], xs_ref.at[pl.ds(slot, 1)],
                                      row_sem).start(priority=kk % 2)
            return c

        lax.fori_loop(0, nrows, body, 0, unroll=DMA_UNROLL)

    prompt_rows = last * tm
    cur = i % 2

    @pl.when(i < last)
    def _():
        stage[cur] = up_ref[...]
        scatter_rows("start", stage.at[cur], pos_p_ref, i * tm, tm, prompt_rows, sem.at[cur])

    @pl.when(i > 0)
    def _():
        scatter_rows("wait", stage.at[1 - cur], pos_p_ref, (i - 1) * tm, tm, prompt_rows, sem.at[1 - cur])

    @pl.when(i == last)
    def _():
        for op in ("start", "wait"):
            scatter_rows(op, us_ref, pos_s_ref, 0, us_ref.shape[0], us_ref.shape[0], sem.at[2])


def _dispatch(pos_p, pos_s, zero_tiles, u2p, u2s, tm):
    n, d = u2p.shape
    nsteps = n // tm
    grid_spec = pltpu.PrefetchScalarGridSpec(
        num_scalar_prefetch=3,
        grid=(nsteps + 1,),
        in_specs=[pl.BlockSpec((tm, d), lambda i, *_: (jnp.minimum(i, nsteps - 1), 0)),
                  pl.BlockSpec(u2s.shape, lambda i, *_: (0, 0))],
        out_specs=pl.BlockSpec(memory_space=pl.ANY),
        scratch_shapes=[pltpu.VMEM((MOE_TILE, d), F32), pltpu.VMEM((2, tm, d), F32),
                        pltpu.SemaphoreType.DMA((3,))],
    )
    return pl.pallas_call(
        functools.partial(_dispatch_kernel, tm=tm),
        out_shape=jax.ShapeDtypeStruct((zero_tiles.shape[0] * MOE_TILE, d), F32),
        grid_spec=grid_spec,
        compiler_params=_params(("arbitrary",), 40),
        name="moe_dispatch",
    )(pos_p, pos_s, zero_tiles, u2p, u2s)


def _gmm_kernel(te_ref, nt_ref, x_ref, wg_ref, wu_ref, wd_ref, o_ref, wgb, wub, wdb):
    i = pl.program_id(0)
    fresh = jnp.logical_or(i == 0, te_ref[i] != te_ref[jnp.maximum(i - 1, 0)])

    @pl.when(jnp.logical_and(fresh, i < nt_ref[0]))
    def _():
        wgb[...] = wg_ref[...].astype(BF16)
        wub[...] = wu_ref[...].astype(BF16)
        wdb[...] = wd_ref[...].astype(BF16)

    @pl.when(i < nt_ref[0])
    def _():
        x = x_ref[...].astype(BF16)
        acc = None
        for c0 in range(0, wgb.shape[1], PROJ_CHUNK):
            cols = slice(c0, c0 + PROJ_CHUNK)
            a = jnp.dot(x, wgb[:, cols], preferred_element_type=F32)
            b = jnp.dot(x, wub[:, cols], preferred_element_type=F32)
            hm = (a * jax.nn.sigmoid(a) * b).astype(BF16)
            part = jnp.dot(hm, wdb[cols, :], preferred_element_type=F32)
            acc = part if acc is None else acc + part
        o_ref[...] = acc

    @pl.when(i >= nt_ref[0])
    def _():
        o_ref[...] = jnp.zeros_like(o_ref)


def _gmm(tile_expert, ntiles_used, xs, w_gate, w_up, w_down):
    npad, d = xs.shape
    f = w_gate.shape[-1]
    ntiles = npad // MOE_TILE
    epg = w_gate.shape[1]

    def xmap(i, te, nt):
        return (jnp.minimum(i, nt[0] - 1), 0)

    def wmap(i, te, nt):
        e = te[i]
        return (e // epg, e % epg, 0, 0)

    grid_spec = pltpu.PrefetchScalarGridSpec(
        num_scalar_prefetch=2,
        grid=(ntiles,),
        in_specs=[pl.BlockSpec((MOE_TILE, d), xmap),
                  pl.BlockSpec((None, None, d, f), wmap),
                  pl.BlockSpec((None, None, d, f), wmap),
                  pl.BlockSpec((None, None, f, d), wmap)],
        out_specs=pl.BlockSpec((MOE_TILE, d), lambda i, te, nt: (i, 0)),
        scratch_shapes=[pltpu.VMEM((d, f), BF16), pltpu.VMEM((d, f), BF16), pltpu.VMEM((f, d), BF16)],
    )
    return pl.pallas_call(
        _gmm_kernel,
        out_shape=jax.ShapeDtypeStruct((npad, d), F32),
        grid_spec=grid_spec,
        compiler_params=_params(("arbitrary",), 56),
        name="moe_grouped_matmul",
    )(tile_expert, ntiles_used, xs, w_gate, w_up, w_down)


def _combine_kernel(pos_ref, h_ref, wt_ref, os_ref, y_ref, g0, g1, sem, *, tm, ntok_total):
    i = pl.program_id(0)
    nsteps = pl.num_programs(0)
    cur = i % 2

    def row_copy(step, r, kk, slot):
        src = pos_ref[kk * ntok_total + step * tm + r]
        buf = (g0, g1)[kk]
        return pltpu.make_async_copy(os_ref.at[pl.ds(src, 1)], buf.at[slot, pl.ds(r, 1)], sem.at[slot])

    def issue_step(step, slot):
        def body(r, c):
            row_copy(step, r, 0, slot).start(priority=0)
            row_copy(step, r, 1, slot).start(priority=1)
            return c

        lax.fori_loop(0, tm, body, 0, unroll=DMA_UNROLL)

    @pl.when(i == 0)
    def _():
        issue_step(0, 0)

    @pl.when(i + 1 < nsteps)
    def _():
        issue_step(i + 1, 1 - cur)

    for buf in (g0, g1):
        pltpu.make_async_copy(os_ref.at[pl.ds(0, tm)], buf.at[cur], sem.at[cur]).wait()
    y_ref[...] = h_ref[...] + wt_ref[:, 0:1] * g0[cur] + wt_ref[:, 1:2] * g1[cur]


def _combine(pos_flat, h, wt_cols, out_sorted, tm):
    n, d = h.shape
    grid_spec = pltpu.PrefetchScalarGridSpec(
        num_scalar_prefetch=1,
        grid=(n // tm,),
        in_specs=[pl.BlockSpec((tm, d), lambda i, pos: (i, 0)),
                  pl.BlockSpec((tm, 8), lambda i, pos: (i, 0)),
                  pl.BlockSpec(memory_space=pl.ANY)],
        out_specs=pl.BlockSpec((tm, d), lambda i, pos: (i, 0)),
        scratch_shapes=[pltpu.VMEM((2, tm, d), F32), pltpu.VMEM((2, tm, d), F32),
                        pltpu.SemaphoreType.DMA((2,))],
    )
    return pl.pallas_call(
        functools.partial(_combine_kernel, tm=tm, ntok_total=n),
        out_shape=jax.ShapeDtypeStruct((n, d), F32),
        grid_spec=grid_spec,
        compiler_params=_params(("arbitrary",), 48),
        name="moe_combine",
    )(pos_flat, h, wt_cols, out_sorted)


def _moe_plan(counts, route_sets):
    npairs = sum(e.shape[1] for e, _ in route_sets) * TOP_K
    tiles = (counts + MOE_TILE - 1) // MOE_TILE
    tile_end = jnp.cumsum(tiles)
    offs = (tile_end - tiles) * MOE_TILE
    ids = jnp.arange(N_EXPERTS)
    slots = []
    for experts, ranks in route_sets:
        base = jnp.sum(jnp.where(experts[..., None] == ids, offs, 0), axis=-1)
        slots.append((base + ranks).reshape(-1).astype(jnp.int32))
    ntiles = (npairs + N_EXPERTS * (MOE_TILE - 1)) // MOE_TILE
    tile_expert = jnp.sum(tile_end[None, :] <= jnp.arange(ntiles)[:, None], axis=1)
    tile_expert = jnp.minimum(tile_expert, N_EXPERTS - 1).astype(jnp.int32)
    tile_ids = jnp.arange(ntiles)
    is_last = jnp.any((tile_ids[:, None] == tile_end[None, :] - 1) & (tiles[None, :] > 0), axis=1)
    zero_tiles = (is_last | (tile_ids >= tile_end[-1])).astype(jnp.int32)
    return slots, tile_expert, tile_end[-1:].astype(jnp.int32), zero_tiles


def _mixers_prompt(x, wts, counts_in, shift_jobs):
    b, l, d = x.shape
    n = b * l
    tm = ROW_TILE
    xf = x.reshape(n, d)
    u = _rmsnorm(xf, wts["norm_attn"], tm)
    tab = _rotary_tables(jnp.arange(l), l)
    (qk_r, vg_r, gates), shifted = _proj_retention_and_gates(u, wts["w_in"], tab, 2 * tm, shift_jobs)

    s0 = jnp.zeros((b, RET_HEADS, HEAD_DIM, HEAD_DIM), F32)
    yr_in, s_fin = _retention_prompt(qk_r.reshape(b, l, -1), vg_r.reshape(b, l, -1), s0, 256, min(l, 2048))

    o_g, lse_g, bufs = [], [], []
    for gi, (w, r) in enumerate(DIL_PAIRS):
        qg, kg, vg, kt, vt = _proj_dilated_prompt(u, wts["w_in"], wts["q_norm"], wts["k_norm"], gi, b, l, tm)
        bias = _dilated_bias(wts["rel_bias"][:, gi * DIL_HPG:(gi + 1) * DIL_HPG], r)
        o, lse = _dilated_prompt(qg, kg, vg, bias, min(8, l // r // DIL_BLOCK))
        o_g.append(o)
        lse_g.append(lse)
        bufs.append(jnp.stack([kt, vt], axis=2).reshape(b, kt.shape[1], 2, DIL_HPG, HEAD_DIM))

    merged = _merge(o_g, lse_g, [r for _, r in DIL_PAIRS], b, l, yr_in.reshape(n, RET_WIDTH), gates,
                    wts["w_ret_out"], wts["w_dil_out"], tm)
    h, u2, ei, wt, counts = _outproj_router(merged, xf, wts["w_o"], wts["norm_ffn"], wts["wr_t"], wts["br_t"],
                                            counts_in, ROUTER_TILE)
    return h, u2, ei, wt, counts, s_fin, bufs, shifted


def _mixers_sample(x, caches, state, wts, counts_in):
    bd, t, d = x.shape
    n = bd * t
    xf = x.reshape(n, d)
    u = _rmsnorm(xf, wts["norm_attn"], n)
    tab = _rotary_tables(PAST_LEN + jnp.arange(t), n)
    (qk_r, vg_r, gates), _ = _proj_retention_and_gates(u, wts["w_in"], tab, n)
    qkv_a = _proj_dilated_sample(u, wts["w_in"], wts["q_norm"], wts["k_norm"])

    def heads(a):
        a = a.astype(F32).reshape(bd, t, RET_HEADS, HEAD_DIM).transpose(0, 2, 1, 3)
        return jnp.pad(a, ((0, 0), (0, 0), (0, 8 - t), (0, 0)))

    y_r, s_new = _retention_sample(heads(qk_r[:, :RET_WIDTH]), heads(qk_r[:, RET_WIDTH:]),
                                   heads(vg_r[:, :RET_WIDTH]), heads(vg_r[:, RET_WIDTH:]), state, t)
    yr_in = y_r[:, :, :t].transpose(0, 2, 1, 3).reshape(n, RET_WIDTH).astype(BF16)

    grp = lambda a: a.reshape(bd, t, DIL_GROUPS, DIL_HPG, HEAD_DIM)
    qa, ka, va = (grp(qkv_a[:, s * DIL_WIDTH:(s + 1) * DIL_WIDTH]) for s in range(3))
    cviews = [c.reshape(bd, DIL_BLOCK, r, 2, DIL_HPG, HEAD_DIM) for c, (_, r) in zip(caches, DIL_PAIRS)]
    bias_c, bias_n = _dilated_sample_bias(wts["rel_bias"], t)
    o, lse = _dilated_sample(qa, ka, va, cviews, bias_c, bias_n, t)
    o_g = [o[:, :, gi].reshape(n, 1, DIL_OUT_WIDTH) for gi in range(DIL_GROUPS)]
    lse_g = [jnp.repeat(lse[:, :, gi].reshape(n, DIL_HPG), LSE_LANES, axis=1).reshape(n, 1, HEAD_DIM)
             for gi in range(DIL_GROUPS)]

    news = [jnp.stack([ka[:, :, gi], va[:, :, gi]], axis=2) for gi in range(DIL_GROUPS)]
    shift_jobs = list(zip(caches, news))

    merged = _merge(o_g, lse_g, [1] * DIL_GROUPS, 1, n, yr_in, gates, wts["w_ret_out"], wts["w_dil_out"], n)
    h, u2, ei, wt, counts = _outproj_router(merged, xf, wts["w_o"], wts["norm_ffn"], wts["wr_t"], wts["br_t"],
                                            counts_in, n)
    return h, u2, ei, wt, counts, s_new, shift_jobs


def kernel(x_prompt, x_sample, cache_kv_g0, cache_kv_g1, cache_kv_g2, state_ret, norm_attn, w_in, q_norm,
           k_norm, rel_bias, w_ret_out, w_dil_out, w_o, norm_ffn, w_router_group, b_router_group,
           w_router_expert, b_router_expert, w_gate, w_up, w_down):
    caches = (cache_kv_g0, cache_kv_g1, cache_kv_g2)
    ntok = x_sample.shape[1]
    for c, (w, r) in zip(caches, DIL_PAIRS):
        assert c.shape[1] == w == DIL_BLOCK * r and (r == 1 or ntok <= r) and ntok <= 8
    wr_t, br_t = _router_weights(w_router_group, b_router_group, w_router_expert, b_router_expert)
    wts = dict(norm_attn=norm_attn, w_in=w_in, q_norm=q_norm, k_norm=k_norm, rel_bias=rel_bias,
               w_ret_out=w_ret_out.astype(BF16), w_dil_out=w_dil_out.astype(BF16), w_o=w_o.astype(BF16),
               norm_ffn=norm_ffn, wr_t=wr_t, br_t=br_t)

    zero_counts = jnp.zeros((N_EXPERTS, 1), F32)
    hs, u2s, eis, wts_s, counts_s, s_s, shift_jobs = _mixers_sample(x_sample, caches, state_ret, wts, zero_counts)
    hp, u2p, eip, wtp, counts, s_p, bufs_p, bufs_s = _mixers_prompt(x_prompt, wts, counts_s, shift_jobs)

    slots, tile_expert, ntiles_used, zero_tiles = _moe_plan(
        counts[:, 0].astype(jnp.int32), [(eip[0:2], eip[2:4]), (eis[0:2], eis[2:4])])
    xs = _dispatch(slots[0], slots[1], zero_tiles, u2p, u2s, ROW_TILE)
    out_sorted = _gmm(tile_expert, ntiles_used, xs, w_gate, w_up, w_down)
    yp = _combine(slots[0], hp, wtp.T, out_sorted, ROW_TILE)
    ys = _combine(slots[1], hs, wts_s.T, out_sorted, hs.shape[0])

    return (yp.reshape(x_prompt.shape), ys.reshape(x_sample.shape), bufs_p[0], bufs_p[1], bufs_p[2], s_p,
            bufs_s[0], bufs_s[1], bufs_s[2], s_s)
```

```python
import functools
import math

import jax
import jax.numpy as jnp
from jax import lax
from jax.experimental import pallas as pl
from jax.experimental.pallas import tpu as pltpu

HEAD_DIM = 128
RET_HEADS = 8
RET_WIDTH = RET_HEADS * HEAD_DIM
ROPE_BASE = 10000.0
GN_EPS = 1e-5
DIL_PAIRS = ((128, 1), (512, 4), (2048, 16))
DIL_GROUPS = len(DIL_PAIRS)
DIL_HPG = 4
DIL_HEADS = DIL_HPG * DIL_GROUPS
DIL_WIDTH = DIL_HEADS * HEAD_DIM
DIL_OUT_WIDTH = DIL_HPG * HEAD_DIM
DIL_BLOCK = 128
LSE_LANES = HEAD_DIM // DIL_HPG
ATTN_SCALE = HEAD_DIM ** -0.5
REL_BUCKETS = 32
REL_MAX_DIST = 2048
N_GROUPS = 4
EXPERTS_PER_GROUP = 8
N_EXPERTS = N_GROUPS * EXPERTS_PER_GROUP
TOP_K = 2
NORM_EPS = 1e-6
PAST_LEN = 16384

COL_QR = 0
COL_KR = COL_QR + RET_WIDTH
COL_VR = COL_KR + RET_WIDTH
COL_GR = COL_VR + RET_WIDTH
COL_QA = COL_GR + RET_WIDTH
COL_KA = COL_QA + DIL_WIDTH
COL_VA = COL_KA + DIL_WIDTH
COL_GATES = COL_VA + DIL_WIDTH

MOE_TILE = 512
ROW_TILE = 512
PROJ_CHUNK = 256
ROUTER_TILE = 512
ROUTER_SUB = 256
SAMPLE_SEQS_PER_STEP = 1
MIB = 1 << 20
BF16 = jnp.bfloat16
F32 = jnp.float32


def _params(semantics, vmem_mib):
    return pltpu.CompilerParams(dimension_semantics=semantics, vmem_limit_bytes=vmem_mib * MIB)


def _head_cols(h):
    return slice(h * HEAD_DIM, (h + 1) * HEAD_DIM)


def _rmsnorm_kernel(x_ref, g_ref, o_ref):
    x = x_ref[...]
    y = x * lax.rsqrt(jnp.mean(x * x, axis=-1, keepdims=True) + NORM_EPS)
    o_ref[...] = (y * g_ref[...]).astype(o_ref.dtype)


def _rmsnorm(x, g, tm):
    n, d = x.shape
    return pl.pallas_call(
        _rmsnorm_kernel,
        out_shape=jax.ShapeDtypeStruct((n, d), BF16),
        grid=(n // tm,),
        in_specs=[pl.BlockSpec((tm, d), lambda i: (i, 0)), pl.BlockSpec((1, d), lambda i: (0, 0))],
        out_specs=pl.BlockSpec((tm, d), lambda i: (i, 0)),
        compiler_params=_params(("parallel",), 40),
        name="rmsnorm",
    )(x, g.reshape(1, d))


def _background_shift_step(step, nsteps, src_ref, new_ref, dst_ref, ring, sems, *, chunks_per_seq, rows):
    shift = new_ref.shape[1]

    def chunk_copy(s, slot, inbound):
        b = s // chunks_per_seq
        r0 = (s % chunks_per_seq) * rows
        if inbound:
            return pltpu.make_async_copy(src_ref.at[b, pl.ds(pl.multiple_of(r0 + shift, 8), rows)],
                                         ring.at[slot], sems.at[0, slot])
        return pltpu.make_async_copy(ring.at[slot], dst_ref.at[b, pl.ds(pl.multiple_of(r0, 8), rows)],
                                     sems.at[1, slot])

    cur = step % 2

    @pl.when(step >= 2)
    def _():
        chunk_copy(step - 2, cur, False).wait()

    chunk_copy(step, cur, True).start()

    @pl.when(step >= 1)
    def _():
        chunk_copy(step - 1, 1 - cur, True).wait()
        chunk_copy(step - 1, 1 - cur, False).start()

    @pl.when(step == nsteps - 1)
    def _():
        tail = pltpu.make_async_copy(new_ref, dst_ref.at[:, pl.ds(dst_ref.shape[1] - shift, shift)], sems.at[2, 0])
        tail.start()
        chunk_copy(step, cur, True).wait()
        chunk_copy(step, cur, False).start()
        chunk_copy(step - 1, 1 - cur, False).wait()
        chunk_copy(step, cur, False).wait()
        tail.wait()


def _proj_kernel(u_ref, *refs, epilogue, n_w, n_extra, n_out, shift):
    w_refs, refs = refs[:n_w], refs[n_w:]
    extras, refs = refs[:n_extra], refs[n_extra:]
    if shift is not None:
        (src_ref, new_ref), refs = refs[:2], refs[2:]
    outs, refs = refs[:n_out], refs[n_out:]
    if shift is not None:
        dst_ref, (ring, sems), refs = refs[0], refs[-2:], refs[1:-2]
        nrow_blocks = pl.num_programs(1)
        _background_shift_step(pl.program_id(0) * nrow_blocks + pl.program_id(1), pl.num_programs(0) * nrow_blocks,
                               src_ref, new_ref, dst_ref, ring, sems, **shift)
    scratch, wb_refs = refs[:len(refs) - n_w], refs[len(refs) - n_w:]

    @pl.when(pl.program_id(1) == 0)
    def _():
        for w_ref, wb_ref in zip(w_refs, wb_refs):
            wb_ref[...] = w_ref[...].astype(BF16)

    tn = wb_refs[0].shape[1]
    for wi, wb_ref in enumerate(wb_refs):
        for c0 in range(0, tn, PROJ_CHUNK):
            acc = jnp.dot(u_ref[...], wb_ref[:, c0:c0 + PROJ_CHUNK], preferred_element_type=F32)
            epilogue(wi, c0, acc, *extras, *outs, *scratch)


def _proj(u, w_in, col_offsets, ncol_blocks, tn, tm, epilogue, extra, extra_specs, out_shape, out_specs,
          scratch, name, shift_job=None, vmem_mib=58):
    n, k = u.shape
    w_specs = [pl.BlockSpec((pl.Element(k), pl.Element(tn)),
                            functools.partial(lambda j, i, o: (0, pl.multiple_of(o + j * tn, HEAD_DIM)), o=o))
               for o in col_offsets]
    multi = isinstance(out_shape, (tuple, list))
    out_shapes = list(out_shape) if multi else [out_shape]
    out_spec_list = list(out_specs) if multi else [out_specs]
    n_out = len(out_shapes)
    side_in, side_in_specs, side_scratch, shift = [], [], [], None
    if shift_job is not None:
        cache, new = shift_job
        shape = cache.shape
        bd = shape[0]
        cache = cache.reshape(bd, -1, HEAD_DIM)
        new = new.reshape(bd, -1, HEAD_DIM)
        nsteps = ncol_blocks * (n // tm)
        chunks_per_seq = nsteps // bd
        rows = (cache.shape[1] - new.shape[1]) // chunks_per_seq
        assert chunks_per_seq * bd == nsteps and rows % 8 == 0
        assert rows * chunks_per_seq == cache.shape[1] - new.shape[1]
        shift = dict(chunks_per_seq=chunks_per_seq, rows=rows)
        any_spec = pl.BlockSpec(memory_space=pl.ANY)
        side_in, side_in_specs = [cache, new], [any_spec, pl.BlockSpec(new.shape, lambda j, i: (0, 0, 0))]
        out_shapes.append(jax.ShapeDtypeStruct(cache.shape, cache.dtype))
        out_spec_list.append(any_spec)
        side_scratch = [pltpu.VMEM((2, rows, HEAD_DIM), F32), pltpu.SemaphoreType.DMA((3, 2))]
    outs = pl.pallas_call(
        functools.partial(_proj_kernel, epilogue=epilogue, n_w=len(col_offsets), n_extra=len(extra), n_out=n_out,
                          shift=shift),
        out_shape=tuple(out_shapes) if (multi or shift) else out_shapes[0],
        grid=(ncol_blocks, n // tm),
        in_specs=[pl.BlockSpec((tm, k), lambda j, i: (i, 0))] + w_specs + list(extra_specs) + side_in_specs,
        out_specs=tuple(out_spec_list) if (multi or shift) else out_spec_list[0],
        scratch_shapes=list(scratch) + [pltpu.VMEM((k, tn), BF16) for _ in col_offsets] + side_scratch,
        compiler_params=_params(("arbitrary", "arbitrary"), vmem_mib),
        name=name,
    )(u, *([w_in] * len(col_offsets)), *extra, *side_in)
    if shift is None:
        return outs
    shifted = outs[-1].reshape(shape)
    return (tuple(outs[:-1]) if multi else outs[0]), shifted


def _chunk_heads(c0, acc):
    return [(slice(c0 + h * HEAD_DIM, c0 + (h + 1) * HEAD_DIM), acc[:, _head_cols(h)])
            for h in range(acc.shape[1] // HEAD_DIM)]


def _epi_rotary(wi, c0, acc, tab_ref, o_ref):
    c = tab_ref[0]
    s = tab_ref[1]
    for cols, xh in _chunk_heads(c0, acc):
        o_ref[:, cols] = (xh * c + pltpu.roll(xh, HEAD_DIM // 2, 1) * s).astype(o_ref.dtype)


def _epi_value_gate(wi, c0, acc, o_ref):
    is_gate = pl.program_id(0) == 1
    o_ref[:, c0:c0 + acc.shape[1]] = jnp.where(is_gate, acc * jax.nn.sigmoid(acc), acc).astype(o_ref.dtype)


def _epi_sigmoid(wi, c0, acc, o_ref):
    o_ref[:, c0:c0 + acc.shape[1]] = jax.nn.sigmoid(acc).astype(o_ref.dtype)


def _head_rms(xh, g):
    return xh * lax.rsqrt(jnp.mean(xh * xh, axis=-1, keepdims=True) + NORM_EPS) * g


def _epi_dilated_prompt(wi, c0, acc, qn_ref, kn_ref, q_ref, k_ref, v_ref, kt_ref, vt_ref, scr, *, r):
    tm = scr.shape[2]
    tail_rows = kt_ref.shape[0]
    gain = (qn_ref[...] * ATTN_SCALE, kn_ref[...], None)[wi]
    out_ref = (q_ref, k_ref, v_ref)[wi]
    tail_ref = (None, kt_ref, vt_ref)[wi]
    for cols, xh in _chunk_heads(c0, acc):
        h = cols.start // HEAD_DIM
        y = xh if gain is None else _head_rms(xh, gain)
        slab = scr.at[wi, h]
        slab[...] = y
        if tail_ref is not None:
            tail_ref[:, cols] = y[tm - tail_rows:tm, :]
        if r == 1:
            out_ref[:, cols] = y.astype(out_ref.dtype)
        else:
            for c in range(r):
                out_ref[c, :, cols] = slab[pl.ds(c, tm // r, stride=r), :].astype(out_ref.dtype)


def _epi_dilated_sample(wi, c0, acc, qn_ref, kn_ref, o_ref):
    j = pl.program_id(0)
    gain = jnp.where(j < DIL_GROUPS, qn_ref[...] * ATTN_SCALE, kn_ref[...])
    for cols, xh in _chunk_heads(c0, acc):
        o_ref[:, cols] = jnp.where(j < 2 * DIL_GROUPS, _head_rms(xh, gain), xh)


def _rotary_tables(pos, rows):
    half = HEAD_DIM // 2
    inv = ROPE_BASE ** (-jnp.arange(half, dtype=F32) / half)
    ang = pos.astype(F32)[:, None] * inv[None, :]
    cos = jnp.cos(ang)
    sin = jnp.sin(ang)
    c = jnp.concatenate([cos, cos], axis=-1)
    s = jnp.concatenate([-sin, sin], axis=-1)
    tab = jnp.stack([jnp.stack([c, s]), jnp.stack([c, s]) * (HEAD_DIM ** -0.5)])
    reps = rows // pos.shape[0]
    return jnp.tile(tab, (1, 1, reps, 1))


def _proj_retention_and_gates(u, w_in, pos_tab, tm, shift_jobs=(None, None, None)):
    n = u.shape[0]
    nblk = pos_tab.shape[2] // tm
    tile_out = lambda: pl.BlockSpec((tm, RET_WIDTH), lambda j, i: (i, j))
    ngate = w_in.shape[1] - COL_GATES
    qk_r = _proj(u, w_in, [COL_QR], 2, RET_WIDTH, tm, _epi_rotary, [pos_tab],
                 [pl.BlockSpec((None, 2, tm, HEAD_DIM), lambda j, i: (j, 0, i % nblk, 0))],
                 jax.ShapeDtypeStruct((n, 2 * RET_WIDTH), BF16), tile_out(), [], "proj_qk_ret", shift_jobs[0])
    vg_r = _proj(u, w_in, [COL_VR], 2, RET_WIDTH, tm, _epi_value_gate, [], [],
                 jax.ShapeDtypeStruct((n, 2 * RET_WIDTH), BF16), tile_out(), [], "proj_vg_ret", shift_jobs[1])
    gates = _proj(u, w_in, [COL_GATES], ngate // RET_WIDTH, RET_WIDTH, tm, _epi_sigmoid, [], [],
                  jax.ShapeDtypeStruct((n, ngate), BF16), tile_out(), [], "proj_gates", shift_jobs[2])
    results = (qk_r, vg_r, gates)
    if any(job is not None for job in shift_jobs):
        return tuple(r[0] for r in results), [r[1] for r in results]
    return results, []


def _proj_dilated_prompt(u, w_in, q_norm, k_norm, gi, b, l, tm):
    w, r = DIL_PAIRS[gi]
    lw = min(w, l)
    bps = l // tm
    tail_rows = min(lw, tm)
    tail_blocks = lw // tail_rows
    cm = (b, r, l // r, DIL_OUT_WIDTH)
    if r == 1:
        cm_shape = jax.ShapeDtypeStruct((b * l, DIL_OUT_WIDTH), BF16)
        cm_spec = lambda: pl.BlockSpec((tm, DIL_OUT_WIDTH), lambda j, i: (i, 0))
    else:
        cm_shape = jax.ShapeDtypeStruct(cm, BF16)
        cm_spec = lambda: pl.BlockSpec((None, r, tm // r, DIL_OUT_WIDTH), lambda j, i: (i // bps, 0, i % bps, 0))
    tail_shape = jax.ShapeDtypeStruct((b, lw, DIL_OUT_WIDTH), F32)
    tail_spec = lambda: pl.BlockSpec(
        (None, tail_rows, DIL_OUT_WIDTH),
        lambda j, i: (i // bps, jnp.maximum(i % bps - (bps - tail_blocks), 0), 0))
    gspec = pl.BlockSpec((1, HEAD_DIM), lambda j, i: (0, 0))
    off = gi * DIL_OUT_WIDTH
    q, k, v, kt, vt = _proj(
        u, w_in, [COL_QA + off, COL_KA + off, COL_VA + off], 1, DIL_OUT_WIDTH, tm,
        functools.partial(_epi_dilated_prompt, r=r),
        [q_norm.reshape(1, HEAD_DIM), k_norm.reshape(1, HEAD_DIM)], [gspec, gspec],
        (cm_shape, cm_shape, cm_shape, tail_shape, tail_shape),
        (cm_spec(), cm_spec(), cm_spec(), tail_spec(), tail_spec()),
        [pltpu.VMEM((3, DIL_HPG, tm, HEAD_DIM), F32)], "proj_dilated_prompt")
    cls = lambda t: t.reshape(b * r, l // r, DIL_OUT_WIDTH)
    return cls(q), cls(k), cls(v), kt, vt


def _proj_dilated_sample(u, w_in, q_norm, k_norm):
    n = u.shape[0]
    gspec = pl.BlockSpec((1, HEAD_DIM), lambda j, i: (0, 0))
    return _proj(u, w_in, [COL_QA], 3 * DIL_GROUPS, DIL_OUT_WIDTH, n, _epi_dilated_sample,
                 [q_norm.reshape(1, HEAD_DIM), k_norm.reshape(1, HEAD_DIM)], [gspec, gspec],
                 jax.ShapeDtypeStruct((n, 3 * DIL_WIDTH), F32),
                 pl.BlockSpec((n, DIL_OUT_WIDTH), lambda j, i: (i, j)), [], "proj_dilated_sample")


def _retention_kernel(q_ref, k_ref, v_ref, g_ref, s0_ref, dm_ref, qd_ref, kd_ref, cd_ref,
                      y_ref, sf_ref, state, *, chunk, nchunks):
    t = pl.program_id(2)

    @pl.when(t == 0)
    def _():
        state[...] = s0_ref[...]

    dmask = dm_ref[...]
    qdec = qd_ref[...]
    kdec = kd_ref[...]
    cdec = cd_ref[...]
    s_prev = state[...]
    for ci in range(nchunks):
        rows = pl.ds(ci * chunk, chunk)
        q = q_ref[rows, :]
        k = k_ref[rows, :]
        v = v_ref[rows, :]
        sc = lax.dot_general(q, k, (((1,), (1,)), ((), ())), preferred_element_type=F32) * dmask
        intra = jnp.dot(sc.astype(BF16), v, preferred_element_type=F32)
        qd = (q.astype(F32) * qdec).astype(BF16)
        cross = jnp.dot(qd, s_prev.astype(BF16), preferred_element_type=F32)
        kd = (k.astype(F32) * kdec).astype(BF16)
        kv = lax.dot_general(kd, v, (((0,), (0,)), ((), ())), preferred_element_type=F32)
        s_prev = s_prev * cdec + kv
        o = intra + cross
        mu = jnp.mean(o, axis=-1, keepdims=True)
        oc = o - mu
        var = jnp.mean(oc * oc, axis=-1, keepdims=True)
        y = g_ref[rows, :].astype(F32) * (oc * lax.rsqrt(var + GN_EPS))
        y_ref[rows, :] = y.astype(y_ref.dtype)
    state[...] = s_prev

    @pl.when(t == pl.num_programs(2) - 1)
    def _():
        sf_ref[...] = state[...]


def _retention_decay(chunk, valid):
    lg = jnp.log1p(-jnp.exp2(-5.0 - jnp.arange(RET_HEADS, dtype=F32)))
    idx = jnp.arange(chunk, dtype=F32)
    rel = idx[:, None] - idx[None, :]
    dmask = jnp.where(rel[None] >= 0, jnp.exp(lg[:, None, None] * jnp.maximum(rel, 0.0)[None]), 0.0)
    qdec = jnp.exp(lg[:, None] * (idx + 1.0)[None, :])[..., None]
    kdec = jnp.exp(lg[:, None] * (valid - 1.0 - idx)[None, :])[..., None]
    kdec = jnp.where((idx < valid)[None, :, None], kdec, 0.0)
    cdec = jnp.exp(lg * valid)[:, None, None]
    return dmask, qdec, kdec, cdec


def _retention_prompt(qk_r, vg_r, s0, chunk, rows_per_step):
    b, l, _ = qk_r.shape
    h = RET_HEADS
    dmask, qdec, kdec, cdec = _retention_decay(chunk, chunk)
    nsteps = l // rows_per_step
    blk = lambda off: pl.BlockSpec((None, rows_per_step, HEAD_DIM), lambda bi, hi, ti: (bi, ti, hi + off))
    per_head = lambda shape: pl.BlockSpec((None,) + shape, lambda bi, hi, ti: (hi,) + (0,) * len(shape))
    state_spec = pl.BlockSpec((None, None, HEAD_DIM, HEAD_DIM), lambda bi, hi, ti: (bi, hi, 0, 0))
    return pl.pallas_call(
        functools.partial(_retention_kernel, chunk=chunk, nchunks=rows_per_step // chunk),
        out_shape=(jax.ShapeDtypeStruct((b, l, RET_WIDTH), BF16),
                   jax.ShapeDtypeStruct((b, h, HEAD_DIM, HEAD_DIM), F32)),
        grid=(b, h, nsteps),
        in_specs=[blk(0), blk(h), blk(0), blk(h), state_spec,
                  per_head((chunk, chunk)), per_head((chunk, 1)), per_head((chunk, 1)), per_head((1, 1))],
        out_specs=(blk(0), state_spec),
        scratch_shapes=[pltpu.VMEM((HEAD_DIM, HEAD_DIM), F32)],
        compiler_params=_params(("parallel", "parallel", "arbitrary"), 32),
        name="retention_prompt",
    )(qk_r, qk_r, vg_r, vg_r, s0, dmask, qdec, kdec, cdec)


def _retention_sample_kernel(q_ref, k_ref, v_ref, g_ref, s0_ref, dm_ref, qd_ref, kd_ref, cd_ref,
                             y_ref, sf_ref, *, ntok):
    for bi in range(q_ref.shape[0]):
        for h in range(RET_HEADS):
            q = q_ref[bi, h]
            k = k_ref[bi, h]
            v = v_ref[bi, h]
            s_prev = s0_ref[bi, h]
            dmask = dm_ref[h]
            o = jnp.dot(q * qd_ref[h], s_prev, preferred_element_type=F32)
            for j in range(ntok):
                sj = jnp.sum(q * k[j:j + 1, :], axis=-1, keepdims=True) * dmask[:, j:j + 1]
                o = o + sj * v[j:j + 1, :]
            kd = k * kd_ref[h]
            kv = lax.dot_general(kd, v, (((0,), (0,)), ((), ())), preferred_element_type=F32)
            sf_ref[bi, h] = s_prev * cd_ref[h] + kv
            mu = jnp.mean(o, axis=-1, keepdims=True)
            oc = o - mu
            var = jnp.mean(oc * oc, axis=-1, keepdims=True)
            y_ref[bi, h] = g_ref[bi, h] * (oc * lax.rsqrt(var + GN_EPS))


def _retention_sample(q, k, v, g, s0, ntok):
    bd, h, tp, _ = q.shape
    dmask, qdec, kdec, cdec = _retention_decay(tp, ntok)
    nb = math.gcd(bd, SAMPLE_SEQS_PER_STEP)
    tok = pl.BlockSpec((nb, h, tp, HEAD_DIM), lambda bi: (bi, 0, 0, 0))
    st = pl.BlockSpec((nb, h, HEAD_DIM, HEAD_DIM), lambda bi: (bi, 0, 0, 0))
    const = lambda a: pl.BlockSpec(a.shape, lambda bi: (0,) * a.ndim)
    return pl.pallas_call(
        functools.partial(_retention_sample_kernel, ntok=ntok),
        out_shape=(jax.ShapeDtypeStruct((bd, h, tp, HEAD_DIM), F32),
                   jax.ShapeDtypeStruct((bd, h, HEAD_DIM, HEAD_DIM), F32)),
        grid=(bd // nb,),
        in_specs=[tok, tok, tok, tok, st, const(dmask), const(qdec), const(kdec), const(cdec)],
        out_specs=(tok, st),
        compiler_params=_params(("parallel",), 32),
        name="retention_sample",
    )(q, k, v, g, s0, dmask, qdec, kdec, cdec)


def _t5_bucket(dist):
    max_exact = REL_BUCKETS // 2
    d = jnp.maximum(dist, 0)
    df = jnp.maximum(d, 1).astype(F32)
    large = max_exact + (jnp.log(df / max_exact) / math.log(REL_MAX_DIST / max_exact)
                         * (REL_BUCKETS - max_exact)).astype(jnp.int32)
    large = jnp.minimum(large, REL_BUCKETS - 1)
    return jnp.where(d < max_exact, d, large)


def _bias_lookup(tab, dist):
    onehot = _t5_bucket(dist)[..., None] == jnp.arange(REL_BUCKETS)
    return jnp.sum(jnp.where(onehot[..., None], tab.astype(F32), 0.0), axis=-2)


def _dilated_kernel(q_ref, kp_ref, kc_ref, vp_ref, vc_ref, bias_ref, o_ref, lse_ref, kfull, vfull, *, nsub):
    i = pl.program_id(1)
    blk = DIL_BLOCK
    kfull[0:blk, :] = kp_ref[...]
    kfull[blk:, :] = kc_ref[...]
    vfull[0:blk, :] = vp_ref[...]
    vfull[blk:, :] = vc_ref[...]
    col = lax.broadcasted_iota(jnp.int32, (blk, 2 * blk), 1)
    lane_head = lax.broadcasted_iota(jnp.int32, (blk, HEAD_DIM), 1) // LSE_LANES
    for s in range(nsub):
        rows = pl.ds(s * blk, blk)
        win = pl.ds(s * blk, 2 * blk)
        lse_tile = jnp.zeros((blk, HEAD_DIM), F32)
        for h in range(DIL_HPG):
            cols = pl.ds(h * HEAD_DIM, HEAD_DIM)
            q = q_ref[rows, cols]
            kw = kfull[win, cols]
            vw = vfull[win, cols]
            sc = lax.dot_general(q, kw, (((1,), (1,)), ((), ())), preferred_element_type=F32) + bias_ref[h]
            if s == 0:
                sc = jnp.where((col >= blk) | (i > 0), sc, -jnp.inf)
            m = jnp.max(sc, axis=-1, keepdims=True)
            p = jnp.exp(sc - m)
            l = jnp.sum(p, axis=-1, keepdims=True)
            o = jnp.dot(p.astype(BF16), vw, preferred_element_type=F32) / l
            o_ref[rows, cols] = o.astype(o_ref.dtype)
            lse_tile = jnp.where(lane_head == h, m + jnp.log(l), lse_tile)
        lse_ref[rows, :] = lse_tile


def _dilated_bias(bias_tab, r):
    blk = DIL_BLOCK
    qi = jnp.arange(blk)[:, None]
    kj = jnp.arange(2 * blk)[None, :]
    dc = blk + qi - kj
    band = (dc >= 0) & (dc <= blk)
    bias = _bias_lookup(bias_tab, dc * r).transpose(2, 0, 1)
    return jnp.where(band[None], bias, -jnp.inf)


def _dilated_prompt(q, k, v, bias, nsub):
    n, lc, w = q.shape
    blk = DIL_BLOCK
    tq = nsub * blk
    cur = pl.BlockSpec((None, tq, w), lambda ni, i: (ni, i, 0))
    prev = pl.BlockSpec((None, blk, w), lambda ni, i: (ni, jnp.maximum(i * nsub - 1, 0), 0))
    return pl.pallas_call(
        functools.partial(_dilated_kernel, nsub=nsub),
        out_shape=(jax.ShapeDtypeStruct((n, lc, w), F32),
                   jax.ShapeDtypeStruct((n, lc, HEAD_DIM), F32)),
        grid=(n, lc // tq),
        in_specs=[cur, prev, cur, prev, cur, pl.BlockSpec(bias.shape, lambda ni, i: (0, 0, 0))],
        out_specs=(cur, pl.BlockSpec((None, tq, HEAD_DIM), lambda ni, i: (ni, i, 0))),
        scratch_shapes=[pltpu.VMEM((tq + blk, w), BF16), pltpu.VMEM((tq + blk, w), BF16)],
        compiler_params=_params(("parallel", "parallel"), 32),
        name="dilated_prompt",
    )(q, k, k, v, v, bias)


def _dilated_sample_kernel(q_ref, kn_ref, vn_ref, c0_ref, c1_ref, c2_ref, bc_ref, bn_ref,
                           o_ref, lse_ref, *, ntok):
    caches = (c0_ref, c1_ref, c2_ref)
    for gi in range(DIL_GROUPS):
        cache = caches[gi]
        nclass = cache.shape[1]
        for t in range(ntok):
            cls = t if nclass > 1 else 0
            qt = q_ref[t, gi]
            kc = cache[:, cls, 0]
            vc = cache[:, cls, 1]
            sc = jnp.sum(kc * qt[None], axis=-1, keepdims=True) + bc_ref[gi, t]
            sn = jnp.sum(kn_ref[:, gi] * qt[None], axis=-1, keepdims=True) + bn_ref[gi, t]
            m = jnp.maximum(jnp.max(sc, axis=0), jnp.max(sn, axis=0))
            pc = jnp.exp(sc - m[None])
            pn = jnp.exp(sn - m[None])
            l = jnp.sum(pc, axis=0) + jnp.sum(pn, axis=0)
            o = jnp.sum(pc * vc, axis=0) + jnp.sum(pn * vn_ref[:, gi], axis=0)
            o_ref[t, gi] = o / l
            lse_ref[t, gi] = m + jnp.log(l)


def _dilated_sample(q, kn, vn, caches, bias_cache, bias_new, ntok):
    bd = q.shape[0]
    tok = pl.BlockSpec((None, ntok, DIL_GROUPS, DIL_HPG, HEAD_DIM), lambda bi: (bi, 0, 0, 0, 0))
    cspecs = []
    for c in caches:
        ncls = min(c.shape[2], ntok)
        cspecs.append(pl.BlockSpec((None, DIL_BLOCK, ncls, 2, DIL_HPG, HEAD_DIM),
                                   lambda bi: (bi, 0, 0, 0, 0, 0)))
    const = lambda a: pl.BlockSpec(a.shape, lambda bi: (0,) * a.ndim)
    return pl.pallas_call(
        functools.partial(_dilated_sample_kernel, ntok=ntok),
        out_shape=(jax.ShapeDtypeStruct(q.shape, F32),
                   jax.ShapeDtypeStruct((bd, ntok, DIL_GROUPS, DIL_HPG, 1), F32)),
        grid=(bd,),
        in_specs=[tok, tok, tok] + cspecs + [const(bias_cache), const(bias_new)],
        out_specs=(tok, pl.BlockSpec((None, ntok, DIL_GROUPS, DIL_HPG, 1), lambda bi: (bi, 0, 0, 0, 0))),
        compiler_params=_params(("parallel",), 32),
        name="dilated_sample",
    )(q, kn, vn, *caches, bias_cache, bias_new)


def _dilated_sample_bias(rel_bias, ntok):
    bc, bn = [], []
    m = jnp.arange(DIL_BLOCK)
    tn = jnp.arange(ntok)
    for gi, (_, r) in enumerate(DIL_PAIRS):
        tab = rel_bias[:, gi * DIL_HPG:(gi + 1) * DIL_HPG]
        rows_c, rows_n = [], []
        for t in range(ntok):
            if r == 1:
                jc = DIL_BLOCK + t - m
                okc = m >= t
                jn = t - tn
                okn = tn <= t
            else:
                jc = DIL_BLOCK - m
                okc = jnp.ones_like(m, bool)
                jn = jnp.zeros_like(tn)
                okn = tn == t
            rows_c.append(jnp.where(okc[:, None], _bias_lookup(tab, jc * r), -jnp.inf))
            rows_n.append(jnp.where(okn[:, None], _bias_lookup(tab, jn * r), -jnp.inf))
        bc.append(jnp.stack(rows_c))
        bn.append(jnp.stack(rows_n))
    return jnp.stack(bc)[..., None], jnp.stack(bn)[..., None]


def _merge_kernel(o0_ref, o1_ref, o2_ref, l0_ref, l1_ref, l2_ref, yr_ref, gt_ref, wr_ref, wd_ref,
                  out_ref, oa_ref, *nat, dilations):
    tm = out_ref.shape[0]
    o_nat, l_nat = [], []
    for g, (o_ref, l_ref, r) in enumerate(zip((o0_ref, o1_ref, o2_ref), (l0_ref, l1_ref, l2_ref), dilations)):
        if r == 1:
            o_nat.append([o_ref[:, _head_cols(h)] for h in range(DIL_HPG)])
            l_nat.append(l_ref[...])
        else:
            on, ln = nat[2 * g], nat[2 * g + 1]
            for c in range(r):
                ln[pl.ds(c, tm // r, stride=r), :] = l_ref[c]
                for h in range(DIL_HPG):
                    on[h, pl.ds(c, tm // r, stride=r), :] = o_ref[c, :, _head_cols(h)]
            o_nat.append([on[h] for h in range(DIL_HPG)])
            l_nat.append(ln[...])
    m = jnp.maximum(jnp.maximum(l_nat[0], l_nat[1]), l_nat[2])
    es = [jnp.exp(l - m) for l in l_nat]
    den = es[0] + es[1] + es[2]
    for h in range(DIL_HPG):
        lane = h * LSE_LANES
        acc = None
        for g in range(DIL_GROUPS):
            term = (es[g][:, lane:lane + 1] / den[:, lane:lane + 1]) * o_nat[g][h]
            acc = term if acc is None else acc + term
        oa_ref[:, _head_cols(h)] = acc.astype(oa_ref.dtype)
    d = out_ref.shape[1]
    for c0 in range(0, d, PROJ_CHUNK):
        cols = slice(c0, c0 + PROJ_CHUNK)
        yr = jnp.dot(yr_ref[...], wr_ref[:, cols], preferred_element_type=F32)
        ya = jnp.dot(oa_ref[...], wd_ref[:, cols], preferred_element_type=F32)
        g_r = gt_ref[:, cols].astype(F32)
        g_a = gt_ref[:, d + c0:d + c0 + PROJ_CHUNK].astype(F32)
        out_ref[:, cols] = (g_r * yr + g_a * ya).astype(out_ref.dtype)


def _merge(o_g, lse_g, dilations, b, l, yr_in, gates, w_ret_b, w_dil_b, tm):
    n = yr_in.shape[0]
    d = w_ret_b.shape[1]
    bps = l // tm
    row = lambda w: pl.BlockSpec((tm, w), lambda i: (i, 0))
    const = lambda a: pl.BlockSpec(a.shape, lambda i: (0, 0))

    def group_specs(width):
        specs = []
        for r in dilations:
            if r == 1:
                specs.append(row(width))
            else:
                specs.append(pl.BlockSpec((None, r, tm // r, width), lambda i: (i // bps, 0, i % bps, 0)))
        return specs

    view = lambda t, r, width: t.reshape(n, width) if r == 1 else t.reshape(b, r, l // r, width)
    o_in = [view(t, r, DIL_OUT_WIDTH) for t, r in zip(o_g, dilations)]
    l_in = [view(t, r, HEAD_DIM) for t, r in zip(lse_g, dilations)]
    scratch = [pltpu.VMEM((tm, DIL_OUT_WIDTH), BF16)]
    for _ in dilations:
        scratch += [pltpu.VMEM((DIL_HPG, tm, HEAD_DIM), F32), pltpu.VMEM((tm, HEAD_DIM), F32)]
    return pl.pallas_call(
        functools.partial(_merge_kernel, dilations=tuple(dilations)),
        out_shape=jax.ShapeDtypeStruct((n, d), BF16),
        grid=(n // tm,),
        in_specs=group_specs(DIL_OUT_WIDTH) + group_specs(HEAD_DIM) + [row(RET_WIDTH), row(2 * d),
                                                                      const(w_ret_b), const(w_dil_b)],
        out_specs=row(d),
        scratch_shapes=scratch,
        compiler_params=_params(("parallel",), 48),
        name="merge_branches",
    )(*o_in, *l_in, yr_in, gates, w_ret_b, w_dil_b)


ROUTER_ROWS = 8 + N_EXPERTS


def _route_select(lt):
    lg = lt[0:N_GROUPS]
    gmax = jnp.max(lg, axis=0, keepdims=True)
    w_coarse = 1.0 / jnp.sum(jnp.exp(lg - gmax), axis=0, keepdims=True)
    gid = lax.broadcasted_iota(jnp.int32, lg.shape, 0)
    gsel = jnp.min(jnp.where(lg == gmax, gid, N_GROUPS), axis=0, keepdims=True)
    le = jnp.zeros((EXPERTS_PER_GROUP, lt.shape[1]), F32)
    for g in range(N_GROUPS):
        le = jnp.where(gsel == g, lt[8 + g * EXPERTS_PER_GROUP:8 + (g + 1) * EXPERTS_PER_GROUP], le)
    eid = lax.broadcasted_iota(jnp.int32, le.shape, 0)
    v1 = jnp.max(le, axis=0, keepdims=True)
    i1 = jnp.min(jnp.where(le == v1, eid, EXPERTS_PER_GROUP), axis=0, keepdims=True)
    le2 = jnp.where(eid == i1, -jnp.inf, le)
    v2 = jnp.max(le2, axis=0, keepdims=True)
    i2 = jnp.min(jnp.where(le2 == v2, eid, EXPERTS_PER_GROUP), axis=0, keepdims=True)
    e21 = jnp.exp(v2 - v1)
    w1 = w_coarse / (1.0 + e21)
    w2 = w_coarse * e21 / (1.0 + e21)
    e1 = gsel * EXPERTS_PER_GROUP + i1
    e2 = gsel * EXPERTS_PER_GROUP + i2
    xid = lax.broadcasted_iota(jnp.int32, (N_EXPERTS, lt.shape[1]), 0)
    oh1 = jnp.where(xid == e1, 1.0, 0.0)
    oh2 = jnp.where(xid == e2, 1.0, 0.0)
    return e1, e2, w1, w2, oh1, oh2


def _route_rank(sel, p1, p2, before):
    e1, e2, w1, w2, oh1, oh2 = sel
    n1 = jnp.sum(oh1, axis=1, keepdims=True)
    n2 = jnp.sum(oh2, axis=1, keepdims=True)
    rank1 = jnp.sum(oh1 * (p1 + before), axis=0, keepdims=True)
    rank2 = jnp.sum(oh2 * (p2 + before + n1), axis=0, keepdims=True)
    row = lax.broadcasted_iota(jnp.int32, (8, e1.shape[1]), 0)
    ei = jnp.where(row == 0, e1, jnp.where(row == 1, e2, jnp.where(
        row == 2, rank1.astype(jnp.int32), jnp.where(row == 3, rank2.astype(jnp.int32), 0))))
    wt = jnp.where(row == 0, w1, jnp.where(row == 1, w2, 0.0))
    return ei, wt, before + n1 + n2


def _outproj_router_kernel(mg_ref, x_ref, wo_ref, g2_ref, wr_ref, br_ref, tri_ref, cin_ref,
                           h_ref, u2_ref, ei_ref, wt_ref, cnt_ref, h_keep):
    i = pl.program_id(0)

    @pl.when(i == 0)
    def _():
        cnt_ref[...] = cin_ref[...]
        h_keep[...] = jnp.zeros_like(h_keep)

    sub = tri_ref.shape[0]
    subtiles = [slice(s0, s0 + sub) for s0 in range(0, h_ref.shape[0], sub)]

    def project(rows):
        h_ref[rows, :] = x_ref[rows, :] + jnp.dot(mg_ref[rows, :], wo_ref[...], preferred_element_type=F32)

    project(subtiles[0])
    logits = []
    for rows in subtiles:
        h = h_keep[rows, :]
        u2 = h * lax.rsqrt(jnp.mean(h * h, axis=-1, keepdims=True) + NORM_EPS) * g2_ref[...]
        u2_ref[rows, :] = u2
        logits.append(lax.dot_general(wr_ref[...], u2.astype(BF16), (((1,), (1,)), ((), ())),
                                      preferred_element_type=F32) + br_ref[...])
    for rows in subtiles[1:]:
        project(rows)
    selected = [_route_select(lt) for lt in logits]
    prefix = [(jnp.dot(sel[4].astype(BF16), tri_ref[...], preferred_element_type=F32),
               jnp.dot(sel[5].astype(BF16), tri_ref[...], preferred_element_type=F32)) for sel in selected]
    h_keep[...] = h_ref[...]
    counts = cnt_ref[...]
    for rows, sel, (p1, p2) in zip(subtiles, selected, prefix):
        ei, wt, routed = _route_rank(sel, p1, p2, counts)
        counts = jnp.where(i > 0, routed, counts)
        ei_ref[:, rows] = ei
        wt_ref[:, rows] = wt
    cnt_ref[...] = counts


def _outproj_router(merged, x, w_o_b, norm_ffn, wr_t, br_t, counts_in, tm):
    n, d = x.shape
    nsteps = n // tm
    sub = min(tm, ROUTER_SUB)
    tri = (jnp.arange(sub)[:, None] < jnp.arange(sub)[None, :]).astype(BF16)
    matmul_tile = lambda: pl.BlockSpec((tm, d), lambda i: (jnp.minimum(i, nsteps - 1), 0))
    routed_rows = lambda: pl.BlockSpec((tm, d), lambda i: (jnp.maximum(i - 1, 0), 0))
    routed_lanes = lambda: pl.BlockSpec((8, tm), lambda i: (0, jnp.maximum(i - 1, 0)))
    const = lambda a: pl.BlockSpec(a.shape, lambda i: (0, 0))
    return pl.pallas_call(
        _outproj_router_kernel,
        out_shape=(jax.ShapeDtypeStruct((n, d), F32), jax.ShapeDtypeStruct((n, d), F32),
                   jax.ShapeDtypeStruct((8, n), jnp.int32), jax.ShapeDtypeStruct((8, n), F32),
                   jax.ShapeDtypeStruct((N_EXPERTS, 1), F32)),
        grid=(nsteps + 1,),
        in_specs=[matmul_tile(), matmul_tile(), const(w_o_b), pl.BlockSpec((1, d), lambda i: (0, 0)),
                  const(wr_t), const(br_t), const(tri), const(counts_in)],
        out_specs=(matmul_tile(), routed_rows(), routed_lanes(), routed_lanes(), const(counts_in)),
        scratch_shapes=[pltpu.VMEM((tm, d), F32)],
        compiler_params=_params(("arbitrary",), 60),
        name="outproj_router",
    )(merged, x, w_o_b, norm_ffn.reshape(1, d), wr_t, br_t, tri, counts_in)


def _router_weights(w_rg, b_rg, w_re, b_re):
    d = w_rg.shape[0]
    wr = jnp.zeros((ROUTER_ROWS, d), F32)
    wr = wr.at[0:N_GROUPS].set(w_rg.T)
    wr = wr.at[8:].set(w_re.transpose(0, 2, 1).reshape(N_EXPERTS, d))
    br = jnp.zeros((ROUTER_ROWS, 1), F32)
    br = br.at[0:N_GROUPS, 0].set(b_rg)
    br = br.at[8:, 0].set(b_re.reshape(N_EXPERTS))
    return wr.astype(BF16), br


DMA_UNROLL = 8


def _dispatch_kernel(pos_p_ref, pos_s_ref, zero_ref, up_ref, us_ref, xs_ref, zbuf, stage, sem, *, tm):
    i = pl.program_id(0)
    last = pl.num_programs(0) - 1
    ntiles = xs_ref.shape[0] // MOE_TILE

    def tile_copy(t):
        return pltpu.make_async_copy(zbuf, xs_ref.at[pl.ds(pl.multiple_of(t * MOE_TILE, MOE_TILE), MOE_TILE)],
                                     sem.at[2])

    @pl.when(i == 0)
    def _():
        zbuf[...] = jnp.zeros_like(zbuf)
        for op in ("start", "wait"):
            def per_tile(t, c, op=op):
                @pl.when(zero_ref[t] != 0)
                def _():
                    getattr(tile_copy(t), op)()
                return c

            lax.fori_loop(0, ntiles, per_tile, 0)

    def scatter_rows(op, src_ref, pos_ref, base, nrows, ntok_total, row_sem):
        if op == "wait":
            for _ in range(TOP_K):
                pltpu.make_async_copy(src_ref, xs_ref.at[pl.ds(0, nrows)], row_sem).wait()
            return

        def body(r, c):
            for kk in range(TOP_K):
                slot = pos_ref[kk * ntok_total + base + r]
                pltpu.make_async_copy(src_ref.at[pl.ds(r, 1)], xs_ref.at[pl.ds(slot, 1)],
                                      row_sem).start(priority=kk % 2)
            return c

        lax.fori_loop(0, nrows, body, 0, unroll=DMA_UNROLL)

    prompt_rows = last * tm
    cur = i % 2

    @pl.when(i < last)
    def _():
        stage[cur] = up_ref[...]
        scatter_rows("start", stage.at[cur], pos_p_ref, i * tm, tm, prompt_rows, sem.at[cur])

    @pl.when(i > 0)
    def _():
        scatter_rows("wait", stage.at[1 - cur], pos_p_ref, (i - 1) * tm, tm, prompt_rows, sem.at[1 - cur])

    @pl.when(i == last)
    def _():
        for op in ("start", "wait"):
            scatter_rows(op, us_ref, pos_s_ref, 0, us_ref.shape[0], us_ref.shape[0], sem.at[2])


def _dispatch(pos_p, pos_s, zero_tiles, u2p, u2s, tm):
    n, d = u2p.shape
    nsteps = n // tm
    grid_spec = pltpu.PrefetchScalarGridSpec(
        num_scalar_prefetch=3,
        grid=(nsteps + 1,),
        in_specs=[pl.BlockSpec((tm, d), lambda i, *_: (jnp.minimum(i, nsteps - 1), 0)),
                  pl.BlockSpec(u2s.shape, lambda i, *_: (0, 0))],
        out_specs=pl.BlockSpec(memory_space=pl.ANY),
        scratch_shapes=[pltpu.VMEM((MOE_TILE, d), F32), pltpu.VMEM((2, tm, d), F32),
                        pltpu.SemaphoreType.DMA((3,))],
    )
    return pl.pallas_call(
        functools.partial(_dispatch_kernel, tm=tm),
        out_shape=jax.ShapeDtypeStruct((zero_tiles.shape[0] * MOE_TILE, d), F32),
        grid_spec=grid_spec,
        compiler_params=_params(("arbitrary",), 40),
        name="moe_dispatch",
    )(pos_p, pos_s, zero_tiles, u2p, u2s)


def _gmm_kernel(te_ref, nt_ref, x_ref, wg_ref, wu_ref, wd_ref, o_ref, wgb, wub, wdb):
    i = pl.program_id(0)
    fresh = jnp.logical_or(i == 0, te_ref[i] != te_ref[jnp.maximum(i - 1, 0)])

    @pl.when(jnp.logical_and(fresh, i < nt_ref[0]))
    def _():
        wgb[...] = wg_ref[...].astype(BF16)
        wub[...] = wu_ref[...].astype(BF16)
        wdb[...] = wd_ref[...].astype(BF16)

    @pl.when(i < nt_ref[0])
    def _():
        x = x_ref[...].astype(BF16)
        acc = None
        for c0 in range(0, wgb.shape[1], PROJ_CHUNK):
            cols = slice(c0, c0 + PROJ_CHUNK)
            a = jnp.dot(x, wgb[:, cols], preferred_element_type=F32)
            b = jnp.dot(x, wub[:, cols], preferred_element_type=F32)
            hm = (a * jax.nn.sigmoid(a) * b).astype(BF16)
            part = jnp.dot(hm, wdb[cols, :], preferred_element_type=F32)
            acc = part if acc is None else acc + part
        o_ref[...] = acc

    @pl.when(i >= nt_ref[0])
    def _():
        o_ref[...] = jnp.zeros_like(o_ref)


def _gmm(tile_expert, ntiles_used, xs, w_gate, w_up, w_down):
    npad, d = xs.shape
    f = w_gate.shape[-1]
    ntiles = npad // MOE_TILE
    epg = w_gate.shape[1]

    def xmap(i, te, nt):
        return (jnp.minimum(i, nt[0] - 1), 0)

    def wmap(i, te, nt):
        e = te[i]
        return (e // epg, e % epg, 0, 0)

    grid_spec = pltpu.PrefetchScalarGridSpec(
        num_scalar_prefetch=2,
        grid=(ntiles,),
        in_specs=[pl.BlockSpec((MOE_TILE, d), xmap),
                  pl.BlockSpec((None, None, d, f), wmap),
                  pl.BlockSpec((None, None, d, f), wmap),
                  pl.BlockSpec((None, None, f, d), wmap)],
        out_specs=pl.BlockSpec((MOE_TILE, d), lambda i, te, nt: (i, 0)),
        scratch_shapes=[pltpu.VMEM((d, f), BF16), pltpu.VMEM((d, f), BF16), pltpu.VMEM((f, d), BF16)],
    )
    return pl.pallas_call(
        _gmm_kernel,
        out_shape=jax.ShapeDtypeStruct((npad, d), F32),
        grid_spec=grid_spec,
        compiler_params=_params(("arbitrary",), 60),
        name="moe_grouped_matmul",
    )(tile_expert, ntiles_used, xs, w_gate, w_up, w_down)


def _combine_kernel(pos_ref, h_ref, wt_ref, os_ref, y_ref, g0, g1, sem, *, tm, ntok_total):
    i = pl.program_id(0)
    nsteps = pl.num_programs(0)
    cur = i % 2

    def row_copy(step, r, kk, slot):
        src = pos_ref[kk * ntok_total + step * tm + r]
        buf = (g0, g1)[kk]
        return pltpu.make_async_copy(os_ref.at[pl.ds(src, 1)], buf.at[slot, pl.ds(r, 1)], sem.at[slot])

    def issue_step(step, slot):
        def body(r, c):
            row_copy(step, r, 0, slot).start(priority=0)
            row_copy(step, r, 1, slot).start(priority=1)
            return c

        lax.fori_loop(0, tm, body, 0, unroll=DMA_UNROLL)

    @pl.when(i == 0)
    def _():
        issue_step(0, 0)

    @pl.when(i + 1 < nsteps)
    def _():
        issue_step(i + 1, 1 - cur)

    for buf in (g0, g1):
        pltpu.make_async_copy(os_ref.at[pl.ds(0, tm)], buf.at[cur], sem.at[cur]).wait()
    y_ref[...] = h_ref[...] + wt_ref[:, 0:1] * g0[cur] + wt_ref[:, 1:2] * g1[cur]


def _combine(pos_flat, h, wt_cols, out_sorted, tm):
    n, d = h.shape
    grid_spec = pltpu.PrefetchScalarGridSpec(
        num_scalar_prefetch=1,
        grid=(n // tm,),
        in_specs=[pl.BlockSpec((tm, d), lambda i, pos: (i, 0)),
                  pl.BlockSpec((tm, 8), lambda i, pos: (i, 0)),
                  pl.BlockSpec(memory_space=pl.ANY)],
        out_specs=pl.BlockSpec((tm, d), lambda i, pos: (i, 0)),
        scratch_shapes=[pltpu.VMEM((2, tm, d), F32), pltpu.VMEM((2, tm, d), F32),
                        pltpu.SemaphoreType.DMA((2,))],
    )
    return pl.pallas_call(
        functools.partial(_combine_kernel, tm=tm, ntok_total=n),
        out_shape=jax.ShapeDtypeStruct((n, d), F32),
        grid_spec=grid_spec,
        compiler_params=_params(("arbitrary",), 48),
        name="moe_combine",
    )(pos_flat, h, wt_cols, out_sorted)


def _moe_plan(counts, route_sets):
    npairs = sum(e.shape[1] for e, _ in route_sets) * TOP_K
    tiles = (counts + MOE_TILE - 1) // MOE_TILE
    tile_end = jnp.cumsum(tiles)
    offs = (tile_end - tiles) * MOE_TILE
    ids = jnp.arange(N_EXPERTS)
    slots = []
    for experts, ranks in route_sets:
        base = jnp.sum(jnp.where(experts[..., None] == ids, offs, 0), axis=-1)
        slots.append((base + ranks).reshape(-1).astype(jnp.int32))
    ntiles = (npairs + N_EXPERTS * (MOE_TILE - 1)) // MOE_TILE
    tile_expert = jnp.sum(tile_end[None, :] <= jnp.arange(ntiles)[:, None], axis=1)
    tile_expert = jnp.minimum(tile_expert, N_EXPERTS - 1).astype(jnp.int32)
    tile_ids = jnp.arange(ntiles)
    is_last = jnp.any((tile_ids[:, None] == tile_end[None, :] - 1) & (tiles[None, :] > 0), axis=1)
    zero_tiles = (is_last | (tile_ids >= tile_end[-1])).astype(jnp.int32)
    return slots, tile_expert, tile_end[-1:].astype(jnp.int32), zero_tiles


def _mixers_prompt(x, wts, counts_in, shift_jobs):
    b, l, d = x.shape
    n = b * l
    tm = ROW_TILE
    xf = x.reshape(n, d)
    u = _rmsnorm(xf, wts["norm_attn"], tm)
    tab = _rotary_tables(jnp.arange(l), l)
    (qk_r, vg_r, gates), shifted = _proj_retention_and_gates(u, wts["w_in"], tab, 2 * tm, shift_jobs)

    s0 = jnp.zeros((b, RET_HEADS, HEAD_DIM, HEAD_DIM), F32)
    yr_in, s_fin = _retention_prompt(qk_r.reshape(b, l, -1), vg_r.reshape(b, l, -1), s0, 256, min(l, 2048))

    o_g, lse_g, bufs = [], [], []
    for gi, (w, r) in enumerate(DIL_PAIRS):
        qg, kg, vg, kt, vt = _proj_dilated_prompt(u, wts["w_in"], wts["q_norm"], wts["k_norm"], gi, b, l, tm)
        bias = _dilated_bias(wts["rel_bias"][:, gi * DIL_HPG:(gi + 1) * DIL_HPG], r)
        o, lse = _dilated_prompt(qg, kg, vg, bias, min(8, l // r // DIL_BLOCK))
        o_g.append(o)
        lse_g.append(lse)
        bufs.append(jnp.stack([kt, vt], axis=2).reshape(b, kt.shape[1], 2, DIL_HPG, HEAD_DIM))

    merged = _merge(o_g, lse_g, [r for _, r in DIL_PAIRS], b, l, yr_in.reshape(n, RET_WIDTH), gates,
                    wts["w_ret_out"], wts["w_dil_out"], tm)
    h, u2, ei, wt, counts = _outproj_router(merged, xf, wts["w_o"], wts["norm_ffn"], wts["wr_t"], wts["br_t"],
                                            counts_in, ROUTER_TILE)
    return h, u2, ei, wt, counts, s_fin, bufs, shifted


def _mixers_sample(x, caches, state, wts, counts_in):
    bd, t, d = x.shape
    n = bd * t
    xf = x.reshape(n, d)
    u = _rmsnorm(xf, wts["norm_attn"], n)
    tab = _rotary_tables(PAST_LEN + jnp.arange(t), n)
    (qk_r, vg_r, gates), _ = _proj_retention_and_gates(u, wts["w_in"], tab, n)
    qkv_a = _proj_dilated_sample(u, wts["w_in"], wts["q_norm"], wts["k_norm"])

    def heads(a):
        a = a.astype(F32).reshape(bd, t, RET_HEADS, HEAD_DIM).transpose(0, 2, 1, 3)
        return jnp.pad(a, ((0, 0), (0, 0), (0, 8 - t), (0, 0)))

    y_r, s_new = _retention_sample(heads(qk_r[:, :RET_WIDTH]), heads(qk_r[:, RET_WIDTH:]),
                                   heads(vg_r[:, :RET_WIDTH]), heads(vg_r[:, RET_WIDTH:]), state, t)
    yr_in = y_r[:, :, :t].transpose(0, 2, 1, 3).reshape(n, RET_WIDTH).astype(BF16)

    grp = lambda a: a.reshape(bd, t, DIL_GROUPS, DIL_HPG, HEAD_DIM)
    qa, ka, va = (grp(qkv_a[:, s * DIL_WIDTH:(s + 1) * DIL_WIDTH]) for s in range(3))
    cviews = [c.reshape(bd, DIL_BLOCK, r, 2, DIL_HPG, HEAD_DIM) for c, (_, r) in zip(caches, DIL_PAIRS)]
    bias_c, bias_n = _dilated_sample_bias(wts["rel_bias"], t)
    o, lse = _dilated_sample(qa, ka, va, cviews, bias_c, bias_n, t)
    o_g = [o[:, :, gi].reshape(n, 1, DIL_OUT_WIDTH) for gi in range(DIL_GROUPS)]
    lse_g = [jnp.repeat(lse[:, :, gi].reshape(n, DIL_HPG), LSE_LANES, axis=1).reshape(n, 1, HEAD_DIM)
             for gi in range(DIL_GROUPS)]

    news = [jnp.stack([ka[:, :, gi], va[:, :, gi]], axis=2) for gi in range(DIL_GROUPS)]
    shift_jobs = list(zip(caches, news))

    merged = _merge(o_g, lse_g, [1] * DIL_GROUPS, 1, n, yr_in, gates, wts["w_ret_out"], wts["w_dil_out"], n)
    h, u2, ei, wt, counts = _outproj_router(merged, xf, wts["w_o"], wts["norm_ffn"], wts["wr_t"], wts["br_t"],
                                            counts_in, n)
    return h, u2, ei, wt, counts, s_new, shift_jobs


def kernel(x_prompt, x_sample, cache_kv_g0, cache_kv_g1, cache_kv_g2, state_ret, norm_attn, w_in, q_norm,
           k_norm, rel_bias, w_ret_out, w_dil_out, w_o, norm_ffn, w_router_group, b_router_group,
           w_router_expert, b_router_expert, w_gate, w_up, w_down):
    caches = (cache_kv_g0, cache_kv_g1, cache_kv_g2)
    ntok = x_sample.shape[1]
    for c, (w, r) in zip(caches, DIL_PAIRS):
        assert c.shape[1] == w == DIL_BLOCK * r and (r == 1 or ntok <= r) and ntok <= 8
    wr_t, br_t = _router_weights(w_router_group, b_router_group, w_router_expert, b_router_expert)
    wts = dict(norm_attn=norm_attn, w_in=w_in, q_norm=q_norm, k_norm=k_norm, rel_bias=rel_bias,
               w_ret_out=w_ret_out.astype(BF16), w_dil_out=w_dil_out.astype(BF16), w_o=w_o.astype(BF16),
               norm_ffn=norm_ffn, wr_t=wr_t, br_t=br_t)

    zero_counts = jnp.zeros((N_EXPERTS, 1), F32)
    hs, u2s, eis, wts_s, counts_s, s_s, shift_jobs = _mixers_sample(x_sample, caches, state_ret, wts, zero_counts)
    hp, u2p, eip, wtp, counts, s_p, bufs_p, bufs_s = _mixers_prompt(x_prompt, wts, counts_s, shift_jobs)

    slots, tile_expert, ntiles_used, zero_tiles = _moe_plan(
        counts[:, 0].astype(jnp.int32), [(eip[0:2], eip[2:4]), (eis[0:2], eis[2:4])])
    xs = _dispatch(slots[0], slots[1], zero_tiles, u2p, u2s, ROW_TILE)
    out_sorted = _gmm(tile_expert, ntiles_used, xs, w_gate, w_up, w_down)
    yp = _combine(slots[0], hp, wtp.T, out_sorted, ROW_TILE)
    ys = _combine(slots[1], hs, wts_s.T, out_sorted, hs.shape[0])

    return (yp.reshape(x_prompt.shape), ys.reshape(x_sample.shape), bufs_p[0], bufs_p[1], bufs_p[2], s_p,
            bufs_s[0], bufs_s[1], bufs_s[2], s_s)
```

```python
import functools
import math

import jax
import jax.numpy as jnp
from jax import lax
from jax.experimental import pallas as pl
from jax.experimental.pallas import tpu as pltpu

HEAD_DIM = 128
RET_HEADS = 8
RET_WIDTH = RET_HEADS * HEAD_DIM
ROPE_BASE = 10000.0
GN_EPS = 1e-5
DIL_PAIRS = ((128, 1), (512, 4), (2048, 16))
DIL_GROUPS = len(DIL_PAIRS)
DIL_HPG = 4
DIL_HEADS = DIL_HPG * DIL_GROUPS
DIL_WIDTH = DIL_HEADS * HEAD_DIM
DIL_OUT_WIDTH = DIL_HPG * HEAD_DIM
DIL_BLOCK = 128
LSE_LANES = HEAD_DIM // DIL_HPG
ATTN_SCALE = HEAD_DIM ** -0.5
REL_BUCKETS = 32
REL_MAX_DIST = 2048
N_GROUPS = 4
EXPERTS_PER_GROUP = 8
N_EXPERTS = N_GROUPS * EXPERTS_PER_GROUP
TOP_K = 2
NORM_EPS = 1e-6
PAST_LEN = 16384

COL_QR = 0
COL_KR = COL_QR + RET_WIDTH
COL_VR = COL_KR + RET_WIDTH
COL_GR = COL_VR + RET_WIDTH
COL_QA = COL_GR + RET_WIDTH
COL_KA = COL_QA + DIL_WIDTH
COL_VA = COL_KA + DIL_WIDTH
COL_GATES = COL_VA + DIL_WIDTH

MOE_TILE = 256
ROW_TILE = 512
PROJ_CHUNK = 256
ROUTER_TILE = 512
ROUTER_SUB = 256
SAMPLE_SEQS_PER_STEP = 1
MIB = 1 << 20
BF16 = jnp.bfloat16
F32 = jnp.float32


def _params(semantics, vmem_mib):
    return pltpu.CompilerParams(dimension_semantics=semantics, vmem_limit_bytes=vmem_mib * MIB)


def _head_cols(h):
    return slice(h * HEAD_DIM, (h + 1) * HEAD_DIM)


def _rmsnorm_kernel(x_ref, g_ref, o_ref):
    x = x_ref[...]
    y = x * lax.rsqrt(jnp.mean(x * x, axis=-1, keepdims=True) + NORM_EPS)
    o_ref[...] = (y * g_ref[...]).astype(o_ref.dtype)


def _rmsnorm(x, g, tm):
    n, d = x.shape
    return pl.pallas_call(
        _rmsnorm_kernel,
        out_shape=jax.ShapeDtypeStruct((n, d), BF16),
        grid=(n // tm,),
        in_specs=[pl.BlockSpec((tm, d), lambda i: (i, 0)), pl.BlockSpec((1, d), lambda i: (0, 0))],
        out_specs=pl.BlockSpec((tm, d), lambda i: (i, 0)),
        compiler_params=_params(("parallel",), 40),
        name="rmsnorm",
    )(x, g.reshape(1, d))


def _background_shift_step(step, nsteps, src_ref, new_ref, dst_ref, ring, sems, *, chunks_per_seq, rows):
    shift = new_ref.shape[1]

    def chunk_copy(s, slot, inbound):
        b = s // chunks_per_seq
        r0 = (s % chunks_per_seq) * rows
        if inbound:
            return pltpu.make_async_copy(src_ref.at[b, pl.ds(pl.multiple_of(r0 + shift, 8), rows)],
                                         ring.at[slot], sems.at[0, slot])
        return pltpu.make_async_copy(ring.at[slot], dst_ref.at[b, pl.ds(pl.multiple_of(r0, 8), rows)],
                                     sems.at[1, slot])

    cur = step % 2

    @pl.when(step >= 2)
    def _():
        chunk_copy(step - 2, cur, False).wait()

    chunk_copy(step, cur, True).start()

    @pl.when(step >= 1)
    def _():
        chunk_copy(step - 1, 1 - cur, True).wait()
        chunk_copy(step - 1, 1 - cur, False).start()

    @pl.when(step == nsteps - 1)
    def _():
        tail = pltpu.make_async_copy(new_ref, dst_ref.at[:, pl.ds(dst_ref.shape[1] - shift, shift)], sems.at[2, 0])
        tail.start()
        chunk_copy(step, cur, True).wait()
        chunk_copy(step, cur, False).start()
        chunk_copy(step - 1, 1 - cur, False).wait()
        chunk_copy(step, cur, False).wait()
        tail.wait()


def _proj_kernel(u_ref, *refs, epilogue, n_w, n_extra, n_out, shift):
    w_refs, refs = refs[:n_w], refs[n_w:]
    extras, refs = refs[:n_extra], refs[n_extra:]
    if shift is not None:
        (src_ref, new_ref), refs = refs[:2], refs[2:]
    outs, refs = refs[:n_out], refs[n_out:]
    if shift is not None:
        dst_ref, (ring, sems), refs = refs[0], refs[-2:], refs[1:-2]
        nrow_blocks = pl.num_programs(1)
        _background_shift_step(pl.program_id(0) * nrow_blocks + pl.program_id(1), pl.num_programs(0) * nrow_blocks,
                               src_ref, new_ref, dst_ref, ring, sems, **shift)
    scratch, wb_refs = refs[:len(refs) - n_w], refs[len(refs) - n_w:]

    @pl.when(pl.program_id(1) == 0)
    def _():
        for w_ref, wb_ref in zip(w_refs, wb_refs):
            wb_ref[...] = w_ref[...].astype(BF16)

    tn = wb_refs[0].shape[1]
    for wi, wb_ref in enumerate(wb_refs):
        for c0 in range(0, tn, PROJ_CHUNK):
            acc = jnp.dot(u_ref[...], wb_ref[:, c0:c0 + PROJ_CHUNK], preferred_element_type=F32)
            epilogue(wi, c0, acc, *extras, *outs, *scratch)


def _proj(u, w_in, col_offsets, ncol_blocks, tn, tm, epilogue, extra, extra_specs, out_shape, out_specs,
          scratch, name, shift_job=None, vmem_mib=58):
    n, k = u.shape
    w_specs = [pl.BlockSpec((pl.Element(k), pl.Element(tn)),
                            functools.partial(lambda j, i, o: (0, pl.multiple_of(o + j * tn, HEAD_DIM)), o=o))
               for o in col_offsets]
    multi = isinstance(out_shape, (tuple, list))
    out_shapes = list(out_shape) if multi else [out_shape]
    out_spec_list = list(out_specs) if multi else [out_specs]
    n_out = len(out_shapes)
    side_in, side_in_specs, side_scratch, shift = [], [], [], None
    if shift_job is not None:
        cache, new = shift_job
        shape = cache.shape
        bd = shape[0]
        cache = cache.reshape(bd, -1, HEAD_DIM)
        new = new.reshape(bd, -1, HEAD_DIM)
        nsteps = ncol_blocks * (n // tm)
        chunks_per_seq = nsteps // bd
        rows = (cache.shape[1] - new.shape[1]) // chunks_per_seq
        assert chunks_per_seq * bd == nsteps and rows % 8 == 0
        assert rows * chunks_per_seq == cache.shape[1] - new.shape[1]
        shift = dict(chunks_per_seq=chunks_per_seq, rows=rows)
        any_spec = pl.BlockSpec(memory_space=pl.ANY)
        side_in, side_in_specs = [cache, new], [any_spec, pl.BlockSpec(new.shape, lambda j, i: (0, 0, 0))]
        out_shapes.append(jax.ShapeDtypeStruct(cache.shape, cache.dtype))
        out_spec_list.append(any_spec)
        side_scratch = [pltpu.VMEM((2, rows, HEAD_DIM), F32), pltpu.SemaphoreType.DMA((3, 2))]
    outs = pl.pallas_call(
        functools.partial(_proj_kernel, epilogue=epilogue, n_w=len(col_offsets), n_extra=len(extra), n_out=n_out,
                          shift=shift),
        out_shape=tuple(out_shapes) if (multi or shift) else out_shapes[0],
        grid=(ncol_blocks, n // tm),
        in_specs=[pl.BlockSpec((tm, k), lambda j, i: (i, 0))] + w_specs + list(extra_specs) + side_in_specs,
        out_specs=tuple(out_spec_list) if (multi or shift) else out_spec_list[0],
        scratch_shapes=list(scratch) + [pltpu.VMEM((k, tn), BF16) for _ in col_offsets] + side_scratch,
        compiler_params=_params(("arbitrary", "arbitrary"), vmem_mib),
        name=name,
    )(u, *([w_in] * len(col_offsets)), *extra, *side_in)
    if shift is None:
        return outs
    shifted = outs[-1].reshape(shape)
    return (tuple(outs[:-1]) if multi else outs[0]), shifted


def _chunk_heads(c0, acc):
    return [(slice(c0 + h * HEAD_DIM, c0 + (h + 1) * HEAD_DIM), acc[:, _head_cols(h)])
            for h in range(acc.shape[1] // HEAD_DIM)]


def _epi_rotary(wi, c0, acc, tab_ref, o_ref):
    c = tab_ref[0]
    s = tab_ref[1]
    for cols, xh in _chunk_heads(c0, acc):
        o_ref[:, cols] = (xh * c + pltpu.roll(xh, HEAD_DIM // 2, 1) * s).astype(o_ref.dtype)


def _epi_value_gate(wi, c0, acc, o_ref):
    is_gate = pl.program_id(0) == 1
    o_ref[:, c0:c0 + acc.shape[1]] = jnp.where(is_gate, acc * jax.nn.sigmoid(acc), acc).astype(o_ref.dtype)


def _epi_sigmoid(wi, c0, acc, o_ref):
    o_ref[:, c0:c0 + acc.shape[1]] = jax.nn.sigmoid(acc).astype(o_ref.dtype)


def _head_rms(xh, g):
    return xh * lax.rsqrt(jnp.mean(xh * xh, axis=-1, keepdims=True) + NORM_EPS) * g


def _epi_dilated_prompt(wi, c0, acc, qn_ref, kn_ref, q_ref, k_ref, v_ref, kt_ref, vt_ref, scr, *, r):
    tm = scr.shape[2]
    tail_rows = kt_ref.shape[0]
    gain = (qn_ref[...] * ATTN_SCALE, kn_ref[...], None)[wi]
    out_ref = (q_ref, k_ref, v_ref)[wi]
    tail_ref = (None, kt_ref, vt_ref)[wi]
    for cols, xh in _chunk_heads(c0, acc):
        h = cols.start // HEAD_DIM
        y = xh if gain is None else _head_rms(xh, gain)
        slab = scr.at[wi, h]
        slab[...] = y
        if tail_ref is not None:
            tail_ref[:, cols] = y[tm - tail_rows:tm, :]
        if r == 1:
            out_ref[:, cols] = y.astype(out_ref.dtype)
        else:
            for c in range(r):
                out_ref[c, :, cols] = slab[pl.ds(c, tm // r, stride=r), :].astype(out_ref.dtype)


def _epi_dilated_sample(wi, c0, acc, qn_ref, kn_ref, o_ref):
    j = pl.program_id(0)
    gain = jnp.where(j < DIL_GROUPS, qn_ref[...] * ATTN_SCALE, kn_ref[...])
    for cols, xh in _chunk_heads(c0, acc):
        o_ref[:, cols] = jnp.where(j < 2 * DIL_GROUPS, _head_rms(xh, gain), xh)


def _rotary_tables(pos, rows):
    half = HEAD_DIM // 2
    inv = ROPE_BASE ** (-jnp.arange(half, dtype=F32) / half)
    ang = pos.astype(F32)[:, None] * inv[None, :]
    cos = jnp.cos(ang)
    sin = jnp.sin(ang)
    c = jnp.concatenate([cos, cos], axis=-1)
    s = jnp.concatenate([-sin, sin], axis=-1)
    tab = jnp.stack([jnp.stack([c, s]), jnp.stack([c, s]) * (HEAD_DIM ** -0.5)])
    reps = rows // pos.shape[0]
    return jnp.tile(tab, (1, 1, reps, 1))


def _proj_retention_and_gates(u, w_in, pos_tab, tm, shift_jobs=(None, None, None)):
    n = u.shape[0]
    nblk = pos_tab.shape[2] // tm
    tile_out = lambda: pl.BlockSpec((tm, RET_WIDTH), lambda j, i: (i, j))
    ngate = w_in.shape[1] - COL_GATES
    qk_r = _proj(u, w_in, [COL_QR], 2, RET_WIDTH, tm, _epi_rotary, [pos_tab],
                 [pl.BlockSpec((None, 2, tm, HEAD_DIM), lambda j, i: (j, 0, i % nblk, 0))],
                 jax.ShapeDtypeStruct((n, 2 * RET_WIDTH), BF16), tile_out(), [], "proj_qk_ret", shift_jobs[0])
    vg_r = _proj(u, w_in, [COL_VR], 2, RET_WIDTH, tm, _epi_value_gate, [], [],
                 jax.ShapeDtypeStruct((n, 2 * RET_WIDTH), BF16), tile_out(), [], "proj_vg_ret", shift_jobs[1])
    gates = _proj(u, w_in, [COL_GATES], ngate // RET_WIDTH, RET_WIDTH, tm, _epi_sigmoid, [], [],
                  jax.ShapeDtypeStruct((n, ngate), BF16), tile_out(), [], "proj_gates", shift_jobs[2])
    results = (qk_r, vg_r, gates)
    if any(job is not None for job in shift_jobs):
        return tuple(r[0] for r in results), [r[1] for r in results]
    return results, []


def _proj_dilated_prompt(u, w_in, q_norm, k_norm, gi, b, l, tm):
    w, r = DIL_PAIRS[gi]
    lw = min(w, l)
    bps = l // tm
    tail_rows = min(lw, tm)
    tail_blocks = lw // tail_rows
    cm = (b, r, l // r, DIL_OUT_WIDTH)
    if r == 1:
        cm_shape = jax.ShapeDtypeStruct((b * l, DIL_OUT_WIDTH), BF16)
        cm_spec = lambda: pl.BlockSpec((tm, DIL_OUT_WIDTH), lambda j, i: (i, 0))
    else:
        cm_shape = jax.ShapeDtypeStruct(cm, BF16)
        cm_spec = lambda: pl.BlockSpec((None, r, tm // r, DIL_OUT_WIDTH), lambda j, i: (i // bps, 0, i % bps, 0))
    tail_shape = jax.ShapeDtypeStruct((b, lw, DIL_OUT_WIDTH), F32)
    tail_spec = lambda: pl.BlockSpec(
        (None, tail_rows, DIL_OUT_WIDTH),
        lambda j, i: (i // bps, jnp.maximum(i % bps - (bps - tail_blocks), 0), 0))
    gspec = pl.BlockSpec((1, HEAD_DIM), lambda j, i: (0, 0))
    off = gi * DIL_OUT_WIDTH
    q, k, v, kt, vt = _proj(
        u, w_in, [COL_QA + off, COL_KA + off, COL_VA + off], 1, DIL_OUT_WIDTH, tm,
        functools.partial(_epi_dilated_prompt, r=r),
        [q_norm.reshape(1, HEAD_DIM), k_norm.reshape(1, HEAD_DIM)], [gspec, gspec],
        (cm_shape, cm_shape, cm_shape, tail_shape, tail_shape),
        (cm_spec(), cm_spec(), cm_spec(), tail_spec(), tail_spec()),
        [pltpu.VMEM((3, DIL_HPG, tm, HEAD_DIM), F32)], "proj_dilated_prompt")
    cls = lambda t: t.reshape(b * r, l // r, DIL_OUT_WIDTH)
    return cls(q), cls(k), cls(v), kt, vt


def _proj_dilated_sample(u, w_in, q_norm, k_norm):
    n = u.shape[0]
    gspec = pl.BlockSpec((1, HEAD_DIM), lambda j, i: (0, 0))
    return _proj(u, w_in, [COL_QA], 3 * DIL_GROUPS, DIL_OUT_WIDTH, n, _epi_dilated_sample,
                 [q_norm.reshape(1, HEAD_DIM), k_norm.reshape(1, HEAD_DIM)], [gspec, gspec],
                 jax.ShapeDtypeStruct((n, 3 * DIL_WIDTH), F32),
                 pl.BlockSpec((n, DIL_OUT_WIDTH), lambda j, i: (i, j)), [], "proj_dilated_sample")


def _retention_kernel(q_ref, k_ref, v_ref, g_ref, s0_ref, dm_ref, qd_ref, kd_ref, cd_ref,
                      y_ref, sf_ref, state, *, chunk, nchunks):
    t = pl.program_id(2)

    @pl.when(t == 0)
    def _():
        state[...] = s0_ref[...]

    dmask = dm_ref[...]
    qdec = qd_ref[...]
    kdec = kd_ref[...]
    cdec = cd_ref[...]
    s_prev = state[...]
    for ci in range(nchunks):
        rows = pl.ds(ci * chunk, chunk)
        q = q_ref[rows, :]
        k = k_ref[rows, :]
        v = v_ref[rows, :]
        sc = lax.dot_general(q, k, (((1,), (1,)), ((), ())), preferred_element_type=F32) * dmask
        intra = jnp.dot(sc.astype(BF16), v, preferred_element_type=F32)
        qd = (q.astype(F32) * qdec).astype(BF16)
        cross = jnp.dot(qd, s_prev.astype(BF16), preferred_element_type=F32)
        kd = (k.astype(F32) * kdec).astype(BF16)
        kv = lax.dot_general(kd, v, (((0,), (0,)), ((), ())), preferred_element_type=F32)
        s_prev = s_prev * cdec + kv
        o = intra + cross
        mu = jnp.mean(o, axis=-1, keepdims=True)
        oc = o - mu
        var = jnp.mean(oc * oc, axis=-1, keepdims=True)
        y = g_ref[rows, :].astype(F32) * (oc * lax.rsqrt(var + GN_EPS))
        y_ref[rows, :] = y.astype(y_ref.dtype)
    state[...] = s_prev

    @pl.when(t == pl.num_programs(2) - 1)
    def _():
        sf_ref[...] = state[...]


def _retention_decay(chunk, valid):
    lg = jnp.log1p(-jnp.exp2(-5.0 - jnp.arange(RET_HEADS, dtype=F32)))
    idx = jnp.arange(chunk, dtype=F32)
    rel = idx[:, None] - idx[None, :]
    dmask = jnp.where(rel[None] >= 0, jnp.exp(lg[:, None, None] * jnp.maximum(rel, 0.0)[None]), 0.0)
    qdec = jnp.exp(lg[:, None] * (idx + 1.0)[None, :])[..., None]
    kdec = jnp.exp(lg[:, None] * (valid - 1.0 - idx)[None, :])[..., None]
    kdec = jnp.where((idx < valid)[None, :, None], kdec, 0.0)
    cdec = jnp.exp(lg * valid)[:, None, None]
    return dmask, qdec, kdec, cdec


def _retention_prompt(qk_r, vg_r, s0, chunk, rows_per_step):
    b, l, _ = qk_r.shape
    h = RET_HEADS
    dmask, qdec, kdec, cdec = _retention_decay(chunk, chunk)
    nsteps = l // rows_per_step
    blk = lambda off: pl.BlockSpec((None, rows_per_step, HEAD_DIM), lambda bi, hi, ti: (bi, ti, hi + off))
    per_head = lambda shape: pl.BlockSpec((None,) + shape, lambda bi, hi, ti: (hi,) + (0,) * len(shape))
    state_spec = pl.BlockSpec((None, None, HEAD_DIM, HEAD_DIM), lambda bi, hi, ti: (bi, hi, 0, 0))
    return pl.pallas_call(
        functools.partial(_retention_kernel, chunk=chunk, nchunks=rows_per_step // chunk),
        out_shape=(jax.ShapeDtypeStruct((b, l, RET_WIDTH), BF16),
                   jax.ShapeDtypeStruct((b, h, HEAD_DIM, HEAD_DIM), F32)),
        grid=(b, h, nsteps),
        in_specs=[blk(0), blk(h), blk(0), blk(h), state_spec,
                  per_head((chunk, chunk)), per_head((chunk, 1)), per_head((chunk, 1)), per_head((1, 1))],
        out_specs=(blk(0), state_spec),
        scratch_shapes=[pltpu.VMEM((HEAD_DIM, HEAD_DIM), F32)],
        compiler_params=_params(("parallel", "parallel", "arbitrary"), 32),
        name="retention_prompt",
    )(qk_r, qk_r, vg_r, vg_r, s0, dmask, qdec, kdec, cdec)


def _retention_sample_kernel(q_ref, k_ref, v_ref, g_ref, s0_ref, dm_ref, qd_ref, kd_ref, cd_ref,
                             y_ref, sf_ref, *, ntok):
    for bi in range(q_ref.shape[0]):
        for h in range(RET_HEADS):
            q = q_ref[bi, h]
            k = k_ref[bi, h]
            v = v_ref[bi, h]
            s_prev = s0_ref[bi, h]
            dmask = dm_ref[h]
            o = jnp.dot(q * qd_ref[h], s_prev, preferred_element_type=F32)
            for j in range(ntok):
                sj = jnp.sum(q * k[j:j + 1, :], axis=-1, keepdims=True) * dmask[:, j:j + 1]
                o = o + sj * v[j:j + 1, :]
            kd = k * kd_ref[h]
            kv = lax.dot_general(kd, v, (((0,), (0,)), ((), ())), preferred_element_type=F32)
            sf_ref[bi, h] = s_prev * cd_ref[h] + kv
            mu = jnp.mean(o, axis=-1, keepdims=True)
            oc = o - mu
            var = jnp.mean(oc * oc, axis=-1, keepdims=True)
            y_ref[bi, h] = g_ref[bi, h] * (oc * lax.rsqrt(var + GN_EPS))


def _retention_sample(q, k, v, g, s0, ntok):
    bd, h, tp, _ = q.shape
    dmask, qdec, kdec, cdec = _retention_decay(tp, ntok)
    nb = math.gcd(bd, SAMPLE_SEQS_PER_STEP)
    tok = pl.BlockSpec((nb, h, tp, HEAD_DIM), lambda bi: (bi, 0, 0, 0))
    st = pl.BlockSpec((nb, h, HEAD_DIM, HEAD_DIM), lambda bi: (bi, 0, 0, 0))
    const = lambda a: pl.BlockSpec(a.shape, lambda bi: (0,) * a.ndim)
    return pl.pallas_call(
        functools.partial(_retention_sample_kernel, ntok=ntok),
        out_shape=(jax.ShapeDtypeStruct((bd, h, tp, HEAD_DIM), F32),
                   jax.ShapeDtypeStruct((bd, h, HEAD_DIM, HEAD_DIM), F32)),
        grid=(bd // nb,),
        in_specs=[tok, tok, tok, tok, st, const(dmask), const(qdec), const(kdec), const(cdec)],
        out_specs=(tok, st),
        compiler_params=_params(("parallel",), 32),
        name="retention_sample",
    )(q, k, v, g, s0, dmask, qdec, kdec, cdec)


def _t5_bucket(dist):
    max_exact = REL_BUCKETS // 2
    d = jnp.maximum(dist, 0)
    df = jnp.maximum(d, 1).astype(F32)
    large = max_exact + (jnp.log(df / max_exact) / math.log(REL_MAX_DIST / max_exact)
                         * (REL_BUCKETS - max_exact)).astype(jnp.int32)
    large = jnp.minimum(large, REL_BUCKETS - 1)
    return jnp.where(d < max_exact, d, large)


def _bias_lookup(tab, dist):
    onehot = _t5_bucket(dist)[..., None] == jnp.arange(REL_BUCKETS)
    return jnp.sum(jnp.where(onehot[..., None], tab.astype(F32), 0.0), axis=-2)


def _dilated_kernel(q_ref, kp_ref, kc_ref, vp_ref, vc_ref, bias_ref, o_ref, lse_ref, kfull, vfull, *, nsub):
    i = pl.program_id(1)
    blk = DIL_BLOCK
    kfull[0:blk, :] = kp_ref[...]
    kfull[blk:, :] = kc_ref[...]
    vfull[0:blk, :] = vp_ref[...]
    vfull[blk:, :] = vc_ref[...]
    col = lax.broadcasted_iota(jnp.int32, (blk, 2 * blk), 1)
    lane_head = lax.broadcasted_iota(jnp.int32, (blk, HEAD_DIM), 1) // LSE_LANES
    for s in range(nsub):
        rows = pl.ds(s * blk, blk)
        win = pl.ds(s * blk, 2 * blk)
        lse_tile = jnp.zeros((blk, HEAD_DIM), F32)
        for h in range(DIL_HPG):
            cols = pl.ds(h * HEAD_DIM, HEAD_DIM)
            q = q_ref[rows, cols]
            kw = kfull[win, cols]
            vw = vfull[win, cols]
            sc = lax.dot_general(q, kw, (((1,), (1,)), ((), ())), preferred_element_type=F32) + bias_ref[h]
            if s == 0:
                sc = jnp.where((col >= blk) | (i > 0), sc, -jnp.inf)
            m = jnp.max(sc, axis=-1, keepdims=True)
            p = jnp.exp(sc - m)
            l = jnp.sum(p, axis=-1, keepdims=True)
            o = jnp.dot(p.astype(BF16), vw, preferred_element_type=F32) / l
            o_ref[rows, cols] = o.astype(o_ref.dtype)
            lse_tile = jnp.where(lane_head == h, m + jnp.log(l), lse_tile)
        lse_ref[rows, :] = lse_tile


def _dilated_bias(bias_tab, r):
    blk = DIL_BLOCK
    qi = jnp.arange(blk)[:, None]
    kj = jnp.arange(2 * blk)[None, :]
    dc = blk + qi - kj
    band = (dc >= 0) & (dc <= blk)
    bias = _bias_lookup(bias_tab, dc * r).transpose(2, 0, 1)
    return jnp.where(band[None], bias, -jnp.inf)


def _dilated_prompt(q, k, v, bias, nsub):
    n, lc, w = q.shape
    blk = DIL_BLOCK
    tq = nsub * blk
    cur = pl.BlockSpec((None, tq, w), lambda ni, i: (ni, i, 0))
    prev = pl.BlockSpec((None, blk, w), lambda ni, i: (ni, jnp.maximum(i * nsub - 1, 0), 0))
    return pl.pallas_call(
        functools.partial(_dilated_kernel, nsub=nsub),
        out_shape=(jax.ShapeDtypeStruct((n, lc, w), F32),
                   jax.ShapeDtypeStruct((n, lc, HEAD_DIM), F32)),
        grid=(n, lc // tq),
        in_specs=[cur, prev, cur, prev, cur, pl.BlockSpec(bias.shape, lambda ni, i: (0, 0, 0))],
        out_specs=(cur, pl.BlockSpec((None, tq, HEAD_DIM), lambda ni, i: (ni, i, 0))),
        scratch_shapes=[pltpu.VMEM((tq + blk, w), BF16), pltpu.VMEM((tq + blk, w), BF16)],
        compiler_params=_params(("parallel", "parallel"), 32),
        name="dilated_prompt",
    )(q, k, k, v, v, bias)


def _dilated_sample_kernel(q_ref, kn_ref, vn_ref, c0_ref, c1_ref, c2_ref, bc_ref, bn_ref,
                           o_ref, lse_ref, *, ntok):
    caches = (c0_ref, c1_ref, c2_ref)
    for gi in range(DIL_GROUPS):
        cache = caches[gi]
        nclass = cache.shape[1]
        for t in range(ntok):
            cls = t if nclass > 1 else 0
            qt = q_ref[t, gi]
            kc = cache[:, cls, 0]
            vc = cache[:, cls, 1]
            sc = jnp.sum(kc * qt[None], axis=-1, keepdims=True) + bc_ref[gi, t]
            sn = jnp.sum(kn_ref[:, gi] * qt[None], axis=-1, keepdims=True) + bn_ref[gi, t]
            m = jnp.maximum(jnp.max(sc, axis=0), jnp.max(sn, axis=0))
            pc = jnp.exp(sc - m[None])
            pn = jnp.exp(sn - m[None])
            l = jnp.sum(pc, axis=0) + jnp.sum(pn, axis=0)
            o = jnp.sum(pc * vc, axis=0) + jnp.sum(pn * vn_ref[:, gi], axis=0)
            o_ref[t, gi] = o / l
            lse_ref[t, gi] = m + jnp.log(l)


def _dilated_sample(q, kn, vn, caches, bias_cache, bias_new, ntok):
    bd = q.shape[0]
    tok = pl.BlockSpec((None, ntok, DIL_GROUPS, DIL_HPG, HEAD_DIM), lambda bi: (bi, 0, 0, 0, 0))
    cspecs = []
    for c in caches:
        ncls = min(c.shape[2], ntok)
        cspecs.append(pl.BlockSpec((None, DIL_BLOCK, ncls, 2, DIL_HPG, HEAD_DIM),
                                   lambda bi: (bi, 0, 0, 0, 0, 0)))
    const = lambda a: pl.BlockSpec(a.shape, lambda bi: (0,) * a.ndim)
    return pl.pallas_call(
        functools.partial(_dilated_sample_kernel, ntok=ntok),
        out_shape=(jax.ShapeDtypeStruct(q.shape, F32),
                   jax.ShapeDtypeStruct((bd, ntok, DIL_GROUPS, DIL_HPG, 1), F32)),
        grid=(bd,),
        in_specs=[tok, tok, tok] + cspecs + [const(bias_cache), const(bias_new)],
        out_specs=(tok, pl.BlockSpec((None, ntok, DIL_GROUPS, DIL_HPG, 1), lambda bi: (bi, 0, 0, 0, 0))),
        compiler_params=_params(("parallel",), 32),
        name="dilated_sample",
    )(q, kn, vn, *caches, bias_cache, bias_new)


def _dilated_sample_bias(rel_bias, ntok):
    bc, bn = [], []
    m = jnp.arange(DIL_BLOCK)
    tn = jnp.arange(ntok)
    for gi, (_, r) in enumerate(DIL_PAIRS):
        tab = rel_bias[:, gi * DIL_HPG:(gi + 1) * DIL_HPG]
        rows_c, rows_n = [], []
        for t in range(ntok):
            if r == 1:
                jc = DIL_BLOCK + t - m
                okc = m >= t
                jn = t - tn
                okn = tn <= t
            else:
                jc = DIL_BLOCK - m
                okc = jnp.ones_like(m, bool)
                jn = jnp.zeros_like(tn)
                okn = tn == t
            rows_c.append(jnp.where(okc[:, None], _bias_lookup(tab, jc * r), -jnp.inf))
            rows_n.append(jnp.where(okn[:, None], _bias_lookup(tab, jn * r), -jnp.inf))
        bc.append(jnp.stack(rows_c))
        bn.append(jnp.stack(rows_n))
    return jnp.stack(bc)[..., None], jnp.stack(bn)[..., None]


def _merge_kernel(o0_ref, o1_ref, o2_ref, l0_ref, l1_ref, l2_ref, yr_ref, gt_ref, wr_ref, wd_ref,
                  out_ref, oa_ref, *nat, dilations):
    tm = out_ref.shape[0]
    o_nat, l_nat = [], []
    for g, (o_ref, l_ref, r) in enumerate(zip((o0_ref, o1_ref, o2_ref), (l0_ref, l1_ref, l2_ref), dilations)):
        if r == 1:
            o_nat.append([o_ref[:, _head_cols(h)] for h in range(DIL_HPG)])
            l_nat.append(l_ref[...])
        else:
            on, ln = nat[2 * g], nat[2 * g + 1]
            for c in range(r):
                ln[pl.ds(c, tm // r, stride=r), :] = l_ref[c]
                for h in range(DIL_HPG):
                    on[h, pl.ds(c, tm // r, stride=r), :] = o_ref[c, :, _head_cols(h)]
            o_nat.append([on[h] for h in range(DIL_HPG)])
            l_nat.append(ln[...])
    m = jnp.maximum(jnp.maximum(l_nat[0], l_nat[1]), l_nat[2])
    es = [jnp.exp(l - m) for l in l_nat]
    den = es[0] + es[1] + es[2]
    for h in range(DIL_HPG):
        lane = h * LSE_LANES
        acc = None
        for g in range(DIL_GROUPS):
            term = (es[g][:, lane:lane + 1] / den[:, lane:lane + 1]) * o_nat[g][h]
            acc = term if acc is None else acc + term
        oa_ref[:, _head_cols(h)] = acc.astype(oa_ref.dtype)
    d = out_ref.shape[1]
    for c0 in range(0, d, PROJ_CHUNK):
        cols = slice(c0, c0 + PROJ_CHUNK)
        yr = jnp.dot(yr_ref[...], wr_ref[:, cols], preferred_element_type=F32)
        ya = jnp.dot(oa_ref[...], wd_ref[:, cols], preferred_element_type=F32)
        g_r = gt_ref[:, cols].astype(F32)
        g_a = gt_ref[:, d + c0:d + c0 + PROJ_CHUNK].astype(F32)
        out_ref[:, cols] = (g_r * yr + g_a * ya).astype(out_ref.dtype)


def _merge(o_g, lse_g, dilations, b, l, yr_in, gates, w_ret_b, w_dil_b, tm):
    n = yr_in.shape[0]
    d = w_ret_b.shape[1]
    bps = l // tm
    row = lambda w: pl.BlockSpec((tm, w), lambda i: (i, 0))
    const = lambda a: pl.BlockSpec(a.shape, lambda i: (0, 0))

    def group_specs(width):
        specs = []
        for r in dilations:
            if r == 1:
                specs.append(row(width))
            else:
                specs.append(pl.BlockSpec((None, r, tm // r, width), lambda i: (i // bps, 0, i % bps, 0)))
        return specs

    view = lambda t, r, width: t.reshape(n, width) if r == 1 else t.reshape(b, r, l // r, width)
    o_in = [view(t, r, DIL_OUT_WIDTH) for t, r in zip(o_g, dilations)]
    l_in = [view(t, r, HEAD_DIM) for t, r in zip(lse_g, dilations)]
    scratch = [pltpu.VMEM((tm, DIL_OUT_WIDTH), BF16)]
    for _ in dilations:
        scratch += [pltpu.VMEM((DIL_HPG, tm, HEAD_DIM), F32), pltpu.VMEM((tm, HEAD_DIM), F32)]
    return pl.pallas_call(
        functools.partial(_merge_kernel, dilations=tuple(dilations)),
        out_shape=jax.ShapeDtypeStruct((n, d), BF16),
        grid=(n // tm,),
        in_specs=group_specs(DIL_OUT_WIDTH) + group_specs(HEAD_DIM) + [row(RET_WIDTH), row(2 * d),
                                                                      const(w_ret_b), const(w_dil_b)],
        out_specs=row(d),
        scratch_shapes=scratch,
        compiler_params=_params(("parallel",), 48),
        name="merge_branches",
    )(*o_in, *l_in, yr_in, gates, w_ret_b, w_dil_b)


ROUTER_ROWS = 8 + N_EXPERTS


def _route_select(lt):
    lg = lt[0:N_GROUPS]
    gmax = jnp.max(lg, axis=0, keepdims=True)
    w_coarse = 1.0 / jnp.sum(jnp.exp(lg - gmax), axis=0, keepdims=True)
    gid = lax.broadcasted_iota(jnp.int32, lg.shape, 0)
    gsel = jnp.min(jnp.where(lg == gmax, gid, N_GROUPS), axis=0, keepdims=True)
    le = jnp.zeros((EXPERTS_PER_GROUP, lt.shape[1]), F32)
    for g in range(N_GROUPS):
        le = jnp.where(gsel == g, lt[8 + g * EXPERTS_PER_GROUP:8 + (g + 1) * EXPERTS_PER_GROUP], le)
    eid = lax.broadcasted_iota(jnp.int32, le.shape, 0)
    v1 = jnp.max(le, axis=0, keepdims=True)
    i1 = jnp.min(jnp.where(le == v1, eid, EXPERTS_PER_GROUP), axis=0, keepdims=True)
    le2 = jnp.where(eid == i1, -jnp.inf, le)
    v2 = jnp.max(le2, axis=0, keepdims=True)
    i2 = jnp.min(jnp.where(le2 == v2, eid, EXPERTS_PER_GROUP), axis=0, keepdims=True)
    e21 = jnp.exp(v2 - v1)
    w1 = w_coarse / (1.0 + e21)
    w2 = w_coarse * e21 / (1.0 + e21)
    e1 = gsel * EXPERTS_PER_GROUP + i1
    e2 = gsel * EXPERTS_PER_GROUP + i2
    xid = lax.broadcasted_iota(jnp.int32, (N_EXPERTS, lt.shape[1]), 0)
    oh1 = jnp.where(xid == e1, 1.0, 0.0)
    oh2 = jnp.where(xid == e2, 1.0, 0.0)
    return e1, e2, w1, w2, oh1, oh2


def _route_rank(sel, p1, p2, before):
    e1, e2, w1, w2, oh1, oh2 = sel
    n1 = jnp.sum(oh1, axis=1, keepdims=True)
    n2 = jnp.sum(oh2, axis=1, keepdims=True)
    rank1 = jnp.sum(oh1 * (p1 + before), axis=0, keepdims=True)
    rank2 = jnp.sum(oh2 * (p2 + before + n1), axis=0, keepdims=True)
    row = lax.broadcasted_iota(jnp.int32, (8, e1.shape[1]), 0)
    ei = jnp.where(row == 0, e1, jnp.where(row == 1, e2, jnp.where(
        row == 2, rank1.astype(jnp.int32), jnp.where(row == 3, rank2.astype(jnp.int32), 0))))
    wt = jnp.where(row == 0, w1, jnp.where(row == 1, w2, 0.0))
    return ei, wt, before + n1 + n2


def _outproj_router_kernel(mg_ref, x_ref, wo_ref, g2_ref, wr_ref, br_ref, tri_ref, cin_ref,
                           h_ref, u2_ref, ei_ref, wt_ref, cnt_ref, h_keep):
    i = pl.program_id(0)

    @pl.when(i == 0)
    def _():
        cnt_ref[...] = cin_ref[...]
        h_keep[...] = jnp.zeros_like(h_keep)

    sub = tri_ref.shape[0]
    subtiles = [slice(s0, s0 + sub) for s0 in range(0, h_ref.shape[0], sub)]

    def project(rows):
        h_ref[rows, :] = x_ref[rows, :] + jnp.dot(mg_ref[rows, :], wo_ref[...], preferred_element_type=F32)

    project(subtiles[0])
    logits = []
    for rows in subtiles:
        h = h_keep[rows, :]
        u2 = h * lax.rsqrt(jnp.mean(h * h, axis=-1, keepdims=True) + NORM_EPS) * g2_ref[...]
        u2_ref[rows, :] = u2
        logits.append(lax.dot_general(wr_ref[...], u2.astype(BF16), (((1,), (1,)), ((), ())),
                                      preferred_element_type=F32) + br_ref[...])
    for rows in subtiles[1:]:
        project(rows)
    selected = [_route_select(lt) for lt in logits]
    prefix = [(jnp.dot(sel[4].astype(BF16), tri_ref[...], preferred_element_type=F32),
               jnp.dot(sel[5].astype(BF16), tri_ref[...], preferred_element_type=F32)) for sel in selected]
    h_keep[...] = h_ref[...]
    counts = cnt_ref[...]
    for rows, sel, (p1, p2) in zip(subtiles, selected, prefix):
        ei, wt, routed = _route_rank(sel, p1, p2, counts)
        counts = jnp.where(i > 0, routed, counts)
        ei_ref[:, rows] = ei
        wt_ref[:, rows] = wt
    cnt_ref[...] = counts


def _outproj_router(merged, x, w_o_b, norm_ffn, wr_t, br_t, counts_in, tm):
    n, d = x.shape
    nsteps = n // tm
    sub = min(tm, ROUTER_SUB)
    tri = (jnp.arange(sub)[:, None] < jnp.arange(sub)[None, :]).astype(BF16)
    matmul_tile = lambda: pl.BlockSpec((tm, d), lambda i: (jnp.minimum(i, nsteps - 1), 0))
    routed_rows = lambda: pl.BlockSpec((tm, d), lambda i: (jnp.maximum(i - 1, 0), 0))
    routed_lanes = lambda: pl.BlockSpec((8, tm), lambda i: (0, jnp.maximum(i - 1, 0)))
    const = lambda a: pl.BlockSpec(a.shape, lambda i: (0, 0))
    return pl.pallas_call(
        _outproj_router_kernel,
        out_shape=(jax.ShapeDtypeStruct((n, d), F32), jax.ShapeDtypeStruct((n, d), F32),
                   jax.ShapeDtypeStruct((8, n), jnp.int32), jax.ShapeDtypeStruct((8, n), F32),
                   jax.ShapeDtypeStruct((N_EXPERTS, 1), F32)),
        grid=(nsteps + 1,),
        in_specs=[matmul_tile(), matmul_tile(), const(w_o_b), pl.BlockSpec((1, d), lambda i: (0, 0)),
                  const(wr_t), const(br_t), const(tri), const(counts_in)],
        out_specs=(matmul_tile(), routed_rows(), routed_lanes(), routed_lanes(), const(counts_in)),
        scratch_shapes=[pltpu.VMEM((tm, d), F32)],
        compiler_params=_params(("arbitrary",), 60),
        name="outproj_router",
    )(merged, x, w_o_b, norm_ffn.reshape(1, d), wr_t, br_t, tri, counts_in)


def _router_weights(w_rg, b_rg, w_re, b_re):
    d = w_rg.shape[0]
    wr = jnp.zeros((ROUTER_ROWS, d), F32)
    wr = wr.at[0:N_GROUPS].set(w_rg.T)
    wr = wr.at[8:].set(w_re.transpose(0, 2, 1).reshape(N_EXPERTS, d))
    br = jnp.zeros((ROUTER_ROWS, 1), F32)
    br = br.at[0:N_GROUPS, 0].set(b_rg)
    br = br.at[8:, 0].set(b_re.reshape(N_EXPERTS))
    return wr.astype(BF16), br


DMA_UNROLL = 8


def _dispatch_kernel(pos_p_ref, pos_s_ref, zero_ref, up_ref, us_ref, xs_ref, zbuf, stage, sem, *, tm):
    i = pl.program_id(0)
    last = pl.num_programs(0) - 1
    ntiles = xs_ref.shape[0] // MOE_TILE

    def tile_copy(t):
        return pltpu.make_async_copy(zbuf, xs_ref.at[pl.ds(pl.multiple_of(t * MOE_TILE, MOE_TILE), MOE_TILE)],
                                     sem.at[2])

    @pl.when(i == 0)
    def _():
        zbuf[...] = jnp.zeros_like(zbuf)
        for op in ("start", "wait"):
            def per_tile(t, c, op=op):
                @pl.when(zero_ref[t] != 0)
                def _():
                    getattr(tile_copy(t), op)()
                return c

            lax.fori_loop(0, ntiles, per_tile, 0)

    def scatter_rows(op, src_ref, pos_ref, base, nrows, ntok_total, row_sem):
        if op == "wait":
            for _ in range(TOP_K):
                pltpu.make_async_copy(src_ref, xs_ref.at[pl.ds(0, nrows)], row_sem).wait()
            return

        def body(r, c):
            for kk in range(TOP_K):
                slot = pos_ref[kk * ntok_total + base + r]
                pltpu.make_async_copy(src_ref.at[pl.ds(r, 1)], xs_ref.at[pl.ds(slot, 1)],
                                      row_sem).start(priority=kk % 2)
            return c

        lax.fori_loop(0, nrows, body, 0, unroll=DMA_UNROLL)

    prompt_rows = last * tm
    cur = i % 2

    @pl.when(i < last)
    def _():
        stage[cur] = up_ref[...]
        scatter_rows("start", stage.at[cur], pos_p_ref, i * tm, tm, prompt_rows, sem.at[cur])

    @pl.when(i > 0)
    def _():
        scatter_rows("wait", stage.at[1 - cur], pos_p_ref, (i - 1) * tm, tm, prompt_rows, sem.at[1 - cur])

    @pl.when(i == last)
    def _():
        for op in ("start", "wait"):
            scatter_rows(op, us_ref, pos_s_ref, 0, us_ref.shape[0], us_ref.shape[0], sem.at[2])


def _dispatch(pos_p, pos_s, zero_tiles, u2p, u2s, tm):
    n, d = u2p.shape
    nsteps = n // tm
    grid_spec = pltpu.PrefetchScalarGridSpec(
        num_scalar_prefetch=3,
        grid=(nsteps + 1,),
        in_specs=[pl.BlockSpec((tm, d), lambda i, *_: (jnp.minimum(i, nsteps - 1), 0)),
                  pl.BlockSpec(u2s.shape, lambda i, *_: (0, 0))],
        out_specs=pl.BlockSpec(memory_space=pl.ANY),
        scratch_shapes=[pltpu.VMEM((MOE_TILE, d), F32), pltpu.VMEM((2, tm, d), F32),
                        pltpu.SemaphoreType.DMA((3,))],
    )
    return pl.pallas_call(
        functools.partial(_dispatch_kernel, tm=tm),
        out_shape=jax.ShapeDtypeStruct((zero_tiles.shape[0] * MOE_TILE, d), F32),
        grid_spec=grid_spec,
        compiler_params=_params(("arbitrary",), 40),
        name="moe_dispatch",
    )(pos_p, pos_s, zero_tiles, u2p, u2s)


def _gmm_kernel(te_ref, nt_ref, x_ref, wg_ref, wu_ref, wd_ref, o_ref, wgb, wub, wdb):
    i = pl.program_id(0)
    fresh = jnp.logical_or(i == 0, te_ref[i] != te_ref[jnp.maximum(i - 1, 0)])

    @pl.when(jnp.logical_and(fresh, i < nt_ref[0]))
    def _():
        wgb[...] = wg_ref[...].astype(BF16)
        wub[...] = wu_ref[...].astype(BF16)
        wdb[...] = wd_ref[...].astype(BF16)

    @pl.when(i < nt_ref[0])
    def _():
        x = x_ref[...].astype(BF16)
        acc = None
        for c0 in range(0, wgb.shape[1], PROJ_CHUNK):
            cols = slice(c0, c0 + PROJ_CHUNK)
            a = jnp.dot(x, wgb[:, cols], preferred_element_type=F32)
            b = jnp.dot(x, wub[:, cols], preferred_element_type=F32)
            hm = (a * jax.nn.sigmoid(a) * b).astype(BF16)
            part = jnp.dot(hm, wdb[cols, :], preferred_element_type=F32)
            acc = part if acc is None else acc + part
        o_ref[...] = acc

    @pl.when(i >= nt_ref[0])
    def _():
        o_ref[...] = jnp.zeros_like(o_ref)


def _gmm(tile_expert, ntiles_used, xs, w_gate, w_up, w_down):
    npad, d = xs.shape
    f = w_gate.shape[-1]
    ntiles = npad // MOE_TILE
    epg = w_gate.shape[1]

    def xmap(i, te, nt):
        return (jnp.minimum(i, nt[0] - 1), 0)

    def wmap(i, te, nt):
        e = te[i]
        return (e // epg, e % epg, 0, 0)

    grid_spec = pltpu.PrefetchScalarGridSpec(
        num_scalar_prefetch=2,
        grid=(ntiles,),
        in_specs=[pl.BlockSpec((MOE_TILE, d), xmap),
                  pl.BlockSpec((None, None, d, f), wmap),
                  pl.BlockSpec((None, None, d, f), wmap),
                  pl.BlockSpec((None, None, f, d), wmap)],
        out_specs=pl.BlockSpec((MOE_TILE, d), lambda i, te, nt: (i, 0)),
        scratch_shapes=[pltpu.VMEM((d, f), BF16), pltpu.VMEM((d, f), BF16), pltpu.VMEM((f, d), BF16)],
    )
    return pl.pallas_call(
        _gmm_kernel,
        out_shape=jax.ShapeDtypeStruct((npad, d), F32),
        grid_spec=grid_spec,
        compiler_params=_params(("arbitrary",), 56),
        name="moe_grouped_matmul",
    )(tile_expert, ntiles_used, xs, w_gate, w_up, w_down)


def _combine_kernel(pos_ref, h_ref, wt_ref, os_ref, y_ref, g0, g1, sem, *, tm, ntok_total):
    i = pl.program_id(0)
    nsteps = pl.num_programs(0)
    cur = i % 2

    def row_copy(step, r, kk, slot):
        src = pos_ref[kk * ntok_total + step * tm + r]
        buf = (g0, g1)[kk]
        return pltpu.make_async_copy(os_ref.at[pl.ds(src, 1)], buf.at[slot, pl.ds(r, 1)], sem.at[slot])

    def issue_step(step, slot):
        def body(r, c):
            row_copy(step, r, 0, slot).start(priority=0)
            row_copy(step, r, 1, slot).start(priority=1)
            return c

        lax.fori_loop(0, tm, body, 0, unroll=DMA_UNROLL)

    @pl.when(i == 0)
    def _():
        issue_step(0, 0)

    @pl.when(i + 1 < nsteps)
    def _():
        issue_step(i + 1, 1 - cur)

    for buf in (g0, g1):
        pltpu.make_async_copy(os_ref.at[pl.ds(0, tm)], buf.at[cur], sem.at[cur]).wait()
    y_ref[...] = h_ref[...] + wt_ref[:, 0:1] * g0[cur] + wt_ref[:, 1:2] * g1[cur]


def _combine(pos_flat, h, wt_cols, out_sorted, tm):
    n, d = h.shape
    grid_spec = pltpu.PrefetchScalarGridSpec(
        num_scalar_prefetch=1,
        grid=(n // tm,),
        in_specs=[pl.BlockSpec((tm, d), lambda i, pos: (i, 0)),
                  pl.BlockSpec((tm, 8), lambda i, pos: (i, 0)),
                  pl.BlockSpec(memory_space=pl.ANY)],
        out_specs=pl.BlockSpec((tm, d), lambda i, pos: (i, 0)),
        scratch_shapes=[pltpu.VMEM((2, tm, d), F32), pltpu.VMEM((2, tm, d), F32),
                        pltpu.SemaphoreType.DMA((2,))],
    )
    return pl.pallas_call(
        functools.partial(_combine_kernel, tm=tm, ntok_total=n),
        out_shape=jax.ShapeDtypeStruct((n, d), F32),
        grid_spec=grid_spec,
        compiler_params=_params(("arbitrary",), 48),
        name="moe_combine",
    )(pos_flat, h, wt_cols, out_sorted)


def _moe_plan(counts, route_sets):
    npairs = sum(e.shape[1] for e, _ in route_sets) * TOP_K
    tiles = (counts + MOE_TILE - 1) // MOE_TILE
    tile_end = jnp.cumsum(tiles)
    offs = (tile_end - tiles) * MOE_TILE
    ids = jnp.arange(N_EXPERTS)
    slots = []
    for experts, ranks in route_sets:
        base = jnp.sum(jnp.where(experts[..., None] == ids, offs, 0), axis=-1)
        slots.append((base + ranks).reshape(-1).astype(jnp.int32))
    ntiles = (npairs + N_EXPERTS * (MOE_TILE - 1)) // MOE_TILE
    tile_expert = jnp.sum(tile_end[None, :] <= jnp.arange(ntiles)[:, None], axis=1)
    tile_expert = jnp.minimum(tile_expert, N_EXPERTS - 1).astype(jnp.int32)
    tile_ids = jnp.arange(ntiles)
    is_last = jnp.any((tile_ids[:, None] == tile_end[None, :] - 1) & (tiles[None, :] > 0), axis=1)
    zero_tiles = (is_last | (tile_ids >= tile_end[-1])).astype(jnp.int32)
    return slots, tile_expert, tile_end[-1:].astype(jnp.int32), zero_tiles


def _mixers_prompt(x, wts, counts_in, shift_jobs):
    b, l, d = x.shape
    n = b * l
    tm = ROW_TILE
    xf = x.reshape(n, d)
    u = _rmsnorm(xf, wts["norm_attn"], 2 * tm)
    tab = _rotary_tables(jnp.arange(l), l)
    (qk_r, vg_r, gates), shifted = _proj_retention_and_gates(u, wts["w_in"], tab, 2 * tm, shift_jobs)

    s0 = jnp.zeros((b, RET_HEADS, HEAD_DIM, HEAD_DIM), F32)
    yr_in, s_fin = _retention_prompt(qk_r.reshape(b, l, -1), vg_r.reshape(b, l, -1), s0, 256, min(l, 2048))

    o_g, lse_g, bufs = [], [], []
    for gi, (w, r) in enumerate(DIL_PAIRS):
        qg, kg, vg, kt, vt = _proj_dilated_prompt(u, wts["w_in"], wts["q_norm"], wts["k_norm"], gi, b, l, tm)
        bias = _dilated_bias(wts["rel_bias"][:, gi * DIL_HPG:(gi + 1) * DIL_HPG], r)
        o, lse = _dilated_prompt(qg, kg, vg, bias, min(8, l // r // DIL_BLOCK))
        o_g.append(o)
        lse_g.append(lse)
        bufs.append(jnp.stack([kt, vt], axis=2).reshape(b, kt.shape[1], 2, DIL_HPG, HEAD_DIM))

    merged = _merge(o_g, lse_g, [r for _, r in DIL_PAIRS], b, l, yr_in.reshape(n, RET_WIDTH), gates,
                    wts["w_ret_out"], wts["w_dil_out"], tm)
    h, u2, ei, wt, counts = _outproj_router(merged, xf, wts["w_o"], wts["norm_ffn"], wts["wr_t"], wts["br_t"],
                                            counts_in, ROUTER_TILE)
    return h, u2, ei, wt, counts, s_fin, bufs, shifted


def _mixers_sample(x, caches, state, wts, counts_in):
    bd, t, d = x.shape
    n = bd * t
    xf = x.reshape(n, d)
    u = _rmsnorm(xf, wts["norm_attn"], n)
    tab = _rotary_tables(PAST_LEN + jnp.arange(t), n)
    (qk_r, vg_r, gates), _ = _proj_retention_and_gates(u, wts["w_in"], tab, n)
    qkv_a = _proj_dilated_sample(u, wts["w_in"], wts["q_norm"], wts["k_norm"])

    def heads(a):
        a = a.astype(F32).reshape(bd, t, RET_HEADS, HEAD_DIM).transpose(0, 2, 1, 3)
        return jnp.pad(a, ((0, 0), (0, 0), (0, 8 - t), (0, 0)))

    y_r, s_new = _retention_sample(heads(qk_r[:, :RET_WIDTH]), heads(qk_r[:, RET_WIDTH:]),
                                   heads(vg_r[:, :RET_WIDTH]), heads(vg_r[:, RET_WIDTH:]), state, t)
    yr_in = y_r[:, :, :t].transpose(0, 2, 1, 3).reshape(n, RET_WIDTH).astype(BF16)

    grp = lambda a: a.reshape(bd, t, DIL_GROUPS, DIL_HPG, HEAD_DIM)
    qa, ka, va = (grp(qkv_a[:, s * DIL_WIDTH:(s + 1) * DIL_WIDTH]) for s in range(3))
    cviews = [c.reshape(bd, DIL_BLOCK, r, 2, DIL_HPG, HEAD_DIM) for c, (_, r) in zip(caches, DIL_PAIRS)]
    bias_c, bias_n = _dilated_sample_bias(wts["rel_bias"], t)
    o, lse = _dilated_sample(qa, ka, va, cviews, bias_c, bias_n, t)
    o_g = [o[:, :, gi].reshape(n, 1, DIL_OUT_WIDTH) for gi in range(DIL_GROUPS)]
    lse_g = [jnp.repeat(lse[:, :, gi].reshape(n, DIL_HPG), LSE_LANES, axis=1).reshape(n, 1, HEAD_DIM)
             for gi in range(DIL_GROUPS)]

    news = [jnp.stack([ka[:, :, gi], va[:, :, gi]], axis=2) for gi in range(DIL_GROUPS)]
    shift_jobs = list(zip(caches, news))

    merged = _merge(o_g, lse_g, [1] * DIL_GROUPS, 1, n, yr_in, gates, wts["w_ret_out"], wts["w_dil_out"], n)
    h, u2, ei, wt, counts = _outproj_router(merged, xf, wts["w_o"], wts["norm_ffn"], wts["wr_t"], wts["br_t"],
                                            counts_in, n)
    return h, u2, ei, wt, counts, s_new, shift_jobs


def kernel(x_prompt, x_sample, cache_kv_g0, cache_kv_g1, cache_kv_g2, state_ret, norm_attn, w_in, q_norm,
           k_norm, rel_bias, w_ret_out, w_dil_out, w_o, norm_ffn, w_router_group, b_router_group,
           w_router_expert, b_router_expert, w_gate, w_up, w_down):
    caches = (cache_kv_g0, cache_kv_g1, cache_kv_g2)
    ntok = x_sample.shape[1]
    for c, (w, r) in zip(caches, DIL_PAIRS):
        assert c.shape[1] == w == DIL_BLOCK * r and (r == 1 or ntok <= r) and ntok <= 8
    wr_t, br_t = _router_weights(w_router_group, b_router_group, w_router_expert, b_router_expert)
    wts = dict(norm_attn=norm_attn, w_in=w_in, q_norm=q_norm, k_norm=k_norm, rel_bias=rel_bias,
               w_ret_out=w_ret_out.astype(BF16), w_dil_out=w_dil_out.astype(BF16), w_o=w_o.astype(BF16),
               norm_ffn=norm_ffn, wr_t=wr_t, br_t=br_t)

    zero_counts = jnp.zeros((N_EXPERTS, 1), F32)
    hs, u2s, eis, wts_s, counts_s, s_s, shift_jobs = _mixers_sample(x_sample, caches, state_ret, wts, zero_counts)
    hp, u2p, eip, wtp, counts, s_p, bufs_p, bufs_s = _mixers_prompt(x_prompt, wts, counts_s, shift_jobs)

    slots, tile_expert, ntiles_used, zero_tiles = _moe_plan(
        counts[:, 0].astype(jnp.int32), [(eip[0:2], eip[2:4]), (eis[0:2], eis[2:4])])
    xs = _dispatch(slots[0], slots[1], zero_tiles, u2p, u2s, ROW_TILE)
    out_sorted = _gmm(tile_expert, ntiles_used, xs, w_gate, w_up, w_down)
    yp = _combine(slots[0], hp, wtp.T, out_sorted, ROW_TILE)
    ys = _combine(slots[1], hs, wts_s.T, out_sorted, hs.shape[0])

    return (yp.reshape(x_prompt.shape), ys.reshape(x_sample.shape), bufs_p[0], bufs_p[1], bufs_p[2], s_p,
            bufs_s[0], bufs_s[1], bufs_s[2], s_s)
```

```python
import functools
import math

import jax
import jax.numpy as jnp
from jax import lax
from jax.experimental import pallas as pl
from jax.experimental.pallas import tpu as pltpu

HEAD_DIM = 128
RET_HEADS = 8
RET_WIDTH = RET_HEADS * HEAD_DIM
ROPE_BASE = 10000.0
GN_EPS = 1e-5
DIL_PAIRS = ((128, 1), (512, 4), (2048, 16))
DIL_GROUPS = len(DIL_PAIRS)
DIL_HPG = 4
DIL_HEADS = DIL_HPG * DIL_GROUPS
DIL_WIDTH = DIL_HEADS * HEAD_DIM
DIL_OUT_WIDTH = DIL_HPG * HEAD_DIM
DIL_BLOCK = 128
LSE_LANES = HEAD_DIM // DIL_HPG
ATTN_SCALE = HEAD_DIM ** -0.5
REL_BUCKETS = 32
REL_MAX_DIST = 2048
N_GROUPS = 4
EXPERTS_PER_GROUP = 8
N_EXPERTS = N_GROUPS * EXPERTS_PER_GROUP
TOP_K = 2
NORM_EPS = 1e-6
PAST_LEN = 16384

COL_QR = 0
COL_KR = COL_QR + RET_WIDTH
COL_VR = COL_KR + RET_WIDTH
COL_GR = COL_VR + RET_WIDTH
COL_QA = COL_GR + RET_WIDTH
COL_KA = COL_QA + DIL_WIDTH
COL_VA = COL_KA + DIL_WIDTH
COL_GATES = COL_VA + DIL_WIDTH

MOE_TILE = 256
ROW_TILE = 512
PROJ_CHUNK = 256
ROUTER_TILE = 512
ROUTER_SUB = 256
SAMPLE_SEQS_PER_STEP = 1
MIB = 1 << 20
BF16 = jnp.bfloat16
F32 = jnp.float32


def _params(semantics, vmem_mib):
    return pltpu.CompilerParams(dimension_semantics=semantics, vmem_limit_bytes=vmem_mib * MIB)


def _head_cols(h):
    return slice(h * HEAD_DIM, (h + 1) * HEAD_DIM)


def _rmsnorm_kernel(x_ref, g_ref, o_ref):
    x = x_ref[...]
    y = x * lax.rsqrt(jnp.mean(x * x, axis=-1, keepdims=True) + NORM_EPS)
    o_ref[...] = (y * g_ref[...]).astype(o_ref.dtype)


def _rmsnorm(x, g, tm):
    n, d = x.shape
    return pl.pallas_call(
        _rmsnorm_kernel,
        out_shape=jax.ShapeDtypeStruct((n, d), BF16),
        grid=(n // tm,),
        in_specs=[pl.BlockSpec((tm, d), lambda i: (i, 0)), pl.BlockSpec((1, d), lambda i: (0, 0))],
        out_specs=pl.BlockSpec((tm, d), lambda i: (i, 0)),
        compiler_params=_params(("parallel",), 40),
        name="rmsnorm",
    )(x, g.reshape(1, d))


def _background_shift_step(step, nsteps, src_ref, new_ref, dst_ref, ring, sems, *, chunks_per_seq, rows):
    shift = new_ref.shape[1]

    def chunk_copy(s, slot, inbound):
        b = s // chunks_per_seq
        r0 = (s % chunks_per_seq) * rows
        if inbound:
            return pltpu.make_async_copy(src_ref.at[b, pl.ds(pl.multiple_of(r0 + shift, 8), rows)],
                                         ring.at[slot], sems.at[0, slot])
        return pltpu.make_async_copy(ring.at[slot], dst_ref.at[b, pl.ds(pl.multiple_of(r0, 8), rows)],
                                     sems.at[1, slot])

    cur = step % 2

    @pl.when(step >= 2)
    def _():
        chunk_copy(step - 2, cur, False).wait()

    chunk_copy(step, cur, True).start()

    @pl.when(step >= 1)
    def _():
        chunk_copy(step - 1, 1 - cur, True).wait()
        chunk_copy(step - 1, 1 - cur, False).start()

    @pl.when(step == nsteps - 1)
    def _():
        tail = pltpu.make_async_copy(new_ref, dst_ref.at[:, pl.ds(dst_ref.shape[1] - shift, shift)], sems.at[2, 0])
        tail.start()
        chunk_copy(step, cur, True).wait()
        chunk_copy(step, cur, False).start()
        chunk_copy(step - 1, 1 - cur, False).wait()
        chunk_copy(step, cur, False).wait()
        tail.wait()


def _proj_kernel(u_ref, *refs, epilogue, n_w, n_extra, n_out, shift):
    w_refs, refs = refs[:n_w], refs[n_w:]
    extras, refs = refs[:n_extra], refs[n_extra:]
    if shift is not None:
        (src_ref, new_ref), refs = refs[:2], refs[2:]
    outs, refs = refs[:n_out], refs[n_out:]
    if shift is not None:
        dst_ref, (ring, sems), refs = refs[0], refs[-2:], refs[1:-2]
        nrow_blocks = pl.num_programs(1)
        _background_shift_step(pl.program_id(0) * nrow_blocks + pl.program_id(1), pl.num_programs(0) * nrow_blocks,
                               src_ref, new_ref, dst_ref, ring, sems, **shift)
    scratch, wb_refs = refs[:len(refs) - n_w], refs[len(refs) - n_w:]

    @pl.when(pl.program_id(1) == 0)
    def _():
        for w_ref, wb_ref in zip(w_refs, wb_refs):
            wb_ref[...] = w_ref[...].astype(BF16)

    tn = wb_refs[0].shape[1]
    for wi, wb_ref in enumerate(wb_refs):
        for c0 in range(0, tn, PROJ_CHUNK):
            acc = jnp.dot(u_ref[...], wb_ref[:, c0:c0 + PROJ_CHUNK], preferred_element_type=F32)
            epilogue(wi, c0, acc, *extras, *outs, *scratch)


def _proj(u, w_in, col_offsets, ncol_blocks, tn, tm, epilogue, extra, extra_specs, out_shape, out_specs,
          scratch, name, shift_job=None, vmem_mib=58):
    n, k = u.shape
    w_specs = [pl.BlockSpec((pl.Element(k), pl.Element(tn)),
                            functools.partial(lambda j, i, o: (0, pl.multiple_of(o + j * tn, HEAD_DIM)), o=o))
               for o in col_offsets]
    multi = isinstance(out_shape, (tuple, list))
    out_shapes = list(out_shape) if multi else [out_shape]
    out_spec_list = list(out_specs) if multi else [out_specs]
    n_out = len(out_shapes)
    side_in, side_in_specs, side_scratch, shift = [], [], [], None
    if shift_job is not None:
        cache, new = shift_job
        shape = cache.shape
        bd = shape[0]
        cache = cache.reshape(bd, -1, HEAD_DIM)
        new = new.reshape(bd, -1, HEAD_DIM)
        nsteps = ncol_blocks * (n // tm)
        chunks_per_seq = nsteps // bd
        rows = (cache.shape[1] - new.shape[1]) // chunks_per_seq
        assert chunks_per_seq * bd == nsteps and rows % 8 == 0
        assert rows * chunks_per_seq == cache.shape[1] - new.shape[1]
        shift = dict(chunks_per_seq=chunks_per_seq, rows=rows)
        any_spec = pl.BlockSpec(memory_space=pl.ANY)
        side_in, side_in_specs = [cache, new], [any_spec, pl.BlockSpec(new.shape, lambda j, i: (0, 0, 0))]
        out_shapes.append(jax.ShapeDtypeStruct(cache.shape, cache.dtype))
        out_spec_list.append(any_spec)
        side_scratch = [pltpu.VMEM((2, rows, HEAD_DIM), F32), pltpu.SemaphoreType.DMA((3, 2))]
    outs = pl.pallas_call(
        functools.partial(_proj_kernel, epilogue=epilogue, n_w=len(col_offsets), n_extra=len(extra), n_out=n_out,
                          shift=shift),
        out_shape=tuple(out_shapes) if (multi or shift) else out_shapes[0],
        grid=(ncol_blocks, n // tm),
        in_specs=[pl.BlockSpec((tm, k), lambda j, i: (i, 0))] + w_specs + list(extra_specs) + side_in_specs,
        out_specs=tuple(out_spec_list) if (multi or shift) else out_spec_list[0],
        scratch_shapes=list(scratch) + [pltpu.VMEM((k, tn), BF16) for _ in col_offsets] + side_scratch,
        compiler_params=_params(("arbitrary", "arbitrary"), vmem_mib),
        name=name,
    )(u, *([w_in] * len(col_offsets)), *extra, *side_in)
    if shift is None:
        return outs
    shifted = outs[-1].reshape(shape)
    return (tuple(outs[:-1]) if multi else outs[0]), shifted


def _chunk_heads(c0, acc):
    return [(slice(c0 + h * HEAD_DIM, c0 + (h + 1) * HEAD_DIM), acc[:, _head_cols(h)])
            for h in range(acc.shape[1] // HEAD_DIM)]


def _epi_rotary(wi, c0, acc, tab_ref, o_ref):
    c = tab_ref[0]
    s = tab_ref[1]
    for cols, xh in _chunk_heads(c0, acc):
        o_ref[:, cols] = (xh * c + pltpu.roll(xh, HEAD_DIM // 2, 1) * s).astype(o_ref.dtype)


def _epi_value_gate(wi, c0, acc, o_ref):
    is_gate = pl.program_id(0) == 1
    o_ref[:, c0:c0 + acc.shape[1]] = jnp.where(is_gate, acc * jax.nn.sigmoid(acc), acc).astype(o_ref.dtype)


def _epi_sigmoid(wi, c0, acc, o_ref):
    o_ref[:, c0:c0 + acc.shape[1]] = jax.nn.sigmoid(acc).astype(o_ref.dtype)


def _head_rms(xh, g):
    return xh * lax.rsqrt(jnp.mean(xh * xh, axis=-1, keepdims=True) + NORM_EPS) * g


def _epi_dilated_prompt(wi, c0, acc, qn_ref, kn_ref, q_ref, k_ref, v_ref, kt_ref, vt_ref, scr, *, r):
    tm = scr.shape[2]
    tail_rows = kt_ref.shape[0]
    gain = (qn_ref[...] * ATTN_SCALE, kn_ref[...], None)[wi]
    out_ref = (q_ref, k_ref, v_ref)[wi]
    tail_ref = (None, kt_ref, vt_ref)[wi]
    for cols, xh in _chunk_heads(c0, acc):
        h = cols.start // HEAD_DIM
        y = xh if gain is None else _head_rms(xh, gain)
        slab = scr.at[wi, h]
        slab[...] = y
        if tail_ref is not None:
            tail_ref[:, cols] = y[tm - tail_rows:tm, :]
        if r == 1:
            out_ref[:, cols] = y.astype(out_ref.dtype)
        else:
            for c in range(r):
                out_ref[c, :, cols] = slab[pl.ds(c, tm // r, stride=r), :].astype(out_ref.dtype)


def _epi_dilated_sample(wi, c0, acc, qn_ref, kn_ref, o_ref):
    j = pl.program_id(0)
    gain = jnp.where(j < DIL_GROUPS, qn_ref[...] * ATTN_SCALE, kn_ref[...])
    for cols, xh in _chunk_heads(c0, acc):
        o_ref[:, cols] = jnp.where(j < 2 * DIL_GROUPS, _head_rms(xh, gain), xh)


def _rotary_tables(pos, rows):
    half = HEAD_DIM // 2
    inv = ROPE_BASE ** (-jnp.arange(half, dtype=F32) / half)
    ang = pos.astype(F32)[:, None] * inv[None, :]
    cos = jnp.cos(ang)
    sin = jnp.sin(ang)
    c = jnp.concatenate([cos, cos], axis=-1)
    s = jnp.concatenate([-sin, sin], axis=-1)
    tab = jnp.stack([jnp.stack([c, s]), jnp.stack([c, s]) * (HEAD_DIM ** -0.5)])
    reps = rows // pos.shape[0]
    return jnp.tile(tab, (1, 1, reps, 1))


def _proj_retention_and_gates(u, w_in, pos_tab, tm, shift_jobs=(None, None, None)):
    n = u.shape[0]
    nblk = pos_tab.shape[2] // tm
    tile_out = lambda: pl.BlockSpec((tm, RET_WIDTH), lambda j, i: (i, j))
    ngate = w_in.shape[1] - COL_GATES
    qk_r = _proj(u, w_in, [COL_QR], 2, RET_WIDTH, tm, _epi_rotary, [pos_tab],
                 [pl.BlockSpec((None, 2, tm, HEAD_DIM), lambda j, i: (j, 0, i % nblk, 0))],
                 jax.ShapeDtypeStruct((n, 2 * RET_WIDTH), BF16), tile_out(), [], "proj_qk_ret", shift_jobs[0])
    vg_r = _proj(u, w_in, [COL_VR], 2, RET_WIDTH, tm, _epi_value_gate, [], [],
                 jax.ShapeDtypeStruct((n, 2 * RET_WIDTH), BF16), tile_out(), [], "proj_vg_ret", shift_jobs[1])
    gates = _proj(u, w_in, [COL_GATES], ngate // RET_WIDTH, RET_WIDTH, tm, _epi_sigmoid, [], [],
                  jax.ShapeDtypeStruct((n, ngate), BF16), tile_out(), [], "proj_gates", shift_jobs[2])
    results = (qk_r, vg_r, gates)
    if any(job is not None for job in shift_jobs):
        return tuple(r[0] for r in results), [r[1] for r in results]
    return results, []


def _proj_dilated_prompt(u, w_in, q_norm, k_norm, gi, b, l, tm):
    w, r = DIL_PAIRS[gi]
    lw = min(w, l)
    bps = l // tm
    tail_rows = min(lw, tm)
    tail_blocks = lw // tail_rows
    cm = (b, r, l // r, DIL_OUT_WIDTH)
    if r == 1:
        cm_shape = jax.ShapeDtypeStruct((b * l, DIL_OUT_WIDTH), BF16)
        cm_spec = lambda: pl.BlockSpec((tm, DIL_OUT_WIDTH), lambda j, i: (i, 0))
    else:
        cm_shape = jax.ShapeDtypeStruct(cm, BF16)
        cm_spec = lambda: pl.BlockSpec((None, r, tm // r, DIL_OUT_WIDTH), lambda j, i: (i // bps, 0, i % bps, 0))
    tail_shape = jax.ShapeDtypeStruct((b, lw, DIL_OUT_WIDTH), F32)
    tail_spec = lambda: pl.BlockSpec(
        (None, tail_rows, DIL_OUT_WIDTH),
        lambda j, i: (i // bps, jnp.maximum(i % bps - (bps - tail_blocks), 0), 0))
    gspec = pl.BlockSpec((1, HEAD_DIM), lambda j, i: (0, 0))
    off = gi * DIL_OUT_WIDTH
    q, k, v, kt, vt = _proj(
        u, w_in, [COL_QA + off, COL_KA + off, COL_VA + off], 1, DIL_OUT_WIDTH, tm,
        functools.partial(_epi_dilated_prompt, r=r),
        [q_norm.reshape(1, HEAD_DIM), k_norm.reshape(1, HEAD_DIM)], [gspec, gspec],
        (cm_shape, cm_shape, cm_shape, tail_shape, tail_shape),
        (cm_spec(), cm_spec(), cm_spec(), tail_spec(), tail_spec()),
        [pltpu.VMEM((3, DIL_HPG, tm, HEAD_DIM), F32)], "proj_dilated_prompt")
    cls = lambda t: t.reshape(b * r, l // r, DIL_OUT_WIDTH)
    return cls(q), cls(k), cls(v), kt, vt


def _proj_dilated_sample(u, w_in, q_norm, k_norm):
    n = u.shape[0]
    gspec = pl.BlockSpec((1, HEAD_DIM), lambda j, i: (0, 0))
    return _proj(u, w_in, [COL_QA], 3 * DIL_GROUPS, DIL_OUT_WIDTH, n, _epi_dilated_sample,
                 [q_norm.reshape(1, HEAD_DIM), k_norm.reshape(1, HEAD_DIM)], [gspec, gspec],
                 jax.ShapeDtypeStruct((n, 3 * DIL_WIDTH), F32),
                 pl.BlockSpec((n, DIL_OUT_WIDTH), lambda j, i: (i, j)), [], "proj_dilated_sample")


def _retention_kernel(q_ref, k_ref, v_ref, g_ref, s0_ref, dm_ref, qd_ref, kd_ref, cd_ref,
                      y_ref, sf_ref, state, *, chunk, nchunks):
    t = pl.program_id(2)

    @pl.when(t == 0)
    def _():
        state[...] = s0_ref[...]

    dmask = dm_ref[...]
    qdec = qd_ref[...]
    kdec = kd_ref[...]
    cdec = cd_ref[...]
    s_prev = state[...]
    for ci in range(nchunks):
        rows = pl.ds(ci * chunk, chunk)
        q = q_ref[rows, :]
        k = k_ref[rows, :]
        v = v_ref[rows, :]
        sc = lax.dot_general(q, k, (((1,), (1,)), ((), ())), preferred_element_type=F32) * dmask
        intra = jnp.dot(sc.astype(BF16), v, preferred_element_type=F32)
        qd = (q.astype(F32) * qdec).astype(BF16)
        cross = jnp.dot(qd, s_prev.astype(BF16), preferred_element_type=F32)
        kd = (k.astype(F32) * kdec).astype(BF16)
        kv = lax.dot_general(kd, v, (((0,), (0,)), ((), ())), preferred_element_type=F32)
        s_prev = s_prev * cdec + kv
        o = intra + cross
        mu = jnp.mean(o, axis=-1, keepdims=True)
        oc = o - mu
        var = jnp.mean(oc * oc, axis=-1, keepdims=True)
        y = g_ref[rows, :].astype(F32) * (oc * lax.rsqrt(var + GN_EPS))
        y_ref[rows, :] = y.astype(y_ref.dtype)
    state[...] = s_prev

    @pl.when(t == pl.num_programs(2) - 1)
    def _():
        sf_ref[...] = state[...]


def _retention_decay(chunk, valid):
    lg = jnp.log1p(-jnp.exp2(-5.0 - jnp.arange(RET_HEADS, dtype=F32)))
    idx = jnp.arange(chunk, dtype=F32)
    rel = idx[:, None] - idx[None, :]
    dmask = jnp.where(rel[None] >= 0, jnp.exp(lg[:, None, None] * jnp.maximum(rel, 0.0)[None]), 0.0)
    qdec = jnp.exp(lg[:, None] * (idx + 1.0)[None, :])[..., None]
    kdec = jnp.exp(lg[:, None] * (valid - 1.0 - idx)[None, :])[..., None]
    kdec = jnp.where((idx < valid)[None, :, None], kdec, 0.0)
    cdec = jnp.exp(lg * valid)[:, None, None]
    return dmask, qdec, kdec, cdec


def _retention_prompt(qk_r, vg_r, s0, chunk, rows_per_step):
    b, l, _ = qk_r.shape
    h = RET_HEADS
    dmask, qdec, kdec, cdec = _retention_decay(chunk, chunk)
    nsteps = l // rows_per_step
    blk = lambda off: pl.BlockSpec((None, rows_per_step, HEAD_DIM), lambda bi, hi, ti: (bi, ti, hi + off))
    per_head = lambda shape: pl.BlockSpec((None,) + shape, lambda bi, hi, ti: (hi,) + (0,) * len(shape))
    state_spec = pl.BlockSpec((None, None, HEAD_DIM, HEAD_DIM), lambda bi, hi, ti: (bi, hi, 0, 0))
    return pl.pallas_call(
        functools.partial(_retention_kernel, chunk=chunk, nchunks=rows_per_step // chunk),
        out_shape=(jax.ShapeDtypeStruct((b, l, RET_WIDTH), BF16),
                   jax.ShapeDtypeStruct((b, h, HEAD_DIM, HEAD_DIM), F32)),
        grid=(b, h, nsteps),
        in_specs=[blk(0), blk(h), blk(0), blk(h), state_spec,
                  per_head((chunk, chunk)), per_head((chunk, 1)), per_head((chunk, 1)), per_head((1, 1))],
        out_specs=(blk(0), state_spec),
        scratch_shapes=[pltpu.VMEM((HEAD_DIM, HEAD_DIM), F32)],
        compiler_params=_params(("parallel", "parallel", "arbitrary"), 32),
        name="retention_prompt",
    )(qk_r, qk_r, vg_r, vg_r, s0, dmask, qdec, kdec, cdec)


def _retention_sample_kernel(q_ref, k_ref, v_ref, g_ref, s0_ref, dm_ref, qd_ref, kd_ref, cd_ref,
                             y_ref, sf_ref, *, ntok):
    for bi in range(q_ref.shape[0]):
        for h in range(RET_HEADS):
            q = q_ref[bi, h]
            k = k_ref[bi, h]
            v = v_ref[bi, h]
            s_prev = s0_ref[bi, h]
            dmask = dm_ref[h]
            o = jnp.dot(q * qd_ref[h], s_prev, preferred_element_type=F32)
            for j in range(ntok):
                sj = jnp.sum(q * k[j:j + 1, :], axis=-1, keepdims=True) * dmask[:, j:j + 1]
                o = o + sj * v[j:j + 1, :]
            kd = k * kd_ref[h]
            kv = lax.dot_general(kd, v, (((0,), (0,)), ((), ())), preferred_element_type=F32)
            sf_ref[bi, h] = s_prev * cd_ref[h] + kv
            mu = jnp.mean(o, axis=-1, keepdims=True)
            oc = o - mu
            var = jnp.mean(oc * oc, axis=-1, keepdims=True)
            y_ref[bi, h] = g_ref[bi, h] * (oc * lax.rsqrt(var + GN_EPS))


def _retention_sample(q, k, v, g, s0, ntok):
    bd, h, tp, _ = q.shape
    dmask, qdec, kdec, cdec = _retention_decay(tp, ntok)
    nb = math.gcd(bd, SAMPLE_SEQS_PER_STEP)
    tok = pl.BlockSpec((nb, h, tp, HEAD_DIM), lambda bi: (bi, 0, 0, 0))
    st = pl.BlockSpec((nb, h, HEAD_DIM, HEAD_DIM), lambda bi: (bi, 0, 0, 0))
    const = lambda a: pl.BlockSpec(a.shape, lambda bi: (0,) * a.ndim)
    return pl.pallas_call(
        functools.partial(_retention_sample_kernel, ntok=ntok),
        out_shape=(jax.ShapeDtypeStruct((bd, h, tp, HEAD_DIM), F32),
                   jax.ShapeDtypeStruct((bd, h, HEAD_DIM, HEAD_DIM), F32)),
        grid=(bd // nb,),
        in_specs=[tok, tok, tok, tok, st, const(dmask), const(qdec), const(kdec), const(cdec)],
        out_specs=(tok, st),
        compiler_params=_params(("parallel",), 32),
        name="retention_sample",
    )(q, k, v, g, s0, dmask, qdec, kdec, cdec)


def _t5_bucket(dist):
    max_exact = REL_BUCKETS // 2
    d = jnp.maximum(dist, 0)
    df = jnp.maximum(d, 1).astype(F32)
    large = max_exact + (jnp.log(df / max_exact) / math.log(REL_MAX_DIST / max_exact)
                         * (REL_BUCKETS - max_exact)).astype(jnp.int32)
    large = jnp.minimum(large, REL_BUCKETS - 1)
    return jnp.where(d < max_exact, d, large)


def _bias_lookup(tab, dist):
    onehot = _t5_bucket(dist)[..., None] == jnp.arange(REL_BUCKETS)
    return jnp.sum(jnp.where(onehot[..., None], tab.astype(F32), 0.0), axis=-2)


def _dilated_kernel(q_ref, kp_ref, kc_ref, vp_ref, vc_ref, bias_ref, o_ref, lse_ref, kfull, vfull, *, nsub):
    i = pl.program_id(1)
    blk = DIL_BLOCK
    kfull[0:blk, :] = kp_ref[...]
    kfull[blk:, :] = kc_ref[...]
    vfull[0:blk, :] = vp_ref[...]
    vfull[blk:, :] = vc_ref[...]
    col = lax.broadcasted_iota(jnp.int32, (blk, 2 * blk), 1)
    lane_head = lax.broadcasted_iota(jnp.int32, (blk, HEAD_DIM), 1) // LSE_LANES
    for s in range(nsub):
        rows = pl.ds(s * blk, blk)
        win = pl.ds(s * blk, 2 * blk)
        lse_tile = jnp.zeros((blk, HEAD_DIM), F32)
        for h in range(DIL_HPG):
            cols = pl.ds(h * HEAD_DIM, HEAD_DIM)
            q = q_ref[rows, cols]
            kw = kfull[win, cols]
            vw = vfull[win, cols]
            sc = lax.dot_general(q, kw, (((1,), (1,)), ((), ())), preferred_element_type=F32) + bias_ref[h]
            if s == 0:
                sc = jnp.where((col >= blk) | (i > 0), sc, -jnp.inf)
            m = jnp.max(sc, axis=-1, keepdims=True)
            p = jnp.exp(sc - m)
            l = jnp.sum(p, axis=-1, keepdims=True)
            o = jnp.dot(p.astype(BF16), vw, preferred_element_type=F32) / l
            o_ref[rows, cols] = o.astype(o_ref.dtype)
            lse_tile = jnp.where(lane_head == h, m + jnp.log(l), lse_tile)
        lse_ref[rows, :] = lse_tile


def _dilated_bias(bias_tab, r):
    blk = DIL_BLOCK
    qi = jnp.arange(blk)[:, None]
    kj = jnp.arange(2 * blk)[None, :]
    dc = blk + qi - kj
    band = (dc >= 0) & (dc <= blk)
    bias = _bias_lookup(bias_tab, dc * r).transpose(2, 0, 1)
    return jnp.where(band[None], bias, -jnp.inf)


def _dilated_prompt(q, k, v, bias, nsub):
    n, lc, w = q.shape
    blk = DIL_BLOCK
    tq = nsub * blk
    cur = pl.BlockSpec((None, tq, w), lambda ni, i: (ni, i, 0))
    prev = pl.BlockSpec((None, blk, w), lambda ni, i: (ni, jnp.maximum(i * nsub - 1, 0), 0))
    return pl.pallas_call(
        functools.partial(_dilated_kernel, nsub=nsub),
        out_shape=(jax.ShapeDtypeStruct((n, lc, w), BF16),
                   jax.ShapeDtypeStruct((n, lc, HEAD_DIM), F32)),
        grid=(n, lc // tq),
        in_specs=[cur, prev, cur, prev, cur, pl.BlockSpec(bias.shape, lambda ni, i: (0, 0, 0))],
        out_specs=(cur, pl.BlockSpec((None, tq, HEAD_DIM), lambda ni, i: (ni, i, 0))),
        scratch_shapes=[pltpu.VMEM((tq + blk, w), BF16), pltpu.VMEM((tq + blk, w), BF16)],
        compiler_params=_params(("parallel", "parallel"), 32),
        name="dilated_prompt",
    )(q, k, k, v, v, bias)


def _dilated_sample_kernel(q_ref, kn_ref, vn_ref, c0_ref, c1_ref, c2_ref, bc_ref, bn_ref,
                           o_ref, lse_ref, *, ntok):
    caches = (c0_ref, c1_ref, c2_ref)
    for gi in range(DIL_GROUPS):
        cache = caches[gi]
        nclass = cache.shape[1]
        for t in range(ntok):
            cls = t if nclass > 1 else 0
            qt = q_ref[t, gi]
            kc = cache[:, cls, 0]
            vc = cache[:, cls, 1]
            sc = jnp.sum(kc * qt[None], axis=-1, keepdims=True) + bc_ref[gi, t]
            sn = jnp.sum(kn_ref[:, gi] * qt[None], axis=-1, keepdims=True) + bn_ref[gi, t]
            m = jnp.maximum(jnp.max(sc, axis=0), jnp.max(sn, axis=0))
            pc = jnp.exp(sc - m[None])
            pn = jnp.exp(sn - m[None])
            l = jnp.sum(pc, axis=0) + jnp.sum(pn, axis=0)
            o = jnp.sum(pc * vc, axis=0) + jnp.sum(pn * vn_ref[:, gi], axis=0)
            o_ref[t, gi] = o / l
            lse_ref[t, gi] = m + jnp.log(l)


def _dilated_sample(q, kn, vn, caches, bias_cache, bias_new, ntok):
    bd = q.shape[0]
    tok = pl.BlockSpec((None, ntok, DIL_GROUPS, DIL_HPG, HEAD_DIM), lambda bi: (bi, 0, 0, 0, 0))
    cspecs = []
    for c in caches:
        ncls = min(c.shape[2], ntok)
        cspecs.append(pl.BlockSpec((None, DIL_BLOCK, ncls, 2, DIL_HPG, HEAD_DIM),
                                   lambda bi: (bi, 0, 0, 0, 0, 0)))
    const = lambda a: pl.BlockSpec(a.shape, lambda bi: (0,) * a.ndim)
    return pl.pallas_call(
        functools.partial(_dilated_sample_kernel, ntok=ntok),
        out_shape=(jax.ShapeDtypeStruct(q.shape, F32),
                   jax.ShapeDtypeStruct((bd, ntok, DIL_GROUPS, DIL_HPG, 1), F32)),
        grid=(bd,),
        in_specs=[tok, tok, tok] + cspecs + [const(bias_cache), const(bias_new)],
        out_specs=(tok, pl.BlockSpec((None, ntok, DIL_GROUPS, DIL_HPG, 1), lambda bi: (bi, 0, 0, 0, 0))),
        compiler_params=_params(("parallel",), 32),
        name="dilated_sample",
    )(q, kn, vn, *caches, bias_cache, bias_new)


def _dilated_sample_bias(rel_bias, ntok):
    bc, bn = [], []
    m = jnp.arange(DIL_BLOCK)
    tn = jnp.arange(ntok)
    for gi, (_, r) in enumerate(DIL_PAIRS):
        tab = rel_bias[:, gi * DIL_HPG:(gi + 1) * DIL_HPG]
        rows_c, rows_n = [], []
        for t in range(ntok):
            if r == 1:
                jc = DIL_BLOCK + t - m
                okc = m >= t
                jn = t - tn
                okn = tn <= t
            else:
                jc = DIL_BLOCK - m
                okc = jnp.ones_like(m, bool)
                jn = jnp.zeros_like(tn)
                okn = tn == t
            rows_c.append(jnp.where(okc[:, None], _bias_lookup(tab, jc * r), -jnp.inf))
            rows_n.append(jnp.where(okn[:, None], _bias_lookup(tab, jn * r), -jnp.inf))
        bc.append(jnp.stack(rows_c))
        bn.append(jnp.stack(rows_n))
    return jnp.stack(bc)[..., None], jnp.stack(bn)[..., None]


def _merge_kernel(o0_ref, o1_ref, o2_ref, l0_ref, l1_ref, l2_ref, yr_ref, gt_ref, wr_ref, wd_ref,
                  out_ref, oa_ref, *nat, dilations):
    tm = out_ref.shape[0]
    o_nat, l_nat = [], []
    for g, (o_ref, l_ref, r) in enumerate(zip((o0_ref, o1_ref, o2_ref), (l0_ref, l1_ref, l2_ref), dilations)):
        if r == 1:
            o_nat.append([o_ref[:, _head_cols(h)].astype(F32) for h in range(DIL_HPG)])
            l_nat.append(l_ref[...])
        else:
            on, ln = nat[2 * g], nat[2 * g + 1]
            for c in range(r):
                ln[pl.ds(c, tm // r, stride=r), :] = l_ref[c]
                for h in range(DIL_HPG):
                    on[h, pl.ds(c, tm // r, stride=r), :] = o_ref[c, :, _head_cols(h)].astype(F32)
            o_nat.append([on[h] for h in range(DIL_HPG)])
            l_nat.append(ln[...])
    m = jnp.maximum(jnp.maximum(l_nat[0], l_nat[1]), l_nat[2])
    es = [jnp.exp(l - m) for l in l_nat]
    den = es[0] + es[1] + es[2]
    for h in range(DIL_HPG):
        lane = h * LSE_LANES
        acc = None
        for g in range(DIL_GROUPS):
            term = (es[g][:, lane:lane + 1] / den[:, lane:lane + 1]) * o_nat[g][h]
            acc = term if acc is None else acc + term
        oa_ref[:, _head_cols(h)] = acc.astype(oa_ref.dtype)
    d = out_ref.shape[1]
    for c0 in range(0, d, PROJ_CHUNK):
        cols = slice(c0, c0 + PROJ_CHUNK)
        yr = jnp.dot(yr_ref[...], wr_ref[:, cols], preferred_element_type=F32)
        ya = jnp.dot(oa_ref[...], wd_ref[:, cols], preferred_element_type=F32)
        g_r = gt_ref[:, cols].astype(F32)
        g_a = gt_ref[:, d + c0:d + c0 + PROJ_CHUNK].astype(F32)
        out_ref[:, cols] = (g_r * yr + g_a * ya).astype(out_ref.dtype)


def _merge(o_g, lse_g, dilations, b, l, yr_in, gates, w_ret_b, w_dil_b, tm):
    n = yr_in.shape[0]
    d = w_ret_b.shape[1]
    bps = l // tm
    row = lambda w: pl.BlockSpec((tm, w), lambda i: (i, 0))
    const = lambda a: pl.BlockSpec(a.shape, lambda i: (0, 0))

    def group_specs(width):
        specs = []
        for r in dilations:
            if r == 1:
                specs.append(row(width))
            else:
                specs.append(pl.BlockSpec((None, r, tm // r, width), lambda i: (i // bps, 0, i % bps, 0)))
        return specs

    view = lambda t, r, width: t.reshape(n, width) if r == 1 else t.reshape(b, r, l // r, width)
    o_in = [view(t, r, DIL_OUT_WIDTH) for t, r in zip(o_g, dilations)]
    l_in = [view(t, r, HEAD_DIM) for t, r in zip(lse_g, dilations)]
    scratch = [pltpu.VMEM((tm, DIL_OUT_WIDTH), BF16)]
    for _ in dilations:
        scratch += [pltpu.VMEM((DIL_HPG, tm, HEAD_DIM), F32), pltpu.VMEM((tm, HEAD_DIM), F32)]
    return pl.pallas_call(
        functools.partial(_merge_kernel, dilations=tuple(dilations)),
        out_shape=jax.ShapeDtypeStruct((n, d), BF16),
        grid=(n // tm,),
        in_specs=group_specs(DIL_OUT_WIDTH) + group_specs(HEAD_DIM) + [row(RET_WIDTH), row(2 * d),
                                                                      const(w_ret_b), const(w_dil_b)],
        out_specs=row(d),
        scratch_shapes=scratch,
        compiler_params=_params(("parallel",), 48),
        name="merge_branches",
    )(*o_in, *l_in, yr_in, gates, w_ret_b, w_dil_b)


ROUTER_ROWS = 8 + N_EXPERTS


def _route_select(lt):
    lg = lt[0:N_GROUPS]
    gmax = jnp.max(lg, axis=0, keepdims=True)
    w_coarse = 1.0 / jnp.sum(jnp.exp(lg - gmax), axis=0, keepdims=True)
    gid = lax.broadcasted_iota(jnp.int32, lg.shape, 0)
    gsel = jnp.min(jnp.where(lg == gmax, gid, N_GROUPS), axis=0, keepdims=True)
    le = jnp.zeros((EXPERTS_PER_GROUP, lt.shape[1]), F32)
    for g in range(N_GROUPS):
        le = jnp.where(gsel == g, lt[8 + g * EXPERTS_PER_GROUP:8 + (g + 1) * EXPERTS_PER_GROUP], le)
    eid = lax.broadcasted_iota(jnp.int32, le.shape, 0)
    v1 = jnp.max(le, axis=0, keepdims=True)
    i1 = jnp.min(jnp.where(le == v1, eid, EXPERTS_PER_GROUP), axis=0, keepdims=True)
    le2 = jnp.where(eid == i1, -jnp.inf, le)
    v2 = jnp.max(le2, axis=0, keepdims=True)
    i2 = jnp.min(jnp.where(le2 == v2, eid, EXPERTS_PER_GROUP), axis=0, keepdims=True)
    e21 = jnp.exp(v2 - v1)
    w1 = w_coarse / (1.0 + e21)
    w2 = w_coarse * e21 / (1.0 + e21)
    e1 = gsel * EXPERTS_PER_GROUP + i1
    e2 = gsel * EXPERTS_PER_GROUP + i2
    xid = lax.broadcasted_iota(jnp.int32, (N_EXPERTS, lt.shape[1]), 0)
    oh1 = jnp.where(xid == e1, 1.0, 0.0)
    oh2 = jnp.where(xid == e2, 1.0, 0.0)
    return e1, e2, w1, w2, oh1, oh2


def _route_rank(sel, p1, p2, before):
    e1, e2, w1, w2, oh1, oh2 = sel
    n1 = jnp.sum(oh1, axis=1, keepdims=True)
    n2 = jnp.sum(oh2, axis=1, keepdims=True)
    rank1 = jnp.sum(oh1 * (p1 + before), axis=0, keepdims=True)
    rank2 = jnp.sum(oh2 * (p2 + before + n1), axis=0, keepdims=True)
    row = lax.broadcasted_iota(jnp.int32, (8, e1.shape[1]), 0)
    ei = jnp.where(row == 0, e1, jnp.where(row == 1, e2, jnp.where(
        row == 2, rank1.astype(jnp.int32), jnp.where(row == 3, rank2.astype(jnp.int32), 0))))
    wt = jnp.where(row == 0, w1, jnp.where(row == 1, w2, 0.0))
    return ei, wt, before + n1 + n2


def _outproj_router_kernel(mg_ref, x_ref, wo_ref, g2_ref, wr_ref, br_ref, tri_ref, cin_ref,
                           h_ref, u2_ref, ei_ref, wt_ref, cnt_ref, h_keep):
    i = pl.program_id(0)

    @pl.when(i == 0)
    def _():
        cnt_ref[...] = cin_ref[...]
        h_keep[...] = jnp.zeros_like(h_keep)

    sub = tri_ref.shape[0]
    subtiles = [slice(s0, s0 + sub) for s0 in range(0, h_ref.shape[0], sub)]

    def project(rows):
        h_ref[rows, :] = x_ref[rows, :] + jnp.dot(mg_ref[rows, :], wo_ref[...], preferred_element_type=F32)

    project(subtiles[0])
    logits = []
    for rows in subtiles:
        h = h_keep[rows, :]
        u2 = h * lax.rsqrt(jnp.mean(h * h, axis=-1, keepdims=True) + NORM_EPS) * g2_ref[...]
        u2_ref[rows, :] = u2
        logits.append(lax.dot_general(wr_ref[...], u2.astype(BF16), (((1,), (1,)), ((), ())),
                                      preferred_element_type=F32) + br_ref[...])
    for rows in subtiles[1:]:
        project(rows)
    selected = [_route_select(lt) for lt in logits]
    prefix = [(jnp.dot(sel[4].astype(BF16), tri_ref[...], preferred_element_type=F32),
               jnp.dot(sel[5].astype(BF16), tri_ref[...], preferred_element_type=F32)) for sel in selected]
    h_keep[...] = h_ref[...]
    counts = cnt_ref[...]
    for rows, sel, (p1, p2) in zip(subtiles, selected, prefix):
        ei, wt, routed = _route_rank(sel, p1, p2, counts)
        counts = jnp.where(i > 0, routed, counts)
        ei_ref[:, rows] = ei
        wt_ref[:, rows] = wt
    cnt_ref[...] = counts


def _outproj_router(merged, x, w_o_b, norm_ffn, wr_t, br_t, counts_in, tm):
    n, d = x.shape
    nsteps = n // tm
    sub = min(tm, ROUTER_SUB)
    tri = (jnp.arange(sub)[:, None] < jnp.arange(sub)[None, :]).astype(BF16)
    matmul_tile = lambda: pl.BlockSpec((tm, d), lambda i: (jnp.minimum(i, nsteps - 1), 0))
    routed_rows = lambda: pl.BlockSpec((tm, d), lambda i: (jnp.maximum(i - 1, 0), 0))
    routed_lanes = lambda: pl.BlockSpec((8, tm), lambda i: (0, jnp.maximum(i - 1, 0)))
    const = lambda a: pl.BlockSpec(a.shape, lambda i: (0, 0))
    return pl.pallas_call(
        _outproj_router_kernel,
        out_shape=(jax.ShapeDtypeStruct((n, d), F32), jax.ShapeDtypeStruct((n, d), F32),
                   jax.ShapeDtypeStruct((8, n), jnp.int32), jax.ShapeDtypeStruct((8, n), F32),
                   jax.ShapeDtypeStruct((N_EXPERTS, 1), F32)),
        grid=(nsteps + 1,),
        in_specs=[matmul_tile(), matmul_tile(), const(w_o_b), pl.BlockSpec((1, d), lambda i: (0, 0)),
                  const(wr_t), const(br_t), const(tri), const(counts_in)],
        out_specs=(matmul_tile(), routed_rows(), routed_lanes(), routed_lanes(), const(counts_in)),
        scratch_shapes=[pltpu.VMEM((tm, d), F32)],
        compiler_params=_params(("arbitrary",), 60),
        name="outproj_router",
    )(merged, x, w_o_b, norm_ffn.reshape(1, d), wr_t, br_t, tri, counts_in)


def _router_weights(w_rg, b_rg, w_re, b_re):
    d = w_rg.shape[0]
    wr = jnp.zeros((ROUTER_ROWS, d), F32)
    wr = wr.at[0:N_GROUPS].set(w_rg.T)
    wr = wr.at[8:].set(w_re.transpose(0, 2, 1).reshape(N_EXPERTS, d))
    br = jnp.zeros((ROUTER_ROWS, 1), F32)
    br = br.at[0:N_GROUPS, 0].set(b_rg)
    br = br.at[8:, 0].set(b_re.reshape(N_EXPERTS))
    return wr.astype(BF16), br


DMA_UNROLL = 8


def _dispatch_kernel(pos_p_ref, pos_s_ref, zero_ref, up_ref, us_ref, xs_ref, zbuf, stage, sem, *, tm):
    i = pl.program_id(0)
    last = pl.num_programs(0) - 1
    ntiles = xs_ref.shape[0] // MOE_TILE

    def tile_copy(t):
        return pltpu.make_async_copy(zbuf, xs_ref.at[pl.ds(pl.multiple_of(t * MOE_TILE, MOE_TILE), MOE_TILE)],
                                     sem.at[2])

    @pl.when(i == 0)
    def _():
        zbuf[...] = jnp.zeros_like(zbuf)
        for op in ("start", "wait"):
            def per_tile(t, c, op=op):
                @pl.when(zero_ref[t] != 0)
                def _():
                    getattr(tile_copy(t), op)()
                return c

            lax.fori_loop(0, ntiles, per_tile, 0)

    def scatter_rows(op, src_ref, pos_ref, base, nrows, ntok_total, row_sem):
        if op == "wait":
            for _ in range(TOP_K):
                pltpu.make_async_copy(src_ref, xs_ref.at[pl.ds(0, nrows)], row_sem).wait()
            return

        def body(r, c):
            for kk in range(TOP_K):
                slot = pos_ref[kk * ntok_total + base + r]
                pltpu.make_async_copy(src_ref.at[pl.ds(r, 1)], xs_ref.at[pl.ds(slot, 1)],
                                      row_sem).start(priority=kk % 2)
            return c

        lax.fori_loop(0, nrows, body, 0, unroll=DMA_UNROLL)

    prompt_rows = last * tm
    cur = i % 2

    @pl.when(i < last)
    def _():
        stage[cur] = up_ref[...]
        scatter_rows("start", stage.at[cur], pos_p_ref, i * tm, tm, prompt_rows, sem.at[cur])

    @pl.when(i > 0)
    def _():
        scatter_rows("wait", stage.at[1 - cur], pos_p_ref, (i - 1) * tm, tm, prompt_rows, sem.at[1 - cur])

    @pl.when(i == last)
    def _():
        for op in ("start", "wait"):
            scatter_rows(op, us_ref, pos_s_ref, 0, us_ref.shape[0], us_ref.shape[0], sem.at[2])


def _dispatch(pos_p, pos_s, zero_tiles, u2p, u2s, tm):
    n, d = u2p.shape
    nsteps = n // tm
    grid_spec = pltpu.PrefetchScalarGridSpec(
        num_scalar_prefetch=3,
        grid=(nsteps + 1,),
        in_specs=[pl.BlockSpec((tm, d), lambda i, *_: (jnp.minimum(i, nsteps - 1), 0)),
                  pl.BlockSpec(u2s.shape, lambda i, *_: (0, 0))],
        out_specs=pl.BlockSpec(memory_space=pl.ANY),
        scratch_shapes=[pltpu.VMEM((MOE_TILE, d), F32), pltpu.VMEM((2, tm, d), F32),
                        pltpu.SemaphoreType.DMA((3,))],
    )
    return pl.pallas_call(
        functools.partial(_dispatch_kernel, tm=tm),
        out_shape=jax.ShapeDtypeStruct((zero_tiles.shape[0] * MOE_TILE, d), F32),
        grid_spec=grid_spec,
        compiler_params=_params(("arbitrary",), 40),
        name="moe_dispatch",
    )(pos_p, pos_s, zero_tiles, u2p, u2s)


def _gmm_kernel(te_ref, nt_ref, x_ref, wg_ref, wu_ref, wd_ref, o_ref, wgb, wub, wdb):
    i = pl.program_id(0)
    fresh = jnp.logical_or(i == 0, te_ref[i] != te_ref[jnp.maximum(i - 1, 0)])

    @pl.when(jnp.logical_and(fresh, i < nt_ref[0]))
    def _():
        wgb[...] = wg_ref[...].astype(BF16)
        wub[...] = wu_ref[...].astype(BF16)
        wdb[...] = wd_ref[...].astype(BF16)

    @pl.when(i < nt_ref[0])
    def _():
        x = x_ref[...].astype(BF16)
        acc = None
        for c0 in range(0, wgb.shape[1], PROJ_CHUNK):
            cols = slice(c0, c0 + PROJ_CHUNK)
            a = jnp.dot(x, wgb[:, cols], preferred_element_type=F32)
            b = jnp.dot(x, wub[:, cols], preferred_element_type=F32)
            hm = (a * jax.nn.sigmoid(a) * b).astype(BF16)
            part = jnp.dot(hm, wdb[cols, :], preferred_element_type=F32)
            acc = part if acc is None else acc + part
        o_ref[...] = acc

    @pl.when(i >= nt_ref[0])
    def _():
        o_ref[...] = jnp.zeros_like(o_ref)


def _gmm(tile_expert, ntiles_used, xs, w_gate, w_up, w_down):
    npad, d = xs.shape
    f = w_gate.shape[-1]
    ntiles = npad // MOE_TILE
    epg = w_gate.shape[1]

    def xmap(i, te, nt):
        return (jnp.minimum(i, nt[0] - 1), 0)

    def wmap(i, te, nt):
        e = te[i]
        return (e // epg, e % epg, 0, 0)

    grid_spec = pltpu.PrefetchScalarGridSpec(
        num_scalar_prefetch=2,
        grid=(ntiles,),
        in_specs=[pl.BlockSpec((MOE_TILE, d), xmap),
                  pl.BlockSpec((None, None, d, f), wmap),
                  pl.BlockSpec((None, None, d, f), wmap),
                  pl.BlockSpec((None, None, f, d), wmap)],
        out_specs=pl.BlockSpec((MOE_TILE, d), lambda i, te, nt: (i, 0)),
        scratch_shapes=[pltpu.VMEM((d, f), BF16), pltpu.VMEM((d, f), BF16), pltpu.VMEM((f, d), BF16)],
    )
    return pl.pallas_call(
        _gmm_kernel,
        out_shape=jax.ShapeDtypeStruct((npad, d), F32),
        grid_spec=grid_spec,
        compiler_params=_params(("arbitrary",), 56),
        name="moe_grouped_matmul",
    )(tile_expert, ntiles_used, xs, w_gate, w_up, w_down)


def _combine_kernel(pos_ref, h_ref, wt_ref, os_ref, y_ref, g0, g1, sem, *, tm, ntok_total):
    i = pl.program_id(0)
    nsteps = pl.num_programs(0)
    cur = i % 2

    def row_copy(step, r, kk, slot):
        src = pos_ref[kk * ntok_total + step * tm + r]
        buf = (g0, g1)[kk]
        return pltpu.make_async_copy(os_ref.at[pl.ds(src, 1)], buf.at[slot, pl.ds(r, 1)], sem.at[slot])

    def issue_step(step, slot):
        def body(r, c):
            row_copy(step, r, 0, slot).start(priority=0)
            row_copy(step, r, 1, slot).start(priority=1)
            return c

        lax.fori_loop(0, tm, body, 0, unroll=DMA_UNROLL)

    @pl.when(i == 0)
    def _():
        issue_step(0, 0)

    @pl.when(i + 1 < nsteps)
    def _():
        issue_step(i + 1, 1 - cur)

    for buf in (g0, g1):
        pltpu.make_async_copy(os_ref.at[pl.ds(0, tm)], buf.at[cur], sem.at[cur]).wait()
    y_ref[...] = h_ref[...] + wt_ref[:, 0:1] * g0[cur] + wt_ref[:, 1:2] * g1[cur]


def _combine(pos_flat, h, wt_cols, out_sorted, tm):
    n, d = h.shape
    grid_spec = pltpu.PrefetchScalarGridSpec(
        num_scalar_prefetch=1,
        grid=(n // tm,),
        in_specs=[pl.BlockSpec((tm, d), lambda i, pos: (i, 0)),
                  pl.BlockSpec((tm, 8), lambda i, pos: (i, 0)),
                  pl.BlockSpec(memory_space=pl.ANY)],
        out_specs=pl.BlockSpec((tm, d), lambda i, pos: (i, 0)),
        scratch_shapes=[pltpu.VMEM((2, tm, d), F32), pltpu.VMEM((2, tm, d), F32),
                        pltpu.SemaphoreType.DMA((2,))],
    )
    return pl.pallas_call(
        functools.partial(_combine_kernel, tm=tm, ntok_total=n),
        out_shape=jax.ShapeDtypeStruct((n, d), F32),
        grid_spec=grid_spec,
        compiler_params=_params(("arbitrary",), 48),
        name="moe_combine",
    )(pos_flat, h, wt_cols, out_sorted)


def _moe_plan(counts, route_sets):
    npairs = sum(e.shape[1] for e, _ in route_sets) * TOP_K
    tiles = (counts + MOE_TILE - 1) // MOE_TILE
    tile_end = jnp.cumsum(tiles)
    offs = (tile_end - tiles) * MOE_TILE
    ids = jnp.arange(N_EXPERTS)
    slots = []
    for experts, ranks in route_sets:
        base = jnp.sum(jnp.where(experts[..., None] == ids, offs, 0), axis=-1)
        slots.append((base + ranks).reshape(-1).astype(jnp.int32))
    ntiles = (npairs + N_EXPERTS * (MOE_TILE - 1)) // MOE_TILE
    tile_expert = jnp.sum(tile_end[None, :] <= jnp.arange(ntiles)[:, None], axis=1)
    tile_expert = jnp.minimum(tile_expert, N_EXPERTS - 1).astype(jnp.int32)
    tile_ids = jnp.arange(ntiles)
    is_last = jnp.any((tile_ids[:, None] == tile_end[None, :] - 1) & (tiles[None, :] > 0), axis=1)
    zero_tiles = (is_last | (tile_ids >= tile_end[-1])).astype(jnp.int32)
    return slots, tile_expert, tile_end[-1:].astype(jnp.int32), zero_tiles


def _mixers_prompt(x, wts, counts_in, shift_jobs):
    b, l, d = x.shape
    n = b * l
    tm = ROW_TILE
    xf = x.reshape(n, d)
    u = _rmsnorm(xf, wts["norm_attn"], 2 * tm)
    tab = _rotary_tables(jnp.arange(l), l)
    (qk_r, vg_r, gates), shifted = _proj_retention_and_gates(u, wts["w_in"], tab, 2 * tm, shift_jobs)

    s0 = jnp.zeros((b, RET_HEADS, HEAD_DIM, HEAD_DIM), F32)
    yr_in, s_fin = _retention_prompt(qk_r.reshape(b, l, -1), vg_r.reshape(b, l, -1), s0, 256, min(l, 2048))

    o_g, lse_g, bufs = [], [], []
    for gi, (w, r) in enumerate(DIL_PAIRS):
        qg, kg, vg, kt, vt = _proj_dilated_prompt(u, wts["w_in"], wts["q_norm"], wts["k_norm"], gi, b, l, tm)
        bias = _dilated_bias(wts["rel_bias"][:, gi * DIL_HPG:(gi + 1) * DIL_HPG], r)
        o, lse = _dilated_prompt(qg, kg, vg, bias, min(8, l // r // DIL_BLOCK))
        o_g.append(o)
        lse_g.append(lse)
        bufs.append(jnp.stack([kt, vt], axis=2).reshape(b, kt.shape[1], 2, DIL_HPG, HEAD_DIM))

    merged = _merge(o_g, lse_g, [r for _, r in DIL_PAIRS], b, l, yr_in.reshape(n, RET_WIDTH), gates,
                    wts["w_ret_out"], wts["w_dil_out"], tm)
    h, u2, ei, wt, counts = _outproj_router(merged, xf, wts["w_o"], wts["norm_ffn"], wts["wr_t"], wts["br_t"],
                                            counts_in, ROUTER_TILE)
    return h, u2, ei, wt, counts, s_fin, bufs, shifted


def _mixers_sample(x, caches, state, wts, counts_in):
    bd, t, d = x.shape
    n = bd * t
    xf = x.reshape(n, d)
    u = _rmsnorm(xf, wts["norm_attn"], n)
    tab = _rotary_tables(PAST_LEN + jnp.arange(t), n)
    (qk_r, vg_r, gates), _ = _proj_retention_and_gates(u, wts["w_in"], tab, n)
    qkv_a = _proj_dilated_sample(u, wts["w_in"], wts["q_norm"], wts["k_norm"])

    def heads(a):
        a = a.astype(F32).reshape(bd, t, RET_HEADS, HEAD_DIM).transpose(0, 2, 1, 3)
        return jnp.pad(a, ((0, 0), (0, 0), (0, 8 - t), (0, 0)))

    y_r, s_new = _retention_sample(heads(qk_r[:, :RET_WIDTH]), heads(qk_r[:, RET_WIDTH:]),
                                   heads(vg_r[:, :RET_WIDTH]), heads(vg_r[:, RET_WIDTH:]), state, t)
    yr_in = y_r[:, :, :t].transpose(0, 2, 1, 3).reshape(n, RET_WIDTH).astype(BF16)

    grp = lambda a: a.reshape(bd, t, DIL_GROUPS, DIL_HPG, HEAD_DIM)
    qa, ka, va = (grp(qkv_a[:, s * DIL_WIDTH:(s + 1) * DIL_WIDTH]) for s in range(3))
    cviews = [c.reshape(bd, DIL_BLOCK, r, 2, DIL_HPG, HEAD_DIM) for c, (_, r) in zip(caches, DIL_PAIRS)]
    bias_c, bias_n = _dilated_sample_bias(wts["rel_bias"], t)
    o, lse = _dilated_sample(qa, ka, va, cviews, bias_c, bias_n, t)
    o_g = [o[:, :, gi].reshape(n, 1, DIL_OUT_WIDTH) for gi in range(DIL_GROUPS)]
    lse_g = [jnp.repeat(lse[:, :, gi].reshape(n, DIL_HPG), LSE_LANES, axis=1).reshape(n, 1, HEAD_DIM)
             for gi in range(DIL_GROUPS)]

    news = [jnp.stack([ka[:, :, gi], va[:, :, gi]], axis=2) for gi in range(DIL_GROUPS)]
    shift_jobs = list(zip(caches, news))

    merged = _merge(o_g, lse_g, [1] * DIL_GROUPS, 1, n, yr_in, gates, wts["w_ret_out"], wts["w_dil_out"], n)
    h, u2, ei, wt, counts = _outproj_router(merged, xf, wts["w_o"], wts["norm_ffn"], wts["wr_t"], wts["br_t"],
                                            counts_in, n)
    return h, u2, ei, wt, counts, s_new, shift_jobs


def kernel(x_prompt, x_sample, cache_kv_g0, cache_kv_g1, cache_kv_g2, state_ret, norm_attn, w_in, q_norm,
           k_norm, rel_bias, w_ret_out, w_dil_out, w_o, norm_ffn, w_router_group, b_router_group,
           w_router_expert, b_router_expert, w_gate, w_up, w_down):
    caches = (cache_kv_g0, cache_kv_g1, cache_kv_g2)
    ntok = x_sample.shape[1]
    for c, (w, r) in zip(caches, DIL_PAIRS):
        assert c.shape[1] == w == DIL_BLOCK * r and (r == 1 or ntok <= r) and ntok <= 8
    wr_t, br_t = _router_weights(w_router_group, b_router_group, w_router_expert, b_router_expert)
    wts = dict(norm_attn=norm_attn, w_in=w_in, q_norm=q_norm, k_norm=k_norm, rel_bias=rel_bias,
               w_ret_out=w_ret_out.astype(BF16), w_dil_out=w_dil_out.astype(BF16), w_o=w_o.astype(BF16),
               norm_ffn=norm_ffn, wr_t=wr_t, br_t=br_t)

    zero_counts = jnp.zeros((N_EXPERTS, 1), F32)
    hs, u2s, eis, wts_s, counts_s, s_s, shift_jobs = _mixers_sample(x_sample, caches, state_ret, wts, zero_counts)
    hp, u2p, eip, wtp, counts, s_p, bufs_p, bufs_s = _mixers_prompt(x_prompt, wts, counts_s, shift_jobs)

    slots, tile_expert, ntiles_used, zero_tiles = _moe_plan(
        counts[:, 0].astype(jnp.int32), [(eip[0:2], eip[2:4]), (eis[0:2], eis[2:4])])
    xs = _dispatch(slots[0], slots[1], zero_tiles, u2p, u2s, ROW_TILE)
    out_sorted = _gmm(tile_expert, ntiles_used, xs, w_gate, w_up, w_down)
    yp = _combine(slots[0], hp, wtp.T, out_sorted, ROW_TILE)
    ys = _combine(slots[1], hs, wts_s.T, out_sorted, hs.shape[0])

    return (yp.reshape(x_prompt.shape), ys.reshape(x_sample.shape), bufs_p[0], bufs_p[1], bufs_p[2], s_p,
            bufs_s[0], bufs_s[1], bufs_s[2], s_s)
```
